```python
import jax, jax.numpy as jnp
from jax import lax
import numpy as np

D_MODEL = 1024
BATCH = 8
SEQ = 8192
DEPTH = 2

CHUNK = 64
N_EVEN = (DEPTH + 1) // 2
N_ODD = DEPTH // 2

A_WIDTH = D_MODEL // 2
A_HEADS = 4
A_HEAD_DIM = A_WIDTH // A_HEADS
A_BLOCK = 128
B_WIDTH = D_MODEL // 2
POOL_WINDOWS = (2, 4, 8, 16)
B_GROUPS = len(POOL_WINDOWS)
B_GROUP_DIM = B_WIDTH // B_GROUPS
EVEN_IN = 2 * A_WIDTH + B_WIDTH
EVEN_MIX = A_WIDTH + B_WIDTH

C_WIDTH = D_MODEL // 2
C_KERNEL = 31
D_WIDTH = D_MODEL // 2
D_KERNEL = 3
ODD_IN = 2 * C_WIDTH + 3 * D_WIDTH
ODD_MIX = C_WIDTH + D_WIDTH

D_FF = -(-8 * D_MODEL // (3 * 256)) * 256
EPS = 1e-6

kernel_name = "hybrid_gmlp_pool_conformer_shortconv_trunk"


def rmsnorm(x, g):
    xf = x.astype(jnp.float32)
    y = xf * lax.rsqrt(jnp.mean(xf * xf, axis=-1, keepdims=True) + EPS)
    return (y * g.astype(jnp.float32)).astype(x.dtype)


def layernorm(x, g, b):
    xf = x.astype(jnp.float32)
    mu = jnp.mean(xf, axis=-1, keepdims=True)
    var = jnp.mean(jnp.square(xf - mu), axis=-1, keepdims=True)
    y = (xf - mu) * lax.rsqrt(var + EPS)
    return (y * g.astype(jnp.float32) + b.astype(jnp.float32)).astype(x.dtype)


def causal_depthwise_conv(x, w):
    k, c = w.shape
    return lax.conv_general_dilated(
        x, w[:, None, :].astype(x.dtype), window_strides=(1,), padding=[(k - 1, 0)],
        dimension_numbers=("NWC", "WIO", "NWC"), feature_group_count=c)


def mixer_a(u, v, w_s, b_s, ln_g, ln_b):
    bsz, s, _ = u.shape
    v = layernorm(v, ln_g, ln_b).reshape(bsz, s // A_BLOCK, A_BLOCK, A_HEADS, A_HEAD_DIM)
    pos = jnp.arange(A_BLOCK)
    mask = (pos[:, None] // CHUNK) >= (pos[None, :] // CHUNK)
    w = jnp.where(mask[None], w_s, jnp.zeros_like(w_s))
    sv = jnp.einsum("hij,bnjhd->bnihd", w, v) + b_s.T[None, None, :, :, None]
    return u * sv.reshape(bsz, s, A_WIDTH)


def mixer_b(x, w_pool, scale):
    bsz, s, _ = x.shape
    xf = x.astype(jnp.float32)
    csum = jnp.cumsum(xf, axis=1)
    t = jnp.arange(s)
    outs = []
    for gi, win in enumerate(POOL_WINDOWS):
        sl = slice(gi * B_GROUP_DIM, (gi + 1) * B_GROUP_DIM)
        cg = csum[..., sl]
        shifted = jnp.pad(cg, ((0, 0), (win, 0), (0, 0)))[:, :s]
        cnt = jnp.minimum(t + 1, win).astype(jnp.float32)[None, :, None]
        outs.append((cg - shifted) / cnt - xf[..., sl])
    pooled = jnp.stack(outs, axis=2).astype(x.dtype)
    y = jnp.einsum("bsgc,gcd->bsgd", pooled, w_pool).reshape(bsz, s, B_WIDTH)
    return y * scale


def mixer_c(a, g, w_dw, b_dw, ln_g, ln_b):
    h = a * jax.nn.sigmoid(g)
    h = causal_depthwise_conv(h, w_dw) + b_dw
    h = layernorm(h, ln_g, ln_b)
    return jax.nn.silu(h)


def mixer_d(bg, cg, xin, w_dw):
    return bg * causal_depthwise_conv(cg * xin, w_dw)


def swiglu(h, wg, wu, wd):
    return (jax.nn.silu(h @ wg) * (h @ wu)) @ wd


def _fwd_setup_inputs(seed: int = 0) -> dict:
    key = jax.random.key(seed)
    ks = jax.random.split(key, 24)
    f = jnp.float32

    def nrm(k, shape, fan_in):
        return jax.random.normal(k, shape, f) * (fan_in ** -0.5)

    def gain(k, shape):
        return jnp.ones(shape, f) + 0.02 * jax.random.normal(k, shape, f)

    def small(k, shape):
        return 0.02 * jax.random.normal(k, shape, f)

    return {
        "x": jax.random.normal(ks[0], (BATCH, SEQ, D_MODEL), f),
        "even_w_in": nrm(ks[1], (N_EVEN, D_MODEL, EVEN_IN), D_MODEL),
        "even_w_out": nrm(ks[2], (N_EVEN, EVEN_MIX, D_MODEL), EVEN_MIX),
        "a_w_s": nrm(ks[3], (N_EVEN, A_HEADS, A_BLOCK, A_BLOCK), A_BLOCK),
        "a_b_s": gain(ks[4], (N_EVEN, A_HEADS, A_BLOCK)),
        "a_ln_g": gain(ks[5], (N_EVEN, A_WIDTH)),
        "a_ln_b": small(ks[6], (N_EVEN, A_WIDTH)),
        "b_w_pool": nrm(ks[7], (N_EVEN, B_GROUPS, B_GROUP_DIM, B_GROUP_DIM), B_GROUP_DIM),
        "b_scale": gain(ks[8], (N_EVEN, B_WIDTH)),
        "odd_w_in": nrm(ks[9], (N_ODD, D_MODEL, ODD_IN), D_MODEL),
        "odd_w_out": nrm(ks[10], (N_ODD, ODD_MIX, D_MODEL), ODD_MIX),
        "c_w_dw": nrm(ks[11], (N_ODD, C_KERNEL, C_WIDTH), C_KERNEL),
        "c_b_dw": small(ks[12], (N_ODD, C_WIDTH)),
        "c_ln_g": gain(ks[13], (N_ODD, C_WIDTH)),
        "c_ln_b": small(ks[14], (N_ODD, C_WIDTH)),
        "d_w_dw": nrm(ks[15], (N_ODD, D_KERNEL, D_WIDTH), D_KERNEL),
        "norm_mix_g": gain(ks[16], (DEPTH, D_MODEL)),
        "norm_ffn_g": gain(ks[17], (DEPTH, D_MODEL)),
        "ffn_w_gate": nrm(ks[18], (DEPTH, D_MODEL, D_FF), D_MODEL),
        "ffn_w_up": nrm(ks[19], (DEPTH, D_MODEL, D_FF), D_MODEL),
        "ffn_w_down": nrm(ks[20], (DEPTH, D_FF, D_MODEL), D_FF),
        "final_norm_g": gain(ks[21], (D_MODEL,)),
    }


def _fwd_reference(x, even_w_in, even_w_out, a_w_s, a_b_s, a_ln_g, a_ln_b, b_w_pool, b_scale,
              odd_w_in, odd_w_out, c_w_dw, c_b_dw, c_ln_g, c_ln_b, d_w_dw,
              norm_mix_g, norm_ffn_g, ffn_w_gate, ffn_w_up, ffn_w_down, final_norm_g):
    h = x
    for layer in range(DEPTH):
        hn = rmsnorm(h, norm_mix_g[layer])
        if layer % 2 == 0:
            e = layer // 2
            z = hn @ even_w_in[e]
            za = jax.nn.gelu(z[..., :2 * A_WIDTH])
            u, v = za[..., :A_WIDTH], za[..., A_WIDTH:]
            zb = z[..., 2 * A_WIDTH:]
            ya = mixer_a(u, v, a_w_s[e], a_b_s[e], a_ln_g[e], a_ln_b[e])
            yb = mixer_b(zb, b_w_pool[e], b_scale[e])
            h = h + jnp.concatenate([ya, yb], axis=-1) @ even_w_out[e]
        else:
            o = layer // 2
            z = hn @ odd_w_in[o]
            ca = z[..., :C_WIDTH]
            cgt = z[..., C_WIDTH:2 * C_WIDTH]
            off = 2 * C_WIDTH
            dbg = z[..., off:off + D_WIDTH]
            dcg = z[..., off + D_WIDTH:off + 2 * D_WIDTH]
            dxin = z[..., off + 2 * D_WIDTH:]
            yc = mixer_c(ca, cgt, c_w_dw[o], c_b_dw[o], c_ln_g[o], c_ln_b[o])
            yd = mixer_d(dbg, dcg, dxin, d_w_dw[o])
            h = h + jnp.concatenate([yc, yd], axis=-1) @ odd_w_out[o]
        h = h + swiglu(rmsnorm(h, norm_ffn_g[layer]), ffn_w_gate[layer], ffn_w_up[layer],
                       ffn_w_down[layer])
    return rmsnorm(h, final_norm_g)


import jax as _jax
import jax.numpy as _jnp

TWIN_FORMAT = 'train_step'
FWD_PARAMS = ['x', 'even_w_in', 'even_w_out', 'a_w_s', 'a_b_s', 'a_ln_g', 'a_ln_b', 'b_w_pool', 'b_scale', 'odd_w_in', 'odd_w_out', 'c_w_dw', 'c_b_dw', 'c_ln_g', 'c_ln_b', 'd_w_dw', 'norm_mix_g', 'norm_ffn_g', 'ffn_w_gate', 'ffn_w_up', 'ffn_w_down', 'final_norm_g']
TWIN_WEIGHTS = ['even_w_in', 'even_w_out', 'a_w_s', 'a_b_s', 'a_ln_g', 'a_ln_b', 'b_w_pool', 'b_scale', 'odd_w_in', 'odd_w_out', 'c_w_dw', 'c_b_dw', 'c_ln_g', 'c_ln_b', 'd_w_dw', 'norm_mix_g', 'norm_ffn_g', 'ffn_w_gate', 'ffn_w_up', 'ffn_w_down', 'final_norm_g']
TWIN_DIFF_INPUT = 'x'
TWIN_INPUTS = ['x', 'even_w_in', 'even_w_out', 'a_w_s', 'a_b_s', 'a_ln_g', 'a_ln_b', 'b_w_pool', 'b_scale', 'odd_w_in', 'odd_w_out', 'c_w_dw', 'c_b_dw', 'c_ln_g', 'c_ln_b', 'd_w_dw', 'norm_mix_g', 'norm_ffn_g', 'ffn_w_gate', 'ffn_w_up', 'ffn_w_down', 'final_norm_g', 'loss_target', 'm_even_w_in', 'm_even_w_out', 'm_a_w_s', 'm_a_b_s', 'm_a_ln_g', 'm_a_ln_b', 'm_b_w_pool', 'm_b_scale', 'm_odd_w_in', 'm_odd_w_out', 'm_c_w_dw', 'm_c_b_dw', 'm_c_ln_g', 'm_c_ln_b', 'm_d_w_dw', 'm_norm_mix_g', 'm_norm_ffn_g', 'm_ffn_w_gate', 'm_ffn_w_up', 'm_ffn_w_down', 'm_final_norm_g', 'v_even_w_in', 'v_even_w_out', 'v_a_w_s', 'v_a_b_s', 'v_a_ln_g', 'v_a_ln_b', 'v_b_w_pool', 'v_b_scale', 'v_odd_w_in', 'v_odd_w_out', 'v_c_w_dw', 'v_c_b_dw', 'v_c_ln_g', 'v_c_ln_b', 'v_d_w_dw', 'v_norm_mix_g', 'v_norm_ffn_g', 'v_ffn_w_gate', 'v_ffn_w_up', 'v_ffn_w_down', 'v_final_norm_g']
TWIN_OUTPUTS = ['loss', 'grad_x', 'grad_even_w_in', 'grad_even_w_out', 'grad_a_w_s', 'grad_a_b_s', 'grad_a_ln_g', 'grad_a_ln_b', 'grad_b_w_pool', 'grad_b_scale', 'grad_odd_w_in', 'grad_odd_w_out', 'grad_c_w_dw', 'grad_c_b_dw', 'grad_c_ln_g', 'grad_c_ln_b', 'grad_d_w_dw', 'grad_norm_mix_g', 'grad_norm_ffn_g', 'grad_ffn_w_gate', 'grad_ffn_w_up', 'grad_ffn_w_down', 'grad_final_norm_g', 'delta_even_w_in', 'delta_even_w_out', 'delta_a_w_s', 'delta_a_b_s', 'delta_a_ln_g', 'delta_a_ln_b', 'delta_b_w_pool', 'delta_b_scale', 'delta_odd_w_in', 'delta_odd_w_out', 'delta_c_w_dw', 'delta_c_b_dw', 'delta_c_ln_g', 'delta_c_ln_b', 'delta_d_w_dw', 'delta_norm_mix_g', 'delta_norm_ffn_g', 'delta_ffn_w_gate', 'delta_ffn_w_up', 'delta_ffn_w_down', 'delta_final_norm_g', 'new_m_even_w_in', 'new_m_even_w_out', 'new_m_a_w_s', 'new_m_a_b_s', 'new_m_a_ln_g', 'new_m_a_ln_b', 'new_m_b_w_pool', 'new_m_b_scale', 'new_m_odd_w_in', 'new_m_odd_w_out', 'new_m_c_w_dw', 'new_m_c_b_dw', 'new_m_c_ln_g', 'new_m_c_ln_b', 'new_m_d_w_dw', 'new_m_norm_mix_g', 'new_m_norm_ffn_g', 'new_m_ffn_w_gate', 'new_m_ffn_w_up', 'new_m_ffn_w_down', 'new_m_final_norm_g', 'new_v_even_w_in', 'new_v_even_w_out', 'new_v_a_w_s', 'new_v_a_b_s', 'new_v_a_ln_g', 'new_v_a_ln_b', 'new_v_b_w_pool', 'new_v_b_scale', 'new_v_odd_w_in', 'new_v_odd_w_out', 'new_v_c_w_dw', 'new_v_c_b_dw', 'new_v_c_ln_g', 'new_v_c_ln_b', 'new_v_d_w_dw', 'new_v_norm_mix_g', 'new_v_norm_ffn_g', 'new_v_ffn_w_gate', 'new_v_ffn_w_up', 'new_v_ffn_w_down', 'new_v_final_norm_g']
TWIN_LEAF_KINDS = {'loss': 'loss', 'grad_x': 'grad_x', 'grad_even_w_in': 'grad_w', 'grad_even_w_out': 'grad_w', 'grad_a_w_s': 'grad_w', 'grad_a_b_s': 'grad_w', 'grad_a_ln_g': 'grad_w', 'grad_a_ln_b': 'grad_w', 'grad_b_w_pool': 'grad_w', 'grad_b_scale': 'grad_w', 'grad_odd_w_in': 'grad_w', 'grad_odd_w_out': 'grad_w', 'grad_c_w_dw': 'grad_w', 'grad_c_b_dw': 'grad_w', 'grad_c_ln_g': 'grad_w', 'grad_c_ln_b': 'grad_w', 'grad_d_w_dw': 'grad_w', 'grad_norm_mix_g': 'grad_w', 'grad_norm_ffn_g': 'grad_w', 'grad_ffn_w_gate': 'grad_w', 'grad_ffn_w_up': 'grad_w', 'grad_ffn_w_down': 'grad_w', 'grad_final_norm_g': 'grad_w', 'delta_even_w_in': 'delta_w', 'delta_even_w_out': 'delta_w', 'delta_a_w_s': 'delta_w', 'delta_a_b_s': 'delta_w', 'delta_a_ln_g': 'delta_w', 'delta_a_ln_b': 'delta_w', 'delta_b_w_pool': 'delta_w', 'delta_b_scale': 'delta_w', 'delta_odd_w_in': 'delta_w', 'delta_odd_w_out': 'delta_w', 'delta_c_w_dw': 'delta_w', 'delta_c_b_dw': 'delta_w', 'delta_c_ln_g': 'delta_w', 'delta_c_ln_b': 'delta_w', 'delta_d_w_dw': 'delta_w', 'delta_norm_mix_g': 'delta_w', 'delta_norm_ffn_g': 'delta_w', 'delta_ffn_w_gate': 'delta_w', 'delta_ffn_w_up': 'delta_w', 'delta_ffn_w_down': 'delta_w', 'delta_final_norm_g': 'delta_w', 'new_m_even_w_in': 'new_m', 'new_m_even_w_out': 'new_m', 'new_m_a_w_s': 'new_m', 'new_m_a_b_s': 'new_m', 'new_m_a_ln_g': 'new_m', 'new_m_a_ln_b': 'new_m', 'new_m_b_w_pool': 'new_m', 'new_m_b_scale': 'new_m', 'new_m_odd_w_in': 'new_m', 'new_m_odd_w_out': 'new_m', 'new_m_c_w_dw': 'new_m', 'new_m_c_b_dw': 'new_m', 'new_m_c_ln_g': 'new_m', 'new_m_c_ln_b': 'new_m', 'new_m_d_w_dw': 'new_m', 'new_m_norm_mix_g': 'new_m', 'new_m_norm_ffn_g': 'new_m', 'new_m_ffn_w_gate': 'new_m', 'new_m_ffn_w_up': 'new_m', 'new_m_ffn_w_down': 'new_m', 'new_m_final_norm_g': 'new_m', 'new_v_even_w_in': 'new_v', 'new_v_even_w_out': 'new_v', 'new_v_a_w_s': 'new_v', 'new_v_a_b_s': 'new_v', 'new_v_a_ln_g': 'new_v', 'new_v_a_ln_b': 'new_v', 'new_v_b_w_pool': 'new_v', 'new_v_b_scale': 'new_v', 'new_v_odd_w_in': 'new_v', 'new_v_odd_w_out': 'new_v', 'new_v_c_w_dw': 'new_v', 'new_v_c_b_dw': 'new_v', 'new_v_c_ln_g': 'new_v', 'new_v_c_ln_b': 'new_v', 'new_v_d_w_dw': 'new_v', 'new_v_norm_mix_g': 'new_v', 'new_v_norm_ffn_g': 'new_v', 'new_v_ffn_w_gate': 'new_v', 'new_v_ffn_w_up': 'new_v', 'new_v_ffn_w_down': 'new_v', 'new_v_final_norm_g': 'new_v'}


def _forward(args):
    return _fwd_reference(*[args[k] for k in FWD_PARAMS])


def _output_shape():
    def fwd():
        inp = _fwd_setup_inputs(0)
        return _fwd_reference(*[inp[k] for k in FWD_PARAMS])
    out = _jax.eval_shape(fwd)
    return out.shape, out.dtype

N_MICROBATCH = 1
ADAM_LR = 0.001
ADAM_B1 = 0.9
ADAM_B2 = 0.999
ADAM_EPS = 1e-08
ADAM_WD = 0.01
ADAM_STEP = 10
PER_EXAMPLE_BATCH_AXIS = {'x': 0, 'loss_target': 0}
SHARED_INPUTS = []
_WEIGHT_DTYPES = {'even_w_in': _jnp.float32, 'even_w_out': _jnp.float32, 'a_w_s': _jnp.float32, 'a_b_s': _jnp.float32, 'a_ln_g': _jnp.float32, 'a_ln_b': _jnp.float32, 'b_w_pool': _jnp.float32, 'b_scale': _jnp.float32, 'odd_w_in': _jnp.float32, 'odd_w_out': _jnp.float32, 'c_w_dw': _jnp.float32, 'c_b_dw': _jnp.float32, 'c_ln_g': _jnp.float32, 'c_ln_b': _jnp.float32, 'd_w_dw': _jnp.float32, 'norm_mix_g': _jnp.float32, 'norm_ffn_g': _jnp.float32, 'ffn_w_gate': _jnp.float32, 'ffn_w_up': _jnp.float32, 'ffn_w_down': _jnp.float32, 'final_norm_g': _jnp.float32}
MOMENT_SCALE = {'even_w_in': 2.085575e-01, 'even_w_out': 2.239373e-01, 'a_w_s': 1.436801e-01, 'a_b_s': 1.686935e-01, 'a_ln_g': 1.562925e-01, 'a_ln_b': 1.523094e-01, 'b_w_pool': 2.273378e-01, 'b_scale': 2.218098e-01, 'odd_w_in': 1.337649e-01, 'odd_w_out': 1.339037e-01, 'c_w_dw': 9.963387e-02, 'c_b_dw': 2.018031e-01, 'c_ln_g': 1.180851e-01, 'c_ln_b': 1.104076e-01, 'd_w_dw': 1.564180e-01, 'norm_mix_g': 2.337898e-01, 'norm_ffn_g': 1.580247e-01, 'ffn_w_gate': 6.778566e-02, 'ffn_w_up': 6.565296e-02, 'ffn_w_down': 1.088622e-01, 'final_norm_g': 6.411331e+01}


def _to_microbatches(a, axis):
    t = _jnp.moveaxis(a, axis, 0)
    t = t.reshape((N_MICROBATCH, t.shape[0] // N_MICROBATCH) + t.shape[1:])
    return _jnp.moveaxis(t, 1, axis + 1)


def setup_inputs(seed: int = 0) -> dict:
    inp = _fwd_setup_inputs(seed)
    key = _jax.random.fold_in(_jax.random.key(seed), 7919)
    shape, _ = _output_shape()
    out = dict(inp)
    out["loss_target"] = _jax.random.normal(_jax.random.fold_in(key, 0), shape, _jnp.float32)
    for i, name in enumerate(TWIN_WEIGHTS):
        w = inp[name].astype(_jnp.float32)
        if MOMENT_SCALE is None:
            s = _jnp.sqrt(_jnp.mean(_jnp.square(w)) + 1e-30)
        else:
            s = MOMENT_SCALE[name]
        km, kv = _jax.random.split(_jax.random.fold_in(key, i + 1))
        out[name] = w
        out["m_" + name] = s * _jax.random.normal(km, w.shape, _jnp.float32)
        out["v_" + name] = (s * s) * _jax.random.uniform(kv, w.shape, _jnp.float32, 0.5, 1.5)
    if N_MICROBATCH > 1:
        for name, axis in PER_EXAMPLE_BATCH_AXIS.items():
            out[name] = _to_microbatches(out[name], axis)
    return {'x': out['x'], 'even_w_in': out['even_w_in'], 'even_w_out': out['even_w_out'], 'a_w_s': out['a_w_s'], 'a_b_s': out['a_b_s'], 'a_ln_g': out['a_ln_g'], 'a_ln_b': out['a_ln_b'], 'b_w_pool': out['b_w_pool'], 'b_scale': out['b_scale'], 'odd_w_in': out['odd_w_in'], 'odd_w_out': out['odd_w_out'], 'c_w_dw': out['c_w_dw'], 'c_b_dw': out['c_b_dw'], 'c_ln_g': out['c_ln_g'], 'c_ln_b': out['c_ln_b'], 'd_w_dw': out['d_w_dw'], 'norm_mix_g': out['norm_mix_g'], 'norm_ffn_g': out['norm_ffn_g'], 'ffn_w_gate': out['ffn_w_gate'], 'ffn_w_up': out['ffn_w_up'], 'ffn_w_down': out['ffn_w_down'], 'final_norm_g': out['final_norm_g'], 'loss_target': out['loss_target'], 'm_even_w_in': out['m_even_w_in'], 'm_even_w_out': out['m_even_w_out'], 'm_a_w_s': out['m_a_w_s'], 'm_a_b_s': out['m_a_b_s'], 'm_a_ln_g': out['m_a_ln_g'], 'm_a_ln_b': out['m_a_ln_b'], 'm_b_w_pool': out['m_b_w_pool'], 'm_b_scale': out['m_b_scale'], 'm_odd_w_in': out['m_odd_w_in'], 'm_odd_w_out': out['m_odd_w_out'], 'm_c_w_dw': out['m_c_w_dw'], 'm_c_b_dw': out['m_c_b_dw'], 'm_c_ln_g': out['m_c_ln_g'], 'm_c_ln_b': out['m_c_ln_b'], 'm_d_w_dw': out['m_d_w_dw'], 'm_norm_mix_g': out['m_norm_mix_g'], 'm_norm_ffn_g': out['m_norm_ffn_g'], 'm_ffn_w_gate': out['m_ffn_w_gate'], 'm_ffn_w_up': out['m_ffn_w_up'], 'm_ffn_w_down': out['m_ffn_w_down'], 'm_final_norm_g': out['m_final_norm_g'], 'v_even_w_in': out['v_even_w_in'], 'v_even_w_out': out['v_even_w_out'], 'v_a_w_s': out['v_a_w_s'], 'v_a_b_s': out['v_a_b_s'], 'v_a_ln_g': out['v_a_ln_g'], 'v_a_ln_b': out['v_a_ln_b'], 'v_b_w_pool': out['v_b_w_pool'], 'v_b_scale': out['v_b_scale'], 'v_odd_w_in': out['v_odd_w_in'], 'v_odd_w_out': out['v_odd_w_out'], 'v_c_w_dw': out['v_c_w_dw'], 'v_c_b_dw': out['v_c_b_dw'], 'v_c_ln_g': out['v_c_ln_g'], 'v_c_ln_b': out['v_c_ln_b'], 'v_d_w_dw': out['v_d_w_dw'], 'v_norm_mix_g': out['v_norm_mix_g'], 'v_norm_ffn_g': out['v_norm_ffn_g'], 'v_ffn_w_gate': out['v_ffn_w_gate'], 'v_ffn_w_up': out['v_ffn_w_up'], 'v_ffn_w_down': out['v_ffn_w_down'], 'v_final_norm_g': out['v_final_norm_g']}


def _loss(weights, diff, rest, loss_target):
    with _jax.named_scope("forward"):
        args = {**rest, TWIN_DIFF_INPUT: diff, **{k: w.astype(_WEIGHT_DTYPES[k]) for k, w in weights.items()}}
        y = _forward(args)
    with _jax.named_scope("loss_head"):
        err = _jnp.square(y.astype(_jnp.float32) - loss_target)
        return 0.5 * _jnp.sum(_jnp.mean(err, axis=-1)) if err.ndim else 0.5 * err


def _adamw(w, g, m, v):
    m = ADAM_B1 * m + (1.0 - ADAM_B1) * g
    v = ADAM_B2 * v + (1.0 - ADAM_B2) * _jnp.square(g)
    m_hat = m / (1.0 - ADAM_B1 ** ADAM_STEP)
    v_hat = v / (1.0 - ADAM_B2 ** ADAM_STEP)
    delta = -ADAM_LR * (m_hat / (_jnp.sqrt(v_hat) + ADAM_EPS) + ADAM_WD * w)
    return delta, m, v


def reference(x, even_w_in, even_w_out, a_w_s, a_b_s, a_ln_g, a_ln_b, b_w_pool, b_scale, odd_w_in, odd_w_out, c_w_dw, c_b_dw, c_ln_g, c_ln_b, d_w_dw, norm_mix_g, norm_ffn_g, ffn_w_gate, ffn_w_up, ffn_w_down, final_norm_g, loss_target, m_even_w_in, m_even_w_out, m_a_w_s, m_a_b_s, m_a_ln_g, m_a_ln_b, m_b_w_pool, m_b_scale, m_odd_w_in, m_odd_w_out, m_c_w_dw, m_c_b_dw, m_c_ln_g, m_c_ln_b, m_d_w_dw, m_norm_mix_g, m_norm_ffn_g, m_ffn_w_gate, m_ffn_w_up, m_ffn_w_down, m_final_norm_g, v_even_w_in, v_even_w_out, v_a_w_s, v_a_b_s, v_a_ln_g, v_a_ln_b, v_b_w_pool, v_b_scale, v_odd_w_in, v_odd_w_out, v_c_w_dw, v_c_b_dw, v_c_ln_g, v_c_ln_b, v_d_w_dw, v_norm_mix_g, v_norm_ffn_g, v_ffn_w_gate, v_ffn_w_up, v_ffn_w_down, v_final_norm_g):
    given = dict(x=x, even_w_in=even_w_in, even_w_out=even_w_out, a_w_s=a_w_s, a_b_s=a_b_s, a_ln_g=a_ln_g, a_ln_b=a_ln_b, b_w_pool=b_w_pool, b_scale=b_scale, odd_w_in=odd_w_in, odd_w_out=odd_w_out, c_w_dw=c_w_dw, c_b_dw=c_b_dw, c_ln_g=c_ln_g, c_ln_b=c_ln_b, d_w_dw=d_w_dw, norm_mix_g=norm_mix_g, norm_ffn_g=norm_ffn_g, ffn_w_gate=ffn_w_gate, ffn_w_up=ffn_w_up, ffn_w_down=ffn_w_down, final_norm_g=final_norm_g, loss_target=loss_target, m_even_w_in=m_even_w_in, m_even_w_out=m_even_w_out, m_a_w_s=m_a_w_s, m_a_b_s=m_a_b_s, m_a_ln_g=m_a_ln_g, m_a_ln_b=m_a_ln_b, m_b_w_pool=m_b_w_pool, m_b_scale=m_b_scale, m_odd_w_in=m_odd_w_in, m_odd_w_out=m_odd_w_out, m_c_w_dw=m_c_w_dw, m_c_b_dw=m_c_b_dw, m_c_ln_g=m_c_ln_g, m_c_ln_b=m_c_ln_b, m_d_w_dw=m_d_w_dw, m_norm_mix_g=m_norm_mix_g, m_norm_ffn_g=m_norm_ffn_g, m_ffn_w_gate=m_ffn_w_gate, m_ffn_w_up=m_ffn_w_up, m_ffn_w_down=m_ffn_w_down, m_final_norm_g=m_final_norm_g, v_even_w_in=v_even_w_in, v_even_w_out=v_even_w_out, v_a_w_s=v_a_w_s, v_a_b_s=v_a_b_s, v_a_ln_g=v_a_ln_g, v_a_ln_b=v_a_ln_b, v_b_w_pool=v_b_w_pool, v_b_scale=v_b_scale, v_odd_w_in=v_odd_w_in, v_odd_w_out=v_odd_w_out, v_c_w_dw=v_c_w_dw, v_c_b_dw=v_c_b_dw, v_c_ln_g=v_c_ln_g, v_c_ln_b=v_c_ln_b, v_d_w_dw=v_d_w_dw, v_norm_mix_g=v_norm_mix_g, v_norm_ffn_g=v_norm_ffn_g, v_ffn_w_gate=v_ffn_w_gate, v_ffn_w_up=v_ffn_w_up, v_ffn_w_down=v_ffn_w_down, v_final_norm_g=v_final_norm_g)
    weights = {n: given[n] for n in TWIN_WEIGHTS}
    shared = {n: given[n] for n in SHARED_INPUTS}
    per_example = {n: given[n] for n in ['x']}
    grad_fn = _jax.value_and_grad(_loss, argnums=(0, 1))

    def one_microbatch(ex, loss_target):
        ex = dict(ex)
        diff = ex.pop(TWIN_DIFF_INPUT)
        return grad_fn(weights, diff, {**shared, **ex}, loss_target)

    if N_MICROBATCH == 1:
        loss, (grad_w, grad_x) = one_microbatch(per_example, given["loss_target"])
    else:
        def body(carry, xs):
            loss_sum, grad_sum = carry
            l_k, (gw_k, gx_k) = one_microbatch(xs[0], xs[1])
            with _jax.named_scope("update"):
                return (loss_sum + l_k, _jax.tree.map(_jnp.add, grad_sum, gw_k)), gx_k

        init = (_jnp.zeros((), _jnp.float32), _jax.tree.map(_jnp.zeros_like, weights))
        (loss, grad_w), grad_x = _jax.lax.scan(body, init, (per_example, given["loss_target"]))
    with _jax.named_scope("update"):
        delta_w, new_m, new_v = {}, {}, {}
        for n in TWIN_WEIGHTS:
            delta_w[n], new_m[n], new_v[n] = _adamw(weights[n], grad_w[n], given["m_" + n], given["v_" + n])
    return (loss, grad_x, *[grad_w[n] for n in TWIN_WEIGHTS], *[delta_w[n] for n in TWIN_WEIGHTS],
            *[new_m[n] for n in TWIN_WEIGHTS], *[new_v[n] for n in TWIN_WEIGHTS])
```

```python
import jax
import jax.numpy as jnp
from jax import lax
from jax.experimental import pallas as pl
from jax.experimental.pallas import tpu as pltpu

f32 = jnp.float32
bf16 = jnp.bfloat16

EPS = 1e-6
D_MODEL = 1024
A_WIDTH = 512
HEAD = 128
N_HEADS = 4
CHUNK = 64
POOL_WINDOWS = (2, 4, 8, 16)
POOL_HALO = 16
C_KERNEL = 31
D_KERNEL = 3
CONV_HALO = 32
D_FF = 2816
N_DEV = 8
N_CHIP = 4

ADAM_LR = 0.001
ADAM_B1 = 0.9
ADAM_B2 = 0.999
ADAM_EPS = 1e-08
ADAM_WD = 0.01
ADAM_STEP = 10

MIX_TILE = 512
FFN_TILE = 512
FFN_CHUNK = 256
DW_TK = 512
DW_TN = 512
VMEM_LIMIT = 56 * 1024 * 1024

MESH = pl.DeviceIdType.MESH
ANY = pl.BlockSpec(memory_space=pl.ANY)


def _params(n_axes):
    return pltpu.CompilerParams(dimension_semantics=("arbitrary",) * n_axes, vmem_limit_bytes=VMEM_LIMIT)


def _mm(a, b):
    return jnp.dot(a, b, preferred_element_type=f32)


def _mm_nt(a, b):
    return lax.dot_general(a, b, (((1,), (1,)), ((), ())), preferred_element_type=f32)


def _mm_tn(a, b):
    return lax.dot_general(a, b, (((0,), (0,)), ((), ())), preferred_element_type=f32)


def _sigmoid(x):
    return 1.0 / (1.0 + jnp.exp(-x))


def _rms_r(h):
    return lax.rsqrt(jnp.mean(h * h, axis=-1, keepdims=True) + EPS)


def _rms_bwd(dy, h, g):
    r = _rms_r(h)
    xh = h * r
    dxh = dy * g
    dh = r * (dxh - xh * jnp.mean(dxh * xh, axis=-1, keepdims=True))
    return dh, jnp.sum(dy * xh, axis=0, keepdims=True)


def _ln_fwd(x, g, b):
    mu = jnp.mean(x, axis=-1, keepdims=True)
    xc = x - mu
    r = lax.rsqrt(jnp.mean(xc * xc, axis=-1, keepdims=True) + EPS)
    xh = xc * r
    return xh * g + b, xh, r


def _ln_bwd(dy, xh, r, g):
    dxh = dy * g
    return r * (dxh - jnp.mean(dxh, axis=-1, keepdims=True) - xh * jnp.mean(dxh * xh, axis=-1, keepdims=True))


_GELU_C = 0.7978845608028654
_GELU_A = 0.044715


def _gelu(x):
    th = jnp.tanh(_GELU_C * (x + _GELU_A * x * x * x))
    return 0.5 * x * (1.0 + th), th


def _gelu_grad(x, th):
    return 0.5 * (1.0 + th) + 0.5 * x * (1.0 - th * th) * (_GELU_C * (1.0 + 3.0 * _GELU_A * x * x))


def _down(x, k):
    return x if k == 0 else pltpu.roll(x, k, 0)


def _up(x, k):
    return x if k == 0 else pltpu.roll(x, x.shape[0] - k, 0)


def _window_sum(x, win, shift):
    s = x
    step = 1
    while step < win:
        s = s + shift(s, step)
        step *= 2
    return s


def _inv_count(t0, rows, win):
    t = t0 + lax.broadcasted_iota(jnp.int32, (rows, 1), 0)
    return 1.0 / jnp.minimum(t + 1, win).astype(f32)


def _chunk_mask():
    i = lax.broadcasted_iota(jnp.int32, (HEAD, HEAD), 0)
    j = lax.broadcasted_iota(jnp.int32, (HEAD, HEAD), 1)
    return jnp.logical_or(i >= CHUNK, j < CHUNK)


def _const(shape, n_axes):
    zeros = (0,) * len(shape)
    if n_axes == 1:
        return pl.BlockSpec(shape, lambda i: zeros)
    return pl.BlockSpec(shape, lambda i, j: zeros)


def _prev_halo(tile, halo, cols):
    return pl.BlockSpec((halo, cols), lambda i: (jnp.maximum(i * (tile // halo) - 1, 0), 0))


def _next_halo(tile, halo, cols, seq):
    return pl.BlockSpec((halo, cols), lambda i: (jnp.minimum((i + 1) * (tile // halo), seq // halo - 1), 0))


def _gmlp_gate(vnb, wsm, bst, tile):
    rows = []
    for n in range(tile // HEAD):
        cols = []
        for hh in range(N_HEADS):
            blk = vnb[n * HEAD:(n + 1) * HEAD, hh * HEAD:(hh + 1) * HEAD]
            cols.append(_mm(wsm[hh], blk) + bst[:, hh:hh + 1])
        rows.append(jnp.concatenate(cols, axis=1))
    return jnp.concatenate(rows, axis=0)


def _even_fwd(h, w_in, w_out, ws, bst, lng, lnb, wp, sc, gm):
    seq = h.shape[0]
    tile = min(MIX_TILE, seq)
    n_tiles = seq // tile

    def body(h_ref, hp_ref, win_ref, wout_ref, ws_ref, bst_ref, lng_ref, lnb_ref, wp_ref, sc_ref, g_ref,
             ho_ref, hn_ref, za_ref, pool_ref, mix_ref):
        i = pl.program_id(0)
        g = g_ref[...]
        h = h_ref[...]
        hnb = (h * _rms_r(h) * g).astype(bf16)
        hn_ref[...] = hnb
        z = _mm(hnb, win_ref[...])
        zab = z[:, :2 * A_WIDTH].astype(bf16)
        za_ref[...] = zab
        hp = hp_ref[...]
        zbp = _mm((hp * _rms_r(hp) * g).astype(bf16), win_ref[:, 2 * A_WIDTH:])
        zbe = jnp.concatenate([jnp.where(i > 0, zbp, 0.0), z[:, 2 * A_WIDTH:]], axis=0)
        pooled = []
        for gi, win in enumerate(POOL_WINDOWS):
            xg = zbe[:, gi * HEAD:(gi + 1) * HEAD]
            s = _window_sum(xg, win, _down)
            pooled.append(s[POOL_HALO:] * _inv_count(i * tile, tile, win) - xg[POOL_HALO:])
        plb = jnp.concatenate(pooled, axis=1).astype(bf16)
        pool_ref[...] = plb

        ga, _ = _gelu(zab.astype(f32))
        vn, _, _ = _ln_fwd(ga[:, A_WIDTH:], lng_ref[...], lnb_ref[...])
        mask = _chunk_mask()
        wsm = [jnp.where(mask, ws_ref[hh], 0.0).astype(bf16) for hh in range(N_HEADS)]
        ya = ga[:, :A_WIDTH] * _gmlp_gate(vn.astype(bf16), wsm, bst_ref[...], tile)
        yb = jnp.concatenate([_mm(plb[:, gi * HEAD:(gi + 1) * HEAD], wp_ref[gi].astype(bf16))
                              for gi in range(len(POOL_WINDOWS))], axis=1) * sc_ref[...]
        mix = jnp.concatenate([ya, yb], axis=1).astype(bf16)
        mix_ref[...] = mix
        ho_ref[...] = h + _mm(mix, wout_ref[...])

    row = lambda cols: pl.BlockSpec((tile, cols), lambda i: (i, 0))
    return pl.pallas_call(
        body, name="even_fwd", grid=(n_tiles,),
        in_specs=[row(D_MODEL), _prev_halo(tile, POOL_HALO, D_MODEL), _const(w_in.shape, 1), _const(w_out.shape, 1),
                  _const(ws.shape, 1), _const(bst.shape, 1), _const(lng.shape, 1), _const(lnb.shape, 1),
                  _const(wp.shape, 1), _const(sc.shape, 1), _const(gm.shape, 1)],
        out_specs=[row(D_MODEL), row(D_MODEL), row(2 * A_WIDTH), row(A_WIDTH), row(D_MODEL)],
        out_shape=[jax.ShapeDtypeStruct((seq, D_MODEL), f32), jax.ShapeDtypeStruct((seq, D_MODEL), bf16),
                   jax.ShapeDtypeStruct((seq, 2 * A_WIDTH), bf16), jax.ShapeDtypeStruct((seq, A_WIDTH), bf16),
                   jax.ShapeDtypeStruct((seq, D_MODEL), bf16)],
        compiler_params=_params(1),
    )(h, h, w_in, w_out, ws, bst, lng, lnb, wp, sc, gm)


def _even_bwd(dh, h, za, pooled, w_in, w_out, ws, bst, lng, lnb, wp, sc, gm):
    seq = h.shape[0]
    tile = min(MIX_TILE, seq)
    n_tiles = seq // tile
    n_groups = len(POOL_WINDOWS)

    def body(dh_ref, dhx_ref, h_ref, za_ref, pool_ref, win_ref, wout_ref, ws_ref, bst_ref, lng_ref, lnb_ref,
             wp_ref, sc_ref, g_ref,
             dhi_ref, dz_ref, dws_ref, dbs_ref, dlng_ref, dlnb_ref, dwp_ref, dsc_ref, dg_ref):
        i = pl.program_id(0)

        @pl.when(i == 0)
        def _():
            for ref in (dws_ref, dbs_ref, dlng_ref, dlnb_ref, dwp_ref, dsc_ref, dg_ref):
                ref[...] = jnp.zeros_like(ref)

        dh = dh_ref[...]
        dmix = _mm_nt(dh.astype(bf16), wout_ref[...])
        dya = dmix[:, :A_WIDTH]
        dyb = dmix[:, A_WIDTH:]
        dybx = _mm_nt(dhx_ref[...].astype(bf16), wout_ref[A_WIDTH:, :])
        dybx = jnp.where(i < n_tiles - 1, dybx, 0.0)

        za = za_ref[...].astype(f32)
        ga, th = _gelu(za)
        u = ga[:, :A_WIDTH]
        lng = lng_ref[...]
        vn, vh, r = _ln_fwd(ga[:, A_WIDTH:], lng, lnb_ref[...])
        vnb = vn.astype(bf16)
        mask = _chunk_mask()
        wsf = [jnp.where(mask, ws_ref[hh], 0.0) for hh in range(N_HEADS)]
        sv = _gmlp_gate(vnb, [w.astype(bf16) for w in wsf], bst_ref[...], tile)
        du = dya * sv
        dsvb = (dya * u).astype(bf16)
        wst = [w.T.astype(bf16) for w in wsf]
        ones = jnp.ones((8, HEAD), bf16)
        dws = [jnp.zeros((HEAD, HEAD), f32) for _ in range(N_HEADS)]
        dbs = [jnp.zeros((8, HEAD), f32) for _ in range(N_HEADS)]
        rows = []
        for n in range(tile // HEAD):
            cols = []
            for hh in range(N_HEADS):
                blk = dsvb[n * HEAD:(n + 1) * HEAD, hh * HEAD:(hh + 1) * HEAD]
                cols.append(_mm(wst[hh], blk))
                dws[hh] = dws[hh] + _mm_nt(blk, vnb[n * HEAD:(n + 1) * HEAD, hh * HEAD:(hh + 1) * HEAD])
                dbs[hh] = dbs[hh] + _mm_nt(ones, blk)
            rows.append(jnp.concatenate(cols, axis=1))
        dvn = jnp.concatenate(rows, axis=0)
        for hh in range(N_HEADS):
            dws_ref[hh] += jnp.where(mask, dws[hh], 0.0)
            dbs_ref[pl.ds(hh, 1), :] += dbs[hh][0:1, :]
        dlng_ref[...] += jnp.sum(dvn * vh, axis=0, keepdims=True)
        dlnb_ref[...] += jnp.sum(dvn, axis=0, keepdims=True)
        dv = _ln_bwd(dvn, vh, r, lng)
        dza = jnp.concatenate([du, dv], axis=1) * _gelu_grad(za, th)

        plb = pool_ref[...]
        sc = sc_ref[...]
        dzb = []
        dsc = []
        for gi, win in enumerate(POOL_WINDOWS):
            cs = slice(gi * HEAD, (gi + 1) * HEAD)
            wpb = wp_ref[gi].astype(bf16)
            dsc.append(jnp.sum(dyb[:, cs] * _mm(plb[:, cs], wpb), axis=0, keepdims=True))
            dpre = (dyb[:, cs] * sc[:, cs]).astype(bf16)
            dprex = (dybx[:, cs] * sc[:, cs]).astype(bf16)
            dwp_ref[gi] += _mm_tn(plb[:, cs], dpre)
            dpl = _mm_nt(dpre, wpb)
            dple = jnp.concatenate([dpl, _mm_nt(dprex, wpb)], axis=0)
            q = dple * _inv_count(i * tile, tile + POOL_HALO, win)
            dzb.append(_window_sum(q, win, _up)[:tile] - dpl)
        dsc_ref[...] += jnp.concatenate(dsc, axis=1)

        dzf = jnp.concatenate([dza] + dzb, axis=1).astype(bf16)
        dz_ref[...] = dzf
        dhn = _mm_nt(dzf, win_ref[...])
        dhr, dg = _rms_bwd(dhn, h_ref[...], g_ref[...])
        dhi_ref[...] = dh + dhr
        dg_ref[...] += dg

    row = lambda cols: pl.BlockSpec((tile, cols), lambda i: (i, 0))
    small = [ws.shape, (N_HEADS, HEAD), lng.shape, lnb.shape, wp.shape, sc.shape, gm.shape]
    return pl.pallas_call(
        body, name="even_bwd", grid=(n_tiles,),
        in_specs=[row(D_MODEL), _next_halo(tile, POOL_HALO, D_MODEL, seq), row(D_MODEL), row(2 * A_WIDTH), row(A_WIDTH),
                  _const(w_in.shape, 1), _const(w_out.shape, 1), _const(ws.shape, 1), _const(bst.shape, 1),
                  _const(lng.shape, 1), _const(lnb.shape, 1), _const(wp.shape, 1), _const(sc.shape, 1), _const(gm.shape, 1)],
        out_specs=[row(D_MODEL), row(3 * A_WIDTH)] + [_const(s, 1) for s in small],
        out_shape=[jax.ShapeDtypeStruct((seq, D_MODEL), f32), jax.ShapeDtypeStruct((seq, 3 * A_WIDTH), bf16)]
                  + [jax.ShapeDtypeStruct(s, f32) for s in small],
        compiler_params=_params(1),
    )(dh, dh, h, za, pooled, w_in, w_out, ws, bst, lng, lnb, wp, sc, gm)


def _conv_taps(x, w_ref, n_taps, halo, rows):
    acc = None
    for j in range(n_taps):
        term = w_ref[pl.ds(j, 1), :] * _down(x, n_taps - 1 - j)[halo:halo + rows]
        acc = term if acc is None else acc + term
    return acc


def _odd_fwd(h, w_in, w_out, cw, cb, clg, clb, dw, gm):
    seq = h.shape[0]
    tile = min(MIX_TILE, seq)
    n_tiles = seq // tile
    w = A_WIDTH

    def body(h_ref, hp_ref, win_ref, wout_ref, cw_ref, cb_ref, clg_ref, clb_ref, dw_ref, g_ref,
             ho_ref, hn_ref, z_ref, mix_ref):
        i = pl.program_id(0)
        g = g_ref[...]
        h = h_ref[...]
        hnb = (h * _rms_r(h) * g).astype(bf16)
        hn_ref[...] = hnb
        zb = _mm(hnb, win_ref[...]).astype(bf16)
        z_ref[...] = zb
        hp = hp_ref[...]
        zp = _mm((hp * _rms_r(hp) * g).astype(bf16), win_ref[...]).astype(bf16).astype(f32)
        z = zb.astype(f32)
        ze = jnp.concatenate([jnp.where(i > 0, zp, 0.0), z], axis=0)
        hc = ze[:, :w] * _sigmoid(ze[:, w:2 * w])
        cv = _conv_taps(hc, cw_ref, C_KERNEL, CONV_HALO, tile) + cb_ref[...]
        ln, _, _ = _ln_fwd(cv, clg_ref[...], clb_ref[...])
        yc = ln * _sigmoid(ln)
        p = ze[:, 3 * w:4 * w] * ze[:, 4 * w:]
        yd = z[:, 2 * w:3 * w] * _conv_taps(p, dw_ref, D_KERNEL, CONV_HALO, tile)
        mix = jnp.concatenate([yc, yd], axis=1).astype(bf16)
        mix_ref[...] = mix
        ho_ref[...] = h + _mm(mix, wout_ref[...])

    row = lambda cols: pl.BlockSpec((tile, cols), lambda i: (i, 0))
    return pl.pallas_call(
        body, name="odd_fwd", grid=(n_tiles,),
        in_specs=[row(D_MODEL), _prev_halo(tile, CONV_HALO, D_MODEL), _const(w_in.shape, 1), _const(w_out.shape, 1),
                  _const(cw.shape, 1), _const(cb.shape, 1), _const(clg.shape, 1), _const(clb.shape, 1),
                  _const(dw.shape, 1), _const(gm.shape, 1)],
        out_specs=[row(D_MODEL), row(D_MODEL), row(5 * w), row(D_MODEL)],
        out_shape=[jax.ShapeDtypeStruct((seq, D_MODEL), f32), jax.ShapeDtypeStruct((seq, D_MODEL), bf16),
                   jax.ShapeDtypeStruct((seq, 5 * w), bf16), jax.ShapeDtypeStruct((seq, D_MODEL), bf16)],
        compiler_params=_params(1),
    )(h, h, w_in, w_out, cw, cb, clg, clb, dw, gm)


def _odd_bwd(dh, h, z, w_in, w_out, cw, cb, clg, clb, dw, gm):
    seq = h.shape[0]
    tile = min(MIX_TILE, seq)
    n_tiles = seq // tile
    w = A_WIDTH
    halo = CONV_HALO
    ext = tile + halo

    def body(dh_ref, dhx_ref, h_ref, z_ref, zp_ref, zx_ref, win_ref, wout_ref, cw_ref, cb_ref, clg_ref, clb_ref,
             dw_ref, g_ref,
             dhi_ref, dz_ref, dcw_ref, dcb_ref, dclg_ref, dclb_ref, ddw_ref, dg_ref):
        i = pl.program_id(0)

        @pl.when(i == 0)
        def _():
            for ref in (dcw_ref, dcb_ref, dclg_ref, dclb_ref, ddw_ref, dg_ref):
                ref[...] = jnp.zeros_like(ref)

        dh = dh_ref[...]
        dhe = jnp.concatenate([dh, jnp.where(i < n_tiles - 1, dhx_ref[...], 0.0)], axis=0)
        dmix = _mm_nt(dhe.astype(bf16), wout_ref[...])
        ze = jnp.concatenate([jnp.where(i > 0, zp_ref[...].astype(f32), 0.0), z_ref[...].astype(f32),
                              zx_ref[...].astype(f32)], axis=0)

        sg = _sigmoid(ze[:, w:2 * w])
        ca = ze[:, :w]
        hc = ca * sg
        cv = _conv_taps(hc, cw_ref, C_KERNEL, halo, ext) + cb_ref[...]
        clg = clg_ref[...]
        ln, xh, r = _ln_fwd(cv, clg, clb_ref[...])
        sl = _sigmoid(ln)
        dln = dmix[:, :w] * (sl * (1.0 + ln * (1.0 - sl)))
        dclg_ref[...] += jnp.sum((dln * xh)[:tile], axis=0, keepdims=True)
        dclb_ref[...] += jnp.sum(dln[:tile], axis=0, keepdims=True)
        dcv = _ln_bwd(dln, xh, r, clg)
        dcb_ref[...] += jnp.sum(dcv[:tile], axis=0, keepdims=True)
        dhc = None
        for j in range(C_KERNEL):
            k = C_KERNEL - 1 - j
            dcw_ref[pl.ds(j, 1), :] += jnp.sum(dcv[:tile] * _down(hc, k)[halo:halo + tile], axis=0, keepdims=True)
            term = cw_ref[pl.ds(j, 1), :] * _up(dcv, k)[:tile]
            dhc = term if dhc is None else dhc + term
        sgt = sg[halo:halo + tile]
        cat = ca[halo:halo + tile]
        dca = dhc * sgt
        dcg = dhc * cat * sgt * (1.0 - sgt)

        dcgv = ze[:, 3 * w:4 * w]
        dxin = ze[:, 4 * w:]
        p = dcgv * dxin
        q = _conv_taps(p, dw_ref, D_KERNEL, halo, tile)
        dyd = dmix[:, w:]
        dq = dyd * ze[halo:, 2 * w:3 * w]
        ddbg = dyd[:tile] * q
        dp = None
        for j in range(D_KERNEL):
            k = D_KERNEL - 1 - j
            ddw_ref[pl.ds(j, 1), :] += jnp.sum(dq[:tile] * _down(p, k)[halo:halo + tile], axis=0, keepdims=True)
            term = dw_ref[pl.ds(j, 1), :] * _up(dq, k)[:tile]
            dp = term if dp is None else dp + term
        ddcg = dp * dxin[halo:halo + tile]
        ddxin = dp * dcgv[halo:halo + tile]

        dzf = jnp.concatenate([dca, dcg, ddbg, ddcg, ddxin], axis=1).astype(bf16)
        dz_ref[...] = dzf
        dhn = _mm_nt(dzf, win_ref[...])
        dhr, dg = _rms_bwd(dhn, h_ref[...], g_ref[...])
        dhi_ref[...] = dh + dhr
        dg_ref[...] += dg

    row = lambda cols: pl.BlockSpec((tile, cols), lambda i: (i, 0))
    small = [cw.shape, cb.shape, clg.shape, clb.shape, dw.shape, gm.shape]
    return pl.pallas_call(
        body, name="odd_bwd", grid=(n_tiles,),
        in_specs=[row(D_MODEL), _next_halo(tile, halo, D_MODEL, seq), row(D_MODEL), row(5 * w),
                  _prev_halo(tile, halo, 5 * w), _next_halo(tile, halo, 5 * w, seq),
                  _const(w_in.shape, 1), _const(w_out.shape, 1), _const(cw.shape, 1), _const(cb.shape, 1),
                  _const(clg.shape, 1), _const(clb.shape, 1), _const(dw.shape, 1), _const(gm.shape, 1)],
        out_specs=[row(D_MODEL), row(5 * w)] + [_const(s, 1) for s in small],
        out_shape=[jax.ShapeDtypeStruct((seq, D_MODEL), f32), jax.ShapeDtypeStruct((seq, 5 * w), bf16)]
                  + [jax.ShapeDtypeStruct(s, f32) for s in small],
        compiler_params=_params(1),
    )(dh, dh, h, z, z, z, w_in, w_out, cw, cb, clg, clb, dw, gm)


def _ffn_fwd(h, wg, wu, wd, gm):
    seq = h.shape[0]
    tile = min(FFN_TILE, seq)
    n_tiles = seq // tile
    n_chunks = D_FF // FFN_CHUNK

    def body(h_ref, g_ref, wg_ref, wu_ref, wd_ref, ho_ref, hn_ref, gate_ref, up_ref, acc_ref, hns_ref):
        j = pl.program_id(1)

        @pl.when(j == 0)
        def _():
            h = h_ref[...]
            hnb = (h * _rms_r(h) * g_ref[...]).astype(bf16)
            hns_ref[...] = hnb
            hn_ref[...] = hnb
            acc_ref[...] = jnp.zeros_like(acc_ref)

        hnb = hns_ref[...]
        gb = _mm(hnb, wg_ref[...]).astype(bf16)
        ub = _mm(hnb, wu_ref[...]).astype(bf16)
        gate_ref[...] = gb
        up_ref[...] = ub
        gf = gb.astype(f32)
        act = gf * _sigmoid(gf) * ub.astype(f32)
        acc_ref[...] += _mm(act.astype(bf16), wd_ref[...])

        @pl.when(j == n_chunks - 1)
        def _():
            ho_ref[...] = h_ref[...] + acc_ref[...]

    row = pl.BlockSpec((tile, D_MODEL), lambda i, j: (i, 0))
    col = pl.BlockSpec((tile, FFN_CHUNK), lambda i, j: (i, j))
    return pl.pallas_call(
        body, name="ffn_fwd", grid=(n_tiles, n_chunks),
        in_specs=[row, _const(gm.shape, 2), pl.BlockSpec((D_MODEL, FFN_CHUNK), lambda i, j: (0, j)),
                  pl.BlockSpec((D_MODEL, FFN_CHUNK), lambda i, j: (0, j)),
                  pl.BlockSpec((FFN_CHUNK, D_MODEL), lambda i, j: (j, 0))],
        out_specs=[row, row, col, col],
        out_shape=[jax.ShapeDtypeStruct((seq, D_MODEL), f32), jax.ShapeDtypeStruct((seq, D_MODEL), bf16),
                   jax.ShapeDtypeStruct((seq, D_FF), bf16), jax.ShapeDtypeStruct((seq, D_FF), bf16)],
        scratch_shapes=[pltpu.VMEM((tile, D_MODEL), f32), pltpu.VMEM((tile, D_MODEL), bf16)],
        compiler_params=_params(2),
    )(h, gm, wg, wu, wd)


def _ffn_bwd(dh, h, gate, up, wg, wu, wd, gm):
    seq = h.shape[0]
    tile = min(FFN_TILE, seq)
    n_tiles = seq // tile
    n_chunks = D_FF // FFN_CHUNK

    def body(dh_ref, h_ref, g_ref, gate_ref, up_ref, wg_ref, wu_ref, wd_ref,
             dhi_ref, dgate_ref, dup_ref, act_ref, dg_ref, acc_ref, dhb_ref):
        i = pl.program_id(0)
        j = pl.program_id(1)

        @pl.when(jnp.logical_and(i == 0, j == 0))
        def _():
            dg_ref[...] = jnp.zeros_like(dg_ref)

        @pl.when(j == 0)
        def _():
            dhb_ref[...] = dh_ref[...].astype(bf16)
            acc_ref[...] = jnp.zeros_like(acc_ref)

        dact = _mm_nt(dhb_ref[...], wd_ref[...])
        gf = gate_ref[...].astype(f32)
        uf = up_ref[...].astype(f32)
        s = _sigmoid(gf)
        silu = gf * s
        act_ref[...] = (silu * uf).astype(bf16)
        dgb = (dact * uf * (s * (1.0 + gf * (1.0 - s)))).astype(bf16)
        dub = (dact * silu).astype(bf16)
        dgate_ref[...] = dgb
        dup_ref[...] = dub
        acc_ref[...] += _mm_nt(dgb, wg_ref[...]) + _mm_nt(dub, wu_ref[...])

        @pl.when(j == n_chunks - 1)
        def _():
            dhr, dg = _rms_bwd(acc_ref[...], h_ref[...], g_ref[...])
            dhi_ref[...] = dh_ref[...] + dhr
            dg_ref[...] += dg

    row = pl.BlockSpec((tile, D_MODEL), lambda i, j: (i, 0))
    col = pl.BlockSpec((tile, FFN_CHUNK), lambda i, j: (i, j))
    return pl.pallas_call(
        body, name="ffn_bwd", grid=(n_tiles, n_chunks),
        in_specs=[row, row, _const(gm.shape, 2), col, col,
                  pl.BlockSpec((D_MODEL, FFN_CHUNK), lambda i, j: (0, j)),
                  pl.BlockSpec((D_MODEL, FFN_CHUNK), lambda i, j: (0, j)),
                  pl.BlockSpec((FFN_CHUNK, D_MODEL), lambda i, j: (j, 0))],
        out_specs=[row, col, col, col, _const(gm.shape, 2)],
        out_shape=[jax.ShapeDtypeStruct((seq, D_MODEL), f32), jax.ShapeDtypeStruct((seq, D_FF), bf16),
                   jax.ShapeDtypeStruct((seq, D_FF), bf16), jax.ShapeDtypeStruct((seq, D_FF), bf16),
                   jax.ShapeDtypeStruct(gm.shape, f32)],
        scratch_shapes=[pltpu.VMEM((tile, D_MODEL), f32), pltpu.VMEM((tile, D_MODEL), bf16)],
        compiler_params=_params(2),
    )(dh, h, gm, gate, up, wg, wu, wd)


def _loss_head(h, target, gf):
    seq = h.shape[0]
    tile = min(MIX_TILE, seq)

    def body(h_ref, t_ref, g_ref, dh_ref, loss_ref, dg_ref):
        @pl.when(pl.program_id(0) == 0)
        def _():
            loss_ref[...] = jnp.zeros_like(loss_ref)
            dg_ref[...] = jnp.zeros_like(dg_ref)

        h = h_ref[...]
        g = g_ref[...]
        err = h * _rms_r(h) * g - t_ref[...]
        loss_ref[...] += (0.5 / D_MODEL) * jnp.sum(jnp.sum(err * err, axis=1, keepdims=True), axis=0, keepdims=True)
        dhr, dg = _rms_bwd(err * (1.0 / D_MODEL), h, g)
        dh_ref[...] = dhr
        dg_ref[...] += dg

    row = pl.BlockSpec((tile, D_MODEL), lambda i: (i, 0))
    return pl.pallas_call(
        body, name="loss_head", grid=(seq // tile,),
        in_specs=[row, row, _const(gf.shape, 1)],
        out_specs=[row, _const((1, 1), 1), _const(gf.shape, 1)],
        out_shape=[jax.ShapeDtypeStruct((seq, D_MODEL), f32), jax.ShapeDtypeStruct((1, 1), f32),
                   jax.ShapeDtypeStruct(gf.shape, f32)],
        compiler_params=_params(1),
    )(h, target, gf)


def _weight_grad(a, b, name):
    seq, m = a.shape
    n = b.shape[1]
    tk = min(DW_TK, seq)
    tn = max(t for t in range(HEAD, DW_TN + 1, HEAD) if n % t == 0)

    def body(a_ref, b_ref, o_ref):
        @pl.when(pl.program_id(1) == 0)
        def _():
            o_ref[...] = jnp.zeros_like(o_ref)

        o_ref[...] += _mm_tn(a_ref[...].astype(bf16), b_ref[...].astype(bf16))

    return pl.pallas_call(
        body, name=name, grid=(n // tn, seq // tk),
        in_specs=[pl.BlockSpec((tk, m), lambda j, k: (k, 0)), pl.BlockSpec((tk, tn), lambda j, k: (k, j))],
        out_specs=pl.BlockSpec((m, tn), lambda j, k: (0, j)),
        out_shape=jax.ShapeDtypeStruct((m, n), f32),
        compiler_params=_params(2),
    )(a, b)


def _row_tile(rows, limit=512):
    best = rows
    for t in range(8, min(rows, limit) + 1, 8):
        if rows % t == 0:
            best = t
    return best if rows > limit else rows


def _adamw(w, g, m, v, name):
    rows, cols = w.shape
    tr = _row_tile(rows)

    def body(w_ref, g_ref, m_ref, v_ref, d_ref, mo_ref, vo_ref):
        g = g_ref[...]
        m2 = ADAM_B1 * m_ref[...] + (1.0 - ADAM_B1) * g
        v2 = ADAM_B2 * v_ref[...] + (1.0 - ADAM_B2) * (g * g)
        m_hat = m2 / (1.0 - ADAM_B1 ** ADAM_STEP)
        v_hat = v2 / (1.0 - ADAM_B2 ** ADAM_STEP)
        d_ref[...] = -ADAM_LR * (m_hat / (jnp.sqrt(v_hat) + ADAM_EPS) + ADAM_WD * w_ref[...])
        mo_ref[...] = m2
        vo_ref[...] = v2

    spec = pl.BlockSpec((tr, cols), lambda i: (i, 0))
    return pl.pallas_call(
        body, name=name, grid=(rows // tr,),
        in_specs=[spec] * 4, out_specs=[spec] * 3,
        out_shape=[jax.ShapeDtypeStruct((rows, cols), f32)] * 3,
        compiler_params=_params(1),
    )(w, g, m, v)


def _sum_leading(x, name):
    n, rows, cols = x.shape
    tr = _row_tile(rows)

    def body(x_ref, o_ref):
        acc = x_ref[0]
        for k in range(1, n):
            acc = acc + x_ref[k]
        o_ref[...] = acc

    return pl.pallas_call(
        body, name=name, grid=(rows // tr,),
        in_specs=[pl.BlockSpec((n, tr, cols), lambda i: (0, i, 0))],
        out_specs=pl.BlockSpec((tr, cols), lambda i: (i, 0)),
        out_shape=jax.ShapeDtypeStruct((rows, cols), f32),
        compiler_params=_params(1),
    )(x)


def _pair_sum(g, recv, c_idx):
    _, rows, cols = g.shape
    tr = _row_tile(rows)

    def body(c_ref, g_ref, r_ref, o_ref):
        o_ref[...] = g_ref[...] + r_ref[...]

    return pl.pallas_call(
        body, name="pair_sum",
        grid_spec=pltpu.PrefetchScalarGridSpec(
            num_scalar_prefetch=1, grid=(N_CHIP, rows // tr),
            in_specs=[pl.BlockSpec((1, tr, cols), lambda k, i, c: (2 * k + c[0], i, 0)),
                      pl.BlockSpec((1, tr, cols), lambda k, i, c: (k, i, 0))],
            out_specs=pl.BlockSpec((1, tr, cols), lambda k, i, c: (k, i, 0))),
        out_shape=jax.ShapeDtypeStruct((N_CHIP, rows, cols), f32),
        compiler_params=_params(2),
    )(c_idx, g, recv)


def _position():
    return lax.axis_index("x"), lax.axis_index("y"), lax.axis_index("c")


def _all_gather(block, name):
    rows, cols = block.shape

    def body(x_ref, out_ref, send_sems, recv_sems, local_sem):
        x, y, c = _position()
        me, sibling = (x, y, c), (x, y, 1 - c)
        chips = [(1 - x, y), (x, 1 - y), (1 - x, 1 - y)]

        def slot(px, py, pc):
            return out_ref.at[4 * px + 2 * py + pc]

        def copy(k, block_of, to, src=None):
            return pltpu.make_async_remote_copy(
                src_ref=slot(*block_of) if src is None else src, dst_ref=slot(*block_of),
                send_sem=send_sems.at[k], recv_sem=recv_sems.at[k], device_id=to, device_id_type=MESH)

        mine = pltpu.make_async_copy(x_ref, slot(*me), local_sem)
        mine.start()
        first = [copy(0, me, sibling, src=x_ref)]
        first += [copy(1 + j, me, (*chip, c), src=x_ref) for j, chip in enumerate(chips)]
        for cp in first:
            cp.start()
        passed = [copy(4 + j, (*chip, c), sibling) for j, chip in enumerate(chips)]
        for j, chip in enumerate(chips):
            copy(1 + j, (*chip, c), me).wait_recv()
            passed[j].start()
        copy(0, sibling, me).wait_recv()
        for j, chip in enumerate(chips):
            copy(4 + j, (*chip, 1 - c), me).wait_recv()
        for cp in first + passed:
            cp.wait_send()
        mine.wait()

    return pl.pallas_call(
        body, name=name,
        in_specs=[ANY], out_specs=ANY,
        out_shape=jax.ShapeDtypeStruct((N_DEV, rows, cols), block.dtype),
        scratch_shapes=[pltpu.SemaphoreType.DMA((7,)), pltpu.SemaphoreType.DMA((7,)), pltpu.SemaphoreType.DMA],
    )(block)


def _sibling_exchange(g):
    _, rows, cols = g.shape

    def body(g_ref, recv_ref, send_sems, recv_sems):
        x, y, c = _position()
        copies = [pltpu.make_async_remote_copy(
            src_ref=g_ref.at[2 * k + (1 - c)], dst_ref=recv_ref.at[k], send_sem=send_sems.at[k],
            recv_sem=recv_sems.at[k], device_id=(x, y, 1 - c), device_id_type=MESH) for k in range(N_CHIP)]
        for cp in copies:
            cp.start()
        for cp in copies:
            cp.wait()

    return pl.pallas_call(
        body, name="sibling_exchange",
        in_specs=[ANY], out_specs=ANY,
        out_shape=jax.ShapeDtypeStruct((N_CHIP, rows, cols), g.dtype),
        scratch_shapes=[pltpu.SemaphoreType.DMA((N_CHIP,)), pltpu.SemaphoreType.DMA((N_CHIP,))],
    )(g)


def _chip_exchange(p):
    _, rows, cols = p.shape

    def body(p_ref, recv_ref, send_sems, recv_sems, local_sem):
        x, y, c = _position()
        k_me = 2 * x + y
        mine = pltpu.make_async_copy(p_ref.at[k_me], recv_ref.at[k_me], local_sem)
        mine.start()
        copies = []
        for j, (px, py) in enumerate([(1 - x, y), (x, 1 - y), (1 - x, 1 - y)]):
            copies.append(pltpu.make_async_remote_copy(
                src_ref=p_ref.at[2 * px + py], dst_ref=recv_ref.at[k_me], send_sem=send_sems.at[j],
                recv_sem=recv_sems.at[j], device_id=(px, py, c), device_id_type=MESH))
        for cp in copies:
            cp.start()
        for cp in copies:
            cp.wait()
        mine.wait()

    return pl.pallas_call(
        body, name="chip_exchange",
        in_specs=[ANY], out_specs=ANY,
        out_shape=jax.ShapeDtypeStruct((N_CHIP, rows, cols), p.dtype),
        scratch_shapes=[pltpu.SemaphoreType.DMA((3,)), pltpu.SemaphoreType.DMA((3,)), pltpu.SemaphoreType.DMA],
    )(p)


def _reduce_scatter(g):
    c_idx = lax.axis_index("c").astype(jnp.int32).reshape(1)
    pair = _pair_sum(g, _sibling_exchange(g), c_idx)
    return _sum_leading(_chip_exchange(pair), "chip_sum")


def _cols_to_rows(w):
    return w.reshape(-1, D_MODEL)


def _pack_shards(even_w_in, even_w_out, odd_w_in, odd_w_out, wg, wu, wd):
    parts = [_cols_to_rows(even_w_in[0]), even_w_out[0], _cols_to_rows(odd_w_in[0]), odd_w_out[0]]
    for layer in range(2):
        parts += [_cols_to_rows(wg[layer]), _cols_to_rows(wu[layer]), wd[layer]]
    return parts


def _unpack_cols(blocks, k):
    n_shard = blocks.shape[1] * D_MODEL // k
    return blocks.reshape(N_DEV, k, n_shard).transpose(1, 0, 2).reshape(k, N_DEV * n_shard)


def _pack_cols(w):
    k, n = w.shape
    return w.reshape(k, N_DEV, n // N_DEV).transpose(1, 0, 2).reshape(N_DEV, -1, D_MODEL)


def _pack_rows(w):
    return w.reshape(N_DEV, -1, D_MODEL)


def kernel(x, even_w_in, even_w_out, a_w_s, a_b_s, a_ln_g, a_ln_b, b_w_pool, b_scale, odd_w_in, odd_w_out, c_w_dw, c_b_dw, c_ln_g, c_ln_b, d_w_dw, norm_mix_g, norm_ffn_g, ffn_w_gate, ffn_w_up, ffn_w_down, final_norm_g, loss_target, m_even_w_in, m_even_w_out, m_a_w_s, m_a_b_s, m_a_ln_g, m_a_ln_b, m_b_w_pool, m_b_scale, m_odd_w_in, m_odd_w_out, m_c_w_dw, m_c_b_dw, m_c_ln_g, m_c_ln_b, m_d_w_dw, m_norm_mix_g, m_norm_ffn_g, m_ffn_w_gate, m_ffn_w_up, m_ffn_w_down, m_final_norm_g, v_even_w_in, v_even_w_out, v_a_w_s, v_a_b_s, v_a_ln_g, v_a_ln_b, v_b_w_pool, v_b_scale, v_odd_w_in, v_odd_w_out, v_c_w_dw, v_c_b_dw, v_c_ln_g, v_c_ln_b, v_d_w_dw, v_norm_mix_g, v_norm_ffn_g, v_ffn_w_gate, v_ffn_w_up, v_ffn_w_down, v_final_norm_g):
    weights = dict(even_w_in=even_w_in, even_w_out=even_w_out, a_w_s=a_w_s, a_b_s=a_b_s, a_ln_g=a_ln_g, a_ln_b=a_ln_b,
                   b_w_pool=b_w_pool, b_scale=b_scale, odd_w_in=odd_w_in, odd_w_out=odd_w_out, c_w_dw=c_w_dw,
                   c_b_dw=c_b_dw, c_ln_g=c_ln_g, c_ln_b=c_ln_b, d_w_dw=d_w_dw, norm_mix_g=norm_mix_g,
                   norm_ffn_g=norm_ffn_g, ffn_w_gate=ffn_w_gate, ffn_w_up=ffn_w_up, ffn_w_down=ffn_w_down,
                   final_norm_g=final_norm_g)
    m_in = dict(even_w_in=m_even_w_in, even_w_out=m_even_w_out, a_w_s=m_a_w_s, a_b_s=m_a_b_s, a_ln_g=m_a_ln_g,
                a_ln_b=m_a_ln_b, b_w_pool=m_b_w_pool, b_scale=m_b_scale, odd_w_in=m_odd_w_in, odd_w_out=m_odd_w_out,
                c_w_dw=m_c_w_dw, c_b_dw=m_c_b_dw, c_ln_g=m_c_ln_g, c_ln_b=m_c_ln_b, d_w_dw=m_d_w_dw,
                norm_mix_g=m_norm_mix_g, norm_ffn_g=m_norm_ffn_g, ffn_w_gate=m_ffn_w_gate, ffn_w_up=m_ffn_w_up,
                ffn_w_down=m_ffn_w_down, final_norm_g=m_final_norm_g)
    v_in = dict(even_w_in=v_even_w_in, even_w_out=v_even_w_out, a_w_s=v_a_w_s, a_b_s=v_a_b_s, a_ln_g=v_a_ln_g,
                a_ln_b=v_a_ln_b, b_w_pool=v_b_w_pool, b_scale=v_b_scale, odd_w_in=v_odd_w_in, odd_w_out=v_odd_w_out,
                c_w_dw=v_c_w_dw, c_b_dw=v_c_b_dw, c_ln_g=v_c_ln_g, c_ln_b=v_c_ln_b, d_w_dw=v_d_w_dw,
                norm_mix_g=v_norm_mix_g, norm_ffn_g=v_norm_ffn_g, ffn_w_gate=v_ffn_w_gate, ffn_w_up=v_ffn_w_up,
                ffn_w_down=v_ffn_w_down, final_norm_g=v_final_norm_g)
    names = list(weights)

    parts = _pack_shards(even_w_in, even_w_out, odd_w_in, odd_w_out, ffn_w_gate, ffn_w_up, ffn_w_down)
    part_rows = [p.shape[0] for p in parts]
    offs = [sum(part_rows[:k]) for k in range(len(parts) + 1)]
    big = _all_gather(jnp.concatenate(parts, axis=0).astype(bf16), "gather_weights")

    def gathered(k):
        return big[:, offs[k]:offs[k + 1], :]

    w_in_e = _unpack_cols(gathered(0), D_MODEL)
    w_out_e = gathered(1).reshape(D_MODEL, D_MODEL)
    w_in_o = _unpack_cols(gathered(2), D_MODEL)
    w_out_o = gathered(3).reshape(D_MODEL, D_MODEL)
    w_gate = [_unpack_cols(gathered(4 + 3 * l), D_MODEL) for l in range(2)]
    w_up = [_unpack_cols(gathered(5 + 3 * l), D_MODEL) for l in range(2)]
    w_down = [gathered(6 + 3 * l).reshape(D_FF, D_MODEL) for l in range(2)]

    conv_names = ["c_w_dw", "c_b_dw", "c_ln_g", "c_ln_b", "d_w_dw"]
    conv_rows = [C_KERNEL, 1, 1, 1, D_KERNEL]
    conv_local = jnp.concatenate([weights[n].reshape(r, -1) for n, r in zip(conv_names, conv_rows)]
                                 + [jnp.zeros((3, c_b_dw.shape[-1]), f32)], axis=0)
    conv_all = _all_gather(conv_local, "gather_conv").transpose(1, 0, 2).reshape(conv_local.shape[0], -1)
    conv_offs = [sum(conv_rows[:k]) for k in range(len(conv_rows) + 1)]
    cw, cb, clg, clb, dw = [conv_all[conv_offs[k]:conv_offs[k + 1]] for k in range(len(conv_rows))]

    ws, bst = a_w_s[0], a_b_s[0].T
    lng, lnb, wp, sc = a_ln_g, a_ln_b, b_w_pool[0], b_scale
    gmix = [norm_mix_g[l:l + 1] for l in range(2)]
    gffn = [norm_ffn_g[l:l + 1] for l in range(2)]
    gfin = final_norm_g.reshape(1, D_MODEL)

    h0 = x[0]
    h1, hn_e, za, pooled, mix_e = _even_fwd(h0, w_in_e, w_out_e, ws, bst, lng, lnb, wp, sc, gmix[0])
    h2, hn_f0, gate0, up0 = _ffn_fwd(h1, w_gate[0], w_up[0], w_down[0], gffn[0])
    h3, hn_o, z_o, mix_o = _odd_fwd(h2, w_in_o, w_out_o, cw, cb, clg, clb, dw, gmix[1])
    h4, hn_f1, gate1, up1 = _ffn_fwd(h3, w_gate[1], w_up[1], w_down[1], gffn[1])

    dh4, loss_local, g_final = _loss_head(h4, loss_target[0], gfin)
    dh3, dgate1, dup1, act1, g_ffn1 = _ffn_bwd(dh4, h3, gate1, up1, w_gate[1], w_up[1], w_down[1], gffn[1])
    dh2, dz_o, g_cw, g_cb, g_clg, g_clb, g_dw, g_mix1 = _odd_bwd(dh3, h2, z_o, w_in_o, w_out_o, cw, cb, clg, clb, dw, gmix[1])
    dh1, dgate0, dup0, act0, g_ffn0 = _ffn_bwd(dh2, h1, gate0, up0, w_gate[0], w_up[0], w_down[0], gffn[0])
    dh0, dz_e, g_ws, g_bs, g_lng, g_lnb, g_wp, g_sc, g_mix0 = _even_bwd(
        dh1, h0, za, pooled, w_in_e, w_out_e, ws, bst, lng, lnb, wp, sc, gmix[0])

    full = [
        _pack_cols(_weight_grad(hn_e, dz_e, "dw_even_in")),
        _pack_rows(_weight_grad(mix_e, dh1, "dw_even_out")),
        _pack_cols(_weight_grad(hn_o, dz_o, "dw_odd_in")),
        _pack_rows(_weight_grad(mix_o, dh3, "dw_odd_out")),
    ]
    for hn_f, dgate, dup, act, dh_out, l in ((hn_f0, dgate0, dup0, act0, dh2, 0), (hn_f1, dgate1, dup1, act1, dh4, 1)):
        full += [_pack_cols(_weight_grad(hn_f, dgate, f"dw_gate{l}")), _pack_cols(_weight_grad(hn_f, dup, f"dw_up{l}")),
                 _pack_rows(_weight_grad(act, dh_out, f"dw_down{l}"))]
    g_big = _reduce_scatter(jnp.concatenate(full, axis=1))

    lanes = HEAD
    small = [("a_w_s", g_ws), ("a_b_s", g_bs), ("a_ln_g", g_lng), ("a_ln_b", g_lnb), ("b_w_pool", g_wp),
             ("b_scale", g_sc), ("norm_mix_g", jnp.concatenate([g_mix0, g_mix1], axis=0)),
             ("norm_ffn_g", jnp.concatenate([g_ffn0, g_ffn1], axis=0)), ("final_norm_g", g_final),
             ("c_w_dw", g_cw), ("c_b_dw", g_cb), ("c_ln_g", g_clg), ("c_ln_b", g_clb), ("d_w_dw", g_dw)]
    small_rows = [-(-g.size // (8 * lanes)) * 8 for _, g in small]
    small_offs = [sum(small_rows[:k]) for k in range(len(small) + 1)]
    pad_rows = -small_offs[-1] % 256
    small_buf = jnp.concatenate(
        [jnp.pad(g.reshape(-1), (0, r * lanes - g.size)).reshape(r, lanes) for (_, g), r in zip(small, small_rows)]
        + [jnp.zeros((pad_rows, lanes), f32)], axis=0)
    small_sum = _sum_leading(_all_gather(small_buf, "gather_small_grads"), "small_grad_sum")
    grads = {}
    for k, (n, g) in enumerate(small):
        grads[n] = small_sum[small_offs[k]:small_offs[k + 1]].reshape(-1)[:g.size].reshape(g.shape)
    me = 4 * lax.axis_index("x") + 2 * lax.axis_index("y") + lax.axis_index("c")
    shard = c_b_dw.shape[-1]
    for n in conv_names:
        grads[n] = lax.dynamic_slice_in_dim(grads[n], me * shard, shard, axis=1)

    def big_grad(k):
        return g_big[offs[k]:offs[k + 1]]

    grads["even_w_in"] = big_grad(0).reshape(even_w_in.shape)
    grads["even_w_out"] = big_grad(1).reshape(even_w_out.shape)
    grads["odd_w_in"] = big_grad(2).reshape(odd_w_in.shape)
    grads["odd_w_out"] = big_grad(3).reshape(odd_w_out.shape)
    grads["ffn_w_gate"] = jnp.stack([big_grad(4 + 3 * l).reshape(ffn_w_gate.shape[1:]) for l in range(2)])
    grads["ffn_w_up"] = jnp.stack([big_grad(5 + 3 * l).reshape(ffn_w_up.shape[1:]) for l in range(2)])
    grads["ffn_w_down"] = jnp.stack([big_grad(6 + 3 * l).reshape(ffn_w_down.shape[1:]) for l in range(2)])
    grads = {n: grads[n].reshape(weights[n].shape) for n in names}

    delta, new_m, new_v = {}, {}, {}
    for n in names:
        shape = weights[n].shape
        view = (-1, shape[-1])
        d, m2, v2 = _adamw(weights[n].reshape(view), grads[n].reshape(view), m_in[n].reshape(view),
                           v_in[n].reshape(view), "adamw_" + n)
        delta[n], new_m[n], new_v[n] = d.reshape(shape), m2.reshape(shape), v2.reshape(shape)

    loss = lax.psum(loss_local[0, 0], ("x", "y", "c"))
    return (loss, dh0[None], *[grads[n] for n in names], *[delta[n] for n in names],
            *[new_m[n] for n in names], *[new_v[n] for n in names])
```

```python
import jax
import jax.numpy as jnp
from jax import lax
from jax.experimental import pallas as pl
from jax.experimental.pallas import tpu as pltpu

f32 = jnp.float32
bf16 = jnp.bfloat16

EPS = 1e-6
D_MODEL = 1024
A_WIDTH = 512
HEAD = 128
N_HEADS = 4
CHUNK = 64
POOL_WINDOWS = (2, 4, 8, 16)
POOL_HALO = 16
C_KERNEL = 31
D_KERNEL = 3
CONV_HALO = 32
D_FF = 2816
N_DEV = 8
N_CHIP = 4

ADAM_LR = 0.001
ADAM_B1 = 0.9
ADAM_B2 = 0.999
ADAM_EPS = 1e-08
ADAM_WD = 0.01
ADAM_STEP = 10

MIX_TILE = 512
FFN_TILE = 512
FFN_BWD_TILE = 256
FFN_CHUNK = 1408
DW_TK = 256
DW_TM = 1536
VMEM_LIMIT = 56 * 1024 * 1024

MESH = pl.DeviceIdType.MESH
ANY = pl.BlockSpec(memory_space=pl.ANY)


def _params(n_axes):
    return pltpu.CompilerParams(dimension_semantics=("arbitrary",) * n_axes, vmem_limit_bytes=VMEM_LIMIT)


def _mm(a, b):
    return jnp.dot(a, b, preferred_element_type=f32)


def _mm_nt(a, b):
    return lax.dot_general(a, b, (((1,), (1,)), ((), ())), preferred_element_type=f32)


def _mm_tn(a, b):
    return lax.dot_general(a, b, (((0,), (0,)), ((), ())), preferred_element_type=f32)


def _sigmoid(x):
    return 1.0 / (1.0 + jnp.exp(-x))


def _rms_r(h):
    return lax.rsqrt(jnp.mean(h * h, axis=-1, keepdims=True) + EPS)


def _rms_bwd(dy, h, g):
    r = _rms_r(h)
    xh = h * r
    dxh = dy * g
    dh = r * (dxh - xh * jnp.mean(dxh * xh, axis=-1, keepdims=True))
    return dh, jnp.sum(dy * xh, axis=0, keepdims=True)


def _ln_fwd(x, g, b):
    mu = jnp.mean(x, axis=-1, keepdims=True)
    xc = x - mu
    r = lax.rsqrt(jnp.mean(xc * xc, axis=-1, keepdims=True) + EPS)
    xh = xc * r
    return xh * g + b, xh, r


def _ln_bwd(dy, xh, r, g):
    dxh = dy * g
    return r * (dxh - jnp.mean(dxh, axis=-1, keepdims=True) - xh * jnp.mean(dxh * xh, axis=-1, keepdims=True))


_GELU_C = 0.7978845608028654
_GELU_A = 0.044715


def _gelu(x):
    th = jnp.tanh(_GELU_C * (x + _GELU_A * x * x * x))
    return 0.5 * x * (1.0 + th), th


def _gelu_grad(x, th):
    return 0.5 * (1.0 + th) + 0.5 * x * (1.0 - th * th) * (_GELU_C * (1.0 + 3.0 * _GELU_A * x * x))


def _down(x, k):
    return x if k == 0 else pltpu.roll(x, k, 0)


def _up(x, k):
    return x if k == 0 else pltpu.roll(x, x.shape[0] - k, 0)


def _window_sum(x, win, shift):
    s = x
    step = 1
    while step < win:
        s = s + shift(s, step)
        step *= 2
    return s


def _inv_count(t0, rows, win):
    t = t0 + lax.broadcasted_iota(jnp.int32, (rows, 1), 0)
    return 1.0 / jnp.minimum(t + 1, win).astype(f32)


def _chunk_mask():
    i = lax.broadcasted_iota(jnp.int32, (HEAD, HEAD), 0)
    j = lax.broadcasted_iota(jnp.int32, (HEAD, HEAD), 1)
    return jnp.logical_or(i >= CHUNK, j < CHUNK)


def _const(shape, n_axes):
    zeros = (0,) * len(shape)
    if n_axes == 1:
        return pl.BlockSpec(shape, lambda i: zeros)
    return pl.BlockSpec(shape, lambda i, j: zeros)


def _prev_halo(tile, halo, cols):
    return pl.BlockSpec((halo, cols), lambda i: (jnp.maximum(i * (tile // halo) - 1, 0), 0))


def _next_halo(tile, halo, cols, seq):
    return pl.BlockSpec((halo, cols), lambda i: (jnp.minimum((i + 1) * (tile // halo), seq // halo - 1), 0))


def _gmlp_gate(vnb, wsm, bst, tile):
    rows = []
    for n in range(tile // HEAD):
        cols = []
        for hh in range(N_HEADS):
            blk = vnb[n * HEAD:(n + 1) * HEAD, hh * HEAD:(hh + 1) * HEAD]
            cols.append(_mm(wsm[hh], blk) + bst[:, hh:hh + 1])
        rows.append(jnp.concatenate(cols, axis=1))
    return jnp.concatenate(rows, axis=0)


def _even_fwd(h, w_in, w_out, ws, bst, lng, lnb, wp, sc, gm):
    seq = h.shape[0]
    tile = min(MIX_TILE, seq)
    n_tiles = seq // tile

    def body(h_ref, hp_ref, win_ref, wout_ref, ws_ref, bst_ref, lng_ref, lnb_ref, wp_ref, sc_ref, g_ref,
             ho_ref, hn_ref, za_ref, pool_ref, mix_ref):
        i = pl.program_id(0)
        g = g_ref[...]
        h = h_ref[...]
        hnb = (h * _rms_r(h) * g).astype(bf16)
        hn_ref[...] = hnb
        z = _mm_nt(hnb, win_ref[...])
        zab = z[:, :2 * A_WIDTH].astype(bf16)
        za_ref[...] = zab
        hp = hp_ref[...]
        zbp = _mm_nt((hp * _rms_r(hp) * g).astype(bf16), win_ref[2 * A_WIDTH:, :])
        zbe = jnp.concatenate([jnp.where(i > 0, zbp, 0.0), z[:, 2 * A_WIDTH:]], axis=0)
        pooled = []
        for gi, win in enumerate(POOL_WINDOWS):
            xg = zbe[:, gi * HEAD:(gi + 1) * HEAD]
            s = _window_sum(xg, win, _down)
            pooled.append(s[POOL_HALO:] * _inv_count(i * tile, tile, win) - xg[POOL_HALO:])
        plb = jnp.concatenate(pooled, axis=1).astype(bf16)
        pool_ref[...] = plb

        ga, _ = _gelu(zab.astype(f32))
        vn, _, _ = _ln_fwd(ga[:, A_WIDTH:], lng_ref[...], lnb_ref[...])
        mask = _chunk_mask()
        wsm = [jnp.where(mask, ws_ref[hh], 0.0).astype(bf16) for hh in range(N_HEADS)]
        ya = ga[:, :A_WIDTH] * _gmlp_gate(vn.astype(bf16), wsm, bst_ref[...], tile)
        yb = jnp.concatenate([_mm(plb[:, gi * HEAD:(gi + 1) * HEAD], wp_ref[gi].astype(bf16))
                              for gi in range(len(POOL_WINDOWS))], axis=1) * sc_ref[...]
        mix = jnp.concatenate([ya, yb], axis=1).astype(bf16)
        mix_ref[...] = mix
        ho_ref[...] = h + _mm(mix, wout_ref[...])

    row = lambda cols: pl.BlockSpec((tile, cols), lambda i: (i, 0))
    return pl.pallas_call(
        body, name="even_fwd", grid=(n_tiles,),
        in_specs=[row(D_MODEL), _prev_halo(tile, POOL_HALO, D_MODEL), _const(w_in.shape, 1), _const(w_out.shape, 1),
                  _const(ws.shape, 1), _const(bst.shape, 1), _const(lng.shape, 1), _const(lnb.shape, 1),
                  _const(wp.shape, 1), _const(sc.shape, 1), _const(gm.shape, 1)],
        out_specs=[row(D_MODEL), row(D_MODEL), row(2 * A_WIDTH), row(A_WIDTH), row(D_MODEL)],
        out_shape=[jax.ShapeDtypeStruct((seq, D_MODEL), f32), jax.ShapeDtypeStruct((seq, D_MODEL), bf16),
                   jax.ShapeDtypeStruct((seq, 2 * A_WIDTH), bf16), jax.ShapeDtypeStruct((seq, A_WIDTH), bf16),
                   jax.ShapeDtypeStruct((seq, D_MODEL), bf16)],
        compiler_params=_params(1),
    )(h, h, w_in, w_out, ws, bst, lng, lnb, wp, sc, gm)


def _even_bwd(dh, h, za, pooled, w_in, w_out, ws, bst, lng, lnb, wp, sc, gm):
    seq = h.shape[0]
    tile = min(MIX_TILE, seq)
    n_tiles = seq // tile
    n_groups = len(POOL_WINDOWS)

    def body(dh_ref, dhx_ref, h_ref, za_ref, pool_ref, win_ref, wout_ref, ws_ref, bst_ref, lng_ref, lnb_ref,
             wp_ref, sc_ref, g_ref,
             dhi_ref, dz_ref, dws_ref, dbs_ref, dlng_ref, dlnb_ref, dwp_ref, dsc_ref, dg_ref):
        i = pl.program_id(0)

        @pl.when(i == 0)
        def _():
            for ref in (dws_ref, dbs_ref, dlng_ref, dlnb_ref, dwp_ref, dsc_ref, dg_ref):
                ref[...] = jnp.zeros_like(ref)

        dh = dh_ref[...]
        dmix = _mm_nt(dh.astype(bf16), wout_ref[...])
        dya = dmix[:, :A_WIDTH]
        dyb = dmix[:, A_WIDTH:]
        dybx = _mm_nt(dhx_ref[...].astype(bf16), wout_ref[A_WIDTH:, :])
        dybx = jnp.where(i < n_tiles - 1, dybx, 0.0)

        za = za_ref[...].astype(f32)
        ga, th = _gelu(za)
        u = ga[:, :A_WIDTH]
        lng = lng_ref[...]
        vn, vh, r = _ln_fwd(ga[:, A_WIDTH:], lng, lnb_ref[...])
        vnb = vn.astype(bf16)
        mask = _chunk_mask()
        wsf = [jnp.where(mask, ws_ref[hh], 0.0) for hh in range(N_HEADS)]
        sv = _gmlp_gate(vnb, [w.astype(bf16) for w in wsf], bst_ref[...], tile)
        du = dya * sv
        dsvb = (dya * u).astype(bf16)
        wst = [w.T.astype(bf16) for w in wsf]
        ones = jnp.ones((8, HEAD), bf16)
        dws = [jnp.zeros((HEAD, HEAD), f32) for _ in range(N_HEADS)]
        dbs = [jnp.zeros((8, HEAD), f32) for _ in range(N_HEADS)]
        rows = []
        for n in range(tile // HEAD):
            cols = []
            for hh in range(N_HEADS):
                blk = dsvb[n * HEAD:(n + 1) * HEAD, hh * HEAD:(hh + 1) * HEAD]
                cols.append(_mm(wst[hh], blk))
                dws[hh] = dws[hh] + _mm_nt(blk, vnb[n * HEAD:(n + 1) * HEAD, hh * HEAD:(hh + 1) * HEAD])
                dbs[hh] = dbs[hh] + _mm_nt(ones, blk)
            rows.append(jnp.concatenate(cols, axis=1))
        dvn = jnp.concatenate(rows, axis=0)
        for hh in range(N_HEADS):
            dws_ref[hh] += jnp.where(mask, dws[hh], 0.0)
            dbs_ref[pl.ds(hh, 1), :] += dbs[hh][0:1, :]
        dlng_ref[...] += jnp.sum(dvn * vh, axis=0, keepdims=True)
        dlnb_ref[...] += jnp.sum(dvn, axis=0, keepdims=True)
        dv = _ln_bwd(dvn, vh, r, lng)
        dza = jnp.concatenate([du, dv], axis=1) * _gelu_grad(za, th)

        plb = pool_ref[...]
        sc = sc_ref[...]
        dzb = []
        dsc = []
        for gi, win in enumerate(POOL_WINDOWS):
            cs = slice(gi * HEAD, (gi + 1) * HEAD)
            wpb = wp_ref[gi].astype(bf16)
            dsc.append(jnp.sum(dyb[:, cs] * _mm(plb[:, cs], wpb), axis=0, keepdims=True))
            dpre = (dyb[:, cs] * sc[:, cs]).astype(bf16)
            dprex = (dybx[:, cs] * sc[:, cs]).astype(bf16)
            dwp_ref[gi] += _mm_tn(plb[:, cs], dpre)
            dpl = _mm_nt(dpre, wpb)
            dple = jnp.concatenate([dpl, _mm_nt(dprex, wpb)], axis=0)
            q = dple * _inv_count(i * tile, tile + POOL_HALO, win)
            dzb.append(_window_sum(q, win, _up)[:tile] - dpl)
        dsc_ref[...] += jnp.concatenate(dsc, axis=1)

        dzf = jnp.concatenate([dza] + dzb, axis=1).astype(bf16)
        dz_ref[...] = dzf
        dhn = _mm(dzf, win_ref[...])
        dhr, dg = _rms_bwd(dhn, h_ref[...], g_ref[...])
        dhi_ref[...] = dh + dhr
        dg_ref[...] += dg

    row = lambda cols: pl.BlockSpec((tile, cols), lambda i: (i, 0))
    small = [ws.shape, (N_HEADS, HEAD), lng.shape, lnb.shape, wp.shape, sc.shape, gm.shape]
    return pl.pallas_call(
        body, name="even_bwd", grid=(n_tiles,),
        in_specs=[row(D_MODEL), _next_halo(tile, POOL_HALO, D_MODEL, seq), row(D_MODEL), row(2 * A_WIDTH), row(A_WIDTH),
                  _const(w_in.shape, 1), _const(w_out.shape, 1), _const(ws.shape, 1), _const(bst.shape, 1),
                  _const(lng.shape, 1), _const(lnb.shape, 1), _const(wp.shape, 1), _const(sc.shape, 1), _const(gm.shape, 1)],
        out_specs=[row(D_MODEL), row(3 * A_WIDTH)] + [_const(s, 1) for s in small],
        out_shape=[jax.ShapeDtypeStruct((seq, D_MODEL), f32), jax.ShapeDtypeStruct((seq, 3 * A_WIDTH), bf16)]
                  + [jax.ShapeDtypeStruct(s, f32) for s in small],
        compiler_params=_params(1),
    )(dh, dh, h, za, pooled, w_in, w_out, ws, bst, lng, lnb, wp, sc, gm)


def _conv_taps(x, w_ref, n_taps, halo, rows):
    acc = None
    for j in range(n_taps):
        term = w_ref[pl.ds(j, 1), :] * _down(x, n_taps - 1 - j)[halo:halo + rows]
        acc = term if acc is None else acc + term
    return acc


def _odd_fwd(h, w_in, w_out, cw, cb, clg, clb, dw, gm):
    seq = h.shape[0]
    tile = min(MIX_TILE, seq)
    n_tiles = seq // tile
    w = A_WIDTH

    def body(h_ref, hp_ref, win_ref, wout_ref, cw_ref, cb_ref, clg_ref, clb_ref, dw_ref, g_ref,
             ho_ref, hn_ref, z_ref, mix_ref):
        i = pl.program_id(0)
        g = g_ref[...]
        h = h_ref[...]
        hnb = (h * _rms_r(h) * g).astype(bf16)
        hn_ref[...] = hnb
        zb = _mm_nt(hnb, win_ref[...]).astype(bf16)
        z_ref[...] = zb
        hp = hp_ref[...]
        zp = _mm_nt((hp * _rms_r(hp) * g).astype(bf16), win_ref[...]).astype(bf16).astype(f32)
        z = zb.astype(f32)
        ze = jnp.concatenate([jnp.where(i > 0, zp, 0.0), z], axis=0)
        hc = ze[:, :w] * _sigmoid(ze[:, w:2 * w])
        cv = _conv_taps(hc, cw_ref, C_KERNEL, CONV_HALO, tile) + cb_ref[...]
        ln, _, _ = _ln_fwd(cv, clg_ref[...], clb_ref[...])
        yc = ln * _sigmoid(ln)
        p = ze[:, 3 * w:4 * w] * ze[:, 4 * w:]
        yd = z[:, 2 * w:3 * w] * _conv_taps(p, dw_ref, D_KERNEL, CONV_HALO, tile)
        mix = jnp.concatenate([yc, yd], axis=1).astype(bf16)
        mix_ref[...] = mix
        ho_ref[...] = h + _mm(mix, wout_ref[...])

    row = lambda cols: pl.BlockSpec((tile, cols), lambda i: (i, 0))
    return pl.pallas_call(
        body, name="odd_fwd", grid=(n_tiles,),
        in_specs=[row(D_MODEL), _prev_halo(tile, CONV_HALO, D_MODEL), _const(w_in.shape, 1), _const(w_out.shape, 1),
                  _const(cw.shape, 1), _const(cb.shape, 1), _const(clg.shape, 1), _const(clb.shape, 1),
                  _const(dw.shape, 1), _const(gm.shape, 1)],
        out_specs=[row(D_MODEL), row(D_MODEL), row(5 * w), row(D_MODEL)],
        out_shape=[jax.ShapeDtypeStruct((seq, D_MODEL), f32), jax.ShapeDtypeStruct((seq, D_MODEL), bf16),
                   jax.ShapeDtypeStruct((seq, 5 * w), bf16), jax.ShapeDtypeStruct((seq, D_MODEL), bf16)],
        compiler_params=_params(1),
    )(h, h, w_in, w_out, cw, cb, clg, clb, dw, gm)


def _odd_bwd(dh, h, z, w_in, w_out, cw, cb, clg, clb, dw, gm):
    seq = h.shape[0]
    tile = min(MIX_TILE, seq)
    n_tiles = seq // tile
    w = A_WIDTH
    halo = CONV_HALO
    ext = tile + halo

    def body(dh_ref, dhx_ref, h_ref, z_ref, zp_ref, zx_ref, win_ref, wout_ref, cw_ref, cb_ref, clg_ref, clb_ref,
             dw_ref, g_ref,
             dhi_ref, dz_ref, dcw_ref, dcb_ref, dclg_ref, dclb_ref, ddw_ref, dg_ref):
        i = pl.program_id(0)

        @pl.when(i == 0)
        def _():
            for ref in (dcw_ref, dcb_ref, dclg_ref, dclb_ref, ddw_ref, dg_ref):
                ref[...] = jnp.zeros_like(ref)

        dh = dh_ref[...]
        dhe = jnp.concatenate([dh, jnp.where(i < n_tiles - 1, dhx_ref[...], 0.0)], axis=0)
        dmix = _mm_nt(dhe.astype(bf16), wout_ref[...])
        ze = jnp.concatenate([jnp.where(i > 0, zp_ref[...].astype(f32), 0.0), z_ref[...].astype(f32),
                              zx_ref[...].astype(f32)], axis=0)

        sg = _sigmoid(ze[:, w:2 * w])
        ca = ze[:, :w]
        hc = ca * sg
        cv = _conv_taps(hc, cw_ref, C_KERNEL, halo, ext) + cb_ref[...]
        clg = clg_ref[...]
        ln, xh, r = _ln_fwd(cv, clg, clb_ref[...])
        sl = _sigmoid(ln)
        dln = dmix[:, :w] * (sl * (1.0 + ln * (1.0 - sl)))
        dclg_ref[...] += jnp.sum((dln * xh)[:tile], axis=0, keepdims=True)
        dclb_ref[...] += jnp.sum(dln[:tile], axis=0, keepdims=True)
        dcv = _ln_bwd(dln, xh, r, clg)
        dcb_ref[...] += jnp.sum(dcv[:tile], axis=0, keepdims=True)
        dhc = None
        for j in range(C_KERNEL):
            k = C_KERNEL - 1 - j
            dcw_ref[pl.ds(j, 1), :] += jnp.sum(dcv[:tile] * _down(hc, k)[halo:halo + tile], axis=0, keepdims=True)
            term = cw_ref[pl.ds(j, 1), :] * _up(dcv, k)[:tile]
            dhc = term if dhc is None else dhc + term
        sgt = sg[halo:halo + tile]
        cat = ca[halo:halo + tile]
        dca = dhc * sgt
        dcg = dhc * cat * sgt * (1.0 - sgt)

        dcgv = ze[:, 3 * w:4 * w]
        dxin = ze[:, 4 * w:]
        p = dcgv * dxin
        q = _conv_taps(p, dw_ref, D_KERNEL, halo, tile)
        dyd = dmix[:, w:]
        dq = dyd * ze[halo:, 2 * w:3 * w]
        ddbg = dyd[:tile] * q
        dp = None
        for j in range(D_KERNEL):
            k = D_KERNEL - 1 - j
            ddw_ref[pl.ds(j, 1), :] += jnp.sum(dq[:tile] * _down(p, k)[halo:halo + tile], axis=0, keepdims=True)
            term = dw_ref[pl.ds(j, 1), :] * _up(dq, k)[:tile]
            dp = term if dp is None else dp + term
        ddcg = dp * dxin[halo:halo + tile]
        ddxin = dp * dcgv[halo:halo + tile]

        dzf = jnp.concatenate([dca, dcg, ddbg, ddcg, ddxin], axis=1).astype(bf16)
        dz_ref[...] = dzf
        dhn = _mm(dzf, win_ref[...])
        dhr, dg = _rms_bwd(dhn, h_ref[...], g_ref[...])
        dhi_ref[...] = dh + dhr
        dg_ref[...] += dg

    row = lambda cols: pl.BlockSpec((tile, cols), lambda i: (i, 0))
    small = [cw.shape, cb.shape, clg.shape, clb.shape, dw.shape, gm.shape]
    return pl.pallas_call(
        body, name="odd_bwd", grid=(n_tiles,),
        in_specs=[row(D_MODEL), _next_halo(tile, halo, D_MODEL, seq), row(D_MODEL), row(5 * w),
                  _prev_halo(tile, halo, 5 * w), _next_halo(tile, halo, 5 * w, seq),
                  _const(w_in.shape, 1), _const(w_out.shape, 1), _const(cw.shape, 1), _const(cb.shape, 1),
                  _const(clg.shape, 1), _const(clb.shape, 1), _const(dw.shape, 1), _const(gm.shape, 1)],
        out_specs=[row(D_MODEL), row(5 * w)] + [_const(s, 1) for s in small],
        out_shape=[jax.ShapeDtypeStruct((seq, D_MODEL), f32), jax.ShapeDtypeStruct((seq, 5 * w), bf16)]
                  + [jax.ShapeDtypeStruct(s, f32) for s in small],
        compiler_params=_params(1),
    )(dh, dh, h, z, z, z, w_in, w_out, cw, cb, clg, clb, dw, gm)


def _ffn_fwd(h, wg, wu, wd, gm):
    seq = h.shape[0]
    tile = min(FFN_TILE, seq)
    n_tiles = seq // tile
    n_chunks = D_FF // FFN_CHUNK

    def body(h_ref, g_ref, wg_ref, wu_ref, wd_ref, ho_ref, hn_ref, gate_ref, up_ref, acc_ref, hns_ref):
        j = pl.program_id(1)

        @pl.when(j == 0)
        def _():
            h = h_ref[...]
            hnb = (h * _rms_r(h) * g_ref[...]).astype(bf16)
            hns_ref[...] = hnb
            hn_ref[...] = hnb
            acc_ref[...] = jnp.zeros_like(acc_ref)

        hnb = hns_ref[...]
        gb = _mm_nt(hnb, wg_ref[...]).astype(bf16)
        ub = _mm_nt(hnb, wu_ref[...]).astype(bf16)
        gate_ref[...] = gb
        up_ref[...] = ub
        gf = gb.astype(f32)
        act = gf * _sigmoid(gf) * ub.astype(f32)
        acc_ref[...] += _mm(act.astype(bf16), wd_ref[...])

        @pl.when(j == n_chunks - 1)
        def _():
            ho_ref[...] = h_ref[...] + acc_ref[...]

    row = pl.BlockSpec((tile, D_MODEL), lambda i, j: (i, 0))
    col = pl.BlockSpec((tile, FFN_CHUNK), lambda i, j: (i, j))
    wblk = pl.BlockSpec((FFN_CHUNK, D_MODEL), lambda i, j: (j, 0))
    return pl.pallas_call(
        body, name="ffn_fwd", grid=(n_tiles, n_chunks),
        in_specs=[row, _const(gm.shape, 2), wblk, wblk, wblk],
        out_specs=[row, row, col, col],
        out_shape=[jax.ShapeDtypeStruct((seq, D_MODEL), f32), jax.ShapeDtypeStruct((seq, D_MODEL), bf16),
                   jax.ShapeDtypeStruct((seq, D_FF), bf16), jax.ShapeDtypeStruct((seq, D_FF), bf16)],
        scratch_shapes=[pltpu.VMEM((tile, D_MODEL), f32), pltpu.VMEM((tile, D_MODEL), bf16)],
        compiler_params=_params(2),
    )(h, gm, wg, wu, wd)


def _ffn_bwd(dh, h, gate, up, wg, wu, wd, gm):
    seq = h.shape[0]
    tile = min(FFN_BWD_TILE, seq)
    n_tiles = seq // tile
    n_chunks = D_FF // FFN_CHUNK

    def body(dh_ref, h_ref, g_ref, gate_ref, up_ref, wg_ref, wu_ref, wd_ref,
             dhi_ref, dgate_ref, dup_ref, act_ref, dg_ref, acc_ref, dhb_ref):
        i = pl.program_id(0)
        j = pl.program_id(1)

        @pl.when(jnp.logical_and(i == 0, j == 0))
        def _():
            dg_ref[...] = jnp.zeros_like(dg_ref)

        @pl.when(j == 0)
        def _():
            dhb_ref[...] = dh_ref[...].astype(bf16)
            acc_ref[...] = jnp.zeros_like(acc_ref)

        dact = _mm_nt(dhb_ref[...], wd_ref[...])
        gf = gate_ref[...].astype(f32)
        uf = up_ref[...].astype(f32)
        s = _sigmoid(gf)
        silu = gf * s
        act_ref[...] = (silu * uf).astype(bf16)
        dgb = (dact * uf * (s * (1.0 + gf * (1.0 - s)))).astype(bf16)
        dub = (dact * silu).astype(bf16)
        dgate_ref[...] = dgb
        dup_ref[...] = dub
        acc_ref[...] += _mm(dgb, wg_ref[...]) + _mm(dub, wu_ref[...])

        @pl.when(j == n_chunks - 1)
        def _():
            dhr, dg = _rms_bwd(acc_ref[...], h_ref[...], g_ref[...])
            dhi_ref[...] = dh_ref[...] + dhr
            dg_ref[...] += dg

    row = pl.BlockSpec((tile, D_MODEL), lambda i, j: (i, 0))
    col = pl.BlockSpec((tile, FFN_CHUNK), lambda i, j: (i, j))
    wblk = pl.BlockSpec((FFN_CHUNK, D_MODEL), lambda i, j: (j, 0))
    return pl.pallas_call(
        body, name="ffn_bwd", grid=(n_tiles, n_chunks),
        in_specs=[row, row, _const(gm.shape, 2), col, col, wblk, wblk, wblk],
        out_specs=[row, col, col, col, _const(gm.shape, 2)],
        out_shape=[jax.ShapeDtypeStruct((seq, D_MODEL), f32), jax.ShapeDtypeStruct((seq, D_FF), bf16),
                   jax.ShapeDtypeStruct((seq, D_FF), bf16), jax.ShapeDtypeStruct((seq, D_FF), bf16),
                   jax.ShapeDtypeStruct(gm.shape, f32)],
        scratch_shapes=[pltpu.VMEM((tile, D_MODEL), f32), pltpu.VMEM((tile, D_MODEL), bf16)],
        compiler_params=_params(2),
    )(dh, h, gm, gate, up, wg, wu, wd)


def _loss_head(h, target, gf):
    seq = h.shape[0]
    tile = min(MIX_TILE, seq)

    def body(h_ref, t_ref, g_ref, dh_ref, loss_ref, dg_ref):
        @pl.when(pl.program_id(0) == 0)
        def _():
            loss_ref[...] = jnp.zeros_like(loss_ref)
            dg_ref[...] = jnp.zeros_like(dg_ref)

        h = h_ref[...]
        g = g_ref[...]
        err = h * _rms_r(h) * g - t_ref[...]
        loss_ref[...] += (0.5 / D_MODEL) * jnp.sum(jnp.sum(err * err, axis=1, keepdims=True), axis=0, keepdims=True)
        dhr, dg = _rms_bwd(err * (1.0 / D_MODEL), h, g)
        dh_ref[...] = dhr
        dg_ref[...] += dg

    row = pl.BlockSpec((tile, D_MODEL), lambda i: (i, 0))
    return pl.pallas_call(
        body, name="loss_head", grid=(seq // tile,),
        in_specs=[row, row, _const(gf.shape, 1)],
        out_specs=[row, _const((1, 1), 1), _const(gf.shape, 1)],
        out_shape=[jax.ShapeDtypeStruct((seq, D_MODEL), f32), jax.ShapeDtypeStruct((1, 1), f32),
                   jax.ShapeDtypeStruct(gf.shape, f32)],
        compiler_params=_params(1),
    )(h, target, gf)


def _weight_grads(pairs, name):
    seq, m = pairs[0][0].shape
    tk = min(DW_TK, seq)
    tm = m if m <= DW_TM else m // 2
    n_k = seq // tk
    n_pairs = len(pairs)

    def body(*refs):
        x_refs = refs[0:2 * n_pairs:2]
        y_refs = refs[1:2 * n_pairs:2]
        o_refs = refs[2 * n_pairs:3 * n_pairs]
        acc_refs = refs[3 * n_pairs:]
        k = pl.program_id(1)
        for x_ref, y_ref, o_ref, acc_ref in zip(x_refs, y_refs, o_refs, acc_refs):
            part = _mm_tn(x_ref[...].astype(bf16), y_ref[...].astype(bf16))

            @pl.when(k == 0)
            def _():
                acc_ref[...] = part

            @pl.when(k > 0)
            def _():
                acc_ref[...] += part

            @pl.when(k == n_k - 1)
            def _():
                o_ref[...] = acc_ref[...].astype(bf16)

    in_specs = []
    for _ in pairs:
        in_specs += [pl.BlockSpec((tk, tm), lambda j, k: (k, j)), pl.BlockSpec((tk, D_MODEL), lambda j, k: (k, 0))]
    return pl.pallas_call(
        body, name=name, grid=(m // tm, n_k),
        in_specs=in_specs,
        out_specs=[pl.BlockSpec((tm, D_MODEL), lambda j, k: (j, 0))] * n_pairs,
        out_shape=[jax.ShapeDtypeStruct((m, D_MODEL), bf16)] * n_pairs,
        scratch_shapes=[pltpu.VMEM((tm, D_MODEL), f32)] * n_pairs,
        compiler_params=_params(2),
    )(*[a for pair in pairs for a in pair])


def _row_tile(rows, limit=512):
    best = rows
    for t in range(8, min(rows, limit) + 1, 8):
        if rows % t == 0:
            best = t
    return best if rows > limit else rows


def _adamw(w, g, m, v, name):
    rows, cols = w.shape
    tr = _row_tile(rows)

    def body(w_ref, g_ref, m_ref, v_ref, d_ref, mo_ref, vo_ref):
        g = g_ref[...]
        m2 = ADAM_B1 * m_ref[...] + (1.0 - ADAM_B1) * g
        v2 = ADAM_B2 * v_ref[...] + (1.0 - ADAM_B2) * (g * g)
        m_hat = m2 / (1.0 - ADAM_B1 ** ADAM_STEP)
        v_hat = v2 / (1.0 - ADAM_B2 ** ADAM_STEP)
        d_ref[...] = -ADAM_LR * (m_hat / (jnp.sqrt(v_hat) + ADAM_EPS) + ADAM_WD * w_ref[...])
        mo_ref[...] = m2
        vo_ref[...] = v2

    spec = pl.BlockSpec((tr, cols), lambda i: (i, 0))
    return pl.pallas_call(
        body, name=name, grid=(rows // tr,),
        in_specs=[spec] * 4, out_specs=[spec] * 3,
        out_shape=[jax.ShapeDtypeStruct((rows, cols), f32)] * 3,
        compiler_params=_params(1),
    )(w, g, m, v)


def _sum_leading(x, name):
    n, rows, cols = x.shape
    tr = _row_tile(rows)

    def body(x_ref, o_ref):
        acc = x_ref[0].astype(f32)
        for k in range(1, n):
            acc = acc + x_ref[k].astype(f32)
        o_ref[...] = acc

    return pl.pallas_call(
        body, name=name, grid=(rows // tr,),
        in_specs=[pl.BlockSpec((n, tr, cols), lambda i: (0, i, 0))],
        out_specs=pl.BlockSpec((tr, cols), lambda i: (i, 0)),
        out_shape=jax.ShapeDtypeStruct((rows, cols), f32),
        compiler_params=_params(1),
    )(x)


def _pair_sum(g, recv, c_idx):
    _, rows, cols = g.shape
    tr = _row_tile(rows)

    def body(c_ref, g_ref, r_ref, o_ref):
        o_ref[...] = (g_ref[...].astype(f32) + r_ref[...].astype(f32)).astype(o_ref.dtype)

    return pl.pallas_call(
        body, name="pair_sum",
        grid_spec=pltpu.PrefetchScalarGridSpec(
            num_scalar_prefetch=1, grid=(N_CHIP, rows // tr),
            in_specs=[pl.BlockSpec((1, tr, cols), lambda k, i, c: (2 * k + c[0], i, 0)),
                      pl.BlockSpec((1, tr, cols), lambda k, i, c: (k, i, 0))],
            out_specs=pl.BlockSpec((1, tr, cols), lambda k, i, c: (k, i, 0))),
        out_shape=jax.ShapeDtypeStruct((N_CHIP, rows, cols), g.dtype),
        compiler_params=_params(2),
    )(c_idx, g, recv)


def _position():
    return lax.axis_index("x"), lax.axis_index("y"), lax.axis_index("c")


def _all_gather(block, name):
    rows, cols = block.shape

    def body(x_ref, out_ref, send_sems, recv_sems, local_sem):
        x, y, c = _position()
        me, sibling = (x, y, c), (x, y, 1 - c)
        chips = [(1 - x, y), (x, 1 - y), (1 - x, 1 - y)]

        def slot(px, py, pc):
            return out_ref.at[4 * px + 2 * py + pc]

        def copy(k, block_of, to, src=None):
            return pltpu.make_async_remote_copy(
                src_ref=slot(*block_of) if src is None else src, dst_ref=slot(*block_of),
                send_sem=send_sems.at[k], recv_sem=recv_sems.at[k], device_id=to, device_id_type=MESH)

        mine = pltpu.make_async_copy(x_ref, slot(*me), local_sem)
        mine.start()
        first = [copy(0, me, sibling, src=x_ref)]
        first += [copy(1 + j, me, (*chip, c), src=x_ref) for j, chip in enumerate(chips)]
        for cp in first:
            cp.start()
        passed = [copy(4 + j, (*chip, c), sibling) for j, chip in enumerate(chips)]
        for j, chip in enumerate(chips):
            copy(1 + j, (*chip, c), me).wait_recv()
            passed[j].start()
        copy(0, sibling, me).wait_recv()
        for j, chip in enumerate(chips):
            copy(4 + j, (*chip, 1 - c), me).wait_recv()
        for cp in first + passed:
            cp.wait_send()
        mine.wait()

    return pl.pallas_call(
        body, name=name,
        in_specs=[ANY], out_specs=ANY,
        out_shape=jax.ShapeDtypeStruct((N_DEV, rows, cols), block.dtype),
        scratch_shapes=[pltpu.SemaphoreType.DMA((7,)), pltpu.SemaphoreType.DMA((7,)), pltpu.SemaphoreType.DMA],
    )(block)


def _sibling_exchange(g):
    _, rows, cols = g.shape

    def body(g_ref, recv_ref, send_sems, recv_sems):
        x, y, c = _position()
        copies = [pltpu.make_async_remote_copy(
            src_ref=g_ref.at[2 * k + (1 - c)], dst_ref=recv_ref.at[k], send_sem=send_sems.at[k],
            recv_sem=recv_sems.at[k], device_id=(x, y, 1 - c), device_id_type=MESH) for k in range(N_CHIP)]
        for cp in copies:
            cp.start()
        for cp in copies:
            cp.wait()

    return pl.pallas_call(
        body, name="sibling_exchange",
        in_specs=[ANY], out_specs=ANY,
        out_shape=jax.ShapeDtypeStruct((N_CHIP, rows, cols), g.dtype),
        scratch_shapes=[pltpu.SemaphoreType.DMA((N_CHIP,)), pltpu.SemaphoreType.DMA((N_CHIP,))],
    )(g)


def _chip_exchange(p):
    _, rows, cols = p.shape

    def body(p_ref, recv_ref, send_sems, recv_sems, local_sem):
        x, y, c = _position()
        k_me = 2 * x + y
        mine = pltpu.make_async_copy(p_ref.at[k_me], recv_ref.at[k_me], local_sem)
        mine.start()
        copies = []
        for j, (px, py) in enumerate([(1 - x, y), (x, 1 - y), (1 - x, 1 - y)]):
            copies.append(pltpu.make_async_remote_copy(
                src_ref=p_ref.at[2 * px + py], dst_ref=recv_ref.at[k_me], send_sem=send_sems.at[j],
                recv_sem=recv_sems.at[j], device_id=(px, py, c), device_id_type=MESH))
        for cp in copies:
            cp.start()
        for cp in copies:
            cp.wait()
        mine.wait()

    return pl.pallas_call(
        body, name="chip_exchange",
        in_specs=[ANY], out_specs=ANY,
        out_shape=jax.ShapeDtypeStruct((N_CHIP, rows, cols), p.dtype),
        scratch_shapes=[pltpu.SemaphoreType.DMA((3,)), pltpu.SemaphoreType.DMA((3,)), pltpu.SemaphoreType.DMA],
    )(p)


def _reduce_scatter(g):
    c_idx = lax.axis_index("c").astype(jnp.int32).reshape(1)
    pair = _pair_sum(g, _sibling_exchange(g), c_idx)
    return _sum_leading(_chip_exchange(pair), "chip_sum")


_TRANSPOSED = ("even_w_in", "odd_w_in", "ffn_w_gate", "ffn_w_up")


def _pack_shards(even_w_in, even_w_out, odd_w_in, odd_w_out, wg, wu, wd):
    parts = [even_w_in[0].T, even_w_out[0], odd_w_in[0].T, odd_w_out[0]]
    for layer in range(2):
        parts += [wg[layer].T, wu[layer].T, wd[layer]]
    return parts


def _pack_rows(w):
    return w.reshape(N_DEV, -1, D_MODEL)


def kernel(x, even_w_in, even_w_out, a_w_s, a_b_s, a_ln_g, a_ln_b, b_w_pool, b_scale, odd_w_in, odd_w_out, c_w_dw, c_b_dw, c_ln_g, c_ln_b, d_w_dw, norm_mix_g, norm_ffn_g, ffn_w_gate, ffn_w_up, ffn_w_down, final_norm_g, loss_target, m_even_w_in, m_even_w_out, m_a_w_s, m_a_b_s, m_a_ln_g, m_a_ln_b, m_b_w_pool, m_b_scale, m_odd_w_in, m_odd_w_out, m_c_w_dw, m_c_b_dw, m_c_ln_g, m_c_ln_b, m_d_w_dw, m_norm_mix_g, m_norm_ffn_g, m_ffn_w_gate, m_ffn_w_up, m_ffn_w_down, m_final_norm_g, v_even_w_in, v_even_w_out, v_a_w_s, v_a_b_s, v_a_ln_g, v_a_ln_b, v_b_w_pool, v_b_scale, v_odd_w_in, v_odd_w_out, v_c_w_dw, v_c_b_dw, v_c_ln_g, v_c_ln_b, v_d_w_dw, v_norm_mix_g, v_norm_ffn_g, v_ffn_w_gate, v_ffn_w_up, v_ffn_w_down, v_final_norm_g):
    weights = dict(even_w_in=even_w_in, even_w_out=even_w_out, a_w_s=a_w_s, a_b_s=a_b_s, a_ln_g=a_ln_g, a_ln_b=a_ln_b,
                   b_w_pool=b_w_pool, b_scale=b_scale, odd_w_in=odd_w_in, odd_w_out=odd_w_out, c_w_dw=c_w_dw,
                   c_b_dw=c_b_dw, c_ln_g=c_ln_g, c_ln_b=c_ln_b, d_w_dw=d_w_dw, norm_mix_g=norm_mix_g,
                   norm_ffn_g=norm_ffn_g, ffn_w_gate=ffn_w_gate, ffn_w_up=ffn_w_up, ffn_w_down=ffn_w_down,
                   final_norm_g=final_norm_g)
    m_in = dict(even_w_in=m_even_w_in, even_w_out=m_even_w_out, a_w_s=m_a_w_s, a_b_s=m_a_b_s, a_ln_g=m_a_ln_g,
                a_ln_b=m_a_ln_b, b_w_pool=m_b_w_pool, b_scale=m_b_scale, odd_w_in=m_odd_w_in, odd_w_out=m_odd_w_out,
                c_w_dw=m_c_w_dw, c_b_dw=m_c_b_dw, c_ln_g=m_c_ln_g, c_ln_b=m_c_ln_b, d_w_dw=m_d_w_dw,
                norm_mix_g=m_norm_mix_g, norm_ffn_g=m_norm_ffn_g, ffn_w_gate=m_ffn_w_gate, ffn_w_up=m_ffn_w_up,
                ffn_w_down=m_ffn_w_down, final_norm_g=m_final_norm_g)
    v_in = dict(even_w_in=v_even_w_in, even_w_out=v_even_w_out, a_w_s=v_a_w_s, a_b_s=v_a_b_s, a_ln_g=v_a_ln_g,
                a_ln_b=v_a_ln_b, b_w_pool=v_b_w_pool, b_scale=v_b_scale, odd_w_in=v_odd_w_in, odd_w_out=v_odd_w_out,
                c_w_dw=v_c_w_dw, c_b_dw=v_c_b_dw, c_ln_g=v_c_ln_g, c_ln_b=v_c_ln_b, d_w_dw=v_d_w_dw,
                norm_mix_g=v_norm_mix_g, norm_ffn_g=v_norm_ffn_g, ffn_w_gate=v_ffn_w_gate, ffn_w_up=v_ffn_w_up,
                ffn_w_down=v_ffn_w_down, final_norm_g=v_final_norm_g)
    names = list(weights)

    parts = _pack_shards(even_w_in, even_w_out, odd_w_in, odd_w_out, ffn_w_gate, ffn_w_up, ffn_w_down)
    part_rows = [p.shape[0] for p in parts]
    offs = [sum(part_rows[:k]) for k in range(len(parts) + 1)]
    big = _all_gather(jnp.concatenate(parts, axis=0).astype(bf16), "gather_weights")

    def gathered(k):
        return big[:, offs[k]:offs[k + 1], :].reshape(-1, D_MODEL)

    w_in_e, w_out_e, w_in_o, w_out_o = [gathered(k) for k in range(4)]
    w_gate = [gathered(4 + 3 * l) for l in range(2)]
    w_up = [gathered(5 + 3 * l) for l in range(2)]
    w_down = [gathered(6 + 3 * l) for l in range(2)]

    conv_names = ["c_w_dw", "c_b_dw", "c_ln_g", "c_ln_b", "d_w_dw"]
    conv_rows = [C_KERNEL, 1, 1, 1, D_KERNEL]
    conv_local = jnp.concatenate([weights[n].reshape(r, -1) for n, r in zip(conv_names, conv_rows)]
                                 + [jnp.zeros((3, c_b_dw.shape[-1]), f32)], axis=0)
    conv_all = _all_gather(conv_local, "gather_conv").transpose(1, 0, 2).reshape(conv_local.shape[0], -1)
    conv_offs = [sum(conv_rows[:k]) for k in range(len(conv_rows) + 1)]
    cw, cb, clg, clb, dw = [conv_all[conv_offs[k]:conv_offs[k + 1]] for k in range(len(conv_rows))]

    ws, bst = a_w_s[0], a_b_s[0].T
    lng, lnb, wp, sc = a_ln_g, a_ln_b, b_w_pool[0], b_scale
    gmix = [norm_mix_g[l:l + 1] for l in range(2)]
    gffn = [norm_ffn_g[l:l + 1] for l in range(2)]
    gfin = final_norm_g.reshape(1, D_MODEL)

    h0 = x[0]
    h1, hn_e, za, pooled, mix_e = _even_fwd(h0, w_in_e, w_out_e, ws, bst, lng, lnb, wp, sc, gmix[0])
    h2, hn_f0, gate0, up0 = _ffn_fwd(h1, w_gate[0], w_up[0], w_down[0], gffn[0])
    h3, hn_o, z_o, mix_o = _odd_fwd(h2, w_in_o, w_out_o, cw, cb, clg, clb, dw, gmix[1])
    h4, hn_f1, gate1, up1 = _ffn_fwd(h3, w_gate[1], w_up[1], w_down[1], gffn[1])

    dh4, loss_local, g_final = _loss_head(h4, loss_target[0], gfin)
    dh3, dgate1, dup1, act1, g_ffn1 = _ffn_bwd(dh4, h3, gate1, up1, w_gate[1], w_up[1], w_down[1], gffn[1])
    dh2, dz_o, g_cw, g_cb, g_clg, g_clb, g_dw, g_mix1 = _odd_bwd(dh3, h2, z_o, w_in_o, w_out_o, cw, cb, clg, clb, dw, gmix[1])
    dh1, dgate0, dup0, act0, g_ffn0 = _ffn_bwd(dh2, h1, gate0, up0, w_gate[0], w_up[0], w_down[0], gffn[0])
    dh0, dz_e, g_ws, g_bs, g_lng, g_lnb, g_wp, g_sc, g_mix0 = _even_bwd(
        dh1, h0, za, pooled, w_in_e, w_out_e, ws, bst, lng, lnb, wp, sc, gmix[0])

    full = [_weight_grads([(dz_e, hn_e)], "dw_even_in")[0], _weight_grads([(mix_e, dh1)], "dw_even_out")[0],
            _weight_grads([(dz_o, hn_o)], "dw_odd_in")[0], _weight_grads([(mix_o, dh3)], "dw_odd_out")[0]]
    for hn_f, dgate, dup, act, dh_out, l in ((hn_f0, dgate0, dup0, act0, dh2, 0), (hn_f1, dgate1, dup1, act1, dh4, 1)):
        full += _weight_grads([(dgate, hn_f), (dup, hn_f), (act, dh_out)], f"dw_ffn{l}")
    g_big = _reduce_scatter(jnp.concatenate([_pack_rows(g) for g in full], axis=1))

    lanes = HEAD
    small = [("a_w_s", g_ws), ("a_b_s", g_bs), ("a_ln_g", g_lng), ("a_ln_b", g_lnb), ("b_w_pool", g_wp),
             ("b_scale", g_sc), ("norm_mix_g", jnp.concatenate([g_mix0, g_mix1], axis=0)),
             ("norm_ffn_g", jnp.concatenate([g_ffn0, g_ffn1], axis=0)), ("final_norm_g", g_final),
             ("c_w_dw", g_cw), ("c_b_dw", g_cb), ("c_ln_g", g_clg), ("c_ln_b", g_clb), ("d_w_dw", g_dw)]
    small_rows = [-(-g.size // (8 * lanes)) * 8 for _, g in small]
    small_offs = [sum(small_rows[:k]) for k in range(len(small) + 1)]
    pad_rows = -small_offs[-1] % 256
    small_buf = jnp.concatenate(
        [jnp.pad(g.reshape(-1), (0, r * lanes - g.size)).reshape(r, lanes) for (_, g), r in zip(small, small_rows)]
        + [jnp.zeros((pad_rows, lanes), f32)], axis=0)
    small_sum = _sum_leading(_all_gather(small_buf, "gather_small_grads"), "small_grad_sum")
    grads = {}
    for k, (n, g) in enumerate(small):
        grads[n] = small_sum[small_offs[k]:small_offs[k + 1]].reshape(-1)[:g.size].reshape(g.shape)
    me = 4 * lax.axis_index("x") + 2 * lax.axis_index("y") + lax.axis_index("c")
    shard = c_b_dw.shape[-1]
    for n in conv_names:
        grads[n] = lax.dynamic_slice_in_dim(grads[n], me * shard, shard, axis=1)

    def big_grad(k, name):
        g = g_big[offs[k]:offs[k + 1]]
        return g.T if name in _TRANSPOSED else g

    for k, n in enumerate(["even_w_in", "even_w_out", "odd_w_in", "odd_w_out"]):
        grads[n] = big_grad(k, n)
    for k, n in enumerate(["ffn_w_gate", "ffn_w_up", "ffn_w_down"]):
        grads[n] = jnp.stack([big_grad(4 + k + 3 * l, n) for l in range(2)])
    grads = {n: grads[n].reshape(weights[n].shape) for n in names}

    delta, new_m, new_v = {}, {}, {}
    for n in names:
        shape = weights[n].shape
        view = (-1, shape[-1])
        d, m2, v2 = _adamw(weights[n].reshape(view), grads[n].reshape(view), m_in[n].reshape(view),
                           v_in[n].reshape(view), "adamw_" + n)
        delta[n], new_m[n], new_v[n] = d.reshape(shape), m2.reshape(shape), v2.reshape(shape)

    loss = lax.psum(loss_local[0, 0], ("x", "y", "c"))
    return (loss, dh0[None], *[grads[n] for n in names], *[delta[n] for n in names],
            *[new_m[n] for n in names], *[new_v[n] for n in names])
```

```python
import jax
import jax.numpy as jnp
from jax import lax
from jax.experimental import pallas as pl
from jax.experimental.pallas import tpu as pltpu

f32 = jnp.float32
bf16 = jnp.bfloat16

EPS = 1e-6
D_MODEL = 1024
A_WIDTH = 512
HEAD = 128
N_HEADS = 4
CHUNK = 64
POOL_WINDOWS = (2, 4, 8, 16)
POOL_HALO = 16
C_KERNEL = 31
D_KERNEL = 3
CONV_HALO = 32
D_FF = 2816
N_DEV = 8
N_CHIP = 4

ADAM_LR = 0.001
ADAM_B1 = 0.9
ADAM_B2 = 0.999
ADAM_EPS = 1e-08
ADAM_WD = 0.01
ADAM_STEP = 10

MIX_TILE = 512
FFN_TILE = 512
FFN_BWD_TILE = 256
FFN_CHUNK = 1408
DW_TK = 256
DW_TM = 1536
VMEM_LIMIT = 56 * 1024 * 1024

MESH = pl.DeviceIdType.MESH
ANY = pl.BlockSpec(memory_space=pl.ANY)


def _params(n_axes):
    return pltpu.CompilerParams(dimension_semantics=("arbitrary",) * n_axes, vmem_limit_bytes=VMEM_LIMIT)


def _mm(a, b):
    return jnp.dot(a, b, preferred_element_type=f32)


def _mm_nt(a, b):
    return lax.dot_general(a, b, (((1,), (1,)), ((), ())), preferred_element_type=f32)


def _mm_tn(a, b):
    return lax.dot_general(a, b, (((0,), (0,)), ((), ())), preferred_element_type=f32)


def _sigmoid(x):
    return 1.0 / (1.0 + jnp.exp(-x))


def _rms_r(h):
    return lax.rsqrt(jnp.mean(h * h, axis=-1, keepdims=True) + EPS)


def _rms_bwd(dy, h, g):
    r = _rms_r(h)
    xh = h * r
    dxh = dy * g
    dh = r * (dxh - xh * jnp.mean(dxh * xh, axis=-1, keepdims=True))
    return dh, jnp.sum(dy * xh, axis=0, keepdims=True)


def _ln_fwd(x, g, b):
    mu = jnp.mean(x, axis=-1, keepdims=True)
    xc = x - mu
    r = lax.rsqrt(jnp.mean(xc * xc, axis=-1, keepdims=True) + EPS)
    xh = xc * r
    return xh * g + b, xh, r


def _ln_bwd(dy, xh, r, g):
    dxh = dy * g
    return r * (dxh - jnp.mean(dxh, axis=-1, keepdims=True) - xh * jnp.mean(dxh * xh, axis=-1, keepdims=True))


_GELU_C = 0.7978845608028654
_GELU_A = 0.044715


def _gelu(x):
    th = jnp.tanh(_GELU_C * (x + _GELU_A * x * x * x))
    return 0.5 * x * (1.0 + th), th


def _gelu_grad(x, th):
    return 0.5 * (1.0 + th) + 0.5 * x * (1.0 - th * th) * (_GELU_C * (1.0 + 3.0 * _GELU_A * x * x))


def _down(x, k):
    return x if k == 0 else pltpu.roll(x, k, 0)


def _up(x, k):
    return x if k == 0 else pltpu.roll(x, x.shape[0] - k, 0)


def _window_sum(x, win, shift):
    s = x
    step = 1
    while step < win:
        s = s + shift(s, step)
        step *= 2
    return s


def _inv_count(t0, rows, win):
    t = t0 + lax.broadcasted_iota(jnp.int32, (rows, 1), 0)
    return 1.0 / jnp.minimum(t + 1, win).astype(f32)


def _chunk_mask():
    i = lax.broadcasted_iota(jnp.int32, (HEAD, HEAD), 0)
    j = lax.broadcasted_iota(jnp.int32, (HEAD, HEAD), 1)
    return jnp.logical_or(i >= CHUNK, j < CHUNK)


def _const(shape, n_axes):
    zeros = (0,) * len(shape)
    if n_axes == 1:
        return pl.BlockSpec(shape, lambda i: zeros)
    return pl.BlockSpec(shape, lambda i, j: zeros)


def _prev_halo(tile, halo, cols):
    return pl.BlockSpec((halo, cols), lambda i: (jnp.maximum(i * (tile // halo) - 1, 0), 0))


def _next_halo(tile, halo, cols, seq):
    return pl.BlockSpec((halo, cols), lambda i: (jnp.minimum((i + 1) * (tile // halo), seq // halo - 1), 0))


class _Job:
    def __init__(self, inputs, out_shape, sems, hooks):
        self.inputs, self.out_shape, self.sems, self.hooks = inputs, out_shape, sems, hooks


def _position():
    return lax.axis_index("x"), lax.axis_index("y"), lax.axis_index("c")


def _all_gather_job(block):
    rows, cols = block.shape

    def hooks(ins, outs, sems):
        (x_ref,), (out_ref,), (send_sems, recv_sems, local_sem) = ins, outs, sems
        x, y, c = _position()
        me, sibling = (x, y, c), (x, y, 1 - c)
        chips = [(1 - x, y), (x, 1 - y), (1 - x, 1 - y)]

        def slot(px, py, pc):
            return out_ref.at[4 * px + 2 * py + pc]

        def copy(k, block_of, to, src=None):
            return pltpu.make_async_remote_copy(
                src_ref=slot(*block_of) if src is None else src, dst_ref=slot(*block_of),
                send_sem=send_sems.at[k], recv_sem=recv_sems.at[k], device_id=to, device_id_type=MESH)

        mine = pltpu.make_async_copy(x_ref, slot(*me), local_sem)
        first = [copy(0, me, sibling, src=x_ref)]
        first += [copy(1 + j, me, (*chip, c), src=x_ref) for j, chip in enumerate(chips)]
        passed = [copy(4 + j, (*chip, c), sibling) for j, chip in enumerate(chips)]

        def start():
            mine.start()
            for cp in first:
                cp.start()

        def middle():
            for j, chip in enumerate(chips):
                copy(1 + j, (*chip, c), me).wait_recv()
                passed[j].start()

        def finish():
            copy(0, sibling, me).wait_recv()
            for j, chip in enumerate(chips):
                copy(4 + j, (*chip, 1 - c), me).wait_recv()
            for cp in first + passed:
                cp.wait_send()
            mine.wait()

        return start, middle, finish

    return _Job([block], [jax.ShapeDtypeStruct((N_DEV, rows, cols), block.dtype)],
                [pltpu.SemaphoreType.DMA((7,)), pltpu.SemaphoreType.DMA((7,)), pltpu.SemaphoreType.DMA], hooks)


def _sibling_exchange_job(g):
    _, rows, cols = g.shape

    def hooks(ins, outs, sems):
        (g_ref,), (recv_ref,), (send_sems, recv_sems) = ins, outs, sems
        x, y, c = _position()
        copies = [pltpu.make_async_remote_copy(
            src_ref=g_ref.at[2 * k + (1 - c)], dst_ref=recv_ref.at[k], send_sem=send_sems.at[k],
            recv_sem=recv_sems.at[k], device_id=(x, y, 1 - c), device_id_type=MESH) for k in range(N_CHIP)]

        def start():
            for cp in copies:
                cp.start()

        def finish():
            for cp in copies:
                cp.wait()

        return start, lambda: None, finish

    return _Job([g], [jax.ShapeDtypeStruct((N_CHIP, rows, cols), g.dtype)],
                [pltpu.SemaphoreType.DMA((N_CHIP,)), pltpu.SemaphoreType.DMA((N_CHIP,))], hooks)


def _chip_exchange_job(p):
    _, rows, cols = p.shape

    def hooks(ins, outs, sems):
        (p_ref,), (recv_ref,), (send_sems, recv_sems, local_sem) = ins, outs, sems
        x, y, c = _position()
        k_me = 2 * x + y
        mine = pltpu.make_async_copy(p_ref.at[k_me], recv_ref.at[k_me], local_sem)
        copies = [pltpu.make_async_remote_copy(
            src_ref=p_ref.at[2 * px + py], dst_ref=recv_ref.at[k_me], send_sem=send_sems.at[j],
            recv_sem=recv_sems.at[j], device_id=(px, py, c), device_id_type=MESH)
            for j, (px, py) in enumerate([(1 - x, y), (x, 1 - y), (1 - x, 1 - y)])]

        def start():
            mine.start()
            for cp in copies:
                cp.start()

        def finish():
            for cp in copies:
                cp.wait()
            mine.wait()

        return start, lambda: None, finish

    return _Job([p], [jax.ShapeDtypeStruct((N_CHIP, rows, cols), p.dtype)],
                [pltpu.SemaphoreType.DMA((3,)), pltpu.SemaphoreType.DMA((3,)), pltpu.SemaphoreType.DMA], hooks)


def _job_hooks(jobs, ins, outs, sems):
    hooks = []
    for job in jobs:
        n_in, n_out, n_sem = len(job.inputs), len(job.out_shape), len(job.sems)
        hooks.append(job.hooks(ins[:n_in], outs[:n_out], sems[:n_sem]))
        ins, outs, sems = ins[n_in:], outs[n_out:], sems[n_sem:]
    return hooks


def _run_jobs(jobs, name):
    n_in = sum(len(job.inputs) for job in jobs)
    n_out = sum(len(job.out_shape) for job in jobs)

    def body(*refs):
        hooks = _job_hooks(jobs, refs[:n_in], refs[n_in:n_in + n_out], refs[n_in + n_out:])
        for phase in range(3):
            for h in hooks:
                h[phase]()

    return list(pl.pallas_call(
        body, name=name, in_specs=[ANY] * n_in, out_specs=[ANY] * n_out,
        out_shape=[s for job in jobs for s in job.out_shape],
        scratch_shapes=[s for job in jobs for s in job.sems],
    )(*[a for job in jobs for a in job.inputs]))


def _launch(body, *, name, grid, in_specs, out_specs, out_shape, args, scratch=(), jobs=()):
    in_specs, out_specs, out_shape, scratch = list(in_specs), list(out_specs), list(out_shape), list(scratch)
    if not jobs:
        return list(pl.pallas_call(body, name=name, grid=grid, in_specs=in_specs, out_specs=out_specs,
                                   out_shape=out_shape, scratch_shapes=scratch,
                                   compiler_params=_params(len(grid)))(*args))
    n_in, n_out, n_sc = len(in_specs), len(out_specs), len(scratch)
    j_in = [a for job in jobs for a in job.inputs]
    j_out = [s for job in jobs for s in job.out_shape]
    j_sems = [s for job in jobs for s in job.sems]
    n_steps = 1
    for g in grid:
        n_steps *= g

    def wrapped(*refs):
        ins, refs = refs[:n_in], refs[n_in:]
        jins, refs = refs[:len(j_in)], refs[len(j_in):]
        outs, refs = refs[:n_out], refs[n_out:]
        jouts, refs = refs[:len(j_out)], refs[len(j_out):]
        sc, jsems = refs[:n_sc], refs[n_sc:]
        step = pl.program_id(0)
        for axis in range(1, len(grid)):
            step = step * grid[axis] + pl.program_id(axis)
        hooks = _job_hooks(jobs, jins, jouts, jsems)

        @pl.when(step == 0)
        def _():
            for h in hooks:
                h[0]()

        body(*ins, *outs, *sc)

        @pl.when(step == n_steps // 2)
        def _():
            for h in hooks:
                h[1]()

        @pl.when(step == n_steps - 1)
        def _():
            for h in hooks:
                h[2]()

    return list(pl.pallas_call(
        wrapped, name=name, grid=grid, in_specs=in_specs + [ANY] * len(j_in), out_specs=out_specs + [ANY] * len(j_out),
        out_shape=out_shape + j_out, scratch_shapes=scratch + j_sems, compiler_params=_params(len(grid)),
    )(*args, *j_in))


def _gmlp_gate(vnb, wsm, bst, tile):
    rows = []
    for n in range(tile // HEAD):
        cols = []
        for hh in range(N_HEADS):
            blk = vnb[n * HEAD:(n + 1) * HEAD, hh * HEAD:(hh + 1) * HEAD]
            cols.append(_mm(wsm[hh], blk) + bst[:, hh:hh + 1])
        rows.append(jnp.concatenate(cols, axis=1))
    return jnp.concatenate(rows, axis=0)


def _even_fwd(h, w_in, w_out, ws, bst, lng, lnb, wp, sc, gm, jobs=()):
    seq = h.shape[0]
    tile = min(MIX_TILE, seq)
    n_tiles = seq // tile

    def body(h_ref, hp_ref, win_ref, wout_ref, ws_ref, bst_ref, lng_ref, lnb_ref, wp_ref, sc_ref, g_ref,
             ho_ref, hn_ref, za_ref, pool_ref, mix_ref):
        i = pl.program_id(0)
        g = g_ref[...]
        h = h_ref[...]
        hnb = (h * _rms_r(h) * g).astype(bf16)
        hn_ref[...] = hnb
        z = _mm_nt(hnb, win_ref[...])
        zab = z[:, :2 * A_WIDTH].astype(bf16)
        za_ref[...] = zab
        hp = hp_ref[...]
        zbp = _mm_nt((hp * _rms_r(hp) * g).astype(bf16), win_ref[2 * A_WIDTH:, :])
        zbe = jnp.concatenate([jnp.where(i > 0, zbp, 0.0), z[:, 2 * A_WIDTH:]], axis=0)
        pooled = []
        for gi, win in enumerate(POOL_WINDOWS):
            xg = zbe[:, gi * HEAD:(gi + 1) * HEAD]
            s = _window_sum(xg, win, _down)
            pooled.append(s[POOL_HALO:] * _inv_count(i * tile, tile, win) - xg[POOL_HALO:])
        plb = jnp.concatenate(pooled, axis=1).astype(bf16)
        pool_ref[...] = plb

        ga, _ = _gelu(zab.astype(f32))
        vn, _, _ = _ln_fwd(ga[:, A_WIDTH:], lng_ref[...], lnb_ref[...])
        mask = _chunk_mask()
        wsm = [jnp.where(mask, ws_ref[hh], 0.0).astype(bf16) for hh in range(N_HEADS)]
        ya = ga[:, :A_WIDTH] * _gmlp_gate(vn.astype(bf16), wsm, bst_ref[...], tile)
        yb = jnp.concatenate([_mm(plb[:, gi * HEAD:(gi + 1) * HEAD], wp_ref[gi].astype(bf16))
                              for gi in range(len(POOL_WINDOWS))], axis=1) * sc_ref[...]
        mix = jnp.concatenate([ya, yb], axis=1).astype(bf16)
        mix_ref[...] = mix
        ho_ref[...] = h + _mm(mix, wout_ref[...])

    row = lambda cols: pl.BlockSpec((tile, cols), lambda i: (i, 0))
    return _launch(
        body, name="even_fwd", grid=(n_tiles,), jobs=jobs,
        in_specs=[row(D_MODEL), _prev_halo(tile, POOL_HALO, D_MODEL), _const(w_in.shape, 1), _const(w_out.shape, 1),
                  _const(ws.shape, 1), _const(bst.shape, 1), _const(lng.shape, 1), _const(lnb.shape, 1),
                  _const(wp.shape, 1), _const(sc.shape, 1), _const(gm.shape, 1)],
        out_specs=[row(D_MODEL), row(D_MODEL), row(2 * A_WIDTH), row(A_WIDTH), row(D_MODEL)],
        out_shape=[jax.ShapeDtypeStruct((seq, D_MODEL), f32), jax.ShapeDtypeStruct((seq, D_MODEL), bf16),
                   jax.ShapeDtypeStruct((seq, 2 * A_WIDTH), bf16), jax.ShapeDtypeStruct((seq, A_WIDTH), bf16),
                   jax.ShapeDtypeStruct((seq, D_MODEL), bf16)],
        args=(h, h, w_in, w_out, ws, bst, lng, lnb, wp, sc, gm))


def _even_bwd(dh, h, za, pooled, w_in, w_out, ws, bst, lng, lnb, wp, sc, gm, jobs=()):
    seq = h.shape[0]
    tile = min(MIX_TILE, seq)
    n_tiles = seq // tile
    n_groups = len(POOL_WINDOWS)

    def body(dh_ref, dhx_ref, h_ref, za_ref, pool_ref, win_ref, wout_ref, ws_ref, bst_ref, lng_ref, lnb_ref,
             wp_ref, sc_ref, g_ref,
             dhi_ref, dz_ref, dws_ref, dbs_ref, dlng_ref, dlnb_ref, dwp_ref, dsc_ref, dg_ref):
        i = pl.program_id(0)

        @pl.when(i == 0)
        def _():
            for ref in (dws_ref, dbs_ref, dlng_ref, dlnb_ref, dwp_ref, dsc_ref, dg_ref):
                ref[...] = jnp.zeros_like(ref)

        dh = dh_ref[...]
        dmix = _mm_nt(dh.astype(bf16), wout_ref[...])
        dya = dmix[:, :A_WIDTH]
        dyb = dmix[:, A_WIDTH:]
        dybx = _mm_nt(dhx_ref[...].astype(bf16), wout_ref[A_WIDTH:, :])
        dybx = jnp.where(i < n_tiles - 1, dybx, 0.0)

        za = za_ref[...].astype(f32)
        ga, th = _gelu(za)
        u = ga[:, :A_WIDTH]
        lng = lng_ref[...]
        vn, vh, r = _ln_fwd(ga[:, A_WIDTH:], lng, lnb_ref[...])
        vnb = vn.astype(bf16)
        mask = _chunk_mask()
        wsf = [jnp.where(mask, ws_ref[hh], 0.0) for hh in range(N_HEADS)]
        sv = _gmlp_gate(vnb, [w.astype(bf16) for w in wsf], bst_ref[...], tile)
        du = dya * sv
        dsvb = (dya * u).astype(bf16)
        wst = [w.T.astype(bf16) for w in wsf]
        ones = jnp.ones((8, HEAD), bf16)
        dws = [jnp.zeros((HEAD, HEAD), f32) for _ in range(N_HEADS)]
        dbs = [jnp.zeros((8, HEAD), f32) for _ in range(N_HEADS)]
        rows = []
        for n in range(tile // HEAD):
            cols = []
            for hh in range(N_HEADS):
                blk = dsvb[n * HEAD:(n + 1) * HEAD, hh * HEAD:(hh + 1) * HEAD]
                cols.append(_mm(wst[hh], blk))
                dws[hh] = dws[hh] + _mm_nt(blk, vnb[n * HEAD:(n + 1) * HEAD, hh * HEAD:(hh + 1) * HEAD])
                dbs[hh] = dbs[hh] + _mm_nt(ones, blk)
            rows.append(jnp.concatenate(cols, axis=1))
        dvn = jnp.concatenate(rows, axis=0)
        for hh in range(N_HEADS):
            dws_ref[hh] += jnp.where(mask, dws[hh], 0.0)
            dbs_ref[pl.ds(hh, 1), :] += dbs[hh][0:1, :]
        dlng_ref[...] += jnp.sum(dvn * vh, axis=0, keepdims=True)
        dlnb_ref[...] += jnp.sum(dvn, axis=0, keepdims=True)
        dv = _ln_bwd(dvn, vh, r, lng)
        dza = jnp.concatenate([du, dv], axis=1) * _gelu_grad(za, th)

        plb = pool_ref[...]
        sc = sc_ref[...]
        dzb = []
        dsc = []
        for gi, win in enumerate(POOL_WINDOWS):
            cs = slice(gi * HEAD, (gi + 1) * HEAD)
            wpb = wp_ref[gi].astype(bf16)
            dsc.append(jnp.sum(dyb[:, cs] * _mm(plb[:, cs], wpb), axis=0, keepdims=True))
            dpre = (dyb[:, cs] * sc[:, cs]).astype(bf16)
            dprex = (dybx[:, cs] * sc[:, cs]).astype(bf16)
            dwp_ref[gi] += _mm_tn(plb[:, cs], dpre)
            dpl = _mm_nt(dpre, wpb)
            dple = jnp.concatenate([dpl, _mm_nt(dprex, wpb)], axis=0)
            q = dple * _inv_count(i * tile, tile + POOL_HALO, win)
            dzb.append(_window_sum(q, win, _up)[:tile] - dpl)
        dsc_ref[...] += jnp.concatenate(dsc, axis=1)

        dzf = jnp.concatenate([dza] + dzb, axis=1).astype(bf16)
        dz_ref[...] = dzf
        dhn = _mm(dzf, win_ref[...])
        dhr, dg = _rms_bwd(dhn, h_ref[...], g_ref[...])
        dhi_ref[...] = dh + dhr
        dg_ref[...] += dg

    row = lambda cols: pl.BlockSpec((tile, cols), lambda i: (i, 0))
    small = [ws.shape, (N_HEADS, HEAD), lng.shape, lnb.shape, wp.shape, sc.shape, gm.shape]
    return _launch(
        body, name="even_bwd", grid=(n_tiles,), jobs=jobs,
        in_specs=[row(D_MODEL), _next_halo(tile, POOL_HALO, D_MODEL, seq), row(D_MODEL), row(2 * A_WIDTH), row(A_WIDTH),
                  _const(w_in.shape, 1), _const(w_out.shape, 1), _const(ws.shape, 1), _const(bst.shape, 1),
                  _const(lng.shape, 1), _const(lnb.shape, 1), _const(wp.shape, 1), _const(sc.shape, 1), _const(gm.shape, 1)],
        out_specs=[row(D_MODEL), row(3 * A_WIDTH)] + [_const(s, 1) for s in small],
        out_shape=[jax.ShapeDtypeStruct((seq, D_MODEL), f32), jax.ShapeDtypeStruct((seq, 3 * A_WIDTH), bf16)]
                  + [jax.ShapeDtypeStruct(s, f32) for s in small],
        args=(dh, dh, h, za, pooled, w_in, w_out, ws, bst, lng, lnb, wp, sc, gm))


def _conv_taps(x, w_ref, n_taps, halo, rows):
    acc = None
    for j in range(n_taps):
        term = w_ref[pl.ds(j, 1), :] * _down(x, n_taps - 1 - j)[halo:halo + rows]
        acc = term if acc is None else acc + term
    return acc


def _odd_fwd(h, w_in, w_out, cw, cb, clg, clb, dw, gm):
    seq = h.shape[0]
    tile = min(MIX_TILE, seq)
    n_tiles = seq // tile
    w = A_WIDTH

    def body(h_ref, hp_ref, win_ref, wout_ref, cw_ref, cb_ref, clg_ref, clb_ref, dw_ref, g_ref,
             ho_ref, hn_ref, z_ref, mix_ref):
        i = pl.program_id(0)
        g = g_ref[...]
        h = h_ref[...]
        hnb = (h * _rms_r(h) * g).astype(bf16)
        hn_ref[...] = hnb
        zb = _mm_nt(hnb, win_ref[...]).astype(bf16)
        z_ref[...] = zb
        hp = hp_ref[...]
        zp = _mm_nt((hp * _rms_r(hp) * g).astype(bf16), win_ref[...]).astype(bf16).astype(f32)
        z = zb.astype(f32)
        ze = jnp.concatenate([jnp.where(i > 0, zp, 0.0), z], axis=0)
        hc = ze[:, :w] * _sigmoid(ze[:, w:2 * w])
        cv = _conv_taps(hc, cw_ref, C_KERNEL, CONV_HALO, tile) + cb_ref[...]
        ln, _, _ = _ln_fwd(cv, clg_ref[...], clb_ref[...])
        yc = ln * _sigmoid(ln)
        p = ze[:, 3 * w:4 * w] * ze[:, 4 * w:]
        yd = z[:, 2 * w:3 * w] * _conv_taps(p, dw_ref, D_KERNEL, CONV_HALO, tile)
        mix = jnp.concatenate([yc, yd], axis=1).astype(bf16)
        mix_ref[...] = mix
        ho_ref[...] = h + _mm(mix, wout_ref[...])

    row = lambda cols: pl.BlockSpec((tile, cols), lambda i: (i, 0))
    return pl.pallas_call(
        body, name="odd_fwd", grid=(n_tiles,),
        in_specs=[row(D_MODEL), _prev_halo(tile, CONV_HALO, D_MODEL), _const(w_in.shape, 1), _const(w_out.shape, 1),
                  _const(cw.shape, 1), _const(cb.shape, 1), _const(clg.shape, 1), _const(clb.shape, 1),
                  _const(dw.shape, 1), _const(gm.shape, 1)],
        out_specs=[row(D_MODEL), row(D_MODEL), row(5 * w), row(D_MODEL)],
        out_shape=[jax.ShapeDtypeStruct((seq, D_MODEL), f32), jax.ShapeDtypeStruct((seq, D_MODEL), bf16),
                   jax.ShapeDtypeStruct((seq, 5 * w), bf16), jax.ShapeDtypeStruct((seq, D_MODEL), bf16)],
        compiler_params=_params(1),
    )(h, h, w_in, w_out, cw, cb, clg, clb, dw, gm)


def _odd_bwd(dh, h, z, w_in, w_out, cw, cb, clg, clb, dw, gm, jobs=()):
    seq = h.shape[0]
    tile = min(MIX_TILE, seq)
    n_tiles = seq // tile
    w = A_WIDTH
    halo = CONV_HALO
    ext = tile + halo

    def body(dh_ref, dhx_ref, h_ref, z_ref, zp_ref, zx_ref, win_ref, wout_ref, cw_ref, cb_ref, clg_ref, clb_ref,
             dw_ref, g_ref,
             dhi_ref, dz_ref, dcw_ref, dcb_ref, dclg_ref, dclb_ref, ddw_ref, dg_ref):
        i = pl.program_id(0)

        @pl.when(i == 0)
        def _():
            for ref in (dcw_ref, dcb_ref, dclg_ref, dclb_ref, ddw_ref, dg_ref):
                ref[...] = jnp.zeros_like(ref)

        dh = dh_ref[...]
        dhe = jnp.concatenate([dh, jnp.where(i < n_tiles - 1, dhx_ref[...], 0.0)], axis=0)
        dmix = _mm_nt(dhe.astype(bf16), wout_ref[...])
        ze = jnp.concatenate([jnp.where(i > 0, zp_ref[...].astype(f32), 0.0), z_ref[...].astype(f32),
                              zx_ref[...].astype(f32)], axis=0)

        sg = _sigmoid(ze[:, w:2 * w])
        ca = ze[:, :w]
        hc = ca * sg
        cv = _conv_taps(hc, cw_ref, C_KERNEL, halo, ext) + cb_ref[...]
        clg = clg_ref[...]
        ln, xh, r = _ln_fwd(cv, clg, clb_ref[...])
        sl = _sigmoid(ln)
        dln = dmix[:, :w] * (sl * (1.0 + ln * (1.0 - sl)))
        dclg_ref[...] += jnp.sum((dln * xh)[:tile], axis=0, keepdims=True)
        dclb_ref[...] += jnp.sum(dln[:tile], axis=0, keepdims=True)
        dcv = _ln_bwd(dln, xh, r, clg)
        dcb_ref[...] += jnp.sum(dcv[:tile], axis=0, keepdims=True)
        dhc = None
        for j in range(C_KERNEL):
            k = C_KERNEL - 1 - j
            dcw_ref[pl.ds(j, 1), :] += jnp.sum(dcv[:tile] * _down(hc, k)[halo:halo + tile], axis=0, keepdims=True)
            term = cw_ref[pl.ds(j, 1), :] * _up(dcv, k)[:tile]
            dhc = term if dhc is None else dhc + term
        sgt = sg[halo:halo + tile]
        cat = ca[halo:halo + tile]
        dca = dhc * sgt
        dcg = dhc * cat * sgt * (1.0 - sgt)

        dcgv = ze[:, 3 * w:4 * w]
        dxin = ze[:, 4 * w:]
        p = dcgv * dxin
        q = _conv_taps(p, dw_ref, D_KERNEL, halo, tile)
        dyd = dmix[:, w:]
        dq = dyd * ze[halo:, 2 * w:3 * w]
        ddbg = dyd[:tile] * q
        dp = None
        for j in range(D_KERNEL):
            k = D_KERNEL - 1 - j
            ddw_ref[pl.ds(j, 1), :] += jnp.sum(dq[:tile] * _down(p, k)[halo:halo + tile], axis=0, keepdims=True)
            term = dw_ref[pl.ds(j, 1), :] * _up(dq, k)[:tile]
            dp = term if dp is None else dp + term
        ddcg = dp * dxin[halo:halo + tile]
        ddxin = dp * dcgv[halo:halo + tile]

        dzf = jnp.concatenate([dca, dcg, ddbg, ddcg, ddxin], axis=1).astype(bf16)
        dz_ref[...] = dzf
        dhn = _mm(dzf, win_ref[...])
        dhr, dg = _rms_bwd(dhn, h_ref[...], g_ref[...])
        dhi_ref[...] = dh + dhr
        dg_ref[...] += dg

    row = lambda cols: pl.BlockSpec((tile, cols), lambda i: (i, 0))
    small = [cw.shape, cb.shape, clg.shape, clb.shape, dw.shape, gm.shape]
    return _launch(
        body, name="odd_bwd", grid=(n_tiles,), jobs=jobs,
        in_specs=[row(D_MODEL), _next_halo(tile, halo, D_MODEL, seq), row(D_MODEL), row(5 * w),
                  _prev_halo(tile, halo, 5 * w), _next_halo(tile, halo, 5 * w, seq),
                  _const(w_in.shape, 1), _const(w_out.shape, 1), _const(cw.shape, 1), _const(cb.shape, 1),
                  _const(clg.shape, 1), _const(clb.shape, 1), _const(dw.shape, 1), _const(gm.shape, 1)],
        out_specs=[row(D_MODEL), row(5 * w)] + [_const(s, 1) for s in small],
        out_shape=[jax.ShapeDtypeStruct((seq, D_MODEL), f32), jax.ShapeDtypeStruct((seq, 5 * w), bf16)]
                  + [jax.ShapeDtypeStruct(s, f32) for s in small],
        args=(dh, dh, h, z, z, z, w_in, w_out, cw, cb, clg, clb, dw, gm))


def _ffn_fwd(h, wg, wu, wd, gm, jobs=()):
    seq = h.shape[0]
    tile = min(FFN_TILE, seq)
    n_tiles = seq // tile
    n_chunks = D_FF // FFN_CHUNK

    def body(h_ref, g_ref, wg_ref, wu_ref, wd_ref, ho_ref, hn_ref, gate_ref, up_ref, acc_ref, hns_ref):
        j = pl.program_id(1)

        @pl.when(j == 0)
        def _():
            h = h_ref[...]
            hnb = (h * _rms_r(h) * g_ref[...]).astype(bf16)
            hns_ref[...] = hnb
            hn_ref[...] = hnb
            acc_ref[...] = jnp.zeros_like(acc_ref)

        hnb = hns_ref[...]
        gb = _mm_nt(hnb, wg_ref[...]).astype(bf16)
        ub = _mm_nt(hnb, wu_ref[...]).astype(bf16)
        gate_ref[...] = gb
        up_ref[...] = ub
        gf = gb.astype(f32)
        act = gf * _sigmoid(gf) * ub.astype(f32)
        acc_ref[...] += _mm(act.astype(bf16), wd_ref[...])

        @pl.when(j == n_chunks - 1)
        def _():
            ho_ref[...] = h_ref[...] + acc_ref[...]

    row = pl.BlockSpec((tile, D_MODEL), lambda i, j: (i, 0))
    col = pl.BlockSpec((tile, FFN_CHUNK), lambda i, j: (i, j))
    wblk = pl.BlockSpec((FFN_CHUNK, D_MODEL), lambda i, j: (j, 0))
    return _launch(
        body, name="ffn_fwd", grid=(n_tiles, n_chunks), jobs=jobs,
        in_specs=[row, _const(gm.shape, 2), wblk, wblk, wblk],
        out_specs=[row, row, col, col],
        out_shape=[jax.ShapeDtypeStruct((seq, D_MODEL), f32), jax.ShapeDtypeStruct((seq, D_MODEL), bf16),
                   jax.ShapeDtypeStruct((seq, D_FF), bf16), jax.ShapeDtypeStruct((seq, D_FF), bf16)],
        scratch=[pltpu.VMEM((tile, D_MODEL), f32), pltpu.VMEM((tile, D_MODEL), bf16)],
        args=(h, gm, wg, wu, wd))


def _ffn_bwd(dh, h, gate, up, wg, wu, wd, gm, jobs=()):
    seq = h.shape[0]
    tile = min(FFN_BWD_TILE, seq)
    n_tiles = seq // tile
    n_chunks = D_FF // FFN_CHUNK

    def body(dh_ref, h_ref, g_ref, gate_ref, up_ref, wg_ref, wu_ref, wd_ref,
             dhi_ref, dgate_ref, dup_ref, act_ref, dg_ref, acc_ref, dhb_ref):
        i = pl.program_id(0)
        j = pl.program_id(1)

        @pl.when(jnp.logical_and(i == 0, j == 0))
        def _():
            dg_ref[...] = jnp.zeros_like(dg_ref)

        @pl.when(j == 0)
        def _():
            dhb_ref[...] = dh_ref[...].astype(bf16)
            acc_ref[...] = jnp.zeros_like(acc_ref)

        dact = _mm_nt(dhb_ref[...], wd_ref[...])
        gf = gate_ref[...].astype(f32)
        uf = up_ref[...].astype(f32)
        s = _sigmoid(gf)
        silu = gf * s
        act_ref[...] = (silu * uf).astype(bf16)
        dgb = (dact * uf * (s * (1.0 + gf * (1.0 - s)))).astype(bf16)
        dub = (dact * silu).astype(bf16)
        dgate_ref[...] = dgb
        dup_ref[...] = dub
        acc_ref[...] += _mm(dgb, wg_ref[...]) + _mm(dub, wu_ref[...])

        @pl.when(j == n_chunks - 1)
        def _():
            dhr, dg = _rms_bwd(acc_ref[...], h_ref[...], g_ref[...])
            dhi_ref[...] = dh_ref[...] + dhr
            dg_ref[...] += dg

    row = pl.BlockSpec((tile, D_MODEL), lambda i, j: (i, 0))
    col = pl.BlockSpec((tile, FFN_CHUNK), lambda i, j: (i, j))
    wblk = pl.BlockSpec((FFN_CHUNK, D_MODEL), lambda i, j: (j, 0))
    return _launch(
        body, name="ffn_bwd", grid=(n_tiles, n_chunks), jobs=jobs,
        in_specs=[row, row, _const(gm.shape, 2), col, col, wblk, wblk, wblk],
        out_specs=[row, col, col, col, _const(gm.shape, 2)],
        out_shape=[jax.ShapeDtypeStruct((seq, D_MODEL), f32), jax.ShapeDtypeStruct((seq, D_FF), bf16),
                   jax.ShapeDtypeStruct((seq, D_FF), bf16), jax.ShapeDtypeStruct((seq, D_FF), bf16),
                   jax.ShapeDtypeStruct(gm.shape, f32)],
        scratch=[pltpu.VMEM((tile, D_MODEL), f32), pltpu.VMEM((tile, D_MODEL), bf16)],
        args=(dh, h, gm, gate, up, wg, wu, wd))


def _loss_head(h, target, gf):
    seq = h.shape[0]
    tile = min(MIX_TILE, seq)

    def body(h_ref, t_ref, g_ref, dh_ref, loss_ref, dg_ref):
        @pl.when(pl.program_id(0) == 0)
        def _():
            loss_ref[...] = jnp.zeros_like(loss_ref)
            dg_ref[...] = jnp.zeros_like(dg_ref)

        h = h_ref[...]
        g = g_ref[...]
        err = h * _rms_r(h) * g - t_ref[...]
        loss_ref[...] += (0.5 / D_MODEL) * jnp.sum(jnp.sum(err * err, axis=1, keepdims=True), axis=0, keepdims=True)
        dhr, dg = _rms_bwd(err * (1.0 / D_MODEL), h, g)
        dh_ref[...] = dhr
        dg_ref[...] += dg

    row = pl.BlockSpec((tile, D_MODEL), lambda i: (i, 0))
    return pl.pallas_call(
        body, name="loss_head", grid=(seq // tile,),
        in_specs=[row, row, _const(gf.shape, 1)],
        out_specs=[row, _const((1, 1), 1), _const(gf.shape, 1)],
        out_shape=[jax.ShapeDtypeStruct((seq, D_MODEL), f32), jax.ShapeDtypeStruct((1, 1), f32),
                   jax.ShapeDtypeStruct(gf.shape, f32)],
        compiler_params=_params(1),
    )(h, target, gf)


def _weight_grads(pairs, name, jobs=()):
    seq, m = pairs[0][0].shape
    tk = min(DW_TK, seq)
    tm = m if m <= DW_TM else m // 2
    n_k = seq // tk
    n_pairs = len(pairs)

    def body(*refs):
        x_refs = refs[0:2 * n_pairs:2]
        y_refs = refs[1:2 * n_pairs:2]
        o_refs = refs[2 * n_pairs:3 * n_pairs]
        acc_refs = refs[3 * n_pairs:]
        k = pl.program_id(1)
        for x_ref, y_ref, o_ref, acc_ref in zip(x_refs, y_refs, o_refs, acc_refs):
            part = _mm_tn(x_ref[...].astype(bf16), y_ref[...].astype(bf16))

            @pl.when(k == 0)
            def _():
                acc_ref[...] = part

            @pl.when(k > 0)
            def _():
                acc_ref[...] += part

            @pl.when(k == n_k - 1)
            def _():
                o_ref[...] = acc_ref[...].astype(bf16)

    in_specs = []
    for _ in pairs:
        in_specs += [pl.BlockSpec((tk, tm), lambda j, k: (k, j)), pl.BlockSpec((tk, D_MODEL), lambda j, k: (k, 0))]
    return _launch(
        body, name=name, grid=(m // tm, n_k), jobs=jobs,
        in_specs=in_specs,
        out_specs=[pl.BlockSpec((tm, D_MODEL), lambda j, k: (j, 0))] * n_pairs,
        out_shape=[jax.ShapeDtypeStruct((m, D_MODEL), bf16)] * n_pairs,
        scratch=[pltpu.VMEM((tm, D_MODEL), f32)] * n_pairs,
        args=[a for pair in pairs for a in pair])


def _row_tile(rows, limit=512):
    best = rows
    for t in range(8, min(rows, limit) + 1, 8):
        if rows % t == 0:
            best = t
    return best if rows > limit else rows


def _adamw(w, g, m, v, name):
    rows, cols = w.shape
    tr = _row_tile(rows)

    def body(w_ref, g_ref, m_ref, v_ref, d_ref, mo_ref, vo_ref):
        g = g_ref[...]
        m2 = ADAM_B1 * m_ref[...] + (1.0 - ADAM_B1) * g
        v2 = ADAM_B2 * v_ref[...] + (1.0 - ADAM_B2) * (g * g)
        m_hat = m2 / (1.0 - ADAM_B1 ** ADAM_STEP)
        v_hat = v2 / (1.0 - ADAM_B2 ** ADAM_STEP)
        d_ref[...] = -ADAM_LR * (m_hat / (jnp.sqrt(v_hat) + ADAM_EPS) + ADAM_WD * w_ref[...])
        mo_ref[...] = m2
        vo_ref[...] = v2

    spec = pl.BlockSpec((tr, cols), lambda i: (i, 0))
    return pl.pallas_call(
        body, name=name, grid=(rows // tr,),
        in_specs=[spec] * 4, out_specs=[spec] * 3,
        out_shape=[jax.ShapeDtypeStruct((rows, cols), f32)] * 3,
        compiler_params=_params(1),
    )(w, g, m, v)


def _sum_leading(x, name):
    n, rows, cols = x.shape
    tr = _row_tile(rows)

    def body(x_ref, o_ref):
        acc = x_ref[0].astype(f32)
        for k in range(1, n):
            acc = acc + x_ref[k].astype(f32)
        o_ref[...] = acc

    return pl.pallas_call(
        body, name=name, grid=(rows // tr,),
        in_specs=[pl.BlockSpec((n, tr, cols), lambda i: (0, i, 0))],
        out_specs=pl.BlockSpec((tr, cols), lambda i: (i, 0)),
        out_shape=jax.ShapeDtypeStruct((rows, cols), f32),
        compiler_params=_params(1),
    )(x)


def _pair_sum(g, recv, c_idx, name):
    _, rows, cols = g.shape
    tr = _row_tile(rows)

    def body(c_ref, g_ref, r_ref, o_ref):
        o_ref[...] = (g_ref[...].astype(f32) + r_ref[...].astype(f32)).astype(o_ref.dtype)

    return pl.pallas_call(
        body, name=name,
        grid_spec=pltpu.PrefetchScalarGridSpec(
            num_scalar_prefetch=1, grid=(N_CHIP, rows // tr),
            in_specs=[pl.BlockSpec((1, tr, cols), lambda k, i, c: (2 * k + c[0], i, 0)),
                      pl.BlockSpec((1, tr, cols), lambda k, i, c: (k, i, 0))],
            out_specs=pl.BlockSpec((1, tr, cols), lambda k, i, c: (k, i, 0))),
        out_shape=jax.ShapeDtypeStruct((N_CHIP, rows, cols), g.dtype),
        compiler_params=_params(2),
    )(c_idx, g, recv)


def _pack_rows(w):
    return w.reshape(N_DEV, -1, D_MODEL)


def kernel(x, even_w_in, even_w_out, a_w_s, a_b_s, a_ln_g, a_ln_b, b_w_pool, b_scale, odd_w_in, odd_w_out, c_w_dw, c_b_dw, c_ln_g, c_ln_b, d_w_dw, norm_mix_g, norm_ffn_g, ffn_w_gate, ffn_w_up, ffn_w_down, final_norm_g, loss_target, m_even_w_in, m_even_w_out, m_a_w_s, m_a_b_s, m_a_ln_g, m_a_ln_b, m_b_w_pool, m_b_scale, m_odd_w_in, m_odd_w_out, m_c_w_dw, m_c_b_dw, m_c_ln_g, m_c_ln_b, m_d_w_dw, m_norm_mix_g, m_norm_ffn_g, m_ffn_w_gate, m_ffn_w_up, m_ffn_w_down, m_final_norm_g, v_even_w_in, v_even_w_out, v_a_w_s, v_a_b_s, v_a_ln_g, v_a_ln_b, v_b_w_pool, v_b_scale, v_odd_w_in, v_odd_w_out, v_c_w_dw, v_c_b_dw, v_c_ln_g, v_c_ln_b, v_d_w_dw, v_norm_mix_g, v_norm_ffn_g, v_ffn_w_gate, v_ffn_w_up, v_ffn_w_down, v_final_norm_g):
    weights = dict(even_w_in=even_w_in, even_w_out=even_w_out, a_w_s=a_w_s, a_b_s=a_b_s, a_ln_g=a_ln_g, a_ln_b=a_ln_b,
                   b_w_pool=b_w_pool, b_scale=b_scale, odd_w_in=odd_w_in, odd_w_out=odd_w_out, c_w_dw=c_w_dw,
                   c_b_dw=c_b_dw, c_ln_g=c_ln_g, c_ln_b=c_ln_b, d_w_dw=d_w_dw, norm_mix_g=norm_mix_g,
                   norm_ffn_g=norm_ffn_g, ffn_w_gate=ffn_w_gate, ffn_w_up=ffn_w_up, ffn_w_down=ffn_w_down,
                   final_norm_g=final_norm_g)
    m_in = dict(even_w_in=m_even_w_in, even_w_out=m_even_w_out, a_w_s=m_a_w_s, a_b_s=m_a_b_s, a_ln_g=m_a_ln_g,
                a_ln_b=m_a_ln_b, b_w_pool=m_b_w_pool, b_scale=m_b_scale, odd_w_in=m_odd_w_in, odd_w_out=m_odd_w_out,
                c_w_dw=m_c_w_dw, c_b_dw=m_c_b_dw, c_ln_g=m_c_ln_g, c_ln_b=m_c_ln_b, d_w_dw=m_d_w_dw,
                norm_mix_g=m_norm_mix_g, norm_ffn_g=m_norm_ffn_g, ffn_w_gate=m_ffn_w_gate, ffn_w_up=m_ffn_w_up,
                ffn_w_down=m_ffn_w_down, final_norm_g=m_final_norm_g)
    v_in = dict(even_w_in=v_even_w_in, even_w_out=v_even_w_out, a_w_s=v_a_w_s, a_b_s=v_a_b_s, a_ln_g=v_a_ln_g,
                a_ln_b=v_a_ln_b, b_w_pool=v_b_w_pool, b_scale=v_b_scale, odd_w_in=v_odd_w_in, odd_w_out=v_odd_w_out,
                c_w_dw=v_c_w_dw, c_b_dw=v_c_b_dw, c_ln_g=v_c_ln_g, c_ln_b=v_c_ln_b, d_w_dw=v_d_w_dw,
                norm_mix_g=v_norm_mix_g, norm_ffn_g=v_norm_ffn_g, ffn_w_gate=v_ffn_w_gate, ffn_w_up=v_ffn_w_up,
                ffn_w_down=v_ffn_w_down, final_norm_g=v_final_norm_g)
    names = list(weights)

    group_parts = {
        "even": [even_w_in[0].T, even_w_out[0]],
        "ffn0": [ffn_w_gate[0].T, ffn_w_up[0].T, ffn_w_down[0]],
        "odd": [odd_w_in[0].T, odd_w_out[0]],
        "ffn1": [ffn_w_gate[1].T, ffn_w_up[1].T, ffn_w_down[1]],
    }
    group_rows = {k: [p.shape[0] for p in parts] for k, parts in group_parts.items()}

    def pack(*groups):
        return jnp.concatenate([p for k in groups for p in group_parts[k]], axis=0).astype(bf16)

    def unpack(gathered_blocks, *groups):
        out, off = [], 0
        for k in groups:
            for r in group_rows[k]:
                out.append(gathered_blocks[:, off:off + r, :].reshape(-1, D_MODEL))
                off += r
        return out

    conv_names = ["c_w_dw", "c_b_dw", "c_ln_g", "c_ln_b", "d_w_dw"]
    conv_rows = [C_KERNEL, 1, 1, 1, D_KERNEL]
    conv_local = jnp.concatenate([weights[n].reshape(r, -1) for n, r in zip(conv_names, conv_rows)]
                                 + [jnp.zeros((3, c_b_dw.shape[-1]), f32)], axis=0)
    big_even, conv_all = _run_jobs([_all_gather_job(pack("even")), _all_gather_job(conv_local)], "gather_even_conv")
    w_in_e, w_out_e = unpack(big_even, "even")
    conv_all = conv_all.transpose(1, 0, 2).reshape(conv_local.shape[0], -1)
    conv_offs = [sum(conv_rows[:k]) for k in range(len(conv_rows) + 1)]
    cw, cb, clg, clb, dw = [conv_all[conv_offs[k]:conv_offs[k + 1]] for k in range(len(conv_rows))]

    ws, bst = a_w_s[0], a_b_s[0].T
    lng, lnb, wp, sc = a_ln_g, a_ln_b, b_w_pool[0], b_scale
    gmix = [norm_mix_g[l:l + 1] for l in range(2)]
    gffn = [norm_ffn_g[l:l + 1] for l in range(2)]
    gfin = final_norm_g.reshape(1, D_MODEL)

    h0 = x[0]
    h1, hn_e, za, pooled, mix_e, big_ffn0 = _even_fwd(
        h0, w_in_e, w_out_e, ws, bst, lng, lnb, wp, sc, gmix[0], jobs=[_all_gather_job(pack("ffn0"))])
    w_gate0, w_up0, w_down0 = unpack(big_ffn0, "ffn0")
    h2, hn_f0, gate0, up0, big_rest = _ffn_fwd(h1, w_gate0, w_up0, w_down0, gffn[0],
                                               jobs=[_all_gather_job(pack("odd", "ffn1"))])
    w_in_o, w_out_o, w_gate1, w_up1, w_down1 = unpack(big_rest, "odd", "ffn1")
    h3, hn_o, z_o, mix_o = _odd_fwd(h2, w_in_o, w_out_o, cw, cb, clg, clb, dw, gmix[1])
    h4, hn_f1, gate1, up1 = _ffn_fwd(h3, w_gate1, w_up1, w_down1, gffn[1])

    c_idx = lax.axis_index("c").astype(jnp.int32).reshape(1)

    def shard_major(gs):
        return jnp.concatenate([_pack_rows(g) for g in gs], axis=1)

    dh4, loss_local, g_final = _loss_head(h4, loss_target[0], gfin)
    dh3, dgate1, dup1, act1, g_ffn1 = _ffn_bwd(dh4, h3, gate1, up1, w_gate1, w_up1, w_down1, gffn[1])
    part_ffn1 = shard_major(_weight_grads([(dgate1, hn_f1), (dup1, hn_f1), (act1, dh4)], "dw_ffn1"))
    dh2, dz_o, g_cw, g_cb, g_clg, g_clb, g_dw, g_mix1, recv_ffn1 = _odd_bwd(
        dh3, h2, z_o, w_in_o, w_out_o, cw, cb, clg, clb, dw, gmix[1], jobs=[_sibling_exchange_job(part_ffn1)])
    pair_ffn1 = _pair_sum(part_ffn1, recv_ffn1, c_idx, "pair_sum_ffn1")
    part_odd = shard_major(_weight_grads([(dz_o, hn_o)], "dw_odd_in") + _weight_grads([(mix_o, dh3)], "dw_odd_out"))
    dh1, dgate0, dup0, act0, g_ffn0, chips_ffn1, recv_odd = _ffn_bwd(
        dh2, h1, gate0, up0, w_gate0, w_up0, w_down0, gffn[0],
        jobs=[_chip_exchange_job(pair_ffn1), _sibling_exchange_job(part_odd)])
    pair_odd = _pair_sum(part_odd, recv_odd, c_idx, "pair_sum_odd")
    *dw_ffn0, chips_odd = _weight_grads([(dgate0, hn_f0), (dup0, hn_f0), (act0, dh2)], "dw_ffn0",
                                        jobs=[_chip_exchange_job(pair_odd)])
    part_ffn0 = shard_major(dw_ffn0)
    recv_ffn0, = _run_jobs([_sibling_exchange_job(part_ffn0)], "sibling_exchange_ffn0")
    pair_ffn0 = _pair_sum(part_ffn0, recv_ffn0, c_idx, "pair_sum_ffn0")
    dh0, dz_e, g_ws, g_bs, g_lng, g_lnb, g_wp, g_sc, g_mix0, chips_ffn0 = _even_bwd(
        dh1, h0, za, pooled, w_in_e, w_out_e, ws, bst, lng, lnb, wp, sc, gmix[0], jobs=[_chip_exchange_job(pair_ffn0)])

    lanes = HEAD
    small = [("a_w_s", g_ws), ("a_b_s", g_bs), ("a_ln_g", g_lng), ("a_ln_b", g_lnb), ("b_w_pool", g_wp),
             ("b_scale", g_sc), ("norm_mix_g", jnp.concatenate([g_mix0, g_mix1], axis=0)),
             ("norm_ffn_g", jnp.concatenate([g_ffn0, g_ffn1], axis=0)), ("final_norm_g", g_final),
             ("c_w_dw", g_cw), ("c_b_dw", g_cb), ("c_ln_g", g_clg), ("c_ln_b", g_clb), ("d_w_dw", g_dw)]
    small_rows = [-(-g.size // (8 * lanes)) * 8 for _, g in small]
    small_offs = [sum(small_rows[:k]) for k in range(len(small) + 1)]
    pad_rows = -small_offs[-1] % 256
    small_buf = jnp.concatenate(
        [jnp.pad(g.reshape(-1), (0, r * lanes - g.size)).reshape(r, lanes) for (_, g), r in zip(small, small_rows)]
        + [jnp.zeros((pad_rows, lanes), f32)], axis=0)
    dw_even_in, small_all = _weight_grads([(dz_e, hn_e)], "dw_even_in", jobs=[_all_gather_job(small_buf)])
    small_sum = _sum_leading(small_all, "small_grad_sum")
    part_even = shard_major([dw_even_in] + _weight_grads([(mix_e, dh1)], "dw_even_out"))
    recv_even, = _run_jobs([_sibling_exchange_job(part_even)], "sibling_exchange_even")
    chips_even, = _run_jobs([_chip_exchange_job(_pair_sum(part_even, recv_even, c_idx, "pair_sum_even"))],
                            "chip_exchange_even")
    group_grads = {"even": _sum_leading(chips_even, "chip_sum_even"), "ffn0": _sum_leading(chips_ffn0, "chip_sum_ffn0"),
                   "odd": _sum_leading(chips_odd, "chip_sum_odd"), "ffn1": _sum_leading(chips_ffn1, "chip_sum_ffn1")}
    grads = {}
    for k, (n, g) in enumerate(small):
        grads[n] = small_sum[small_offs[k]:small_offs[k + 1]].reshape(-1)[:g.size].reshape(g.shape)
    me = 4 * lax.axis_index("x") + 2 * lax.axis_index("y") + lax.axis_index("c")
    shard = c_b_dw.shape[-1]
    for n in conv_names:
        grads[n] = lax.dynamic_slice_in_dim(grads[n], me * shard, shard, axis=1)

    def own_shards(group):
        out, off = [], 0
        for r in group_rows[group]:
            out.append(group_grads[group][off:off + r])
            off += r
        return out

    g_in, grads["even_w_out"] = own_shards("even")
    grads["even_w_in"] = g_in.T
    g_in, grads["odd_w_out"] = own_shards("odd")
    grads["odd_w_in"] = g_in.T
    per_layer = [own_shards("ffn0"), own_shards("ffn1")]
    grads["ffn_w_gate"] = jnp.stack([per_layer[l][0].T for l in range(2)])
    grads["ffn_w_up"] = jnp.stack([per_layer[l][1].T for l in range(2)])
    grads["ffn_w_down"] = jnp.stack([per_layer[l][2] for l in range(2)])
    grads = {n: grads[n].reshape(weights[n].shape) for n in names}

    delta, new_m, new_v = {}, {}, {}
    for n in names:
        shape = weights[n].shape
        view = (-1, shape[-1])
        d, m2, v2 = _adamw(weights[n].reshape(view), grads[n].reshape(view), m_in[n].reshape(view),
                           v_in[n].reshape(view), "adamw_" + n)
        delta[n], new_m[n], new_v[n] = d.reshape(shape), m2.reshape(shape), v2.reshape(shape)

    loss = lax.psum(loss_local[0, 0], ("x", "y", "c"))
    return (loss, dh0[None], *[grads[n] for n in names], *[delta[n] for n in names],
            *[new_m[n] for n in names], *[new_v[n] for n in names])
```

```python
import jax
import jax.numpy as jnp
from jax import lax
from jax.experimental import pallas as pl
from jax.experimental.pallas import tpu as pltpu

f32 = jnp.float32
bf16 = jnp.bfloat16

EPS = 1e-6
D_MODEL = 1024
A_WIDTH = 512
HEAD = 128
N_HEADS = 4
CHUNK = 64
POOL_WINDOWS = (2, 4, 8, 16)
POOL_HALO = 16
C_KERNEL = 31
D_KERNEL = 3
CONV_HALO = 32
D_FF = 2816
N_DEV = 8
N_CHIP = 4

ADAM_LR = 0.001
ADAM_B1 = 0.9
ADAM_B2 = 0.999
ADAM_EPS = 1e-08
ADAM_WD = 0.01
ADAM_STEP = 10

MIX_TILE = 512
FFN_TILE = 512
FFN_BWD_TILE = 256
FFN_CHUNK = 1408
DW_TK = 2048
DW_TM = 1536
MIDDLE_AT, MIDDLE_OF = 7, 8
VMEM_LIMIT = 56 * 1024 * 1024

MESH = pl.DeviceIdType.MESH
ANY = pl.BlockSpec(memory_space=pl.ANY)


def _params(n_axes):
    return pltpu.CompilerParams(dimension_semantics=("arbitrary",) * n_axes, vmem_limit_bytes=VMEM_LIMIT)


def _mm(a, b):
    return jnp.dot(a, b, preferred_element_type=f32)


def _mm_nt(a, b):
    return lax.dot_general(a, b, (((1,), (1,)), ((), ())), preferred_element_type=f32)


def _mm_tn(a, b):
    return lax.dot_general(a, b, (((0,), (0,)), ((), ())), preferred_element_type=f32)


def _sigmoid(x):
    return 1.0 / (1.0 + jnp.exp(-x))


def _rms_r(h):
    return lax.rsqrt(jnp.mean(h * h, axis=-1, keepdims=True) + EPS)


def _rms_bwd(dy, h, g):
    r = _rms_r(h)
    xh = h * r
    dxh = dy * g
    dh = r * (dxh - xh * jnp.mean(dxh * xh, axis=-1, keepdims=True))
    return dh, jnp.sum(dy * xh, axis=0, keepdims=True)


def _ln_fwd(x, g, b):
    mu = jnp.mean(x, axis=-1, keepdims=True)
    xc = x - mu
    r = lax.rsqrt(jnp.mean(xc * xc, axis=-1, keepdims=True) + EPS)
    xh = xc * r
    return xh * g + b, xh, r


def _ln_bwd(dy, xh, r, g):
    dxh = dy * g
    return r * (dxh - jnp.mean(dxh, axis=-1, keepdims=True) - xh * jnp.mean(dxh * xh, axis=-1, keepdims=True))


_GELU_C = 0.7978845608028654
_GELU_A = 0.044715


def _gelu(x):
    th = jnp.tanh(_GELU_C * (x + _GELU_A * x * x * x))
    return 0.5 * x * (1.0 + th), th


def _gelu_grad(x, th):
    return 0.5 * (1.0 + th) + 0.5 * x * (1.0 - th * th) * (_GELU_C * (1.0 + 3.0 * _GELU_A * x * x))


def _down(x, k):
    return x if k == 0 else pltpu.roll(x, k, 0)


def _up(x, k):
    return x if k == 0 else pltpu.roll(x, x.shape[0] - k, 0)


def _window_sum(x, win, shift):
    s = x
    step = 1
    while step < win:
        s = s + shift(s, step)
        step *= 2
    return s


def _inv_count(t0, rows, win):
    t = t0 + lax.broadcasted_iota(jnp.int32, (rows, 1), 0)
    return 1.0 / jnp.minimum(t + 1, win).astype(f32)


def _chunk_mask():
    i = lax.broadcasted_iota(jnp.int32, (HEAD, HEAD), 0)
    j = lax.broadcasted_iota(jnp.int32, (HEAD, HEAD), 1)
    return jnp.logical_or(i >= CHUNK, j < CHUNK)


def _const(shape, n_axes):
    zeros = (0,) * len(shape)
    if n_axes == 1:
        return pl.BlockSpec(shape, lambda i: zeros)
    return pl.BlockSpec(shape, lambda i, j: zeros)


def _prev_halo(tile, halo, cols):
    return pl.BlockSpec((halo, cols), lambda i: (jnp.maximum(i * (tile // halo) - 1, 0), 0))


def _next_halo(tile, halo, cols, seq):
    return pl.BlockSpec((halo, cols), lambda i: (jnp.minimum((i + 1) * (tile // halo), seq // halo - 1), 0))


class _Job:
    def __init__(self, inputs, out_shape, sems, hooks):
        self.inputs, self.out_shape, self.sems, self.hooks = inputs, out_shape, sems, hooks


def _position():
    return lax.axis_index("x"), lax.axis_index("y"), lax.axis_index("c")


def _all_gather_job(block):
    rows, cols = block.shape

    def hooks(ins, outs, sems):
        (x_ref,), (out_ref,), (send_sems, recv_sems, local_sem) = ins, outs, sems
        x, y, c = _position()
        me, sibling = (x, y, c), (x, y, 1 - c)
        chips = [(1 - x, y), (x, 1 - y), (1 - x, 1 - y)]

        def slot(px, py, pc):
            return out_ref.at[4 * px + 2 * py + pc]

        def copy(k, block_of, to, src=None):
            return pltpu.make_async_remote_copy(
                src_ref=slot(*block_of) if src is None else src, dst_ref=slot(*block_of),
                send_sem=send_sems.at[k], recv_sem=recv_sems.at[k], device_id=to, device_id_type=MESH)

        mine = pltpu.make_async_copy(x_ref, slot(*me), local_sem)
        first = [copy(0, me, sibling, src=x_ref)]
        first += [copy(1 + j, me, (*chip, c), src=x_ref) for j, chip in enumerate(chips)]
        passed = [copy(4 + j, (*chip, c), sibling) for j, chip in enumerate(chips)]

        def start():
            mine.start()
            for cp in first:
                cp.start()

        def middle():
            for j, chip in enumerate(chips):
                copy(1 + j, (*chip, c), me).wait_recv()
                passed[j].start()

        def finish():
            copy(0, sibling, me).wait_recv()
            for j, chip in enumerate(chips):
                copy(4 + j, (*chip, 1 - c), me).wait_recv()
            for cp in first + passed:
                cp.wait_send()
            mine.wait()

        return start, middle, finish

    return _Job([block], [jax.ShapeDtypeStruct((N_DEV, rows, cols), block.dtype)],
                [pltpu.SemaphoreType.DMA((7,)), pltpu.SemaphoreType.DMA((7,)), pltpu.SemaphoreType.DMA], hooks)


def _sibling_exchange_job(g):
    _, rows, cols = g.shape

    def hooks(ins, outs, sems):
        (g_ref,), (recv_ref,), (send_sems, recv_sems) = ins, outs, sems
        x, y, c = _position()
        copies = [pltpu.make_async_remote_copy(
            src_ref=g_ref.at[2 * k + (1 - c)], dst_ref=recv_ref.at[k], send_sem=send_sems.at[k],
            recv_sem=recv_sems.at[k], device_id=(x, y, 1 - c), device_id_type=MESH) for k in range(N_CHIP)]

        def start():
            for cp in copies:
                cp.start()

        def finish():
            for cp in copies:
                cp.wait()

        return start, lambda: None, finish

    return _Job([g], [jax.ShapeDtypeStruct((N_CHIP, rows, cols), g.dtype)],
                [pltpu.SemaphoreType.DMA((N_CHIP,)), pltpu.SemaphoreType.DMA((N_CHIP,))], hooks)


def _chip_exchange_job(p):
    _, rows, cols = p.shape

    def hooks(ins, outs, sems):
        (p_ref,), (recv_ref,), (send_sems, recv_sems, local_sem) = ins, outs, sems
        x, y, c = _position()
        k_me = 2 * x + y
        mine = pltpu.make_async_copy(p_ref.at[k_me], recv_ref.at[k_me], local_sem)
        copies = [pltpu.make_async_remote_copy(
            src_ref=p_ref.at[2 * px + py], dst_ref=recv_ref.at[k_me], send_sem=send_sems.at[j],
            recv_sem=recv_sems.at[j], device_id=(px, py, c), device_id_type=MESH)
            for j, (px, py) in enumerate([(1 - x, y), (x, 1 - y), (1 - x, 1 - y)])]

        def start():
            mine.start()
            for cp in copies:
                cp.start()

        def finish():
            for cp in copies:
                cp.wait()
            mine.wait()

        return start, lambda: None, finish

    return _Job([p], [jax.ShapeDtypeStruct((N_CHIP, rows, cols), p.dtype)],
                [pltpu.SemaphoreType.DMA((3,)), pltpu.SemaphoreType.DMA((3,)), pltpu.SemaphoreType.DMA], hooks)


def _job_hooks(jobs, ins, outs, sems):
    hooks = []
    for job in jobs:
        n_in, n_out, n_sem = len(job.inputs), len(job.out_shape), len(job.sems)
        hooks.append(job.hooks(ins[:n_in], outs[:n_out], sems[:n_sem]))
        ins, outs, sems = ins[n_in:], outs[n_out:], sems[n_sem:]
    return hooks


def _run_jobs(jobs, name):
    n_in = sum(len(job.inputs) for job in jobs)
    n_out = sum(len(job.out_shape) for job in jobs)

    def body(*refs):
        hooks = _job_hooks(jobs, refs[:n_in], refs[n_in:n_in + n_out], refs[n_in + n_out:])
        for phase in range(3):
            for h in hooks:
                h[phase]()

    return list(pl.pallas_call(
        body, name=name, in_specs=[ANY] * n_in, out_specs=[ANY] * n_out,
        out_shape=[s for job in jobs for s in job.out_shape],
        scratch_shapes=[s for job in jobs for s in job.sems],
    )(*[a for job in jobs for a in job.inputs]))


def _launch(body, *, name, grid, in_specs, out_specs, out_shape, args, scratch=(), jobs=()):
    in_specs, out_specs, out_shape, scratch = list(in_specs), list(out_specs), list(out_shape), list(scratch)
    if not jobs:
        return list(pl.pallas_call(body, name=name, grid=grid, in_specs=in_specs, out_specs=out_specs,
                                   out_shape=out_shape, scratch_shapes=scratch,
                                   compiler_params=_params(len(grid)))(*args))
    n_in, n_out, n_sc = len(in_specs), len(out_specs), len(scratch)
    j_in = [a for job in jobs for a in job.inputs]
    j_out = [s for job in jobs for s in job.out_shape]
    j_sems = [s for job in jobs for s in job.sems]
    n_steps = 1
    for g in grid:
        n_steps *= g

    def wrapped(*refs):
        ins, refs = refs[:n_in], refs[n_in:]
        jins, refs = refs[:len(j_in)], refs[len(j_in):]
        outs, refs = refs[:n_out], refs[n_out:]
        jouts, refs = refs[:len(j_out)], refs[len(j_out):]
        sc, jsems = refs[:n_sc], refs[n_sc:]
        step = pl.program_id(0)
        for axis in range(1, len(grid)):
            step = step * grid[axis] + pl.program_id(axis)
        hooks = _job_hooks(jobs, jins, jouts, jsems)

        @pl.when(step == 0)
        def _():
            for h in hooks:
                h[0]()

        body(*ins, *outs, *sc)

        @pl.when(step == (MIDDLE_AT * n_steps) // MIDDLE_OF)
        def _():
            for h in hooks:
                h[1]()

        @pl.when(step == n_steps - 1)
        def _():
            for h in hooks:
                h[2]()

    return list(pl.pallas_call(
        wrapped, name=name, grid=grid, in_specs=in_specs + [ANY] * len(j_in), out_specs=out_specs + [ANY] * len(j_out),
        out_shape=out_shape + j_out, scratch_shapes=scratch + j_sems, compiler_params=_params(len(grid)),
    )(*args, *j_in))


def _gmlp_gate(vnb, wsm, bst, tile):
    rows = []
    for n in range(tile // HEAD):
        cols = []
        for hh in range(N_HEADS):
            blk = vnb[n * HEAD:(n + 1) * HEAD, hh * HEAD:(hh + 1) * HEAD]
            cols.append(_mm(wsm[hh], blk) + bst[:, hh:hh + 1])
        rows.append(jnp.concatenate(cols, axis=1))
    return jnp.concatenate(rows, axis=0)


def _even_fwd(h, w_in, w_out, ws, bst, lng, lnb, wp, sc, gm, jobs=()):
    seq = h.shape[0]
    tile = min(MIX_TILE, seq)
    n_tiles = seq // tile

    def body(h_ref, hp_ref, win_ref, wout_ref, ws_ref, bst_ref, lng_ref, lnb_ref, wp_ref, sc_ref, g_ref,
             ho_ref, hn_ref, za_ref, pool_ref, mix_ref):
        i = pl.program_id(0)
        g = g_ref[...]
        h = h_ref[...]
        hnb = (h * _rms_r(h) * g).astype(bf16)
        hn_ref[...] = hnb
        z = _mm_nt(hnb, win_ref[...])
        zab = z[:, :2 * A_WIDTH].astype(bf16)
        za_ref[...] = zab
        hp = hp_ref[...]
        zbp = _mm_nt((hp * _rms_r(hp) * g).astype(bf16), win_ref[2 * A_WIDTH:, :])
        zbe = jnp.concatenate([jnp.where(i > 0, zbp, 0.0), z[:, 2 * A_WIDTH:]], axis=0)
        pooled = []
        for gi, win in enumerate(POOL_WINDOWS):
            xg = zbe[:, gi * HEAD:(gi + 1) * HEAD]
            s = _window_sum(xg, win, _down)
            pooled.append(s[POOL_HALO:] * _inv_count(i * tile, tile, win) - xg[POOL_HALO:])
        plb = jnp.concatenate(pooled, axis=1).astype(bf16)
        pool_ref[...] = plb

        ga, _ = _gelu(zab.astype(f32))
        vn, _, _ = _ln_fwd(ga[:, A_WIDTH:], lng_ref[...], lnb_ref[...])
        mask = _chunk_mask()
        wsm = [jnp.where(mask, ws_ref[hh], 0.0).astype(bf16) for hh in range(N_HEADS)]
        ya = ga[:, :A_WIDTH] * _gmlp_gate(vn.astype(bf16), wsm, bst_ref[...], tile)
        yb = jnp.concatenate([_mm(plb[:, gi * HEAD:(gi + 1) * HEAD], wp_ref[gi].astype(bf16))
                              for gi in range(len(POOL_WINDOWS))], axis=1) * sc_ref[...]
        mix = jnp.concatenate([ya, yb], axis=1).astype(bf16)
        mix_ref[...] = mix
        ho_ref[...] = h + _mm(mix, wout_ref[...])

    row = lambda cols: pl.BlockSpec((tile, cols), lambda i: (i, 0))
    return _launch(
        body, name="even_fwd", grid=(n_tiles,), jobs=jobs,
        in_specs=[row(D_MODEL), _prev_halo(tile, POOL_HALO, D_MODEL), _const(w_in.shape, 1), _const(w_out.shape, 1),
                  _const(ws.shape, 1), _const(bst.shape, 1), _const(lng.shape, 1), _const(lnb.shape, 1),
                  _const(wp.shape, 1), _const(sc.shape, 1), _const(gm.shape, 1)],
        out_specs=[row(D_MODEL), row(D_MODEL), row(2 * A_WIDTH), row(A_WIDTH), row(D_MODEL)],
        out_shape=[jax.ShapeDtypeStruct((seq, D_MODEL), f32), jax.ShapeDtypeStruct((seq, D_MODEL), bf16),
                   jax.ShapeDtypeStruct((seq, 2 * A_WIDTH), bf16), jax.ShapeDtypeStruct((seq, A_WIDTH), bf16),
                   jax.ShapeDtypeStruct((seq, D_MODEL), bf16)],
        args=(h, h, w_in, w_out, ws, bst, lng, lnb, wp, sc, gm))


def _even_bwd(dh, h, za, pooled, w_in, w_out, ws, bst, lng, lnb, wp, sc, gm, jobs=()):
    seq = h.shape[0]
    tile = min(MIX_TILE, seq)
    n_tiles = seq // tile
    n_groups = len(POOL_WINDOWS)

    def body(dh_ref, dhx_ref, h_ref, za_ref, pool_ref, win_ref, wout_ref, ws_ref, bst_ref, lng_ref, lnb_ref,
             wp_ref, sc_ref, g_ref,
             dhi_ref, dz_ref, dws_ref, dbs_ref, dlng_ref, dlnb_ref, dwp_ref, dsc_ref, dg_ref):
        i = pl.program_id(0)

        @pl.when(i == 0)
        def _():
            for ref in (dws_ref, dbs_ref, dlng_ref, dlnb_ref, dwp_ref, dsc_ref, dg_ref):
                ref[...] = jnp.zeros_like(ref)

        dh = dh_ref[...]
        dmix = _mm_nt(dh.astype(bf16), wout_ref[...])
        dya = dmix[:, :A_WIDTH]
        dyb = dmix[:, A_WIDTH:]
        dybx = _mm_nt(dhx_ref[...].astype(bf16), wout_ref[A_WIDTH:, :])
        dybx = jnp.where(i < n_tiles - 1, dybx, 0.0)

        za = za_ref[...].astype(f32)
        ga, th = _gelu(za)
        u = ga[:, :A_WIDTH]
        lng = lng_ref[...]
        vn, vh, r = _ln_fwd(ga[:, A_WIDTH:], lng, lnb_ref[...])
        vnb = vn.astype(bf16)
        mask = _chunk_mask()
        wsf = [jnp.where(mask, ws_ref[hh], 0.0) for hh in range(N_HEADS)]
        sv = _gmlp_gate(vnb, [w.astype(bf16) for w in wsf], bst_ref[...], tile)
        du = dya * sv
        dsvb = (dya * u).astype(bf16)
        wst = [w.T.astype(bf16) for w in wsf]
        ones = jnp.ones((8, HEAD), bf16)
        dws = [jnp.zeros((HEAD, HEAD), f32) for _ in range(N_HEADS)]
        dbs = [jnp.zeros((8, HEAD), f32) for _ in range(N_HEADS)]
        rows = []
        for n in range(tile // HEAD):
            cols = []
            for hh in range(N_HEADS):
                blk = dsvb[n * HEAD:(n + 1) * HEAD, hh * HEAD:(hh + 1) * HEAD]
                cols.append(_mm(wst[hh], blk))
                dws[hh] = dws[hh] + _mm_nt(blk, vnb[n * HEAD:(n + 1) * HEAD, hh * HEAD:(hh + 1) * HEAD])
                dbs[hh] = dbs[hh] + _mm_nt(ones, blk)
            rows.append(jnp.concatenate(cols, axis=1))
        dvn = jnp.concatenate(rows, axis=0)
        for hh in range(N_HEADS):
            dws_ref[hh] += jnp.where(mask, dws[hh], 0.0)
            dbs_ref[pl.ds(hh, 1), :] += dbs[hh][0:1, :]
        dlng_ref[...] += jnp.sum(dvn * vh, axis=0, keepdims=True)
        dlnb_ref[...] += jnp.sum(dvn, axis=0, keepdims=True)
        dv = _ln_bwd(dvn, vh, r, lng)
        dza = jnp.concatenate([du, dv], axis=1) * _gelu_grad(za, th)

        plb = pool_ref[...]
        sc = sc_ref[...]
        dzb = []
        dsc = []
        for gi, win in enumerate(POOL_WINDOWS):
            cs = slice(gi * HEAD, (gi + 1) * HEAD)
            wpb = wp_ref[gi].astype(bf16)
            dsc.append(jnp.sum(dyb[:, cs] * _mm(plb[:, cs], wpb), axis=0, keepdims=True))
            dpre = (dyb[:, cs] * sc[:, cs]).astype(bf16)
            dprex = (dybx[:, cs] * sc[:, cs]).astype(bf16)
            dwp_ref[gi] += _mm_tn(plb[:, cs], dpre)
            dpl = _mm_nt(dpre, wpb)
            dple = jnp.concatenate([dpl, _mm_nt(dprex, wpb)], axis=0)
            q = dple * _inv_count(i * tile, tile + POOL_HALO, win)
            dzb.append(_window_sum(q, win, _up)[:tile] - dpl)
        dsc_ref[...] += jnp.concatenate(dsc, axis=1)

        dzf = jnp.concatenate([dza] + dzb, axis=1).astype(bf16)
        dz_ref[...] = dzf
        dhn = _mm(dzf, win_ref[...])
        dhr, dg = _rms_bwd(dhn, h_ref[...], g_ref[...])
        dhi_ref[...] = dh + dhr
        dg_ref[...] += dg

    row = lambda cols: pl.BlockSpec((tile, cols), lambda i: (i, 0))
    small = [ws.shape, (N_HEADS, HEAD), lng.shape, lnb.shape, wp.shape, sc.shape, gm.shape]
    return _launch(
        body, name="even_bwd", grid=(n_tiles,), jobs=jobs,
        in_specs=[row(D_MODEL), _next_halo(tile, POOL_HALO, D_MODEL, seq), row(D_MODEL), row(2 * A_WIDTH), row(A_WIDTH),
                  _const(w_in.shape, 1), _const(w_out.shape, 1), _const(ws.shape, 1), _const(bst.shape, 1),
                  _const(lng.shape, 1), _const(lnb.shape, 1), _const(wp.shape, 1), _const(sc.shape, 1), _const(gm.shape, 1)],
        out_specs=[row(D_MODEL), row(3 * A_WIDTH)] + [_const(s, 1) for s in small],
        out_shape=[jax.ShapeDtypeStruct((seq, D_MODEL), f32), jax.ShapeDtypeStruct((seq, 3 * A_WIDTH), bf16)]
                  + [jax.ShapeDtypeStruct(s, f32) for s in small],
        args=(dh, dh, h, za, pooled, w_in, w_out, ws, bst, lng, lnb, wp, sc, gm))


SUBLANES = 8


class _Shifted:
    def __init__(self, x, shift, max_shift):
        self.rolled = [shift(x, b) for b in range(min(SUBLANES, max_shift + 1))]
        self.back = shift is _down

    def rows(self, k, start, count):
        whole = k - k % SUBLANES
        lo = start - whole if self.back else start + whole
        return self.rolled[k % SUBLANES][lo:lo + count]


def _conv_taps(xs, w_ref, n_taps, halo, rows):
    acc = None
    for j in range(n_taps):
        term = w_ref[pl.ds(j, 1), :] * xs.rows(n_taps - 1 - j, halo, rows)
        acc = term if acc is None else acc + term
    return acc


def _odd_fwd(h, w_in, w_out, cw, cb, clg, clb, dw, gm):
    seq = h.shape[0]
    tile = min(MIX_TILE, seq)
    n_tiles = seq // tile
    w = A_WIDTH

    def body(h_ref, hp_ref, win_ref, wout_ref, cw_ref, cb_ref, clg_ref, clb_ref, dw_ref, g_ref,
             ho_ref, hn_ref, z_ref, mix_ref):
        i = pl.program_id(0)
        g = g_ref[...]
        h = h_ref[...]
        hnb = (h * _rms_r(h) * g).astype(bf16)
        hn_ref[...] = hnb
        zb = _mm_nt(hnb, win_ref[...]).astype(bf16)
        z_ref[...] = zb
        hp = hp_ref[...]
        zp = _mm_nt((hp * _rms_r(hp) * g).astype(bf16), win_ref[...]).astype(bf16).astype(f32)
        z = zb.astype(f32)
        ze = jnp.concatenate([jnp.where(i > 0, zp, 0.0), z], axis=0)
        hc = ze[:, :w] * _sigmoid(ze[:, w:2 * w])
        cv = _conv_taps(_Shifted(hc, _down, C_KERNEL - 1), cw_ref, C_KERNEL, CONV_HALO, tile) + cb_ref[...]
        ln, _, _ = _ln_fwd(cv, clg_ref[...], clb_ref[...])
        yc = ln * _sigmoid(ln)
        p = ze[:, 3 * w:4 * w] * ze[:, 4 * w:]
        yd = z[:, 2 * w:3 * w] * _conv_taps(_Shifted(p, _down, D_KERNEL - 1), dw_ref, D_KERNEL, CONV_HALO, tile)
        mix = jnp.concatenate([yc, yd], axis=1).astype(bf16)
        mix_ref[...] = mix
        ho_ref[...] = h + _mm(mix, wout_ref[...])

    row = lambda cols: pl.BlockSpec((tile, cols), lambda i: (i, 0))
    return pl.pallas_call(
        body, name="odd_fwd", grid=(n_tiles,),
        in_specs=[row(D_MODEL), _prev_halo(tile, CONV_HALO, D_MODEL), _const(w_in.shape, 1), _const(w_out.shape, 1),
                  _const(cw.shape, 1), _const(cb.shape, 1), _const(clg.shape, 1), _const(clb.shape, 1),
                  _const(dw.shape, 1), _const(gm.shape, 1)],
        out_specs=[row(D_MODEL), row(D_MODEL), row(5 * w), row(D_MODEL)],
        out_shape=[jax.ShapeDtypeStruct((seq, D_MODEL), f32), jax.ShapeDtypeStruct((seq, D_MODEL), bf16),
                   jax.ShapeDtypeStruct((seq, 5 * w), bf16), jax.ShapeDtypeStruct((seq, D_MODEL), bf16)],
        compiler_params=_params(1),
    )(h, h, w_in, w_out, cw, cb, clg, clb, dw, gm)


def _odd_bwd(dh, h, z, w_in, w_out, cw, cb, clg, clb, dw, gm, jobs=()):
    seq = h.shape[0]
    tile = min(MIX_TILE, seq)
    n_tiles = seq // tile
    w = A_WIDTH
    halo = CONV_HALO
    ext = tile + halo

    def body(dh_ref, dhx_ref, h_ref, z_ref, zp_ref, zx_ref, win_ref, wout_ref, cw_ref, cb_ref, clg_ref, clb_ref,
             dw_ref, g_ref,
             dhi_ref, dz_ref, dcw_ref, dcb_ref, dclg_ref, dclb_ref, ddw_ref, dg_ref):
        i = pl.program_id(0)

        @pl.when(i == 0)
        def _():
            for ref in (dcw_ref, dcb_ref, dclg_ref, dclb_ref, ddw_ref, dg_ref):
                ref[...] = jnp.zeros_like(ref)

        dh = dh_ref[...]
        dhe = jnp.concatenate([dh, jnp.where(i < n_tiles - 1, dhx_ref[...], 0.0)], axis=0)
        dmix = _mm_nt(dhe.astype(bf16), wout_ref[...])
        ze = jnp.concatenate([jnp.where(i > 0, zp_ref[...].astype(f32), 0.0), z_ref[...].astype(f32),
                              zx_ref[...].astype(f32)], axis=0)

        sg = _sigmoid(ze[:, w:2 * w])
        ca = ze[:, :w]
        hc = ca * sg
        hcs = _Shifted(hc, _down, C_KERNEL - 1)
        cv = _conv_taps(hcs, cw_ref, C_KERNEL, halo, ext) + cb_ref[...]
        clg = clg_ref[...]
        ln, xh, r = _ln_fwd(cv, clg, clb_ref[...])
        sl = _sigmoid(ln)
        dln = dmix[:, :w] * (sl * (1.0 + ln * (1.0 - sl)))
        dclg_ref[...] += jnp.sum((dln * xh)[:tile], axis=0, keepdims=True)
        dclb_ref[...] += jnp.sum(dln[:tile], axis=0, keepdims=True)
        dcv = _ln_bwd(dln, xh, r, clg)
        dcb_ref[...] += jnp.sum(dcv[:tile], axis=0, keepdims=True)
        dcvs = _Shifted(dcv, _up, C_KERNEL - 1)
        dhc = None
        for j in range(C_KERNEL):
            k = C_KERNEL - 1 - j
            dcw_ref[pl.ds(j, 1), :] += jnp.sum(dcv[:tile] * hcs.rows(k, halo, tile), axis=0, keepdims=True)
            term = cw_ref[pl.ds(j, 1), :] * dcvs.rows(k, 0, tile)
            dhc = term if dhc is None else dhc + term
        sgt = sg[halo:halo + tile]
        cat = ca[halo:halo + tile]
        dca = dhc * sgt
        dcg = dhc * cat * sgt * (1.0 - sgt)

        dcgv = ze[:, 3 * w:4 * w]
        dxin = ze[:, 4 * w:]
        p = dcgv * dxin
        ps = _Shifted(p, _down, D_KERNEL - 1)
        q = _conv_taps(ps, dw_ref, D_KERNEL, halo, tile)
        dyd = dmix[:, w:]
        dq = dyd * ze[halo:, 2 * w:3 * w]
        ddbg = dyd[:tile] * q
        dqs = _Shifted(dq, _up, D_KERNEL - 1)
        dp = None
        for j in range(D_KERNEL):
            k = D_KERNEL - 1 - j
            ddw_ref[pl.ds(j, 1), :] += jnp.sum(dq[:tile] * ps.rows(k, halo, tile), axis=0, keepdims=True)
            term = dw_ref[pl.ds(j, 1), :] * dqs.rows(k, 0, tile)
            dp = term if dp is None else dp + term
        ddcg = dp * dxin[halo:halo + tile]
        ddxin = dp * dcgv[halo:halo + tile]

        dzf = jnp.concatenate([dca, dcg, ddbg, ddcg, ddxin], axis=1).astype(bf16)
        dz_ref[...] = dzf
        dhn = _mm(dzf, win_ref[...])
        dhr, dg = _rms_bwd(dhn, h_ref[...], g_ref[...])
        dhi_ref[...] = dh + dhr
        dg_ref[...] += dg

    row = lambda cols: pl.BlockSpec((tile, cols), lambda i: (i, 0))
    small = [cw.shape, cb.shape, clg.shape, clb.shape, dw.shape, gm.shape]
    return _launch(
        body, name="odd_bwd", grid=(n_tiles,), jobs=jobs,
        in_specs=[row(D_MODEL), _next_halo(tile, halo, D_MODEL, seq), row(D_MODEL), row(5 * w),
                  _prev_halo(tile, halo, 5 * w), _next_halo(tile, halo, 5 * w, seq),
                  _const(w_in.shape, 1), _const(w_out.shape, 1), _const(cw.shape, 1), _const(cb.shape, 1),
                  _const(clg.shape, 1), _const(clb.shape, 1), _const(dw.shape, 1), _const(gm.shape, 1)],
        out_specs=[row(D_MODEL), row(5 * w)] + [_const(s, 1) for s in small],
        out_shape=[jax.ShapeDtypeStruct((seq, D_MODEL), f32), jax.ShapeDtypeStruct((seq, 5 * w), bf16)]
                  + [jax.ShapeDtypeStruct(s, f32) for s in small],
        args=(dh, dh, h, z, z, z, w_in, w_out, cw, cb, clg, clb, dw, gm))


def _ffn_fwd(h, wg, wu, wd, gm, jobs=()):
    seq = h.shape[0]
    tile = min(FFN_TILE, seq)
    n_tiles = seq // tile
    n_chunks = D_FF // FFN_CHUNK

    def body(h_ref, g_ref, wg_ref, wu_ref, wd_ref, ho_ref, hn_ref, gate_ref, up_ref, acc_ref, hns_ref):
        j = pl.program_id(1)

        @pl.when(j == 0)
        def _():
            h = h_ref[...]
            hnb = (h * _rms_r(h) * g_ref[...]).astype(bf16)
            hns_ref[...] = hnb
            hn_ref[...] = hnb
            acc_ref[...] = jnp.zeros_like(acc_ref)

        hnb = hns_ref[...]
        gb = _mm_nt(hnb, wg_ref[...]).astype(bf16)
        ub = _mm_nt(hnb, wu_ref[...]).astype(bf16)
        gate_ref[...] = gb
        up_ref[...] = ub
        gf = gb.astype(f32)
        act = gf * _sigmoid(gf) * ub.astype(f32)
        acc_ref[...] += _mm(act.astype(bf16), wd_ref[...])

        @pl.when(j == n_chunks - 1)
        def _():
            ho_ref[...] = h_ref[...] + acc_ref[...]

    row = pl.BlockSpec((tile, D_MODEL), lambda i, j: (i, 0))
    col = pl.BlockSpec((tile, FFN_CHUNK), lambda i, j: (i, j))
    wblk = pl.BlockSpec((FFN_CHUNK, D_MODEL), lambda i, j: (j, 0))
    return _launch(
        body, name="ffn_fwd", grid=(n_tiles, n_chunks), jobs=jobs,
        in_specs=[row, _const(gm.shape, 2), wblk, wblk, wblk],
        out_specs=[row, row, col, col],
        out_shape=[jax.ShapeDtypeStruct((seq, D_MODEL), f32), jax.ShapeDtypeStruct((seq, D_MODEL), bf16),
                   jax.ShapeDtypeStruct((seq, D_FF), bf16), jax.ShapeDtypeStruct((seq, D_FF), bf16)],
        scratch=[pltpu.VMEM((tile, D_MODEL), f32), pltpu.VMEM((tile, D_MODEL), bf16)],
        args=(h, gm, wg, wu, wd))


def _ffn_bwd(dh, h, gate, up, wg, wu, wd, gm, jobs=()):
    seq = h.shape[0]
    tile = min(FFN_BWD_TILE, seq)
    n_tiles = seq // tile
    n_chunks = D_FF // FFN_CHUNK

    def body(dh_ref, h_ref, g_ref, gate_ref, up_ref, wg_ref, wu_ref, wd_ref,
             dhi_ref, dgate_ref, dup_ref, act_ref, dg_ref):
        @pl.when(pl.program_id(0) == 0)
        def _():
            dg_ref[...] = jnp.zeros_like(dg_ref)

        dh = dh_ref[...]
        dhb = dh.astype(bf16)
        acc = None
        for c in range(n_chunks):
            rows = slice(c * FFN_CHUNK, (c + 1) * FFN_CHUNK)
            dact = _mm_nt(dhb, wd_ref[rows, :])
            gf = gate_ref[:, rows].astype(f32)
            uf = up_ref[:, rows].astype(f32)
            s = _sigmoid(gf)
            silu = gf * s
            act_ref[:, rows] = (silu * uf).astype(bf16)
            dgb = (dact * uf * (s * (1.0 + gf * (1.0 - s)))).astype(bf16)
            dub = (dact * silu).astype(bf16)
            dgate_ref[:, rows] = dgb
            dup_ref[:, rows] = dub
            part = _mm(dgb, wg_ref[rows, :]) + _mm(dub, wu_ref[rows, :])
            acc = part if acc is None else acc + part
        dhr, dg = _rms_bwd(acc, h_ref[...], g_ref[...])
        dhi_ref[...] = dh + dhr
        dg_ref[...] += dg

    row = pl.BlockSpec((tile, D_MODEL), lambda i: (i, 0))
    wide = pl.BlockSpec((tile, D_FF), lambda i: (i, 0))
    return _launch(
        body, name="ffn_bwd", grid=(n_tiles,), jobs=jobs,
        in_specs=[row, row, _const(gm.shape, 1), wide, wide, _const(wg.shape, 1), _const(wu.shape, 1), _const(wd.shape, 1)],
        out_specs=[row, wide, wide, wide, _const(gm.shape, 1)],
        out_shape=[jax.ShapeDtypeStruct((seq, D_MODEL), f32), jax.ShapeDtypeStruct((seq, D_FF), bf16),
                   jax.ShapeDtypeStruct((seq, D_FF), bf16), jax.ShapeDtypeStruct((seq, D_FF), bf16),
                   jax.ShapeDtypeStruct(gm.shape, f32)],
        args=(dh, h, gm, gate, up, wg, wu, wd))


def _loss_head(h, target, gf):
    seq = h.shape[0]
    tile = min(MIX_TILE, seq)

    def body(h_ref, t_ref, g_ref, dh_ref, loss_ref, dg_ref):
        @pl.when(pl.program_id(0) == 0)
        def _():
            loss_ref[...] = jnp.zeros_like(loss_ref)
            dg_ref[...] = jnp.zeros_like(dg_ref)

        h = h_ref[...]
        g = g_ref[...]
        err = h * _rms_r(h) * g - t_ref[...]
        loss_ref[...] += (0.5 / D_MODEL) * jnp.sum(jnp.sum(err * err, axis=1, keepdims=True), axis=0, keepdims=True)
        dhr, dg = _rms_bwd(err * (1.0 / D_MODEL), h, g)
        dh_ref[...] = dhr
        dg_ref[...] += dg

    row = pl.BlockSpec((tile, D_MODEL), lambda i: (i, 0))
    return pl.pallas_call(
        body, name="loss_head", grid=(seq // tile,),
        in_specs=[row, row, _const(gf.shape, 1)],
        out_specs=[row, _const((1, 1), 1), _const(gf.shape, 1)],
        out_shape=[jax.ShapeDtypeStruct((seq, D_MODEL), f32), jax.ShapeDtypeStruct((1, 1), f32),
                   jax.ShapeDtypeStruct(gf.shape, f32)],
        compiler_params=_params(1),
    )(h, target, gf)


def _weight_grads(pairs, name, jobs=()):
    seq, m = pairs[0][0].shape
    tk = min(DW_TK, seq)
    tm = m if m <= DW_TM else m // 2
    n_k = seq // tk
    n_pairs = len(pairs)

    def body(*refs):
        x_refs = refs[0:2 * n_pairs:2]
        y_refs = refs[1:2 * n_pairs:2]
        o_refs = refs[2 * n_pairs:3 * n_pairs]
        acc_refs = refs[3 * n_pairs:]
        k = pl.program_id(1)
        @pl.when(k == 0)
        def _():
            for acc_ref in acc_refs:
                acc_ref[...] = jnp.zeros_like(acc_ref)

        for x_ref, y_ref, acc_ref in zip(x_refs, y_refs, acc_refs):
            acc_ref[...] += _mm_tn(x_ref[...].astype(bf16), y_ref[...].astype(bf16))

        @pl.when(k == n_k - 1)
        def _():
            for o_ref, acc_ref in zip(o_refs, acc_refs):
                o_ref[...] = acc_ref[...].astype(bf16)

    in_specs = []
    for _ in pairs:
        in_specs += [pl.BlockSpec((tk, tm), lambda j, k: (k, j)), pl.BlockSpec((tk, D_MODEL), lambda j, k: (k, 0))]
    return _launch(
        body, name=name, grid=(m // tm, n_k), jobs=jobs,
        in_specs=in_specs,
        out_specs=[pl.BlockSpec((tm, D_MODEL), lambda j, k: (j, 0))] * n_pairs,
        out_shape=[jax.ShapeDtypeStruct((m, D_MODEL), bf16)] * n_pairs,
        scratch=[pltpu.VMEM((tm, D_MODEL), f32)] * n_pairs,
        args=[a for pair in pairs for a in pair])


def _row_tile(rows, limit=512):
    best = rows
    for t in range(8, min(rows, limit) + 1, 8):
        if rows % t == 0:
            best = t
    return best if rows > limit else rows


def _adamw(w, g, m, v, name):
    rows, cols = w.shape
    tr = _row_tile(rows)

    def body(w_ref, g_ref, m_ref, v_ref, d_ref, mo_ref, vo_ref):
        g = g_ref[...]
        m2 = ADAM_B1 * m_ref[...] + (1.0 - ADAM_B1) * g
        v2 = ADAM_B2 * v_ref[...] + (1.0 - ADAM_B2) * (g * g)
        m_hat = m2 / (1.0 - ADAM_B1 ** ADAM_STEP)
        v_hat = v2 / (1.0 - ADAM_B2 ** ADAM_STEP)
        d_ref[...] = -ADAM_LR * (m_hat / (jnp.sqrt(v_hat) + ADAM_EPS) + ADAM_WD * w_ref[...])
        mo_ref[...] = m2
        vo_ref[...] = v2

    spec = pl.BlockSpec((tr, cols), lambda i: (i, 0))
    return pl.pallas_call(
        body, name=name, grid=(rows // tr,),
        in_specs=[spec] * 4, out_specs=[spec] * 3,
        out_shape=[jax.ShapeDtypeStruct((rows, cols), f32)] * 3,
        compiler_params=_params(1),
    )(w, g, m, v)


def _sum_leading(x, name):
    n, rows, cols = x.shape
    tr = _row_tile(rows)

    def body(x_ref, o_ref):
        acc = x_ref[0].astype(f32)
        for k in range(1, n):
            acc = acc + x_ref[k].astype(f32)
        o_ref[...] = acc

    return pl.pallas_call(
        body, name=name, grid=(rows // tr,),
        in_specs=[pl.BlockSpec((n, tr, cols), lambda i: (0, i, 0))],
        out_specs=pl.BlockSpec((tr, cols), lambda i: (i, 0)),
        out_shape=jax.ShapeDtypeStruct((rows, cols), f32),
        compiler_params=_params(1),
    )(x)


def _pair_sum(g, recv, c_idx, name):
    _, rows, cols = g.shape
    tr = _row_tile(rows)

    def body(c_ref, g_ref, r_ref, o_ref):
        o_ref[...] = (g_ref[...].astype(f32) + r_ref[...].astype(f32)).astype(o_ref.dtype)

    return pl.pallas_call(
        body, name=name,
        grid_spec=pltpu.PrefetchScalarGridSpec(
            num_scalar_prefetch=1, grid=(N_CHIP, rows // tr),
            in_specs=[pl.BlockSpec((1, tr, cols), lambda k, i, c: (2 * k + c[0], i, 0)),
                      pl.BlockSpec((1, tr, cols), lambda k, i, c: (k, i, 0))],
            out_specs=pl.BlockSpec((1, tr, cols), lambda k, i, c: (k, i, 0))),
        out_shape=jax.ShapeDtypeStruct((N_CHIP, rows, cols), g.dtype),
        compiler_params=_params(2),
    )(c_idx, g, recv)


def _pack_rows(w):
    return w.reshape(N_DEV, -1, D_MODEL)


def kernel(x, even_w_in, even_w_out, a_w_s, a_b_s, a_ln_g, a_ln_b, b_w_pool, b_scale, odd_w_in, odd_w_out, c_w_dw, c_b_dw, c_ln_g, c_ln_b, d_w_dw, norm_mix_g, norm_ffn_g, ffn_w_gate, ffn_w_up, ffn_w_down, final_norm_g, loss_target, m_even_w_in, m_even_w_out, m_a_w_s, m_a_b_s, m_a_ln_g, m_a_ln_b, m_b_w_pool, m_b_scale, m_odd_w_in, m_odd_w_out, m_c_w_dw, m_c_b_dw, m_c_ln_g, m_c_ln_b, m_d_w_dw, m_norm_mix_g, m_norm_ffn_g, m_ffn_w_gate, m_ffn_w_up, m_ffn_w_down, m_final_norm_g, v_even_w_in, v_even_w_out, v_a_w_s, v_a_b_s, v_a_ln_g, v_a_ln_b, v_b_w_pool, v_b_scale, v_odd_w_in, v_odd_w_out, v_c_w_dw, v_c_b_dw, v_c_ln_g, v_c_ln_b, v_d_w_dw, v_norm_mix_g, v_norm_ffn_g, v_ffn_w_gate, v_ffn_w_up, v_ffn_w_down, v_final_norm_g):
    weights = dict(even_w_in=even_w_in, even_w_out=even_w_out, a_w_s=a_w_s, a_b_s=a_b_s, a_ln_g=a_ln_g, a_ln_b=a_ln_b,
                   b_w_pool=b_w_pool, b_scale=b_scale, odd_w_in=odd_w_in, odd_w_out=odd_w_out, c_w_dw=c_w_dw,
                   c_b_dw=c_b_dw, c_ln_g=c_ln_g, c_ln_b=c_ln_b, d_w_dw=d_w_dw, norm_mix_g=norm_mix_g,
                   norm_ffn_g=norm_ffn_g, ffn_w_gate=ffn_w_gate, ffn_w_up=ffn_w_up, ffn_w_down=ffn_w_down,
                   final_norm_g=final_norm_g)
    m_in = dict(even_w_in=m_even_w_in, even_w_out=m_even_w_out, a_w_s=m_a_w_s, a_b_s=m_a_b_s, a_ln_g=m_a_ln_g,
                a_ln_b=m_a_ln_b, b_w_pool=m_b_w_pool, b_scale=m_b_scale, odd_w_in=m_odd_w_in, odd_w_out=m_odd_w_out,
                c_w_dw=m_c_w_dw, c_b_dw=m_c_b_dw, c_ln_g=m_c_ln_g, c_ln_b=m_c_ln_b, d_w_dw=m_d_w_dw,
                norm_mix_g=m_norm_mix_g, norm_ffn_g=m_norm_ffn_g, ffn_w_gate=m_ffn_w_gate, ffn_w_up=m_ffn_w_up,
                ffn_w_down=m_ffn_w_down, final_norm_g=m_final_norm_g)
    v_in = dict(even_w_in=v_even_w_in, even_w_out=v_even_w_out, a_w_s=v_a_w_s, a_b_s=v_a_b_s, a_ln_g=v_a_ln_g,
                a_ln_b=v_a_ln_b, b_w_pool=v_b_w_pool, b_scale=v_b_scale, odd_w_in=v_odd_w_in, odd_w_out=v_odd_w_out,
                c_w_dw=v_c_w_dw, c_b_dw=v_c_b_dw, c_ln_g=v_c_ln_g, c_ln_b=v_c_ln_b, d_w_dw=v_d_w_dw,
                norm_mix_g=v_norm_mix_g, norm_ffn_g=v_norm_ffn_g, ffn_w_gate=v_ffn_w_gate, ffn_w_up=v_ffn_w_up,
                ffn_w_down=v_ffn_w_down, final_norm_g=v_final_norm_g)
    names = list(weights)

    group_parts = {
        "even": [even_w_in[0].T, even_w_out[0]],
        "ffn0": [ffn_w_gate[0].T, ffn_w_up[0].T, ffn_w_down[0]],
        "odd": [odd_w_in[0].T, odd_w_out[0]],
        "ffn1": [ffn_w_gate[1].T, ffn_w_up[1].T, ffn_w_down[1]],
    }
    group_rows = {k: [p.shape[0] for p in parts] for k, parts in group_parts.items()}

    def pack(*groups):
        return jnp.concatenate([p for k in groups for p in group_parts[k]], axis=0).astype(bf16)

    def unpack(gathered_blocks, *groups):
        out, off = [], 0
        for k in groups:
            for r in group_rows[k]:
                out.append(gathered_blocks[:, off:off + r, :].reshape(-1, D_MODEL))
                off += r
        return out

    conv_names = ["c_w_dw", "c_b_dw", "c_ln_g", "c_ln_b", "d_w_dw"]
    conv_rows = [C_KERNEL, 1, 1, 1, D_KERNEL]
    conv_local = jnp.concatenate([weights[n].reshape(r, -1) for n, r in zip(conv_names, conv_rows)]
                                 + [jnp.zeros((3, c_b_dw.shape[-1]), f32)], axis=0)
    big_even, conv_all = _run_jobs([_all_gather_job(pack("even")), _all_gather_job(conv_local)], "gather_even_conv")
    w_in_e, w_out_e = unpack(big_even, "even")
    conv_all = conv_all.transpose(1, 0, 2).reshape(conv_local.shape[0], -1)
    conv_offs = [sum(conv_rows[:k]) for k in range(len(conv_rows) + 1)]
    cw, cb, clg, clb, dw = [conv_all[conv_offs[k]:conv_offs[k + 1]] for k in range(len(conv_rows))]

    ws, bst = a_w_s[0], a_b_s[0].T
    lng, lnb, wp, sc = a_ln_g, a_ln_b, b_w_pool[0], b_scale
    gmix = [norm_mix_g[l:l + 1] for l in range(2)]
    gffn = [norm_ffn_g[l:l + 1] for l in range(2)]
    gfin = final_norm_g.reshape(1, D_MODEL)

    h0 = x[0]
    h1, hn_e, za, pooled, mix_e, big_ffn0 = _even_fwd(
        h0, w_in_e, w_out_e, ws, bst, lng, lnb, wp, sc, gmix[0], jobs=[_all_gather_job(pack("ffn0"))])
    w_gate0, w_up0, w_down0 = unpack(big_ffn0, "ffn0")
    h2, hn_f0, gate0, up0, big_rest = _ffn_fwd(h1, w_gate0, w_up0, w_down0, gffn[0],
                                               jobs=[_all_gather_job(pack("odd", "ffn1"))])
    w_in_o, w_out_o, w_gate1, w_up1, w_down1 = unpack(big_rest, "odd", "ffn1")
    h3, hn_o, z_o, mix_o = _odd_fwd(h2, w_in_o, w_out_o, cw, cb, clg, clb, dw, gmix[1])
    h4, hn_f1, gate1, up1 = _ffn_fwd(h3, w_gate1, w_up1, w_down1, gffn[1])

    c_idx = lax.axis_index("c").astype(jnp.int32).reshape(1)

    def shard_major(gs):
        return jnp.concatenate([_pack_rows(g) for g in gs], axis=1)

    dh4, loss_local, g_final = _loss_head(h4, loss_target[0], gfin)
    dh3, dgate1, dup1, act1, g_ffn1 = _ffn_bwd(dh4, h3, gate1, up1, w_gate1, w_up1, w_down1, gffn[1])
    part_ffn1 = shard_major(_weight_grads([(dgate1, hn_f1)], "dw_gate1") + _weight_grads([(dup1, hn_f1)], "dw_up1")
                            + _weight_grads([(act1, dh4)], "dw_down1"))
    dh2, dz_o, g_cw, g_cb, g_clg, g_clb, g_dw, g_mix1, recv_ffn1 = _odd_bwd(
        dh3, h2, z_o, w_in_o, w_out_o, cw, cb, clg, clb, dw, gmix[1], jobs=[_sibling_exchange_job(part_ffn1)])
    pair_ffn1 = _pair_sum(part_ffn1, recv_ffn1, c_idx, "pair_sum_ffn1")
    part_odd = shard_major(_weight_grads([(dz_o, hn_o)], "dw_odd_in") + _weight_grads([(mix_o, dh3)], "dw_odd_out"))
    dh1, dgate0, dup0, act0, g_ffn0, chips_ffn1, recv_odd = _ffn_bwd(
        dh2, h1, gate0, up0, w_gate0, w_up0, w_down0, gffn[0],
        jobs=[_chip_exchange_job(pair_ffn1), _sibling_exchange_job(part_odd)])
    pair_odd = _pair_sum(part_odd, recv_odd, c_idx, "pair_sum_odd")
    dw_gate0, chips_odd = _weight_grads([(dgate0, hn_f0)], "dw_gate0", jobs=[_chip_exchange_job(pair_odd)])
    part_ffn0 = shard_major([dw_gate0] + _weight_grads([(dup0, hn_f0)], "dw_up0")
                            + _weight_grads([(act0, dh2)], "dw_down0"))
    recv_ffn0, = _run_jobs([_sibling_exchange_job(part_ffn0)], "sibling_exchange_ffn0")
    pair_ffn0 = _pair_sum(part_ffn0, recv_ffn0, c_idx, "pair_sum_ffn0")
    dh0, dz_e, g_ws, g_bs, g_lng, g_lnb, g_wp, g_sc, g_mix0, chips_ffn0 = _even_bwd(
        dh1, h0, za, pooled, w_in_e, w_out_e, ws, bst, lng, lnb, wp, sc, gmix[0], jobs=[_chip_exchange_job(pair_ffn0)])

    lanes = HEAD
    small = [("a_w_s", g_ws), ("a_b_s", g_bs), ("a_ln_g", g_lng), ("a_ln_b", g_lnb), ("b_w_pool", g_wp),
             ("b_scale", g_sc), ("norm_mix_g", jnp.concatenate([g_mix0, g_mix1], axis=0)),
             ("norm_ffn_g", jnp.concatenate([g_ffn0, g_ffn1], axis=0)), ("final_norm_g", g_final),
             ("c_w_dw", g_cw), ("c_b_dw", g_cb), ("c_ln_g", g_clg), ("c_ln_b", g_clb), ("d_w_dw", g_dw)]
    small_rows = [-(-g.size // (8 * lanes)) * 8 for _, g in small]
    small_offs = [sum(small_rows[:k]) for k in range(len(small) + 1)]
    pad_rows = -small_offs[-1] % 256
    small_buf = jnp.concatenate(
        [jnp.pad(g.reshape(-1), (0, r * lanes - g.size)).reshape(r, lanes) for (_, g), r in zip(small, small_rows)]
        + [jnp.zeros((pad_rows, lanes), f32)], axis=0)
    dw_even_in, small_all = _weight_grads([(dz_e, hn_e)], "dw_even_in", jobs=[_all_gather_job(small_buf)])
    small_sum = _sum_leading(small_all, "small_grad_sum")
    part_even = shard_major([dw_even_in] + _weight_grads([(mix_e, dh1)], "dw_even_out"))
    recv_even, = _run_jobs([_sibling_exchange_job(part_even)], "sibling_exchange_even")
    chips_even, = _run_jobs([_chip_exchange_job(_pair_sum(part_even, recv_even, c_idx, "pair_sum_even"))],
                            "chip_exchange_even")
    group_grads = {"even": _sum_leading(chips_even, "chip_sum_even"), "ffn0": _sum_leading(chips_ffn0, "chip_sum_ffn0"),
                   "odd": _sum_leading(chips_odd, "chip_sum_odd"), "ffn1": _sum_leading(chips_ffn1, "chip_sum_ffn1")}
    grads = {}
    for k, (n, g) in enumerate(small):
        grads[n] = small_sum[small_offs[k]:small_offs[k + 1]].reshape(-1)[:g.size].reshape(g.shape)
    me = 4 * lax.axis_index("x") + 2 * lax.axis_index("y") + lax.axis_index("c")
    shard = c_b_dw.shape[-1]
    for n in conv_names:
        grads[n] = lax.dynamic_slice_in_dim(grads[n], me * shard, shard, axis=1)

    def own_shards(group):
        out, off = [], 0
        for r in group_rows[group]:
            out.append(group_grads[group][off:off + r])
            off += r
        return out

    g_in, grads["even_w_out"] = own_shards("even")
    grads["even_w_in"] = g_in.T
    g_in, grads["odd_w_out"] = own_shards("odd")
    grads["odd_w_in"] = g_in.T
    per_layer = [own_shards("ffn0"), own_shards("ffn1")]
    grads["ffn_w_gate"] = jnp.stack([per_layer[l][0].T for l in range(2)])
    grads["ffn_w_up"] = jnp.stack([per_layer[l][1].T for l in range(2)])
    grads["ffn_w_down"] = jnp.stack([per_layer[l][2] for l in range(2)])
    grads = {n: grads[n].reshape(weights[n].shape) for n in names}

    delta, new_m, new_v = {}, {}, {}
    for n in names:
        shape = weights[n].shape
        view = (-1, shape[-1])
        d, m2, v2 = _adamw(weights[n].reshape(view), grads[n].reshape(view), m_in[n].reshape(view),
                           v_in[n].reshape(view), "adamw_" + n)
        delta[n], new_m[n], new_v[n] = d.reshape(shape), m2.reshape(shape), v2.reshape(shape)

    loss = lax.psum(loss_local[0, 0], ("x", "y", "c"))
    return (loss, dh0[None], *[grads[n] for n in names], *[delta[n] for n in names],
            *[new_m[n] for n in names], *[new_v[n] for n in names])
```

```python
import jax
import jax.numpy as jnp
from jax import lax
from jax.experimental import pallas as pl
from jax.experimental.pallas import tpu as pltpu

f32 = jnp.float32
bf16 = jnp.bfloat16

EPS = 1e-6
D_MODEL = 1024
A_WIDTH = 512
HEAD = 128
N_HEADS = 4
CHUNK = 64
POOL_WINDOWS = (2, 4, 8, 16)
POOL_HALO = 16
C_KERNEL = 31
D_KERNEL = 3
CONV_HALO = 32
D_FF = 2816
N_DEV = 8
N_CHIP = 4

ADAM_LR = 0.001
ADAM_B1 = 0.9
ADAM_B2 = 0.999
ADAM_EPS = 1e-08
ADAM_WD = 0.01
ADAM_STEP = 10

MIX_TILE = 512
FFN_TILE = 512
FFN_BWD_TILE = 256
FFN_CHUNK = 1408
DW_TK = 2048
DW_TM = 1536
MIDDLE_AT, MIDDLE_OF = 7, 8
VMEM_LIMIT = 56 * 1024 * 1024

MESH = pl.DeviceIdType.MESH
ANY = pl.BlockSpec(memory_space=pl.ANY)


def _params(n_axes):
    return pltpu.CompilerParams(dimension_semantics=("arbitrary",) * n_axes, vmem_limit_bytes=VMEM_LIMIT)


def _mm(a, b):
    return jnp.dot(a, b, preferred_element_type=f32)


def _mm_nt(a, b):
    return lax.dot_general(a, b, (((1,), (1,)), ((), ())), preferred_element_type=f32)


def _mm_tn(a, b):
    return lax.dot_general(a, b, (((0,), (0,)), ((), ())), preferred_element_type=f32)


def _sigmoid(x):
    return 1.0 / (1.0 + jnp.exp(-x))


def _rms_r(h):
    return lax.rsqrt(jnp.mean(h * h, axis=-1, keepdims=True) + EPS)


def _rms_bwd(dy, h, g):
    r = _rms_r(h)
    xh = h * r
    dxh = dy * g
    dh = r * (dxh - xh * jnp.mean(dxh * xh, axis=-1, keepdims=True))
    return dh, jnp.sum(dy * xh, axis=0, keepdims=True)


def _ln_fwd(x, g, b):
    mu = jnp.mean(x, axis=-1, keepdims=True)
    xc = x - mu
    r = lax.rsqrt(jnp.mean(xc * xc, axis=-1, keepdims=True) + EPS)
    xh = xc * r
    return xh * g + b, xh, r


def _ln_bwd(dy, xh, r, g):
    dxh = dy * g
    return r * (dxh - jnp.mean(dxh, axis=-1, keepdims=True) - xh * jnp.mean(dxh * xh, axis=-1, keepdims=True))


_GELU_C = 0.7978845608028654
_GELU_A = 0.044715


def _gelu(x):
    th = jnp.tanh(_GELU_C * (x + _GELU_A * x * x * x))
    return 0.5 * x * (1.0 + th), th


def _gelu_grad(x, th):
    return 0.5 * (1.0 + th) + 0.5 * x * (1.0 - th * th) * (_GELU_C * (1.0 + 3.0 * _GELU_A * x * x))


def _down(x, k):
    return x if k == 0 else pltpu.roll(x, k, 0)


def _up(x, k):
    return x if k == 0 else pltpu.roll(x, x.shape[0] - k, 0)


def _window_sum(x, win, shift):
    s = x
    step = 1
    while step < win:
        s = s + shift(s, step)
        step *= 2
    return s


def _inv_count(t0, rows, win):
    t = t0 + lax.broadcasted_iota(jnp.int32, (rows, 1), 0)
    return 1.0 / jnp.minimum(t + 1, win).astype(f32)


def _chunk_mask():
    i = lax.broadcasted_iota(jnp.int32, (HEAD, HEAD), 0)
    j = lax.broadcasted_iota(jnp.int32, (HEAD, HEAD), 1)
    return jnp.logical_or(i >= CHUNK, j < CHUNK)


def _const(shape, n_axes):
    zeros = (0,) * len(shape)
    if n_axes == 1:
        return pl.BlockSpec(shape, lambda i: zeros)
    return pl.BlockSpec(shape, lambda i, j: zeros)


def _prev_halo(tile, halo, cols):
    return pl.BlockSpec((halo, cols), lambda i: (jnp.maximum(i * (tile // halo) - 1, 0), 0))


def _next_halo(tile, halo, cols, seq):
    return pl.BlockSpec((halo, cols), lambda i: (jnp.minimum((i + 1) * (tile // halo), seq // halo - 1), 0))


class _Job:
    def __init__(self, inputs, out_shape, sems, hooks):
        self.inputs, self.out_shape, self.sems, self.hooks = inputs, out_shape, sems, hooks


def _position():
    return lax.axis_index("x"), lax.axis_index("y"), lax.axis_index("c")


def _all_gather_job(block):
    rows, cols = block.shape

    def hooks(ins, outs, sems):
        (x_ref,), (out_ref,), (send_sems, recv_sems, local_sem) = ins, outs, sems
        x, y, c = _position()
        me, sibling = (x, y, c), (x, y, 1 - c)
        chips = [(1 - x, y), (x, 1 - y), (1 - x, 1 - y)]

        def slot(px, py, pc):
            return out_ref.at[4 * px + 2 * py + pc]

        def copy(k, block_of, to, src=None):
            return pltpu.make_async_remote_copy(
                src_ref=slot(*block_of) if src is None else src, dst_ref=slot(*block_of),
                send_sem=send_sems.at[k], recv_sem=recv_sems.at[k], device_id=to, device_id_type=MESH)

        mine = pltpu.make_async_copy(x_ref, slot(*me), local_sem)
        first = [copy(0, me, sibling, src=x_ref)]
        first += [copy(1 + j, me, (*chip, c), src=x_ref) for j, chip in enumerate(chips)]
        passed = [copy(4 + j, (*chip, c), sibling) for j, chip in enumerate(chips)]

        def start():
            mine.start()
            for cp in first:
                cp.start()

        def middle():
            for j, chip in enumerate(chips):
                copy(1 + j, (*chip, c), me).wait_recv()
                passed[j].start()

        def finish():
            copy(0, sibling, me).wait_recv()
            for j, chip in enumerate(chips):
                copy(4 + j, (*chip, 1 - c), me).wait_recv()
            for cp in first + passed:
                cp.wait_send()
            mine.wait()

        return start, middle, finish

    return _Job([block], [jax.ShapeDtypeStruct((N_DEV, rows, cols), block.dtype)],
                [pltpu.SemaphoreType.DMA((7,)), pltpu.SemaphoreType.DMA((7,)), pltpu.SemaphoreType.DMA], hooks)


def _sibling_exchange_job(g):
    _, rows, cols = g.shape

    def hooks(ins, outs, sems):
        (g_ref,), (recv_ref,), (send_sems, recv_sems) = ins, outs, sems
        x, y, c = _position()
        copies = [pltpu.make_async_remote_copy(
            src_ref=g_ref.at[2 * k + (1 - c)], dst_ref=recv_ref.at[k], send_sem=send_sems.at[k],
            recv_sem=recv_sems.at[k], device_id=(x, y, 1 - c), device_id_type=MESH) for k in range(N_CHIP)]

        def start():
            for cp in copies:
                cp.start()

        def finish():
            for cp in copies:
                cp.wait()

        return start, lambda: None, finish

    return _Job([g], [jax.ShapeDtypeStruct((N_CHIP, rows, cols), g.dtype)],
                [pltpu.SemaphoreType.DMA((N_CHIP,)), pltpu.SemaphoreType.DMA((N_CHIP,))], hooks)


def _chip_exchange_job(p):
    _, rows, cols = p.shape

    def hooks(ins, outs, sems):
        (p_ref,), (recv_ref,), (send_sems, recv_sems, local_sem) = ins, outs, sems
        x, y, c = _position()
        k_me = 2 * x + y
        mine = pltpu.make_async_copy(p_ref.at[k_me], recv_ref.at[k_me], local_sem)
        copies = [pltpu.make_async_remote_copy(
            src_ref=p_ref.at[2 * px + py], dst_ref=recv_ref.at[k_me], send_sem=send_sems.at[j],
            recv_sem=recv_sems.at[j], device_id=(px, py, c), device_id_type=MESH)
            for j, (px, py) in enumerate([(1 - x, y), (x, 1 - y), (1 - x, 1 - y)])]

        def start():
            mine.start()
            for cp in copies:
                cp.start()

        def finish():
            for cp in copies:
                cp.wait()
            mine.wait()

        return start, lambda: None, finish

    return _Job([p], [jax.ShapeDtypeStruct((N_CHIP, rows, cols), p.dtype)],
                [pltpu.SemaphoreType.DMA((3,)), pltpu.SemaphoreType.DMA((3,)), pltpu.SemaphoreType.DMA], hooks)


def _job_hooks(jobs, ins, outs, sems):
    hooks = []
    for job in jobs:
        n_in, n_out, n_sem = len(job.inputs), len(job.out_shape), len(job.sems)
        hooks.append(job.hooks(ins[:n_in], outs[:n_out], sems[:n_sem]))
        ins, outs, sems = ins[n_in:], outs[n_out:], sems[n_sem:]
    return hooks


def _run_jobs(jobs, name):
    n_in = sum(len(job.inputs) for job in jobs)
    n_out = sum(len(job.out_shape) for job in jobs)

    def body(*refs):
        hooks = _job_hooks(jobs, refs[:n_in], refs[n_in:n_in + n_out], refs[n_in + n_out:])
        for phase in range(3):
            for h in hooks:
                h[phase]()

    return list(pl.pallas_call(
        body, name=name, in_specs=[ANY] * n_in, out_specs=[ANY] * n_out,
        out_shape=[s for job in jobs for s in job.out_shape],
        scratch_shapes=[s for job in jobs for s in job.sems],
    )(*[a for job in jobs for a in job.inputs]))


def _launch(body, *, name, grid, in_specs, out_specs, out_shape, args, scratch=(), jobs=()):
    in_specs, out_specs, out_shape, scratch = list(in_specs), list(out_specs), list(out_shape), list(scratch)
    if not jobs:
        return list(pl.pallas_call(body, name=name, grid=grid, in_specs=in_specs, out_specs=out_specs,
                                   out_shape=out_shape, scratch_shapes=scratch,
                                   compiler_params=_params(len(grid)))(*args))
    n_in, n_out, n_sc = len(in_specs), len(out_specs), len(scratch)
    j_in = [a for job in jobs for a in job.inputs]
    j_out = [s for job in jobs for s in job.out_shape]
    j_sems = [s for job in jobs for s in job.sems]
    n_steps = 1
    for g in grid:
        n_steps *= g

    def wrapped(*refs):
        ins, refs = refs[:n_in], refs[n_in:]
        jins, refs = refs[:len(j_in)], refs[len(j_in):]
        outs, refs = refs[:n_out], refs[n_out:]
        jouts, refs = refs[:len(j_out)], refs[len(j_out):]
        sc, jsems = refs[:n_sc], refs[n_sc:]
        step = pl.program_id(0)
        for axis in range(1, len(grid)):
            step = step * grid[axis] + pl.program_id(axis)
        hooks = _job_hooks(jobs, jins, jouts, jsems)

        @pl.when(step == 0)
        def _():
            for h in hooks:
                h[0]()

        body(*ins, *outs, *sc)

        @pl.when(step == (MIDDLE_AT * n_steps) // MIDDLE_OF)
        def _():
            for h in hooks:
                h[1]()

        @pl.when(step == n_steps - 1)
        def _():
            for h in hooks:
                h[2]()

    return list(pl.pallas_call(
        wrapped, name=name, grid=grid, in_specs=in_specs + [ANY] * len(j_in), out_specs=out_specs + [ANY] * len(j_out),
        out_shape=out_shape + j_out, scratch_shapes=scratch + j_sems, compiler_params=_params(len(grid)),
    )(*args, *j_in))


def _gmlp_gate(vnb, wsm, bst, tile):
    rows = []
    for n in range(tile // HEAD):
        cols = []
        for hh in range(N_HEADS):
            blk = vnb[n * HEAD:(n + 1) * HEAD, hh * HEAD:(hh + 1) * HEAD]
            cols.append(_mm(wsm[hh], blk) + bst[:, hh:hh + 1])
        rows.append(jnp.concatenate(cols, axis=1))
    return jnp.concatenate(rows, axis=0)


def _even_fwd(h, w_in, w_out, ws, bst, lng, lnb, wp, sc, gm, jobs=()):
    seq = h.shape[0]
    tile = min(MIX_TILE, seq)
    n_tiles = seq // tile

    def body(h_ref, hp_ref, win_ref, wout_ref, ws_ref, bst_ref, lng_ref, lnb_ref, wp_ref, sc_ref, g_ref,
             ho_ref, hn_ref, za_ref, pool_ref, mix_ref):
        i = pl.program_id(0)
        g = g_ref[...]
        h = h_ref[...]
        hnb = (h * _rms_r(h) * g).astype(bf16)
        hn_ref[...] = hnb
        z = _mm_nt(hnb, win_ref[...])
        zab = z[:, :2 * A_WIDTH].astype(bf16)
        za_ref[...] = zab
        hp = hp_ref[...]
        zbp = _mm_nt((hp * _rms_r(hp) * g).astype(bf16), win_ref[2 * A_WIDTH:, :])
        zbe = jnp.concatenate([jnp.where(i > 0, zbp, 0.0), z[:, 2 * A_WIDTH:]], axis=0)
        pooled = []
        for gi, win in enumerate(POOL_WINDOWS):
            xg = zbe[:, gi * HEAD:(gi + 1) * HEAD]
            s = _window_sum(xg, win, _down)
            pooled.append(s[POOL_HALO:] * _inv_count(i * tile, tile, win) - xg[POOL_HALO:])
        plb = jnp.concatenate(pooled, axis=1).astype(bf16)
        pool_ref[...] = plb

        ga, _ = _gelu(zab.astype(f32))
        vn, _, _ = _ln_fwd(ga[:, A_WIDTH:], lng_ref[...], lnb_ref[...])
        mask = _chunk_mask()
        wsm = [jnp.where(mask, ws_ref[hh], 0.0).astype(bf16) for hh in range(N_HEADS)]
        ya = ga[:, :A_WIDTH] * _gmlp_gate(vn.astype(bf16), wsm, bst_ref[...], tile)
        yb = jnp.concatenate([_mm(plb[:, gi * HEAD:(gi + 1) * HEAD], wp_ref[gi].astype(bf16))
                              for gi in range(len(POOL_WINDOWS))], axis=1) * sc_ref[...]
        mix = jnp.concatenate([ya, yb], axis=1).astype(bf16)
        mix_ref[...] = mix
        ho_ref[...] = h + _mm(mix, wout_ref[...])

    row = lambda cols: pl.BlockSpec((tile, cols), lambda i: (i, 0))
    return _launch(
        body, name="even_fwd", grid=(n_tiles,), jobs=jobs,
        in_specs=[row(D_MODEL), _prev_halo(tile, POOL_HALO, D_MODEL), _const(w_in.shape, 1), _const(w_out.shape, 1),
                  _const(ws.shape, 1), _const(bst.shape, 1), _const(lng.shape, 1), _const(lnb.shape, 1),
                  _const(wp.shape, 1), _const(sc.shape, 1), _const(gm.shape, 1)],
        out_specs=[row(D_MODEL), row(D_MODEL), row(2 * A_WIDTH), row(A_WIDTH), row(D_MODEL)],
        out_shape=[jax.ShapeDtypeStruct((seq, D_MODEL), f32), jax.ShapeDtypeStruct((seq, D_MODEL), bf16),
                   jax.ShapeDtypeStruct((seq, 2 * A_WIDTH), bf16), jax.ShapeDtypeStruct((seq, A_WIDTH), bf16),
                   jax.ShapeDtypeStruct((seq, D_MODEL), bf16)],
        args=(h, h, w_in, w_out, ws, bst, lng, lnb, wp, sc, gm))


def _even_bwd(dh, h, za, pooled, w_in, w_out, ws, bst, lng, lnb, wp, sc, gm, jobs=()):
    seq = h.shape[0]
    tile = min(MIX_TILE, seq)
    n_tiles = seq // tile
    n_groups = len(POOL_WINDOWS)

    def body(dh_ref, dhx_ref, h_ref, za_ref, pool_ref, win_ref, wout_ref, ws_ref, bst_ref, lng_ref, lnb_ref,
             wp_ref, sc_ref, g_ref,
             dhi_ref, dz_ref, dws_ref, dbs_ref, dlng_ref, dlnb_ref, dwp_ref, dsc_ref, dg_ref):
        i = pl.program_id(0)

        @pl.when(i == 0)
        def _():
            for ref in (dws_ref, dbs_ref, dlng_ref, dlnb_ref, dwp_ref, dsc_ref, dg_ref):
                ref[...] = jnp.zeros_like(ref)

        dh = dh_ref[...]
        dmix = _mm_nt(dh.astype(bf16), wout_ref[...])
        dya = dmix[:, :A_WIDTH]
        dyb = dmix[:, A_WIDTH:]
        dybx = _mm_nt(dhx_ref[...].astype(bf16), wout_ref[A_WIDTH:, :])
        dybx = jnp.where(i < n_tiles - 1, dybx, 0.0)

        za = za_ref[...].astype(f32)
        ga, th = _gelu(za)
        u = ga[:, :A_WIDTH]
        lng = lng_ref[...]
        vn, vh, r = _ln_fwd(ga[:, A_WIDTH:], lng, lnb_ref[...])
        vnb = vn.astype(bf16)
        mask = _chunk_mask()
        wsf = [jnp.where(mask, ws_ref[hh], 0.0) for hh in range(N_HEADS)]
        sv = _gmlp_gate(vnb, [w.astype(bf16) for w in wsf], bst_ref[...], tile)
        du = dya * sv
        dsvb = (dya * u).astype(bf16)
        wst = [w.T.astype(bf16) for w in wsf]
        ones = jnp.ones((8, HEAD), bf16)
        dws = [jnp.zeros((HEAD, HEAD), f32) for _ in range(N_HEADS)]
        dbs = [jnp.zeros((8, HEAD), f32) for _ in range(N_HEADS)]
        rows = []
        for n in range(tile // HEAD):
            cols = []
            for hh in range(N_HEADS):
                blk = dsvb[n * HEAD:(n + 1) * HEAD, hh * HEAD:(hh + 1) * HEAD]
                cols.append(_mm(wst[hh], blk))
                dws[hh] = dws[hh] + _mm_nt(blk, vnb[n * HEAD:(n + 1) * HEAD, hh * HEAD:(hh + 1) * HEAD])
                dbs[hh] = dbs[hh] + _mm_nt(ones, blk)
            rows.append(jnp.concatenate(cols, axis=1))
        dvn = jnp.concatenate(rows, axis=0)
        for hh in range(N_HEADS):
            dws_ref[hh] += jnp.where(mask, dws[hh], 0.0)
            dbs_ref[pl.ds(hh, 1), :] += dbs[hh][0:1, :]
        dlng_ref[...] += jnp.sum(dvn * vh, axis=0, keepdims=True)
        dlnb_ref[...] += jnp.sum(dvn, axis=0, keepdims=True)
        dv = _ln_bwd(dvn, vh, r, lng)
        dza = jnp.concatenate([du, dv], axis=1) * _gelu_grad(za, th)

        plb = pool_ref[...]
        sc = sc_ref[...]
        dzb = []
        dsc = []
        for gi, win in enumerate(POOL_WINDOWS):
            cs = slice(gi * HEAD, (gi + 1) * HEAD)
            wpb = wp_ref[gi].astype(bf16)
            dsc.append(jnp.sum(dyb[:, cs] * _mm(plb[:, cs], wpb), axis=0, keepdims=True))
            dpre = (dyb[:, cs] * sc[:, cs]).astype(bf16)
            dprex = (dybx[:, cs] * sc[:, cs]).astype(bf16)
            dwp_ref[gi] += _mm_tn(plb[:, cs], dpre)
            dpl = _mm_nt(dpre, wpb)
            dple = jnp.concatenate([dpl, _mm_nt(dprex, wpb)], axis=0)
            q = dple * _inv_count(i * tile, tile + POOL_HALO, win)
            dzb.append(_window_sum(q, win, _up)[:tile] - dpl)
        dsc_ref[...] += jnp.concatenate(dsc, axis=1)

        dzf = jnp.concatenate([dza] + dzb, axis=1).astype(bf16)
        dz_ref[...] = dzf
        dhn = _mm(dzf, win_ref[...])
        dhr, dg = _rms_bwd(dhn, h_ref[...], g_ref[...])
        dhi_ref[...] = dh + dhr
        dg_ref[...] += dg

    row = lambda cols: pl.BlockSpec((tile, cols), lambda i: (i, 0))
    small = [ws.shape, (N_HEADS, HEAD), lng.shape, lnb.shape, wp.shape, sc.shape, gm.shape]
    return _launch(
        body, name="even_bwd", grid=(n_tiles,), jobs=jobs,
        in_specs=[row(D_MODEL), _next_halo(tile, POOL_HALO, D_MODEL, seq), row(D_MODEL), row(2 * A_WIDTH), row(A_WIDTH),
                  _const(w_in.shape, 1), _const(w_out.shape, 1), _const(ws.shape, 1), _const(bst.shape, 1),
                  _const(lng.shape, 1), _const(lnb.shape, 1), _const(wp.shape, 1), _const(sc.shape, 1), _const(gm.shape, 1)],
        out_specs=[row(D_MODEL), row(3 * A_WIDTH)] + [_const(s, 1) for s in small],
        out_shape=[jax.ShapeDtypeStruct((seq, D_MODEL), f32), jax.ShapeDtypeStruct((seq, 3 * A_WIDTH), bf16)]
                  + [jax.ShapeDtypeStruct(s, f32) for s in small],
        args=(dh, dh, h, za, pooled, w_in, w_out, ws, bst, lng, lnb, wp, sc, gm))


SUBLANES = 8


class _Shifted:
    def __init__(self, x, shift, max_shift):
        self.rolled = [shift(x, b) for b in range(min(SUBLANES, max_shift + 1))]
        self.back = shift is _down

    def rows(self, k, start, count):
        whole = k - k % SUBLANES
        lo = start - whole if self.back else start + whole
        return self.rolled[k % SUBLANES][lo:lo + count]


def _conv_taps(xs, w_ref, n_taps, halo, rows):
    acc = None
    for j in range(n_taps):
        term = w_ref[pl.ds(j, 1), :] * xs.rows(n_taps - 1 - j, halo, rows)
        acc = term if acc is None else acc + term
    return acc


def _odd_fwd(h, w_in, w_out, cw, cb, clg, clb, dw, gm):
    seq = h.shape[0]
    tile = min(MIX_TILE, seq)
    n_tiles = seq // tile
    w = A_WIDTH

    def body(h_ref, hp_ref, win_ref, wout_ref, cw_ref, cb_ref, clg_ref, clb_ref, dw_ref, g_ref,
             ho_ref, hn_ref, z_ref, mix_ref, cv_ref):
        i = pl.program_id(0)
        g = g_ref[...]
        h = h_ref[...]
        hnb = (h * _rms_r(h) * g).astype(bf16)
        hn_ref[...] = hnb
        zb = _mm_nt(hnb, win_ref[...]).astype(bf16)
        z_ref[...] = zb
        hp = hp_ref[...]
        zp = _mm_nt((hp * _rms_r(hp) * g).astype(bf16), win_ref[...]).astype(bf16).astype(f32)
        z = zb.astype(f32)
        ze = jnp.concatenate([jnp.where(i > 0, zp, 0.0), z], axis=0)
        hc = ze[:, :w] * _sigmoid(ze[:, w:2 * w])
        cv = _conv_taps(_Shifted(hc, _down, C_KERNEL - 1), cw_ref, C_KERNEL, CONV_HALO, tile) + cb_ref[...]
        cv_ref[...] = cv
        ln, _, _ = _ln_fwd(cv, clg_ref[...], clb_ref[...])
        yc = ln * _sigmoid(ln)
        p = ze[:, 3 * w:4 * w] * ze[:, 4 * w:]
        yd = z[:, 2 * w:3 * w] * _conv_taps(_Shifted(p, _down, D_KERNEL - 1), dw_ref, D_KERNEL, CONV_HALO, tile)
        mix = jnp.concatenate([yc, yd], axis=1).astype(bf16)
        mix_ref[...] = mix
        ho_ref[...] = h + _mm(mix, wout_ref[...])

    row = lambda cols: pl.BlockSpec((tile, cols), lambda i: (i, 0))
    return pl.pallas_call(
        body, name="odd_fwd", grid=(n_tiles,),
        in_specs=[row(D_MODEL), _prev_halo(tile, CONV_HALO, D_MODEL), _const(w_in.shape, 1), _const(w_out.shape, 1),
                  _const(cw.shape, 1), _const(cb.shape, 1), _const(clg.shape, 1), _const(clb.shape, 1),
                  _const(dw.shape, 1), _const(gm.shape, 1)],
        out_specs=[row(D_MODEL), row(D_MODEL), row(5 * w), row(D_MODEL), row(w)],
        out_shape=[jax.ShapeDtypeStruct((seq, D_MODEL), f32), jax.ShapeDtypeStruct((seq, D_MODEL), bf16),
                   jax.ShapeDtypeStruct((seq, 5 * w), bf16), jax.ShapeDtypeStruct((seq, D_MODEL), bf16),
                   jax.ShapeDtypeStruct((seq, w), f32)],
        compiler_params=_params(1),
    )(h, h, w_in, w_out, cw, cb, clg, clb, dw, gm)


def _odd_bwd(dh, h, z, cv, w_in, w_out, cw, cb, clg, clb, dw, gm, jobs=()):
    seq = h.shape[0]
    tile = min(MIX_TILE, seq)
    n_tiles = seq // tile
    w = A_WIDTH
    halo = CONV_HALO

    def body(dh_ref, dhx_ref, h_ref, z_ref, zp_ref, zx_ref, cv_ref, cvx_ref, win_ref, wout_ref, cw_ref, cb_ref,
             clg_ref, clb_ref, dw_ref, g_ref,
             dhi_ref, dz_ref, dcw_ref, dcb_ref, dclg_ref, dclb_ref, ddw_ref, dg_ref):
        i = pl.program_id(0)

        @pl.when(i == 0)
        def _():
            for ref in (dcw_ref, dcb_ref, dclg_ref, dclb_ref, ddw_ref, dg_ref):
                ref[...] = jnp.zeros_like(ref)

        dh = dh_ref[...]
        dhe = jnp.concatenate([dh, jnp.where(i < n_tiles - 1, dhx_ref[...], 0.0)], axis=0)
        dmix = _mm_nt(dhe.astype(bf16), wout_ref[...])
        ze = jnp.concatenate([jnp.where(i > 0, zp_ref[...].astype(f32), 0.0), z_ref[...].astype(f32),
                              zx_ref[...].astype(f32)], axis=0)

        sg = _sigmoid(ze[:, w:2 * w])
        ca = ze[:, :w]
        hc = ca * sg
        hcs = _Shifted(hc, _down, C_KERNEL - 1)
        cv = jnp.concatenate([cv_ref[...], cvx_ref[...]], axis=0)
        clg = clg_ref[...]
        ln, xh, r = _ln_fwd(cv, clg, clb_ref[...])
        sl = _sigmoid(ln)
        dln = dmix[:, :w] * (sl * (1.0 + ln * (1.0 - sl)))
        dclg_ref[...] += jnp.sum((dln * xh)[:tile], axis=0, keepdims=True)
        dclb_ref[...] += jnp.sum(dln[:tile], axis=0, keepdims=True)
        dcv = _ln_bwd(dln, xh, r, clg)
        dcb_ref[...] += jnp.sum(dcv[:tile], axis=0, keepdims=True)
        dcvs = _Shifted(dcv, _up, C_KERNEL - 1)
        dhc = None
        for j in range(C_KERNEL):
            k = C_KERNEL - 1 - j
            dcw_ref[pl.ds(j, 1), :] += jnp.sum(dcv[:tile] * hcs.rows(k, halo, tile), axis=0, keepdims=True)
            term = cw_ref[pl.ds(j, 1), :] * dcvs.rows(k, 0, tile)
            dhc = term if dhc is None else dhc + term
        sgt = sg[halo:halo + tile]
        cat = ca[halo:halo + tile]
        dca = dhc * sgt
        dcg = dhc * cat * sgt * (1.0 - sgt)

        dcgv = ze[:, 3 * w:4 * w]
        dxin = ze[:, 4 * w:]
        p = dcgv * dxin
        ps = _Shifted(p, _down, D_KERNEL - 1)
        q = _conv_taps(ps, dw_ref, D_KERNEL, halo, tile)
        dyd = dmix[:, w:]
        dq = dyd * ze[halo:, 2 * w:3 * w]
        ddbg = dyd[:tile] * q
        dqs = _Shifted(dq, _up, D_KERNEL - 1)
        dp = None
        for j in range(D_KERNEL):
            k = D_KERNEL - 1 - j
            ddw_ref[pl.ds(j, 1), :] += jnp.sum(dq[:tile] * ps.rows(k, halo, tile), axis=0, keepdims=True)
            term = dw_ref[pl.ds(j, 1), :] * dqs.rows(k, 0, tile)
            dp = term if dp is None else dp + term
        ddcg = dp * dxin[halo:halo + tile]
        ddxin = dp * dcgv[halo:halo + tile]

        dzf = jnp.concatenate([dca, dcg, ddbg, ddcg, ddxin], axis=1).astype(bf16)
        dz_ref[...] = dzf
        dhn = _mm(dzf, win_ref[...])
        dhr, dg = _rms_bwd(dhn, h_ref[...], g_ref[...])
        dhi_ref[...] = dh + dhr
        dg_ref[...] += dg

    row = lambda cols: pl.BlockSpec((tile, cols), lambda i: (i, 0))
    small = [cw.shape, cb.shape, clg.shape, clb.shape, dw.shape, gm.shape]
    return _launch(
        body, name="odd_bwd", grid=(n_tiles,), jobs=jobs,
        in_specs=[row(D_MODEL), _next_halo(tile, halo, D_MODEL, seq), row(D_MODEL), row(5 * w),
                  _prev_halo(tile, halo, 5 * w), _next_halo(tile, halo, 5 * w, seq),
                  row(w), _next_halo(tile, halo, w, seq),
                  _const(w_in.shape, 1), _const(w_out.shape, 1), _const(cw.shape, 1), _const(cb.shape, 1),
                  _const(clg.shape, 1), _const(clb.shape, 1), _const(dw.shape, 1), _const(gm.shape, 1)],
        out_specs=[row(D_MODEL), row(5 * w)] + [_const(s, 1) for s in small],
        out_shape=[jax.ShapeDtypeStruct((seq, D_MODEL), f32), jax.ShapeDtypeStruct((seq, 5 * w), bf16)]
                  + [jax.ShapeDtypeStruct(s, f32) for s in small],
        args=(dh, dh, h, z, z, z, cv, cv, w_in, w_out, cw, cb, clg, clb, dw, gm))


def _ffn_fwd(h, wg, wu, wd, gm, jobs=(), head=None):
    seq = h.shape[0]
    tile = min(FFN_TILE, seq)
    n_tiles = seq // tile
    n_chunks = D_FF // FFN_CHUNK

    def body(h_ref, g_ref, wg_ref, wu_ref, wd_ref, *refs):
        if head is None:
            ho_ref, hn_ref, gate_ref, up_ref, acc_ref, hns_ref = refs
        else:
            t_ref, gf_ref, ho_ref, hn_ref, gate_ref, up_ref, loss_ref, dgf_ref, acc_ref, hns_ref = refs
        i = pl.program_id(0)
        j = pl.program_id(1)

        @pl.when(j == 0)
        def _():
            h = h_ref[...]
            hnb = (h * _rms_r(h) * g_ref[...]).astype(bf16)
            hns_ref[...] = hnb
            hn_ref[...] = hnb
            acc_ref[...] = jnp.zeros_like(acc_ref)

        hnb = hns_ref[...]
        gb = _mm_nt(hnb, wg_ref[...]).astype(bf16)
        ub = _mm_nt(hnb, wu_ref[...]).astype(bf16)
        gate_ref[...] = gb
        up_ref[...] = ub
        gf = gb.astype(f32)
        act = gf * _sigmoid(gf) * ub.astype(f32)
        acc_ref[...] += _mm(act.astype(bf16), wd_ref[...])

        @pl.when(j == n_chunks - 1)
        def _():
            ho = h_ref[...] + acc_ref[...]
            if head is None:
                ho_ref[...] = ho
                return

            @pl.when(i == 0)
            def _():
                loss_ref[...] = jnp.zeros_like(loss_ref)
                dgf_ref[...] = jnp.zeros_like(dgf_ref)

            g_final = gf_ref[...]
            err = ho * _rms_r(ho) * g_final - t_ref[...]
            loss_ref[...] += (0.5 / D_MODEL) * jnp.sum(jnp.sum(err * err, axis=1, keepdims=True), axis=0, keepdims=True)
            dho, dg = _rms_bwd(err * (1.0 / D_MODEL), ho, g_final)
            ho_ref[...] = dho
            dgf_ref[...] += dg

    row = pl.BlockSpec((tile, D_MODEL), lambda i, j: (i, 0))
    col = pl.BlockSpec((tile, FFN_CHUNK), lambda i, j: (i, j))
    wblk = pl.BlockSpec((FFN_CHUNK, D_MODEL), lambda i, j: (j, 0))
    in_specs = [row, _const(gm.shape, 2), wblk, wblk, wblk]
    out_specs = [row, row, col, col]
    out_shape = [jax.ShapeDtypeStruct((seq, D_MODEL), f32), jax.ShapeDtypeStruct((seq, D_MODEL), bf16),
                 jax.ShapeDtypeStruct((seq, D_FF), bf16), jax.ShapeDtypeStruct((seq, D_FF), bf16)]
    args = (h, gm, wg, wu, wd)
    if head is not None:
        target, g_final = head
        in_specs += [row, _const(g_final.shape, 2)]
        out_specs += [_const((1, 1), 2), _const(g_final.shape, 2)]
        out_shape += [jax.ShapeDtypeStruct((1, 1), f32), jax.ShapeDtypeStruct(g_final.shape, f32)]
        args += (target, g_final)
    return _launch(
        body, name="ffn_fwd" if head is None else "ffn_fwd_loss", grid=(n_tiles, n_chunks), jobs=jobs,
        in_specs=in_specs, out_specs=out_specs, out_shape=out_shape,
        scratch=[pltpu.VMEM((tile, D_MODEL), f32), pltpu.VMEM((tile, D_MODEL), bf16)], args=args)


def _ffn_bwd(dh, h, gate, up, wg, wu, wd, gm, jobs=()):
    seq = h.shape[0]
    tile = min(FFN_BWD_TILE, seq)
    n_tiles = seq // tile
    n_chunks = D_FF // FFN_CHUNK

    def body(dh_ref, h_ref, g_ref, gate_ref, up_ref, wg_ref, wu_ref, wd_ref,
             dhi_ref, dgate_ref, dup_ref, act_ref, dg_ref):
        @pl.when(pl.program_id(0) == 0)
        def _():
            dg_ref[...] = jnp.zeros_like(dg_ref)

        dh = dh_ref[...]
        dhb = dh.astype(bf16)
        acc = None
        for c in range(n_chunks):
            rows = slice(c * FFN_CHUNK, (c + 1) * FFN_CHUNK)
            dact = _mm_nt(dhb, wd_ref[rows, :])
            gf = gate_ref[:, rows].astype(f32)
            uf = up_ref[:, rows].astype(f32)
            s = _sigmoid(gf)
            silu = gf * s
            act_ref[:, rows] = (silu * uf).astype(bf16)
            dgb = (dact * uf * (s * (1.0 + gf * (1.0 - s)))).astype(bf16)
            dub = (dact * silu).astype(bf16)
            dgate_ref[:, rows] = dgb
            dup_ref[:, rows] = dub
            part = _mm(dgb, wg_ref[rows, :]) + _mm(dub, wu_ref[rows, :])
            acc = part if acc is None else acc + part
        dhr, dg = _rms_bwd(acc, h_ref[...], g_ref[...])
        dhi_ref[...] = dh + dhr
        dg_ref[...] += dg

    row = pl.BlockSpec((tile, D_MODEL), lambda i: (i, 0))
    wide = pl.BlockSpec((tile, D_FF), lambda i: (i, 0))
    return _launch(
        body, name="ffn_bwd", grid=(n_tiles,), jobs=jobs,
        in_specs=[row, row, _const(gm.shape, 1), wide, wide, _const(wg.shape, 1), _const(wu.shape, 1), _const(wd.shape, 1)],
        out_specs=[row, wide, wide, wide, _const(gm.shape, 1)],
        out_shape=[jax.ShapeDtypeStruct((seq, D_MODEL), f32), jax.ShapeDtypeStruct((seq, D_FF), bf16),
                   jax.ShapeDtypeStruct((seq, D_FF), bf16), jax.ShapeDtypeStruct((seq, D_FF), bf16),
                   jax.ShapeDtypeStruct(gm.shape, f32)],
        args=(dh, h, gm, gate, up, wg, wu, wd))


def _weight_grads(pairs, name, jobs=()):
    seq, m = pairs[0][0].shape
    tk = min(DW_TK, seq)
    tm = m if m <= DW_TM else m // 2
    n_k = seq // tk
    n_pairs = len(pairs)

    def body(*refs):
        x_refs = refs[0:2 * n_pairs:2]
        y_refs = refs[1:2 * n_pairs:2]
        o_refs = refs[2 * n_pairs:3 * n_pairs]
        acc_refs = refs[3 * n_pairs:]
        k = pl.program_id(1)
        @pl.when(k == 0)
        def _():
            for acc_ref in acc_refs:
                acc_ref[...] = jnp.zeros_like(acc_ref)

        for x_ref, y_ref, acc_ref in zip(x_refs, y_refs, acc_refs):
            acc_ref[...] += _mm_tn(x_ref[...].astype(bf16), y_ref[...].astype(bf16))

        @pl.when(k == n_k - 1)
        def _():
            for o_ref, acc_ref in zip(o_refs, acc_refs):
                o_ref[...] = acc_ref[...].astype(bf16)

    in_specs = []
    for _ in pairs:
        in_specs += [pl.BlockSpec((tk, tm), lambda j, k: (k, j)), pl.BlockSpec((tk, D_MODEL), lambda j, k: (k, 0))]
    return _launch(
        body, name=name, grid=(m // tm, n_k), jobs=jobs,
        in_specs=in_specs,
        out_specs=[pl.BlockSpec((tm, D_MODEL), lambda j, k: (j, 0))] * n_pairs,
        out_shape=[jax.ShapeDtypeStruct((m, D_MODEL), bf16)] * n_pairs,
        scratch=[pltpu.VMEM((tm, D_MODEL), f32)] * n_pairs,
        args=[a for pair in pairs for a in pair])


def _row_tile(rows, limit=512):
    best = rows
    for t in range(8, min(rows, limit) + 1, 8):
        if rows % t == 0:
            best = t
    return best if rows > limit else rows


def _adamw(w, g, m, v, name):
    rows, cols = w.shape
    tr = _row_tile(rows)

    def body(w_ref, g_ref, m_ref, v_ref, d_ref, mo_ref, vo_ref):
        g = g_ref[...]
        m2 = ADAM_B1 * m_ref[...] + (1.0 - ADAM_B1) * g
        v2 = ADAM_B2 * v_ref[...] + (1.0 - ADAM_B2) * (g * g)
        m_hat = m2 / (1.0 - ADAM_B1 ** ADAM_STEP)
        v_hat = v2 / (1.0 - ADAM_B2 ** ADAM_STEP)
        d_ref[...] = -ADAM_LR * (m_hat / (jnp.sqrt(v_hat) + ADAM_EPS) + ADAM_WD * w_ref[...])
        mo_ref[...] = m2
        vo_ref[...] = v2

    spec = pl.BlockSpec((tr, cols), lambda i: (i, 0))
    return pl.pallas_call(
        body, name=name, grid=(rows // tr,),
        in_specs=[spec] * 4, out_specs=[spec] * 3,
        out_shape=[jax.ShapeDtypeStruct((rows, cols), f32)] * 3,
        compiler_params=_params(1),
    )(w, g, m, v)


def _sum_leading(x, name):
    n, rows, cols = x.shape
    tr = _row_tile(rows)

    def body(x_ref, o_ref):
        acc = x_ref[0].astype(f32)
        for k in range(1, n):
            acc = acc + x_ref[k].astype(f32)
        o_ref[...] = acc

    return pl.pallas_call(
        body, name=name, grid=(rows // tr,),
        in_specs=[pl.BlockSpec((n, tr, cols), lambda i: (0, i, 0))],
        out_specs=pl.BlockSpec((tr, cols), lambda i: (i, 0)),
        out_shape=jax.ShapeDtypeStruct((rows, cols), f32),
        compiler_params=_params(1),
    )(x)


def _pair_sum(g, recv, c_idx, name):
    _, rows, cols = g.shape
    tr = _row_tile(rows)

    def body(c_ref, g_ref, r_ref, o_ref):
        o_ref[...] = (g_ref[...].astype(f32) + r_ref[...].astype(f32)).astype(o_ref.dtype)

    return pl.pallas_call(
        body, name=name,
        grid_spec=pltpu.PrefetchScalarGridSpec(
            num_scalar_prefetch=1, grid=(N_CHIP, rows // tr),
            in_specs=[pl.BlockSpec((1, tr, cols), lambda k, i, c: (2 * k + c[0], i, 0)),
                      pl.BlockSpec((1, tr, cols), lambda k, i, c: (k, i, 0))],
            out_specs=pl.BlockSpec((1, tr, cols), lambda k, i, c: (k, i, 0))),
        out_shape=jax.ShapeDtypeStruct((N_CHIP, rows, cols), g.dtype),
        compiler_params=_params(2),
    )(c_idx, g, recv)


def _pack_rows(w):
    return w.reshape(N_DEV, -1, D_MODEL)


def kernel(x, even_w_in, even_w_out, a_w_s, a_b_s, a_ln_g, a_ln_b, b_w_pool, b_scale, odd_w_in, odd_w_out, c_w_dw, c_b_dw, c_ln_g, c_ln_b, d_w_dw, norm_mix_g, norm_ffn_g, ffn_w_gate, ffn_w_up, ffn_w_down, final_norm_g, loss_target, m_even_w_in, m_even_w_out, m_a_w_s, m_a_b_s, m_a_ln_g, m_a_ln_b, m_b_w_pool, m_b_scale, m_odd_w_in, m_odd_w_out, m_c_w_dw, m_c_b_dw, m_c_ln_g, m_c_ln_b, m_d_w_dw, m_norm_mix_g, m_norm_ffn_g, m_ffn_w_gate, m_ffn_w_up, m_ffn_w_down, m_final_norm_g, v_even_w_in, v_even_w_out, v_a_w_s, v_a_b_s, v_a_ln_g, v_a_ln_b, v_b_w_pool, v_b_scale, v_odd_w_in, v_odd_w_out, v_c_w_dw, v_c_b_dw, v_c_ln_g, v_c_ln_b, v_d_w_dw, v_norm_mix_g, v_norm_ffn_g, v_ffn_w_gate, v_ffn_w_up, v_ffn_w_down, v_final_norm_g):
    weights = dict(even_w_in=even_w_in, even_w_out=even_w_out, a_w_s=a_w_s, a_b_s=a_b_s, a_ln_g=a_ln_g, a_ln_b=a_ln_b,
                   b_w_pool=b_w_pool, b_scale=b_scale, odd_w_in=odd_w_in, odd_w_out=odd_w_out, c_w_dw=c_w_dw,
                   c_b_dw=c_b_dw, c_ln_g=c_ln_g, c_ln_b=c_ln_b, d_w_dw=d_w_dw, norm_mix_g=norm_mix_g,
                   norm_ffn_g=norm_ffn_g, ffn_w_gate=ffn_w_gate, ffn_w_up=ffn_w_up, ffn_w_down=ffn_w_down,
                   final_norm_g=final_norm_g)
    m_in = dict(even_w_in=m_even_w_in, even_w_out=m_even_w_out, a_w_s=m_a_w_s, a_b_s=m_a_b_s, a_ln_g=m_a_ln_g,
                a_ln_b=m_a_ln_b, b_w_pool=m_b_w_pool, b_scale=m_b_scale, odd_w_in=m_odd_w_in, odd_w_out=m_odd_w_out,
                c_w_dw=m_c_w_dw, c_b_dw=m_c_b_dw, c_ln_g=m_c_ln_g, c_ln_b=m_c_ln_b, d_w_dw=m_d_w_dw,
                norm_mix_g=m_norm_mix_g, norm_ffn_g=m_norm_ffn_g, ffn_w_gate=m_ffn_w_gate, ffn_w_up=m_ffn_w_up,
                ffn_w_down=m_ffn_w_down, final_norm_g=m_final_norm_g)
    v_in = dict(even_w_in=v_even_w_in, even_w_out=v_even_w_out, a_w_s=v_a_w_s, a_b_s=v_a_b_s, a_ln_g=v_a_ln_g,
                a_ln_b=v_a_ln_b, b_w_pool=v_b_w_pool, b_scale=v_b_scale, odd_w_in=v_odd_w_in, odd_w_out=v_odd_w_out,
                c_w_dw=v_c_w_dw, c_b_dw=v_c_b_dw, c_ln_g=v_c_ln_g, c_ln_b=v_c_ln_b, d_w_dw=v_d_w_dw,
                norm_mix_g=v_norm_mix_g, norm_ffn_g=v_norm_ffn_g, ffn_w_gate=v_ffn_w_gate, ffn_w_up=v_ffn_w_up,
                ffn_w_down=v_ffn_w_down, final_norm_g=v_final_norm_g)
    names = list(weights)

    group_parts = {
        "even": [even_w_in[0].T, even_w_out[0]],
        "ffn0": [ffn_w_gate[0].T, ffn_w_up[0].T, ffn_w_down[0]],
        "odd": [odd_w_in[0].T, odd_w_out[0]],
        "ffn1": [ffn_w_gate[1].T, ffn_w_up[1].T, ffn_w_down[1]],
    }
    group_rows = {k: [p.shape[0] for p in parts] for k, parts in group_parts.items()}

    def pack(*groups):
        return jnp.concatenate([p for k in groups for p in group_parts[k]], axis=0).astype(bf16)

    def unpack(gathered_blocks, *groups):
        out, off = [], 0
        for k in groups:
            for r in group_rows[k]:
                out.append(gathered_blocks[:, off:off + r, :].reshape(-1, D_MODEL))
                off += r
        return out

    conv_names = ["c_w_dw", "c_b_dw", "c_ln_g", "c_ln_b", "d_w_dw"]
    conv_rows = [C_KERNEL, 1, 1, 1, D_KERNEL]
    conv_local = jnp.concatenate([weights[n].reshape(r, -1) for n, r in zip(conv_names, conv_rows)]
                                 + [jnp.zeros((3, c_b_dw.shape[-1]), f32)], axis=0)
    big_even, conv_all = _run_jobs([_all_gather_job(pack("even")), _all_gather_job(conv_local)], "gather_even_conv")
    w_in_e, w_out_e = unpack(big_even, "even")
    conv_all = conv_all.transpose(1, 0, 2).reshape(conv_local.shape[0], -1)
    conv_offs = [sum(conv_rows[:k]) for k in range(len(conv_rows) + 1)]
    cw, cb, clg, clb, dw = [conv_all[conv_offs[k]:conv_offs[k + 1]] for k in range(len(conv_rows))]

    ws, bst = a_w_s[0], a_b_s[0].T
    lng, lnb, wp, sc = a_ln_g, a_ln_b, b_w_pool[0], b_scale
    gmix = [norm_mix_g[l:l + 1] for l in range(2)]
    gffn = [norm_ffn_g[l:l + 1] for l in range(2)]
    gfin = final_norm_g.reshape(1, D_MODEL)

    h0 = x[0]
    h1, hn_e, za, pooled, mix_e, big_ffn0 = _even_fwd(
        h0, w_in_e, w_out_e, ws, bst, lng, lnb, wp, sc, gmix[0], jobs=[_all_gather_job(pack("ffn0"))])
    w_gate0, w_up0, w_down0 = unpack(big_ffn0, "ffn0")
    h2, hn_f0, gate0, up0, big_rest = _ffn_fwd(h1, w_gate0, w_up0, w_down0, gffn[0],
                                               jobs=[_all_gather_job(pack("odd", "ffn1"))])
    w_in_o, w_out_o, w_gate1, w_up1, w_down1 = unpack(big_rest, "odd", "ffn1")
    h3, hn_o, z_o, mix_o, cv_o = _odd_fwd(h2, w_in_o, w_out_o, cw, cb, clg, clb, dw, gmix[1])
    dh4, hn_f1, gate1, up1, loss_local, g_final = _ffn_fwd(h3, w_gate1, w_up1, w_down1, gffn[1],
                                                           head=(loss_target[0], gfin))

    c_idx = lax.axis_index("c").astype(jnp.int32).reshape(1)

    def shard_major(gs):
        return jnp.concatenate([_pack_rows(g) for g in gs], axis=1)

    dh3, dgate1, dup1, act1, g_ffn1 = _ffn_bwd(dh4, h3, gate1, up1, w_gate1, w_up1, w_down1, gffn[1])
    part_ffn1 = shard_major(_weight_grads([(dgate1, hn_f1)], "dw_gate1") + _weight_grads([(dup1, hn_f1)], "dw_up1")
                            + _weight_grads([(act1, dh4)], "dw_down1"))
    dh2, dz_o, g_cw, g_cb, g_clg, g_clb, g_dw, g_mix1, recv_ffn1 = _odd_bwd(
        dh3, h2, z_o, cv_o, w_in_o, w_out_o, cw, cb, clg, clb, dw, gmix[1], jobs=[_sibling_exchange_job(part_ffn1)])
    pair_ffn1 = _pair_sum(part_ffn1, recv_ffn1, c_idx, "pair_sum_ffn1")
    part_odd = shard_major(_weight_grads([(dz_o, hn_o)], "dw_odd_in") + _weight_grads([(mix_o, dh3)], "dw_odd_out"))
    dh1, dgate0, dup0, act0, g_ffn0, chips_ffn1, recv_odd = _ffn_bwd(
        dh2, h1, gate0, up0, w_gate0, w_up0, w_down0, gffn[0],
        jobs=[_chip_exchange_job(pair_ffn1), _sibling_exchange_job(part_odd)])
    pair_odd = _pair_sum(part_odd, recv_odd, c_idx, "pair_sum_odd")
    dw_gate0, chips_odd = _weight_grads([(dgate0, hn_f0)], "dw_gate0", jobs=[_chip_exchange_job(pair_odd)])
    part_ffn0 = shard_major([dw_gate0] + _weight_grads([(dup0, hn_f0)], "dw_up0")
                            + _weight_grads([(act0, dh2)], "dw_down0"))
    dw_even_out, recv_ffn0 = _weight_grads([(mix_e, dh1)], "dw_even_out", jobs=[_sibling_exchange_job(part_ffn0)])
    part_even_out = _pack_rows(dw_even_out)
    pair_ffn0 = _pair_sum(part_ffn0, recv_ffn0, c_idx, "pair_sum_ffn0")
    dh0, dz_e, g_ws, g_bs, g_lng, g_lnb, g_wp, g_sc, g_mix0, chips_ffn0, recv_even_out = _even_bwd(
        dh1, h0, za, pooled, w_in_e, w_out_e, ws, bst, lng, lnb, wp, sc, gmix[0],
        jobs=[_chip_exchange_job(pair_ffn0), _sibling_exchange_job(part_even_out)])
    pair_even_out = _pair_sum(part_even_out, recv_even_out, c_idx, "pair_sum_even_out")

    lanes = HEAD
    small = [("a_w_s", g_ws), ("a_b_s", g_bs), ("a_ln_g", g_lng), ("a_ln_b", g_lnb), ("b_w_pool", g_wp),
             ("b_scale", g_sc), ("norm_mix_g", jnp.concatenate([g_mix0, g_mix1], axis=0)),
             ("norm_ffn_g", jnp.concatenate([g_ffn0, g_ffn1], axis=0)), ("final_norm_g", g_final),
             ("c_w_dw", g_cw), ("c_b_dw", g_cb), ("c_ln_g", g_clg), ("c_ln_b", g_clb), ("d_w_dw", g_dw)]
    small_rows = [-(-g.size // (8 * lanes)) * 8 for _, g in small]
    small_offs = [sum(small_rows[:k]) for k in range(len(small) + 1)]
    pad_rows = -small_offs[-1] % 256
    small_buf = jnp.concatenate(
        [jnp.pad(g.reshape(-1), (0, r * lanes - g.size)).reshape(r, lanes) for (_, g), r in zip(small, small_rows)]
        + [jnp.zeros((pad_rows, lanes), f32)], axis=0)
    dw_even_in, small_all, chips_even_out = _weight_grads(
        [(dz_e, hn_e)], "dw_even_in", jobs=[_all_gather_job(small_buf), _chip_exchange_job(pair_even_out)])
    small_sum = _sum_leading(small_all, "small_grad_sum")
    part_even_in = _pack_rows(dw_even_in)
    recv_even_in, = _run_jobs([_sibling_exchange_job(part_even_in)], "sibling_exchange_even_in")
    chips_even_in, = _run_jobs(
        [_chip_exchange_job(_pair_sum(part_even_in, recv_even_in, c_idx, "pair_sum_even_in"))], "chip_exchange_even_in")
    group_grads = {"even": jnp.concatenate([_sum_leading(chips_even_in, "chip_sum_even_in"),
                                            _sum_leading(chips_even_out, "chip_sum_even_out")], axis=0),
                   "ffn0": _sum_leading(chips_ffn0, "chip_sum_ffn0"),
                   "odd": _sum_leading(chips_odd, "chip_sum_odd"), "ffn1": _sum_leading(chips_ffn1, "chip_sum_ffn1")}
    grads = {}
    for k, (n, g) in enumerate(small):
        grads[n] = small_sum[small_offs[k]:small_offs[k + 1]].reshape(-1)[:g.size].reshape(g.shape)
    me = 4 * lax.axis_index("x") + 2 * lax.axis_index("y") + lax.axis_index("c")
    shard = c_b_dw.shape[-1]
    for n in conv_names:
        grads[n] = lax.dynamic_slice_in_dim(grads[n], me * shard, shard, axis=1)

    def own_shards(group):
        out, off = [], 0
        for r in group_rows[group]:
            out.append(group_grads[group][off:off + r])
            off += r
        return out

    g_in, grads["even_w_out"] = own_shards("even")
    grads["even_w_in"] = g_in.T
    g_in, grads["odd_w_out"] = own_shards("odd")
    grads["odd_w_in"] = g_in.T
    per_layer = [own_shards("ffn0"), own_shards("ffn1")]
    grads["ffn_w_gate"] = jnp.stack([per_layer[l][0].T for l in range(2)])
    grads["ffn_w_up"] = jnp.stack([per_layer[l][1].T for l in range(2)])
    grads["ffn_w_down"] = jnp.stack([per_layer[l][2] for l in range(2)])
    grads = {n: grads[n].reshape(weights[n].shape) for n in names}

    delta, new_m, new_v = {}, {}, {}
    for n in names:
        shape = weights[n].shape
        view = (-1, shape[-1])
        d, m2, v2 = _adamw(weights[n].reshape(view), grads[n].reshape(view), m_in[n].reshape(view),
                           v_in[n].reshape(view), "adamw_" + n)
        delta[n], new_m[n], new_v[n] = d.reshape(shape), m2.reshape(shape), v2.reshape(shape)

    loss = lax.psum(loss_local[0, 0], ("x", "y", "c"))
    return (loss, dh0[None], *[grads[n] for n in names], *[delta[n] for n in names],
            *[new_m[n] for n in names], *[new_v[n] for n in names])
```

```python
import jax
import jax.numpy as jnp
from jax import lax
from jax.experimental import pallas as pl
from jax.experimental.pallas import tpu as pltpu

f32 = jnp.float32
bf16 = jnp.bfloat16

EPS = 1e-6
D_MODEL = 1024
A_WIDTH = 512
HEAD = 128
N_HEADS = 4
CHUNK = 64
POOL_WINDOWS = (2, 4, 8, 16)
POOL_HALO = 16
C_KERNEL = 31
D_KERNEL = 3
CONV_HALO = 32
D_FF = 2816
N_DEV = 8
N_CHIP = 4

ADAM_LR = 0.001
ADAM_B1 = 0.9
ADAM_B2 = 0.999
ADAM_EPS = 1e-08
ADAM_WD = 0.01
ADAM_STEP = 10

MIX_TILE = 512
FFN_TILE = 256
FFN_CHUNKS = (1536, 1280)
DW_TK = 2048
DW_TM = 1536
MIDDLE_AT, MIDDLE_OF = 7, 8
VMEM_LIMIT = 56 * 1024 * 1024

MESH = pl.DeviceIdType.MESH
ANY = pl.BlockSpec(memory_space=pl.ANY)


def _params(n_axes):
    return pltpu.CompilerParams(dimension_semantics=("arbitrary",) * n_axes, vmem_limit_bytes=VMEM_LIMIT)


def _mm(a, b):
    return jnp.dot(a, b, preferred_element_type=f32)


def _mm_nt(a, b):
    return lax.dot_general(a, b, (((1,), (1,)), ((), ())), preferred_element_type=f32)


def _mm_tn(a, b):
    return lax.dot_general(a, b, (((0,), (0,)), ((), ())), preferred_element_type=f32)


def _sigmoid(x):
    return 1.0 / (1.0 + jnp.exp(-x))


def _rms_r(h):
    return lax.rsqrt(jnp.mean(h * h, axis=-1, keepdims=True) + EPS)


def _rms_bwd(dy, h, g):
    r = _rms_r(h)
    xh = h * r
    dxh = dy * g
    dh = r * (dxh - xh * jnp.mean(dxh * xh, axis=-1, keepdims=True))
    return dh, jnp.sum(dy * xh, axis=0, keepdims=True)


def _ln_fwd(x, g, b):
    mu = jnp.mean(x, axis=-1, keepdims=True)
    xc = x - mu
    r = lax.rsqrt(jnp.mean(xc * xc, axis=-1, keepdims=True) + EPS)
    xh = xc * r
    return xh * g + b, xh, r


def _ln_bwd(dy, xh, r, g):
    dxh = dy * g
    return r * (dxh - jnp.mean(dxh, axis=-1, keepdims=True) - xh * jnp.mean(dxh * xh, axis=-1, keepdims=True))


_GELU_C = 0.7978845608028654
_GELU_A = 0.044715


def _gelu(x):
    th = jnp.tanh(_GELU_C * (x + _GELU_A * x * x * x))
    return 0.5 * x * (1.0 + th), th


def _gelu_grad(x, th):
    return 0.5 * (1.0 + th) + 0.5 * x * (1.0 - th * th) * (_GELU_C * (1.0 + 3.0 * _GELU_A * x * x))


def _down(x, k):
    return x if k == 0 else pltpu.roll(x, k, 0)


def _up(x, k):
    return x if k == 0 else pltpu.roll(x, x.shape[0] - k, 0)


def _window_sum(x, win, shift):
    s = x
    step = 1
    while step < win:
        s = s + shift(s, step)
        step *= 2
    return s


def _inv_count(t0, rows, win):
    t = t0 + lax.broadcasted_iota(jnp.int32, (rows, 1), 0)
    return 1.0 / jnp.minimum(t + 1, win).astype(f32)


def _chunk_mask():
    i = lax.broadcasted_iota(jnp.int32, (HEAD, HEAD), 0)
    j = lax.broadcasted_iota(jnp.int32, (HEAD, HEAD), 1)
    return jnp.logical_or(i >= CHUNK, j < CHUNK)


def _const(shape, n_axes):
    zeros = (0,) * len(shape)
    if n_axes == 1:
        return pl.BlockSpec(shape, lambda i: zeros)
    return pl.BlockSpec(shape, lambda i, j: zeros)


def _prev_halo(tile, halo, cols):
    return pl.BlockSpec((halo, cols), lambda i: (jnp.maximum(i * (tile // halo) - 1, 0), 0))


def _next_halo(tile, halo, cols, seq):
    return pl.BlockSpec((halo, cols), lambda i: (jnp.minimum((i + 1) * (tile // halo), seq // halo - 1), 0))


class _Job:
    def __init__(self, inputs, out_shape, sems, hooks):
        self.inputs, self.out_shape, self.sems, self.hooks = inputs, out_shape, sems, hooks


def _position():
    return lax.axis_index("x"), lax.axis_index("y"), lax.axis_index("c")


def _all_gather_job(block):
    rows, cols = block.shape

    def hooks(ins, outs, sems):
        (x_ref,), (out_ref,), (send_sems, recv_sems, local_sem) = ins, outs, sems
        x, y, c = _position()
        me, sibling = (x, y, c), (x, y, 1 - c)
        chips = [(1 - x, y), (x, 1 - y), (1 - x, 1 - y)]

        def slot(px, py, pc):
            return out_ref.at[4 * px + 2 * py + pc]

        def copy(k, block_of, to, src=None):
            return pltpu.make_async_remote_copy(
                src_ref=slot(*block_of) if src is None else src, dst_ref=slot(*block_of),
                send_sem=send_sems.at[k], recv_sem=recv_sems.at[k], device_id=to, device_id_type=MESH)

        mine = pltpu.make_async_copy(x_ref, slot(*me), local_sem)
        first = [copy(0, me, sibling, src=x_ref)]
        first += [copy(1 + j, me, (*chip, c), src=x_ref) for j, chip in enumerate(chips)]
        passed = [copy(4 + j, (*chip, c), sibling) for j, chip in enumerate(chips)]

        def start():
            mine.start()
            for cp in first:
                cp.start()

        def middle():
            for j, chip in enumerate(chips):
                copy(1 + j, (*chip, c), me).wait_recv()
                passed[j].start()

        def finish():
            copy(0, sibling, me).wait_recv()
            for j, chip in enumerate(chips):
                copy(4 + j, (*chip, 1 - c), me).wait_recv()
            for cp in first + passed:
                cp.wait_send()
            mine.wait()

        return start, middle, finish

    return _Job([block], [jax.ShapeDtypeStruct((N_DEV, rows, cols), block.dtype)],
                [pltpu.SemaphoreType.DMA((7,)), pltpu.SemaphoreType.DMA((7,)), pltpu.SemaphoreType.DMA], hooks)


def _sibling_exchange_job(g):
    _, rows, cols = g.shape

    def hooks(ins, outs, sems):
        (g_ref,), (recv_ref,), (send_sems, recv_sems) = ins, outs, sems
        x, y, c = _position()
        copies = [pltpu.make_async_remote_copy(
            src_ref=g_ref.at[2 * k + (1 - c)], dst_ref=recv_ref.at[k], send_sem=send_sems.at[k],
            recv_sem=recv_sems.at[k], device_id=(x, y, 1 - c), device_id_type=MESH) for k in range(N_CHIP)]

        def start():
            for cp in copies:
                cp.start()

        def finish():
            for cp in copies:
                cp.wait()

        return start, lambda: None, finish

    return _Job([g], [jax.ShapeDtypeStruct((N_CHIP, rows, cols), g.dtype)],
                [pltpu.SemaphoreType.DMA((N_CHIP,)), pltpu.SemaphoreType.DMA((N_CHIP,))], hooks)


def _chip_exchange_job(p):
    _, rows, cols = p.shape

    def hooks(ins, outs, sems):
        (p_ref,), (recv_ref,), (send_sems, recv_sems, local_sem) = ins, outs, sems
        x, y, c = _position()
        k_me = 2 * x + y
        mine = pltpu.make_async_copy(p_ref.at[k_me], recv_ref.at[k_me], local_sem)
        copies = [pltpu.make_async_remote_copy(
            src_ref=p_ref.at[2 * px + py], dst_ref=recv_ref.at[k_me], send_sem=send_sems.at[j],
            recv_sem=recv_sems.at[j], device_id=(px, py, c), device_id_type=MESH)
            for j, (px, py) in enumerate([(1 - x, y), (x, 1 - y), (1 - x, 1 - y)])]

        def start():
            mine.start()
            for cp in copies:
                cp.start()

        def finish():
            for cp in copies:
                cp.wait()
            mine.wait()

        return start, lambda: None, finish

    return _Job([p], [jax.ShapeDtypeStruct((N_CHIP, rows, cols), p.dtype)],
                [pltpu.SemaphoreType.DMA((3,)), pltpu.SemaphoreType.DMA((3,)), pltpu.SemaphoreType.DMA], hooks)


def _job_hooks(jobs, ins, outs, sems):
    hooks = []
    for job in jobs:
        n_in, n_out, n_sem = len(job.inputs), len(job.out_shape), len(job.sems)
        hooks.append(job.hooks(ins[:n_in], outs[:n_out], sems[:n_sem]))
        ins, outs, sems = ins[n_in:], outs[n_out:], sems[n_sem:]
    return hooks


def _run_jobs(jobs, name):
    n_in = sum(len(job.inputs) for job in jobs)
    n_out = sum(len(job.out_shape) for job in jobs)

    def body(*refs):
        hooks = _job_hooks(jobs, refs[:n_in], refs[n_in:n_in + n_out], refs[n_in + n_out:])
        for phase in range(3):
            for h in hooks:
                h[phase]()

    return list(pl.pallas_call(
        body, name=name, in_specs=[ANY] * n_in, out_specs=[ANY] * n_out,
        out_shape=[s for job in jobs for s in job.out_shape],
        scratch_shapes=[s for job in jobs for s in job.sems],
    )(*[a for job in jobs for a in job.inputs]))


def _launch(body, *, name, grid, in_specs, out_specs, out_shape, args, scratch=(), jobs=()):
    in_specs, out_specs, out_shape, scratch = list(in_specs), list(out_specs), list(out_shape), list(scratch)
    if not jobs:
        return list(pl.pallas_call(body, name=name, grid=grid, in_specs=in_specs, out_specs=out_specs,
                                   out_shape=out_shape, scratch_shapes=scratch,
                                   compiler_params=_params(len(grid)))(*args))
    n_in, n_out, n_sc = len(in_specs), len(out_specs), len(scratch)
    j_in = [a for job in jobs for a in job.inputs]
    j_out = [s for job in jobs for s in job.out_shape]
    j_sems = [s for job in jobs for s in job.sems]
    n_steps = 1
    for g in grid:
        n_steps *= g

    def wrapped(*refs):
        ins, refs = refs[:n_in], refs[n_in:]
        jins, refs = refs[:len(j_in)], refs[len(j_in):]
        outs, refs = refs[:n_out], refs[n_out:]
        jouts, refs = refs[:len(j_out)], refs[len(j_out):]
        sc, jsems = refs[:n_sc], refs[n_sc:]
        step = pl.program_id(0)
        for axis in range(1, len(grid)):
            step = step * grid[axis] + pl.program_id(axis)
        hooks = _job_hooks(jobs, jins, jouts, jsems)

        @pl.when(step == 0)
        def _():
            for h in hooks:
                h[0]()

        body(*ins, *outs, *sc)

        @pl.when(step == (MIDDLE_AT * n_steps) // MIDDLE_OF)
        def _():
            for h in hooks:
                h[1]()

        @pl.when(step == n_steps - 1)
        def _():
            for h in hooks:
                h[2]()

    return list(pl.pallas_call(
        wrapped, name=name, grid=grid, in_specs=in_specs + [ANY] * len(j_in), out_specs=out_specs + [ANY] * len(j_out),
        out_shape=out_shape + j_out, scratch_shapes=scratch + j_sems, compiler_params=_params(len(grid)),
    )(*args, *j_in))


def _gmlp_gate(vnb, wsm, bst, tile):
    rows = []
    for n in range(tile // HEAD):
        cols = []
        for hh in range(N_HEADS):
            blk = vnb[n * HEAD:(n + 1) * HEAD, hh * HEAD:(hh + 1) * HEAD]
            cols.append(_mm(wsm[hh], blk) + bst[:, hh:hh + 1])
        rows.append(jnp.concatenate(cols, axis=1))
    return jnp.concatenate(rows, axis=0)


def _even_fwd(h, w_in, w_out, ws, bst, lng, lnb, wp, sc, gm, jobs=()):
    seq = h.shape[0]
    tile = min(MIX_TILE, seq)
    n_tiles = seq // tile

    def body(h_ref, hp_ref, win_ref, wout_ref, ws_ref, bst_ref, lng_ref, lnb_ref, wp_ref, sc_ref, g_ref,
             ho_ref, hn_ref, za_ref, pool_ref, mix_ref):
        i = pl.program_id(0)
        g = g_ref[...]
        h = h_ref[...]
        hnb = (h * _rms_r(h) * g).astype(bf16)
        hn_ref[...] = hnb
        z = _mm_nt(hnb, win_ref[...])
        zab = z[:, :2 * A_WIDTH].astype(bf16)
        za_ref[...] = zab
        hp = hp_ref[...]
        zbp = _mm_nt((hp * _rms_r(hp) * g).astype(bf16), win_ref[2 * A_WIDTH:, :])
        zbe = jnp.concatenate([jnp.where(i > 0, zbp, 0.0), z[:, 2 * A_WIDTH:]], axis=0)
        pooled = []
        for gi, win in enumerate(POOL_WINDOWS):
            xg = zbe[:, gi * HEAD:(gi + 1) * HEAD]
            s = _window_sum(xg, win, _down)
            pooled.append(s[POOL_HALO:] * _inv_count(i * tile, tile, win) - xg[POOL_HALO:])
        plb = jnp.concatenate(pooled, axis=1).astype(bf16)
        pool_ref[...] = plb

        ga, _ = _gelu(zab.astype(f32))
        vn, _, _ = _ln_fwd(ga[:, A_WIDTH:], lng_ref[...], lnb_ref[...])
        mask = _chunk_mask()
        wsm = [jnp.where(mask, ws_ref[hh], 0.0).astype(bf16) for hh in range(N_HEADS)]
        ya = ga[:, :A_WIDTH] * _gmlp_gate(vn.astype(bf16), wsm, bst_ref[...], tile)
        yb = jnp.concatenate([_mm(plb[:, gi * HEAD:(gi + 1) * HEAD], wp_ref[gi].astype(bf16))
                              for gi in range(len(POOL_WINDOWS))], axis=1) * sc_ref[...]
        mix = jnp.concatenate([ya, yb], axis=1).astype(bf16)
        mix_ref[...] = mix
        ho_ref[...] = h + _mm(mix, wout_ref[...])

    row = lambda cols: pl.BlockSpec((tile, cols), lambda i: (i, 0))
    return _launch(
        body, name="even_fwd", grid=(n_tiles,), jobs=jobs,
        in_specs=[row(D_MODEL), _prev_halo(tile, POOL_HALO, D_MODEL), _const(w_in.shape, 1), _const(w_out.shape, 1),
                  _const(ws.shape, 1), _const(bst.shape, 1), _const(lng.shape, 1), _const(lnb.shape, 1),
                  _const(wp.shape, 1), _const(sc.shape, 1), _const(gm.shape, 1)],
        out_specs=[row(D_MODEL), row(D_MODEL), row(2 * A_WIDTH), row(A_WIDTH), row(D_MODEL)],
        out_shape=[jax.ShapeDtypeStruct((seq, D_MODEL), f32), jax.ShapeDtypeStruct((seq, D_MODEL), bf16),
                   jax.ShapeDtypeStruct((seq, 2 * A_WIDTH), bf16), jax.ShapeDtypeStruct((seq, A_WIDTH), bf16),
                   jax.ShapeDtypeStruct((seq, D_MODEL), bf16)],
        args=(h, h, w_in, w_out, ws, bst, lng, lnb, wp, sc, gm))


def _even_bwd(dh, h, za, pooled, w_in, w_out, ws, bst, lng, lnb, wp, sc, gm, jobs=()):
    seq = h.shape[0]
    tile = min(MIX_TILE, seq)
    n_tiles = seq // tile
    n_groups = len(POOL_WINDOWS)

    def body(dh_ref, dhx_ref, h_ref, za_ref, pool_ref, win_ref, wout_ref, ws_ref, bst_ref, lng_ref, lnb_ref,
             wp_ref, sc_ref, g_ref,
             dhi_ref, dz_ref, dws_ref, dbs_ref, dlng_ref, dlnb_ref, dwp_ref, dsc_ref, dg_ref):
        i = pl.program_id(0)

        @pl.when(i == 0)
        def _():
            for ref in (dws_ref, dbs_ref, dlng_ref, dlnb_ref, dwp_ref, dsc_ref, dg_ref):
                ref[...] = jnp.zeros_like(ref)

        dh = dh_ref[...]
        dmix = _mm_nt(dh.astype(bf16), wout_ref[...])
        dya = dmix[:, :A_WIDTH]
        dyb = dmix[:, A_WIDTH:]
        dybx = _mm_nt(dhx_ref[...].astype(bf16), wout_ref[A_WIDTH:, :])
        dybx = jnp.where(i < n_tiles - 1, dybx, 0.0)

        za = za_ref[...].astype(f32)
        ga, th = _gelu(za)
        u = ga[:, :A_WIDTH]
        lng = lng_ref[...]
        vn, vh, r = _ln_fwd(ga[:, A_WIDTH:], lng, lnb_ref[...])
        vnb = vn.astype(bf16)
        mask = _chunk_mask()
        wsf = [jnp.where(mask, ws_ref[hh], 0.0) for hh in range(N_HEADS)]
        sv = _gmlp_gate(vnb, [w.astype(bf16) for w in wsf], bst_ref[...], tile)
        du = dya * sv
        dsvb = (dya * u).astype(bf16)
        wst = [w.T.astype(bf16) for w in wsf]
        ones = jnp.ones((8, HEAD), bf16)
        dws = [jnp.zeros((HEAD, HEAD), f32) for _ in range(N_HEADS)]
        dbs = [jnp.zeros((8, HEAD), f32) for _ in range(N_HEADS)]
        rows = []
        for n in range(tile // HEAD):
            cols = []
            for hh in range(N_HEADS):
                blk = dsvb[n * HEAD:(n + 1) * HEAD, hh * HEAD:(hh + 1) * HEAD]
                cols.append(_mm(wst[hh], blk))
                dws[hh] = dws[hh] + _mm_nt(blk, vnb[n * HEAD:(n + 1) * HEAD, hh * HEAD:(hh + 1) * HEAD])
                dbs[hh] = dbs[hh] + _mm_nt(ones, blk)
            rows.append(jnp.concatenate(cols, axis=1))
        dvn = jnp.concatenate(rows, axis=0)
        for hh in range(N_HEADS):
            dws_ref[hh] += jnp.where(mask, dws[hh], 0.0)
            dbs_ref[pl.ds(hh, 1), :] += dbs[hh][0:1, :]
        dlng_ref[...] += jnp.sum(dvn * vh, axis=0, keepdims=True)
        dlnb_ref[...] += jnp.sum(dvn, axis=0, keepdims=True)
        dv = _ln_bwd(dvn, vh, r, lng)
        dza = jnp.concatenate([du, dv], axis=1) * _gelu_grad(za, th)

        plb = pool_ref[...]
        sc = sc_ref[...]
        dzb = []
        dsc = []
        for gi, win in enumerate(POOL_WINDOWS):
            cs = slice(gi * HEAD, (gi + 1) * HEAD)
            wpb = wp_ref[gi].astype(bf16)
            dsc.append(jnp.sum(dyb[:, cs] * _mm(plb[:, cs], wpb), axis=0, keepdims=True))
            dpre = (dyb[:, cs] * sc[:, cs]).astype(bf16)
            dprex = (dybx[:, cs] * sc[:, cs]).astype(bf16)
            dwp_ref[gi] += _mm_tn(plb[:, cs], dpre)
            dpl = _mm_nt(dpre, wpb)
            dple = jnp.concatenate([dpl, _mm_nt(dprex, wpb)], axis=0)
            q = dple * _inv_count(i * tile, tile + POOL_HALO, win)
            dzb.append(_window_sum(q, win, _up)[:tile] - dpl)
        dsc_ref[...] += jnp.concatenate(dsc, axis=1)

        dzf = jnp.concatenate([dza] + dzb, axis=1).astype(bf16)
        dz_ref[...] = dzf
        dhn = _mm(dzf, win_ref[...])
        dhr, dg = _rms_bwd(dhn, h_ref[...], g_ref[...])
        dhi_ref[...] = dh + dhr
        dg_ref[...] += dg

    row = lambda cols: pl.BlockSpec((tile, cols), lambda i: (i, 0))
    small = [ws.shape, (N_HEADS, HEAD), lng.shape, lnb.shape, wp.shape, sc.shape, gm.shape]
    return _launch(
        body, name="even_bwd", grid=(n_tiles,), jobs=jobs,
        in_specs=[row(D_MODEL), _next_halo(tile, POOL_HALO, D_MODEL, seq), row(D_MODEL), row(2 * A_WIDTH), row(A_WIDTH),
                  _const(w_in.shape, 1), _const(w_out.shape, 1), _const(ws.shape, 1), _const(bst.shape, 1),
                  _const(lng.shape, 1), _const(lnb.shape, 1), _const(wp.shape, 1), _const(sc.shape, 1), _const(gm.shape, 1)],
        out_specs=[row(D_MODEL), row(3 * A_WIDTH)] + [_const(s, 1) for s in small],
        out_shape=[jax.ShapeDtypeStruct((seq, D_MODEL), f32), jax.ShapeDtypeStruct((seq, 3 * A_WIDTH), bf16)]
                  + [jax.ShapeDtypeStruct(s, f32) for s in small],
        args=(dh, dh, h, za, pooled, w_in, w_out, ws, bst, lng, lnb, wp, sc, gm))


SUBLANES = 8


class _Shifted:
    def __init__(self, x, shift, max_shift):
        self.rolled = [shift(x, b) for b in range(min(SUBLANES, max_shift + 1))]
        self.back = shift is _down

    def rows(self, k, start, count):
        whole = k - k % SUBLANES
        lo = start - whole if self.back else start + whole
        return self.rolled[k % SUBLANES][lo:lo + count]


def _conv_taps(xs, w_ref, n_taps, halo, rows):
    acc = None
    for j in range(n_taps):
        term = w_ref[pl.ds(j, 1), :] * xs.rows(n_taps - 1 - j, halo, rows)
        acc = term if acc is None else acc + term
    return acc


def _odd_fwd(h, w_in, w_out, cw, cb, clg, clb, dw, gm):
    seq = h.shape[0]
    tile = min(MIX_TILE, seq)
    n_tiles = seq // tile
    w = A_WIDTH

    def body(h_ref, hp_ref, win_ref, wout_ref, cw_ref, cb_ref, clg_ref, clb_ref, dw_ref, g_ref,
             ho_ref, hn_ref, z_ref, mix_ref, cv_ref):
        i = pl.program_id(0)
        g = g_ref[...]
        h = h_ref[...]
        hnb = (h * _rms_r(h) * g).astype(bf16)
        hn_ref[...] = hnb
        zb = _mm_nt(hnb, win_ref[...]).astype(bf16)
        z_ref[...] = zb
        hp = hp_ref[...]
        zp = _mm_nt((hp * _rms_r(hp) * g).astype(bf16), win_ref[...]).astype(bf16).astype(f32)
        z = zb.astype(f32)
        ze = jnp.concatenate([jnp.where(i > 0, zp, 0.0), z], axis=0)
        hc = ze[:, :w] * _sigmoid(ze[:, w:2 * w])
        cv = _conv_taps(_Shifted(hc, _down, C_KERNEL - 1), cw_ref, C_KERNEL, CONV_HALO, tile) + cb_ref[...]
        cv_ref[...] = cv
        ln, _, _ = _ln_fwd(cv, clg_ref[...], clb_ref[...])
        yc = ln * _sigmoid(ln)
        p = ze[:, 3 * w:4 * w] * ze[:, 4 * w:]
        yd = z[:, 2 * w:3 * w] * _conv_taps(_Shifted(p, _down, D_KERNEL - 1), dw_ref, D_KERNEL, CONV_HALO, tile)
        mix = jnp.concatenate([yc, yd], axis=1).astype(bf16)
        mix_ref[...] = mix
        ho_ref[...] = h + _mm(mix, wout_ref[...])

    row = lambda cols: pl.BlockSpec((tile, cols), lambda i: (i, 0))
    return pl.pallas_call(
        body, name="odd_fwd", grid=(n_tiles,),
        in_specs=[row(D_MODEL), _prev_halo(tile, CONV_HALO, D_MODEL), _const(w_in.shape, 1), _const(w_out.shape, 1),
                  _const(cw.shape, 1), _const(cb.shape, 1), _const(clg.shape, 1), _const(clb.shape, 1),
                  _const(dw.shape, 1), _const(gm.shape, 1)],
        out_specs=[row(D_MODEL), row(D_MODEL), row(5 * w), row(D_MODEL), row(w)],
        out_shape=[jax.ShapeDtypeStruct((seq, D_MODEL), f32), jax.ShapeDtypeStruct((seq, D_MODEL), bf16),
                   jax.ShapeDtypeStruct((seq, 5 * w), bf16), jax.ShapeDtypeStruct((seq, D_MODEL), bf16),
                   jax.ShapeDtypeStruct((seq, w), f32)],
        compiler_params=_params(1),
    )(h, h, w_in, w_out, cw, cb, clg, clb, dw, gm)


def _odd_bwd(dh, h, z, cv, w_in, w_out, cw, cb, clg, clb, dw, gm, jobs=()):
    seq = h.shape[0]
    tile = min(MIX_TILE, seq)
    n_tiles = seq // tile
    w = A_WIDTH
    halo = CONV_HALO

    def body(dh_ref, dhx_ref, h_ref, z_ref, zp_ref, zx_ref, cv_ref, cvx_ref, win_ref, wout_ref, cw_ref, cb_ref,
             clg_ref, clb_ref, dw_ref, g_ref,
             dhi_ref, dz_ref, dcw_ref, dcb_ref, dclg_ref, dclb_ref, ddw_ref, dg_ref):
        i = pl.program_id(0)

        @pl.when(i == 0)
        def _():
            for ref in (dcw_ref, dcb_ref, dclg_ref, dclb_ref, ddw_ref, dg_ref):
                ref[...] = jnp.zeros_like(ref)

        dh = dh_ref[...]
        dhe = jnp.concatenate([dh, jnp.where(i < n_tiles - 1, dhx_ref[...], 0.0)], axis=0)
        dmix = _mm_nt(dhe.astype(bf16), wout_ref[...])
        ze = jnp.concatenate([jnp.where(i > 0, zp_ref[...].astype(f32), 0.0), z_ref[...].astype(f32),
                              zx_ref[...].astype(f32)], axis=0)

        sg = _sigmoid(ze[:, w:2 * w])
        ca = ze[:, :w]
        hc = ca * sg
        hcs = _Shifted(hc, _down, C_KERNEL - 1)
        cv = jnp.concatenate([cv_ref[...], cvx_ref[...]], axis=0)
        clg = clg_ref[...]
        ln, xh, r = _ln_fwd(cv, clg, clb_ref[...])
        sl = _sigmoid(ln)
        dln = dmix[:, :w] * (sl * (1.0 + ln * (1.0 - sl)))
        dclg_ref[...] += jnp.sum((dln * xh)[:tile], axis=0, keepdims=True)
        dclb_ref[...] += jnp.sum(dln[:tile], axis=0, keepdims=True)
        dcv = _ln_bwd(dln, xh, r, clg)
        dcb_ref[...] += jnp.sum(dcv[:tile], axis=0, keepdims=True)
        dcvs = _Shifted(dcv, _up, C_KERNEL - 1)
        dhc = None
        for j in range(C_KERNEL):
            k = C_KERNEL - 1 - j
            dcw_ref[pl.ds(j, 1), :] += jnp.sum(dcv[:tile] * hcs.rows(k, halo, tile), axis=0, keepdims=True)
            term = cw_ref[pl.ds(j, 1), :] * dcvs.rows(k, 0, tile)
            dhc = term if dhc is None else dhc + term
        sgt = sg[halo:halo + tile]
        cat = ca[halo:halo + tile]
        dca = dhc * sgt
        dcg = dhc * cat * sgt * (1.0 - sgt)

        dcgv = ze[:, 3 * w:4 * w]
        dxin = ze[:, 4 * w:]
        p = dcgv * dxin
        ps = _Shifted(p, _down, D_KERNEL - 1)
        q = _conv_taps(ps, dw_ref, D_KERNEL, halo, tile)
        dyd = dmix[:, w:]
        dq = dyd * ze[halo:, 2 * w:3 * w]
        ddbg = dyd[:tile] * q
        dqs = _Shifted(dq, _up, D_KERNEL - 1)
        dp = None
        for j in range(D_KERNEL):
            k = D_KERNEL - 1 - j
            ddw_ref[pl.ds(j, 1), :] += jnp.sum(dq[:tile] * ps.rows(k, halo, tile), axis=0, keepdims=True)
            term = dw_ref[pl.ds(j, 1), :] * dqs.rows(k, 0, tile)
            dp = term if dp is None else dp + term
        ddcg = dp * dxin[halo:halo + tile]
        ddxin = dp * dcgv[halo:halo + tile]

        dzf = jnp.concatenate([dca, dcg, ddbg, ddcg, ddxin], axis=1).astype(bf16)
        dz_ref[...] = dzf
        dhn = _mm(dzf, win_ref[...])
        dhr, dg = _rms_bwd(dhn, h_ref[...], g_ref[...])
        dhi_ref[...] = dh + dhr
        dg_ref[...] += dg

    row = lambda cols: pl.BlockSpec((tile, cols), lambda i: (i, 0))
    small = [cw.shape, cb.shape, clg.shape, clb.shape, dw.shape, gm.shape]
    return _launch(
        body, name="odd_bwd", grid=(n_tiles,), jobs=jobs,
        in_specs=[row(D_MODEL), _next_halo(tile, halo, D_MODEL, seq), row(D_MODEL), row(5 * w),
                  _prev_halo(tile, halo, 5 * w), _next_halo(tile, halo, 5 * w, seq),
                  row(w), _next_halo(tile, halo, w, seq),
                  _const(w_in.shape, 1), _const(w_out.shape, 1), _const(cw.shape, 1), _const(cb.shape, 1),
                  _const(clg.shape, 1), _const(clb.shape, 1), _const(dw.shape, 1), _const(gm.shape, 1)],
        out_specs=[row(D_MODEL), row(5 * w)] + [_const(s, 1) for s in small],
        out_shape=[jax.ShapeDtypeStruct((seq, D_MODEL), f32), jax.ShapeDtypeStruct((seq, 5 * w), bf16)]
                  + [jax.ShapeDtypeStruct(s, f32) for s in small],
        args=(dh, dh, h, z, z, z, cv, cv, w_in, w_out, cw, cb, clg, clb, dw, gm))


def _ffn_chunks():
    assert sum(FFN_CHUNKS) == D_FF
    start = 0
    for size in FFN_CHUNKS:
        yield slice(start, start + size)
        start += size


def _ffn_fwd(h, wg, wu, wd, gm, jobs=(), head=None):
    seq = h.shape[0]
    tile = min(FFN_TILE, seq)

    def body(h_ref, g_ref, wg_ref, wu_ref, wd_ref, *refs):
        if head is None:
            ho_ref, hn_ref, gate_ref, up_ref = refs
        else:
            t_ref, gf_ref, ho_ref, hn_ref, gate_ref, up_ref, loss_ref, dgf_ref = refs
        h = h_ref[...]
        hnb = (h * _rms_r(h) * g_ref[...]).astype(bf16)
        hn_ref[...] = hnb
        acc = None
        for rows in _ffn_chunks():
            gb = _mm_nt(hnb, wg_ref[rows, :]).astype(bf16)
            ub = _mm_nt(hnb, wu_ref[rows, :]).astype(bf16)
            gate_ref[:, rows] = gb
            up_ref[:, rows] = ub
            gf = gb.astype(f32)
            act = gf * _sigmoid(gf) * ub.astype(f32)
            part = _mm(act.astype(bf16), wd_ref[rows, :])
            acc = part if acc is None else acc + part
        ho = h + acc
        if head is None:
            ho_ref[...] = ho
            return

        @pl.when(pl.program_id(0) == 0)
        def _():
            loss_ref[...] = jnp.zeros_like(loss_ref)
            dgf_ref[...] = jnp.zeros_like(dgf_ref)

        g_final = gf_ref[...]
        err = ho * _rms_r(ho) * g_final - t_ref[...]
        loss_ref[...] += (0.5 / D_MODEL) * jnp.sum(jnp.sum(err * err, axis=1, keepdims=True), axis=0, keepdims=True)
        dho, dg = _rms_bwd(err * (1.0 / D_MODEL), ho, g_final)
        ho_ref[...] = dho
        dgf_ref[...] += dg

    row = pl.BlockSpec((tile, D_MODEL), lambda i: (i, 0))
    wide = pl.BlockSpec((tile, D_FF), lambda i: (i, 0))
    in_specs = [row, _const(gm.shape, 1), _const(wg.shape, 1), _const(wu.shape, 1), _const(wd.shape, 1)]
    out_specs = [row, row, wide, wide]
    out_shape = [jax.ShapeDtypeStruct((seq, D_MODEL), f32), jax.ShapeDtypeStruct((seq, D_MODEL), bf16),
                 jax.ShapeDtypeStruct((seq, D_FF), bf16), jax.ShapeDtypeStruct((seq, D_FF), bf16)]
    args = (h, gm, wg, wu, wd)
    if head is not None:
        target, g_final = head
        in_specs += [row, _const(g_final.shape, 1)]
        out_specs += [_const((1, 1), 1), _const(g_final.shape, 1)]
        out_shape += [jax.ShapeDtypeStruct((1, 1), f32), jax.ShapeDtypeStruct(g_final.shape, f32)]
        args += (target, g_final)
    return _launch(
        body, name="ffn_fwd" if head is None else "ffn_fwd_loss", grid=(seq // tile,), jobs=jobs,
        in_specs=in_specs, out_specs=out_specs, out_shape=out_shape, args=args)


def _ffn_bwd(dh, h, gate, up, wg, wu, wd, gm, jobs=()):
    seq = h.shape[0]
    tile = min(FFN_TILE, seq)
    n_tiles = seq // tile

    def body(dh_ref, h_ref, g_ref, gate_ref, up_ref, wg_ref, wu_ref, wd_ref,
             dhi_ref, dgate_ref, dup_ref, act_ref, dg_ref):
        @pl.when(pl.program_id(0) == 0)
        def _():
            dg_ref[...] = jnp.zeros_like(dg_ref)

        dh = dh_ref[...]
        dhb = dh.astype(bf16)
        acc = None
        for rows in _ffn_chunks():
            dact = _mm_nt(dhb, wd_ref[rows, :])
            gf = gate_ref[:, rows].astype(f32)
            uf = up_ref[:, rows].astype(f32)
            s = _sigmoid(gf)
            silu = gf * s
            act_ref[:, rows] = (silu * uf).astype(bf16)
            dgb = (dact * uf * (s * (1.0 + gf * (1.0 - s)))).astype(bf16)
            dub = (dact * silu).astype(bf16)
            dgate_ref[:, rows] = dgb
            dup_ref[:, rows] = dub
            part = _mm(dgb, wg_ref[rows, :]) + _mm(dub, wu_ref[rows, :])
            acc = part if acc is None else acc + part
        dhr, dg = _rms_bwd(acc, h_ref[...], g_ref[...])
        dhi_ref[...] = dh + dhr
        dg_ref[...] += dg

    row = pl.BlockSpec((tile, D_MODEL), lambda i: (i, 0))
    wide = pl.BlockSpec((tile, D_FF), lambda i: (i, 0))
    return _launch(
        body, name="ffn_bwd", grid=(n_tiles,), jobs=jobs,
        in_specs=[row, row, _const(gm.shape, 1), wide, wide, _const(wg.shape, 1), _const(wu.shape, 1), _const(wd.shape, 1)],
        out_specs=[row, wide, wide, wide, _const(gm.shape, 1)],
        out_shape=[jax.ShapeDtypeStruct((seq, D_MODEL), f32), jax.ShapeDtypeStruct((seq, D_FF), bf16),
                   jax.ShapeDtypeStruct((seq, D_FF), bf16), jax.ShapeDtypeStruct((seq, D_FF), bf16),
                   jax.ShapeDtypeStruct(gm.shape, f32)],
        args=(dh, h, gm, gate, up, wg, wu, wd))


def _weight_grads(pairs, name, jobs=()):
    seq, m = pairs[0][0].shape
    tk = min(DW_TK, seq)
    tm = m if m <= DW_TM else m // 2
    n_k = seq // tk
    n_pairs = len(pairs)

    def body(*refs):
        x_refs = refs[0:2 * n_pairs:2]
        y_refs = refs[1:2 * n_pairs:2]
        o_refs = refs[2 * n_pairs:3 * n_pairs]
        acc_refs = refs[3 * n_pairs:]
        k = pl.program_id(1)
        @pl.when(k == 0)
        def _():
            for acc_ref in acc_refs:
                acc_ref[...] = jnp.zeros_like(acc_ref)

        for x_ref, y_ref, acc_ref in zip(x_refs, y_refs, acc_refs):
            acc_ref[...] += _mm_tn(x_ref[...].astype(bf16), y_ref[...].astype(bf16))

        @pl.when(k == n_k - 1)
        def _():
            for o_ref, acc_ref in zip(o_refs, acc_refs):
                o_ref[...] = acc_ref[...].astype(bf16)

    in_specs = []
    for _ in pairs:
        in_specs += [pl.BlockSpec((tk, tm), lambda j, k: (k, j)), pl.BlockSpec((tk, D_MODEL), lambda j, k: (k, 0))]
    return _launch(
        body, name=name, grid=(m // tm, n_k), jobs=jobs,
        in_specs=in_specs,
        out_specs=[pl.BlockSpec((tm, D_MODEL), lambda j, k: (j, 0))] * n_pairs,
        out_shape=[jax.ShapeDtypeStruct((m, D_MODEL), bf16)] * n_pairs,
        scratch=[pltpu.VMEM((tm, D_MODEL), f32)] * n_pairs,
        args=[a for pair in pairs for a in pair])


def _row_tile(rows, limit=512):
    best = rows
    for t in range(8, min(rows, limit) + 1, 8):
        if rows % t == 0:
            best = t
    return best if rows > limit else rows


def _adamw(w, g, m, v, name):
    rows, cols = w.shape
    tr = _row_tile(rows)

    def body(w_ref, g_ref, m_ref, v_ref, d_ref, mo_ref, vo_ref):
        g = g_ref[...]
        m2 = ADAM_B1 * m_ref[...] + (1.0 - ADAM_B1) * g
        v2 = ADAM_B2 * v_ref[...] + (1.0 - ADAM_B2) * (g * g)
        m_hat = m2 / (1.0 - ADAM_B1 ** ADAM_STEP)
        v_hat = v2 / (1.0 - ADAM_B2 ** ADAM_STEP)
        d_ref[...] = -ADAM_LR * (m_hat / (jnp.sqrt(v_hat) + ADAM_EPS) + ADAM_WD * w_ref[...])
        mo_ref[...] = m2
        vo_ref[...] = v2

    spec = pl.BlockSpec((tr, cols), lambda i: (i, 0))
    return pl.pallas_call(
        body, name=name, grid=(rows // tr,),
        in_specs=[spec] * 4, out_specs=[spec] * 3,
        out_shape=[jax.ShapeDtypeStruct((rows, cols), f32)] * 3,
        compiler_params=_params(1),
    )(w, g, m, v)


def _sum_leading(x, name):
    n, rows, cols = x.shape
    tr = _row_tile(rows)

    def body(x_ref, o_ref):
        acc = x_ref[0].astype(f32)
        for k in range(1, n):
            acc = acc + x_ref[k].astype(f32)
        o_ref[...] = acc

    return pl.pallas_call(
        body, name=name, grid=(rows // tr,),
        in_specs=[pl.BlockSpec((n, tr, cols), lambda i: (0, i, 0))],
        out_specs=pl.BlockSpec((tr, cols), lambda i: (i, 0)),
        out_shape=jax.ShapeDtypeStruct((rows, cols), f32),
        compiler_params=_params(1),
    )(x)


def _pair_sum(g, recv, c_idx, name):
    _, rows, cols = g.shape
    tr = _row_tile(rows)

    def body(c_ref, g_ref, r_ref, o_ref):
        o_ref[...] = (g_ref[...].astype(f32) + r_ref[...].astype(f32)).astype(o_ref.dtype)

    return pl.pallas_call(
        body, name=name,
        grid_spec=pltpu.PrefetchScalarGridSpec(
            num_scalar_prefetch=1, grid=(N_CHIP, rows // tr),
            in_specs=[pl.BlockSpec((1, tr, cols), lambda k, i, c: (2 * k + c[0], i, 0)),
                      pl.BlockSpec((1, tr, cols), lambda k, i, c: (k, i, 0))],
            out_specs=pl.BlockSpec((1, tr, cols), lambda k, i, c: (k, i, 0))),
        out_shape=jax.ShapeDtypeStruct((N_CHIP, rows, cols), g.dtype),
        compiler_params=_params(2),
    )(c_idx, g, recv)


def _pack_rows(w):
    return w.reshape(N_DEV, -1, D_MODEL)


def kernel(x, even_w_in, even_w_out, a_w_s, a_b_s, a_ln_g, a_ln_b, b_w_pool, b_scale, odd_w_in, odd_w_out, c_w_dw, c_b_dw, c_ln_g, c_ln_b, d_w_dw, norm_mix_g, norm_ffn_g, ffn_w_gate, ffn_w_up, ffn_w_down, final_norm_g, loss_target, m_even_w_in, m_even_w_out, m_a_w_s, m_a_b_s, m_a_ln_g, m_a_ln_b, m_b_w_pool, m_b_scale, m_odd_w_in, m_odd_w_out, m_c_w_dw, m_c_b_dw, m_c_ln_g, m_c_ln_b, m_d_w_dw, m_norm_mix_g, m_norm_ffn_g, m_ffn_w_gate, m_ffn_w_up, m_ffn_w_down, m_final_norm_g, v_even_w_in, v_even_w_out, v_a_w_s, v_a_b_s, v_a_ln_g, v_a_ln_b, v_b_w_pool, v_b_scale, v_odd_w_in, v_odd_w_out, v_c_w_dw, v_c_b_dw, v_c_ln_g, v_c_ln_b, v_d_w_dw, v_norm_mix_g, v_norm_ffn_g, v_ffn_w_gate, v_ffn_w_up, v_ffn_w_down, v_final_norm_g):
    weights = dict(even_w_in=even_w_in, even_w_out=even_w_out, a_w_s=a_w_s, a_b_s=a_b_s, a_ln_g=a_ln_g, a_ln_b=a_ln_b,
                   b_w_pool=b_w_pool, b_scale=b_scale, odd_w_in=odd_w_in, odd_w_out=odd_w_out, c_w_dw=c_w_dw,
                   c_b_dw=c_b_dw, c_ln_g=c_ln_g, c_ln_b=c_ln_b, d_w_dw=d_w_dw, norm_mix_g=norm_mix_g,
                   norm_ffn_g=norm_ffn_g, ffn_w_gate=ffn_w_gate, ffn_w_up=ffn_w_up, ffn_w_down=ffn_w_down,
                   final_norm_g=final_norm_g)
    m_in = dict(even_w_in=m_even_w_in, even_w_out=m_even_w_out, a_w_s=m_a_w_s, a_b_s=m_a_b_s, a_ln_g=m_a_ln_g,
                a_ln_b=m_a_ln_b, b_w_pool=m_b_w_pool, b_scale=m_b_scale, odd_w_in=m_odd_w_in, odd_w_out=m_odd_w_out,
                c_w_dw=m_c_w_dw, c_b_dw=m_c_b_dw, c_ln_g=m_c_ln_g, c_ln_b=m_c_ln_b, d_w_dw=m_d_w_dw,
                norm_mix_g=m_norm_mix_g, norm_ffn_g=m_norm_ffn_g, ffn_w_gate=m_ffn_w_gate, ffn_w_up=m_ffn_w_up,
                ffn_w_down=m_ffn_w_down, final_norm_g=m_final_norm_g)
    v_in = dict(even_w_in=v_even_w_in, even_w_out=v_even_w_out, a_w_s=v_a_w_s, a_b_s=v_a_b_s, a_ln_g=v_a_ln_g,
                a_ln_b=v_a_ln_b, b_w_pool=v_b_w_pool, b_scale=v_b_scale, odd_w_in=v_odd_w_in, odd_w_out=v_odd_w_out,
                c_w_dw=v_c_w_dw, c_b_dw=v_c_b_dw, c_ln_g=v_c_ln_g, c_ln_b=v_c_ln_b, d_w_dw=v_d_w_dw,
                norm_mix_g=v_norm_mix_g, norm_ffn_g=v_norm_ffn_g, ffn_w_gate=v_ffn_w_gate, ffn_w_up=v_ffn_w_up,
                ffn_w_down=v_ffn_w_down, final_norm_g=v_final_norm_g)
    names = list(weights)

    group_parts = {
        "even": [even_w_in[0].T, even_w_out[0]],
        "ffn0": [ffn_w_gate[0].T, ffn_w_up[0].T, ffn_w_down[0]],
        "odd": [odd_w_in[0].T, odd_w_out[0]],
        "ffn1": [ffn_w_gate[1].T, ffn_w_up[1].T, ffn_w_down[1]],
    }
    group_rows = {k: [p.shape[0] for p in parts] for k, parts in group_parts.items()}

    def pack(*groups):
        return jnp.concatenate([p for k in groups for p in group_parts[k]], axis=0).astype(bf16)

    def unpack(gathered_blocks, *groups):
        out, off = [], 0
        for k in groups:
            for r in group_rows[k]:
                out.append(gathered_blocks[:, off:off + r, :].reshape(-1, D_MODEL))
                off += r
        return out

    conv_names = ["c_w_dw", "c_b_dw", "c_ln_g", "c_ln_b", "d_w_dw"]
    conv_rows = [C_KERNEL, 1, 1, 1, D_KERNEL]
    conv_local = jnp.concatenate([weights[n].reshape(r, -1) for n, r in zip(conv_names, conv_rows)]
                                 + [jnp.zeros((3, c_b_dw.shape[-1]), f32)], axis=0)
    big_even, conv_all = _run_jobs([_all_gather_job(pack("even")), _all_gather_job(conv_local)], "gather_even_conv")
    w_in_e, w_out_e = unpack(big_even, "even")
    conv_all = conv_all.transpose(1, 0, 2).reshape(conv_local.shape[0], -1)
    conv_offs = [sum(conv_rows[:k]) for k in range(len(conv_rows) + 1)]
    cw, cb, clg, clb, dw = [conv_all[conv_offs[k]:conv_offs[k + 1]] for k in range(len(conv_rows))]

    ws, bst = a_w_s[0], a_b_s[0].T
    lng, lnb, wp, sc = a_ln_g, a_ln_b, b_w_pool[0], b_scale
    gmix = [norm_mix_g[l:l + 1] for l in range(2)]
    gffn = [norm_ffn_g[l:l + 1] for l in range(2)]
    gfin = final_norm_g.reshape(1, D_MODEL)

    h0 = x[0]
    h1, hn_e, za, pooled, mix_e, big_ffn0 = _even_fwd(
        h0, w_in_e, w_out_e, ws, bst, lng, lnb, wp, sc, gmix[0], jobs=[_all_gather_job(pack("ffn0"))])
    w_gate0, w_up0, w_down0 = unpack(big_ffn0, "ffn0")
    h2, hn_f0, gate0, up0, big_rest = _ffn_fwd(h1, w_gate0, w_up0, w_down0, gffn[0],
                                               jobs=[_all_gather_job(pack("odd", "ffn1"))])
    w_in_o, w_out_o, w_gate1, w_up1, w_down1 = unpack(big_rest, "odd", "ffn1")
    h3, hn_o, z_o, mix_o, cv_o = _odd_fwd(h2, w_in_o, w_out_o, cw, cb, clg, clb, dw, gmix[1])
    dh4, hn_f1, gate1, up1, loss_local, g_final = _ffn_fwd(h3, w_gate1, w_up1, w_down1, gffn[1],
                                                           head=(loss_target[0], gfin))

    c_idx = lax.axis_index("c").astype(jnp.int32).reshape(1)

    def shard_major(gs):
        return jnp.concatenate([_pack_rows(g) for g in gs], axis=1)

    dh3, dgate1, dup1, act1, g_ffn1 = _ffn_bwd(dh4, h3, gate1, up1, w_gate1, w_up1, w_down1, gffn[1])
    part_ffn1 = shard_major(_weight_grads([(dgate1, hn_f1)], "dw_gate1") + _weight_grads([(dup1, hn_f1)], "dw_up1")
                            + _weight_grads([(act1, dh4)], "dw_down1"))
    dh2, dz_o, g_cw, g_cb, g_clg, g_clb, g_dw, g_mix1, recv_ffn1 = _odd_bwd(
        dh3, h2, z_o, cv_o, w_in_o, w_out_o, cw, cb, clg, clb, dw, gmix[1], jobs=[_sibling_exchange_job(part_ffn1)])
    pair_ffn1 = _pair_sum(part_ffn1, recv_ffn1, c_idx, "pair_sum_ffn1")
    part_odd = shard_major(_weight_grads([(dz_o, hn_o)], "dw_odd_in") + _weight_grads([(mix_o, dh3)], "dw_odd_out"))
    dh1, dgate0, dup0, act0, g_ffn0, chips_ffn1, recv_odd = _ffn_bwd(
        dh2, h1, gate0, up0, w_gate0, w_up0, w_down0, gffn[0],
        jobs=[_chip_exchange_job(pair_ffn1), _sibling_exchange_job(part_odd)])
    pair_odd = _pair_sum(part_odd, recv_odd, c_idx, "pair_sum_odd")
    dw_gate0, chips_odd = _weight_grads([(dgate0, hn_f0)], "dw_gate0", jobs=[_chip_exchange_job(pair_odd)])
    part_ffn0 = shard_major([dw_gate0] + _weight_grads([(dup0, hn_f0)], "dw_up0")
                            + _weight_grads([(act0, dh2)], "dw_down0"))
    dw_even_out, recv_ffn0 = _weight_grads([(mix_e, dh1)], "dw_even_out", jobs=[_sibling_exchange_job(part_ffn0)])
    part_even_out = _pack_rows(dw_even_out)
    pair_ffn0 = _pair_sum(part_ffn0, recv_ffn0, c_idx, "pair_sum_ffn0")
    dh0, dz_e, g_ws, g_bs, g_lng, g_lnb, g_wp, g_sc, g_mix0, chips_ffn0, recv_even_out = _even_bwd(
        dh1, h0, za, pooled, w_in_e, w_out_e, ws, bst, lng, lnb, wp, sc, gmix[0],
        jobs=[_chip_exchange_job(pair_ffn0), _sibling_exchange_job(part_even_out)])
    pair_even_out = _pair_sum(part_even_out, recv_even_out, c_idx, "pair_sum_even_out")

    lanes = HEAD
    small = [("a_w_s", g_ws), ("a_b_s", g_bs), ("a_ln_g", g_lng), ("a_ln_b", g_lnb), ("b_w_pool", g_wp),
             ("b_scale", g_sc), ("norm_mix_g", jnp.concatenate([g_mix0, g_mix1], axis=0)),
             ("norm_ffn_g", jnp.concatenate([g_ffn0, g_ffn1], axis=0)), ("final_norm_g", g_final),
             ("c_w_dw", g_cw), ("c_b_dw", g_cb), ("c_ln_g", g_clg), ("c_ln_b", g_clb), ("d_w_dw", g_dw)]
    small_rows = [-(-g.size // (8 * lanes)) * 8 for _, g in small]
    small_offs = [sum(small_rows[:k]) for k in range(len(small) + 1)]
    pad_rows = -small_offs[-1] % 256
    small_buf = jnp.concatenate(
        [jnp.pad(g.reshape(-1), (0, r * lanes - g.size)).reshape(r, lanes) for (_, g), r in zip(small, small_rows)]
        + [jnp.zeros((pad_rows, lanes), f32)], axis=0)
    dw_even_in, small_all, chips_even_out = _weight_grads(
        [(dz_e, hn_e)], "dw_even_in", jobs=[_all_gather_job(small_buf), _chip_exchange_job(pair_even_out)])
    small_sum = _sum_leading(small_all, "small_grad_sum")
    part_even_in = _pack_rows(dw_even_in)
    recv_even_in, = _run_jobs([_sibling_exchange_job(part_even_in)], "sibling_exchange_even_in")
    chips_even_in, = _run_jobs(
        [_chip_exchange_job(_pair_sum(part_even_in, recv_even_in, c_idx, "pair_sum_even_in"))], "chip_exchange_even_in")
    group_grads = {"even": jnp.concatenate([_sum_leading(chips_even_in, "chip_sum_even_in"),
                                            _sum_leading(chips_even_out, "chip_sum_even_out")], axis=0),
                   "ffn0": _sum_leading(chips_ffn0, "chip_sum_ffn0"),
                   "odd": _sum_leading(chips_odd, "chip_sum_odd"), "ffn1": _sum_leading(chips_ffn1, "chip_sum_ffn1")}
    grads = {}
    for k, (n, g) in enumerate(small):
        grads[n] = small_sum[small_offs[k]:small_offs[k + 1]].reshape(-1)[:g.size].reshape(g.shape)
    me = 4 * lax.axis_index("x") + 2 * lax.axis_index("y") + lax.axis_index("c")
    shard = c_b_dw.shape[-1]
    for n in conv_names:
        grads[n] = lax.dynamic_slice_in_dim(grads[n], me * shard, shard, axis=1)

    def own_shards(group):
        out, off = [], 0
        for r in group_rows[group]:
            out.append(group_grads[group][off:off + r])
            off += r
        return out

    g_in, grads["even_w_out"] = own_shards("even")
    grads["even_w_in"] = g_in.T
    g_in, grads["odd_w_out"] = own_shards("odd")
    grads["odd_w_in"] = g_in.T
    per_layer = [own_shards("ffn0"), own_shards("ffn1")]
    grads["ffn_w_gate"] = jnp.stack([per_layer[l][0].T for l in range(2)])
    grads["ffn_w_up"] = jnp.stack([per_layer[l][1].T for l in range(2)])
    grads["ffn_w_down"] = jnp.stack([per_layer[l][2] for l in range(2)])
    grads = {n: grads[n].reshape(weights[n].shape) for n in names}

    delta, new_m, new_v = {}, {}, {}
    for n in names:
        shape = weights[n].shape
        view = (-1, shape[-1])
        d, m2, v2 = _adamw(weights[n].reshape(view), grads[n].reshape(view), m_in[n].reshape(view),
                           v_in[n].reshape(view), "adamw_" + n)
        delta[n], new_m[n], new_v[n] = d.reshape(shape), m2.reshape(shape), v2.reshape(shape)

    loss = lax.psum(loss_local[0, 0], ("x", "y", "c"))
    return (loss, dh0[None], *[grads[n] for n in names], *[delta[n] for n in names],
            *[new_m[n] for n in names], *[new_v[n] for n in names])
```

```python
import jax
import jax.numpy as jnp
from jax import lax
from jax.experimental import pallas as pl
from jax.experimental.pallas import tpu as pltpu

f32 = jnp.float32
bf16 = jnp.bfloat16

EPS = 1e-6
D_MODEL = 1024
A_WIDTH = 512
HEAD = 128
N_HEADS = 4
CHUNK = 64
POOL_WINDOWS = (2, 4, 8, 16)
POOL_HALO = 16
C_KERNEL = 31
D_KERNEL = 3
CONV_HALO = 32
D_FF = 2816
N_DEV = 8
N_CHIP = 4

ADAM_LR = 0.001
ADAM_B1 = 0.9
ADAM_B2 = 0.999
ADAM_EPS = 1e-08
ADAM_WD = 0.01
ADAM_STEP = 10

MIX_TILE = 512
FFN_TILE = 256
FFN_CHUNKS = (1536, 1280)
DW_TK = 2048
DW_TM = 1536
MIDDLE_AT, MIDDLE_OF = 7, 8
VMEM_LIMIT = 56 * 1024 * 1024

MESH = pl.DeviceIdType.MESH
ANY = pl.BlockSpec(memory_space=pl.ANY)


def _params(n_axes):
    return pltpu.CompilerParams(dimension_semantics=("arbitrary",) * n_axes, vmem_limit_bytes=VMEM_LIMIT)


def _mm(a, b):
    return jnp.dot(a, b, preferred_element_type=f32)


def _mm_nt(a, b):
    return lax.dot_general(a, b, (((1,), (1,)), ((), ())), preferred_element_type=f32)


def _mm_tn(a, b):
    return lax.dot_general(a, b, (((0,), (0,)), ((), ())), preferred_element_type=f32)


def _sigmoid(x):
    return 1.0 / (1.0 + jnp.exp(-x))


def _rms_r(h):
    return lax.rsqrt(jnp.mean(h * h, axis=-1, keepdims=True) + EPS)


def _rms_bwd(dy, h, g):
    r = _rms_r(h)
    xh = h * r
    dxh = dy * g
    dh = r * (dxh - xh * jnp.mean(dxh * xh, axis=-1, keepdims=True))
    return dh, jnp.sum(dy * xh, axis=0, keepdims=True)


def _ln_fwd(x, g, b):
    mu = jnp.mean(x, axis=-1, keepdims=True)
    xc = x - mu
    r = lax.rsqrt(jnp.mean(xc * xc, axis=-1, keepdims=True) + EPS)
    xh = xc * r
    return xh * g + b, xh, r


def _ln_bwd(dy, xh, r, g):
    dxh = dy * g
    return r * (dxh - jnp.mean(dxh, axis=-1, keepdims=True) - xh * jnp.mean(dxh * xh, axis=-1, keepdims=True))


_GELU_C = 0.7978845608028654
_GELU_A = 0.044715


def _gelu(x):
    th = jnp.tanh(_GELU_C * (x + _GELU_A * x * x * x))
    return 0.5 * x * (1.0 + th), th


def _gelu_grad(x, th):
    return 0.5 * (1.0 + th) + 0.5 * x * (1.0 - th * th) * (_GELU_C * (1.0 + 3.0 * _GELU_A * x * x))


def _down(x, k):
    return x if k == 0 else pltpu.roll(x, k, 0)


def _up(x, k):
    return x if k == 0 else pltpu.roll(x, x.shape[0] - k, 0)


def _window_sum(x, win, shift):
    s = x
    step = 1
    while step < win:
        s = s + shift(s, step)
        step *= 2
    return s


def _inv_count(t0, rows, win):
    t = t0 + lax.broadcasted_iota(jnp.int32, (rows, 1), 0)
    return 1.0 / jnp.minimum(t + 1, win).astype(f32)


def _chunk_mask():
    i = lax.broadcasted_iota(jnp.int32, (HEAD, HEAD), 0)
    j = lax.broadcasted_iota(jnp.int32, (HEAD, HEAD), 1)
    return jnp.logical_or(i >= CHUNK, j < CHUNK)


def _const(shape, n_axes):
    zeros = (0,) * len(shape)
    if n_axes == 1:
        return pl.BlockSpec(shape, lambda i: zeros)
    return pl.BlockSpec(shape, lambda i, j: zeros)


def _prev_halo(tile, halo, cols):
    return pl.BlockSpec((halo, cols), lambda i: (jnp.maximum(i * (tile // halo) - 1, 0), 0))


def _next_halo(tile, halo, cols, seq):
    return pl.BlockSpec((halo, cols), lambda i: (jnp.minimum((i + 1) * (tile // halo), seq // halo - 1), 0))


class _Job:
    def __init__(self, inputs, out_shape, sems, hooks):
        self.inputs, self.out_shape, self.sems, self.hooks = inputs, out_shape, sems, hooks


def _position():
    return lax.axis_index("x"), lax.axis_index("y"), lax.axis_index("c")


def _all_gather_job(block):
    rows, cols = block.shape

    def hooks(ins, outs, sems):
        (x_ref,), (out_ref,), (send_sems, recv_sems, local_sem) = ins, outs, sems
        x, y, c = _position()
        me, sibling = (x, y, c), (x, y, 1 - c)
        chips = [(1 - x, y), (x, 1 - y), (1 - x, 1 - y)]

        def slot(px, py, pc):
            return out_ref.at[4 * px + 2 * py + pc]

        def copy(k, block_of, to, src=None):
            return pltpu.make_async_remote_copy(
                src_ref=slot(*block_of) if src is None else src, dst_ref=slot(*block_of),
                send_sem=send_sems.at[k], recv_sem=recv_sems.at[k], device_id=to, device_id_type=MESH)

        mine = pltpu.make_async_copy(x_ref, slot(*me), local_sem)
        first = [copy(0, me, sibling, src=x_ref)]
        first += [copy(1 + j, me, (*chip, c), src=x_ref) for j, chip in enumerate(chips)]
        passed = [copy(4 + j, (*chip, c), sibling) for j, chip in enumerate(chips)]

        def start():
            mine.start()
            for cp in first:
                cp.start()

        def middle():
            for j, chip in enumerate(chips):
                copy(1 + j, (*chip, c), me).wait_recv()
                passed[j].start()

        def finish():
            copy(0, sibling, me).wait_recv()
            for j, chip in enumerate(chips):
                copy(4 + j, (*chip, 1 - c), me).wait_recv()
            for cp in first + passed:
                cp.wait_send()
            mine.wait()

        return start, middle, finish

    return _Job([block], [jax.ShapeDtypeStruct((N_DEV, rows, cols), block.dtype)],
                [pltpu.SemaphoreType.DMA((7,)), pltpu.SemaphoreType.DMA((7,)), pltpu.SemaphoreType.DMA], hooks)


def _sibling_exchange_job(g):
    _, rows, cols = g.shape

    def hooks(ins, outs, sems):
        (g_ref,), (recv_ref,), (send_sems, recv_sems) = ins, outs, sems
        x, y, c = _position()
        copies = [pltpu.make_async_remote_copy(
            src_ref=g_ref.at[2 * k + (1 - c)], dst_ref=recv_ref.at[k], send_sem=send_sems.at[k],
            recv_sem=recv_sems.at[k], device_id=(x, y, 1 - c), device_id_type=MESH) for k in range(N_CHIP)]

        def start():
            for cp in copies:
                cp.start()

        def finish():
            for cp in copies:
                cp.wait()

        return start, lambda: None, finish

    return _Job([g], [jax.ShapeDtypeStruct((N_CHIP, rows, cols), g.dtype)],
                [pltpu.SemaphoreType.DMA((N_CHIP,)), pltpu.SemaphoreType.DMA((N_CHIP,))], hooks)


def _chip_exchange_job(p):
    _, rows, cols = p.shape

    def hooks(ins, outs, sems):
        (p_ref,), (recv_ref,), (send_sems, recv_sems, local_sem) = ins, outs, sems
        x, y, c = _position()
        k_me = 2 * x + y
        mine = pltpu.make_async_copy(p_ref.at[k_me], recv_ref.at[k_me], local_sem)
        copies = [pltpu.make_async_remote_copy(
            src_ref=p_ref.at[2 * px + py], dst_ref=recv_ref.at[k_me], send_sem=send_sems.at[j],
            recv_sem=recv_sems.at[j], device_id=(px, py, c), device_id_type=MESH)
            for j, (px, py) in enumerate([(1 - x, y), (x, 1 - y), (1 - x, 1 - y)])]

        def start():
            mine.start()
            for cp in copies:
                cp.start()

        def finish():
            for cp in copies:
                cp.wait()
            mine.wait()

        return start, lambda: None, finish

    return _Job([p], [jax.ShapeDtypeStruct((N_CHIP, rows, cols), p.dtype)],
                [pltpu.SemaphoreType.DMA((3,)), pltpu.SemaphoreType.DMA((3,)), pltpu.SemaphoreType.DMA], hooks)


def _job_hooks(jobs, ins, outs, sems):
    hooks = []
    for job in jobs:
        n_in, n_out, n_sem = len(job.inputs), len(job.out_shape), len(job.sems)
        hooks.append(job.hooks(ins[:n_in], outs[:n_out], sems[:n_sem]))
        ins, outs, sems = ins[n_in:], outs[n_out:], sems[n_sem:]
    return hooks


def _run_jobs(jobs, name):
    n_in = sum(len(job.inputs) for job in jobs)
    n_out = sum(len(job.out_shape) for job in jobs)

    def body(*refs):
        hooks = _job_hooks(jobs, refs[:n_in], refs[n_in:n_in + n_out], refs[n_in + n_out:])
        for phase in range(3):
            for h in hooks:
                h[phase]()

    return list(pl.pallas_call(
        body, name=name, in_specs=[ANY] * n_in, out_specs=[ANY] * n_out,
        out_shape=[s for job in jobs for s in job.out_shape],
        scratch_shapes=[s for job in jobs for s in job.sems],
    )(*[a for job in jobs for a in job.inputs]))


def _launch(body, *, name, grid, in_specs, out_specs, out_shape, args, scratch=(), jobs=()):
    in_specs, out_specs, out_shape, scratch = list(in_specs), list(out_specs), list(out_shape), list(scratch)
    if not jobs:
        return list(pl.pallas_call(body, name=name, grid=grid, in_specs=in_specs, out_specs=out_specs,
                                   out_shape=out_shape, scratch_shapes=scratch,
                                   compiler_params=_params(len(grid)))(*args))
    n_in, n_out, n_sc = len(in_specs), len(out_specs), len(scratch)
    j_in = [a for job in jobs for a in job.inputs]
    j_out = [s for job in jobs for s in job.out_shape]
    j_sems = [s for job in jobs for s in job.sems]
    n_steps = 1
    for g in grid:
        n_steps *= g

    def wrapped(*refs):
        ins, refs = refs[:n_in], refs[n_in:]
        jins, refs = refs[:len(j_in)], refs[len(j_in):]
        outs, refs = refs[:n_out], refs[n_out:]
        jouts, refs = refs[:len(j_out)], refs[len(j_out):]
        sc, jsems = refs[:n_sc], refs[n_sc:]
        step = pl.program_id(0)
        for axis in range(1, len(grid)):
            step = step * grid[axis] + pl.program_id(axis)
        hooks = _job_hooks(jobs, jins, jouts, jsems)

        @pl.when(step == 0)
        def _():
            for h in hooks:
                h[0]()

        body(*ins, *outs, *sc)

        @pl.when(step == (MIDDLE_AT * n_steps) // MIDDLE_OF)
        def _():
            for h in hooks:
                h[1]()

        @pl.when(step == n_steps - 1)
        def _():
            for h in hooks:
                h[2]()

    return list(pl.pallas_call(
        wrapped, name=name, grid=grid, in_specs=in_specs + [ANY] * len(j_in), out_specs=out_specs + [ANY] * len(j_out),
        out_shape=out_shape + j_out, scratch_shapes=scratch + j_sems, compiler_params=_params(len(grid)),
    )(*args, *j_in))


def _gmlp_gate(vnb, wsm, bst, tile):
    rows = []
    for n in range(tile // HEAD):
        cols = []
        for hh in range(N_HEADS):
            blk = vnb[n * HEAD:(n + 1) * HEAD, hh * HEAD:(hh + 1) * HEAD]
            cols.append(_mm(wsm[hh], blk) + bst[:, hh:hh + 1])
        rows.append(jnp.concatenate(cols, axis=1))
    return jnp.concatenate(rows, axis=0)


def _even_fwd(h, w_in, w_out, ws, bst, lng, lnb, wp, sc, gm, jobs=()):
    seq = h.shape[0]
    tile = min(MIX_TILE, seq)
    n_tiles = seq // tile

    def body(h_ref, hp_ref, win_ref, wout_ref, ws_ref, bst_ref, lng_ref, lnb_ref, wp_ref, sc_ref, g_ref,
             ho_ref, hn_ref, za_ref, pool_ref, mix_ref):
        i = pl.program_id(0)
        g = g_ref[...]
        h = h_ref[...]
        hnb = (h * _rms_r(h) * g).astype(bf16)
        hn_ref[...] = hnb
        z = _mm_nt(hnb, win_ref[...])
        zab = z[:, :2 * A_WIDTH].astype(bf16)
        za_ref[...] = zab
        hp = hp_ref[...]
        zbp = _mm_nt((hp * _rms_r(hp) * g).astype(bf16), win_ref[2 * A_WIDTH:, :])
        zbe = jnp.concatenate([jnp.where(i > 0, zbp, 0.0), z[:, 2 * A_WIDTH:]], axis=0)
        pooled = []
        for gi, win in enumerate(POOL_WINDOWS):
            xg = zbe[:, gi * HEAD:(gi + 1) * HEAD]
            s = _window_sum(xg, win, _down)
            pooled.append(s[POOL_HALO:] * _inv_count(i * tile, tile, win) - xg[POOL_HALO:])
        plb = jnp.concatenate(pooled, axis=1).astype(bf16)
        pool_ref[...] = plb

        ga, _ = _gelu(zab.astype(f32))
        vn, _, _ = _ln_fwd(ga[:, A_WIDTH:], lng_ref[...], lnb_ref[...])
        mask = _chunk_mask()
        wsm = [jnp.where(mask, ws_ref[hh], 0.0).astype(bf16) for hh in range(N_HEADS)]
        ya = ga[:, :A_WIDTH] * _gmlp_gate(vn.astype(bf16), wsm, bst_ref[...], tile)
        yb = jnp.concatenate([_mm(plb[:, gi * HEAD:(gi + 1) * HEAD], wp_ref[gi].astype(bf16))
                              for gi in range(len(POOL_WINDOWS))], axis=1) * sc_ref[...]
        mix = jnp.concatenate([ya, yb], axis=1).astype(bf16)
        mix_ref[...] = mix
        ho_ref[...] = h + _mm(mix, wout_ref[...])

    row = lambda cols: pl.BlockSpec((tile, cols), lambda i: (i, 0))
    return _launch(
        body, name="even_fwd", grid=(n_tiles,), jobs=jobs,
        in_specs=[row(D_MODEL), _prev_halo(tile, POOL_HALO, D_MODEL), _const(w_in.shape, 1), _const(w_out.shape, 1),
                  _const(ws.shape, 1), _const(bst.shape, 1), _const(lng.shape, 1), _const(lnb.shape, 1),
                  _const(wp.shape, 1), _const(sc.shape, 1), _const(gm.shape, 1)],
        out_specs=[row(D_MODEL), row(D_MODEL), row(2 * A_WIDTH), row(A_WIDTH), row(D_MODEL)],
        out_shape=[jax.ShapeDtypeStruct((seq, D_MODEL), f32), jax.ShapeDtypeStruct((seq, D_MODEL), bf16),
                   jax.ShapeDtypeStruct((seq, 2 * A_WIDTH), bf16), jax.ShapeDtypeStruct((seq, A_WIDTH), bf16),
                   jax.ShapeDtypeStruct((seq, D_MODEL), bf16)],
        args=(h, h, w_in, w_out, ws, bst, lng, lnb, wp, sc, gm))


def _even_bwd(dh, h, za, pooled, w_in, w_out, ws, bst, lng, lnb, wp, sc, gm, jobs=()):
    seq = h.shape[0]
    tile = min(MIX_TILE, seq)
    n_tiles = seq // tile
    n_groups = len(POOL_WINDOWS)

    def body(dh_ref, dhx_ref, h_ref, za_ref, pool_ref, win_ref, wout_ref, ws_ref, bst_ref, lng_ref, lnb_ref,
             wp_ref, sc_ref, g_ref,
             dhi_ref, dz_ref, dws_ref, dbs_ref, dlng_ref, dlnb_ref, dwp_ref, dsc_ref, dg_ref):
        i = pl.program_id(0)

        @pl.when(i == 0)
        def _():
            for ref in (dws_ref, dbs_ref, dlng_ref, dlnb_ref, dwp_ref, dsc_ref, dg_ref):
                ref[...] = jnp.zeros_like(ref)

        dh = dh_ref[...]
        dmix = _mm_nt(dh.astype(bf16), wout_ref[...])
        dya = dmix[:, :A_WIDTH]
        dyb = dmix[:, A_WIDTH:]
        dybx = _mm_nt(dhx_ref[...].astype(bf16), wout_ref[A_WIDTH:, :])
        dybx = jnp.where(i < n_tiles - 1, dybx, 0.0)

        za = za_ref[...].astype(f32)
        ga, th = _gelu(za)
        u = ga[:, :A_WIDTH]
        lng = lng_ref[...]
        vn, vh, r = _ln_fwd(ga[:, A_WIDTH:], lng, lnb_ref[...])
        vnb = vn.astype(bf16)
        mask = _chunk_mask()
        wsf = [jnp.where(mask, ws_ref[hh], 0.0) for hh in range(N_HEADS)]
        sv = _gmlp_gate(vnb, [w.astype(bf16) for w in wsf], bst_ref[...], tile)
        du = dya * sv
        dsvb = (dya * u).astype(bf16)
        wst = [w.T.astype(bf16) for w in wsf]
        ones = jnp.ones((8, HEAD), bf16)
        dws = [jnp.zeros((HEAD, HEAD), f32) for _ in range(N_HEADS)]
        dbs = [jnp.zeros((8, HEAD), f32) for _ in range(N_HEADS)]
        rows = []
        for n in range(tile // HEAD):
            cols = []
            for hh in range(N_HEADS):
                blk = dsvb[n * HEAD:(n + 1) * HEAD, hh * HEAD:(hh + 1) * HEAD]
                cols.append(_mm(wst[hh], blk))
                dws[hh] = dws[hh] + _mm_nt(blk, vnb[n * HEAD:(n + 1) * HEAD, hh * HEAD:(hh + 1) * HEAD])
                dbs[hh] = dbs[hh] + _mm_nt(ones, blk)
            rows.append(jnp.concatenate(cols, axis=1))
        dvn = jnp.concatenate(rows, axis=0)
        for hh in range(N_HEADS):
            dws_ref[hh] += jnp.where(mask, dws[hh], 0.0)
            dbs_ref[pl.ds(hh, 1), :] += dbs[hh][0:1, :]
        dlng_ref[...] += jnp.sum(dvn * vh, axis=0, keepdims=True)
        dlnb_ref[...] += jnp.sum(dvn, axis=0, keepdims=True)
        dv = _ln_bwd(dvn, vh, r, lng)
        dza = jnp.concatenate([du, dv], axis=1) * _gelu_grad(za, th)

        plb = pool_ref[...]
        sc = sc_ref[...]
        dzb = []
        dsc = []
        for gi, win in enumerate(POOL_WINDOWS):
            cs = slice(gi * HEAD, (gi + 1) * HEAD)
            wpb = wp_ref[gi].astype(bf16)
            dsc.append(jnp.sum(dyb[:, cs] * _mm(plb[:, cs], wpb), axis=0, keepdims=True))
            dpre = (dyb[:, cs] * sc[:, cs]).astype(bf16)
            dprex = (dybx[:, cs] * sc[:, cs]).astype(bf16)
            dwp_ref[gi] += _mm_tn(plb[:, cs], dpre)
            dpl = _mm_nt(dpre, wpb)
            dple = jnp.concatenate([dpl, _mm_nt(dprex, wpb)], axis=0)
            q = dple * _inv_count(i * tile, tile + POOL_HALO, win)
            dzb.append(_window_sum(q, win, _up)[:tile] - dpl)
        dsc_ref[...] += jnp.concatenate(dsc, axis=1)

        dzf = jnp.concatenate([dza] + dzb, axis=1).astype(bf16)
        dz_ref[...] = dzf
        dhn = _mm(dzf, win_ref[...])
        dhr, dg = _rms_bwd(dhn, h_ref[...], g_ref[...])
        dhi_ref[...] = dh + dhr
        dg_ref[...] += dg

    row = lambda cols: pl.BlockSpec((tile, cols), lambda i: (i, 0))
    small = [ws.shape, (N_HEADS, HEAD), lng.shape, lnb.shape, wp.shape, sc.shape, gm.shape]
    return _launch(
        body, name="even_bwd", grid=(n_tiles,), jobs=jobs,
        in_specs=[row(D_MODEL), _next_halo(tile, POOL_HALO, D_MODEL, seq), row(D_MODEL), row(2 * A_WIDTH), row(A_WIDTH),
                  _const(w_in.shape, 1), _const(w_out.shape, 1), _const(ws.shape, 1), _const(bst.shape, 1),
                  _const(lng.shape, 1), _const(lnb.shape, 1), _const(wp.shape, 1), _const(sc.shape, 1), _const(gm.shape, 1)],
        out_specs=[row(D_MODEL), row(3 * A_WIDTH)] + [_const(s, 1) for s in small],
        out_shape=[jax.ShapeDtypeStruct((seq, D_MODEL), f32), jax.ShapeDtypeStruct((seq, 3 * A_WIDTH), bf16)]
                  + [jax.ShapeDtypeStruct(s, f32) for s in small],
        args=(dh, dh, h, za, pooled, w_in, w_out, ws, bst, lng, lnb, wp, sc, gm))


SUBLANES = 8


class _Shifted:
    def __init__(self, x, shift, max_shift):
        self.rolled = [shift(x, b) for b in range(min(SUBLANES, max_shift + 1))]
        self.back = shift is _down

    def rows(self, k, start, count):
        whole = k - k % SUBLANES
        lo = start - whole if self.back else start + whole
        return self.rolled[k % SUBLANES][lo:lo + count]


def _conv_taps(xs, w_ref, n_taps, halo, rows):
    acc = None
    for j in range(n_taps):
        term = w_ref[pl.ds(j, 1), :] * xs.rows(n_taps - 1 - j, halo, rows)
        acc = term if acc is None else acc + term
    return acc


def _odd_fwd(h, w_in, w_out, cw, cb, clg, clb, dw, gm):
    seq = h.shape[0]
    tile = min(MIX_TILE, seq)
    n_tiles = seq // tile
    w = A_WIDTH

    def body(h_ref, hp_ref, win_ref, wout_ref, cw_ref, cb_ref, clg_ref, clb_ref, dw_ref, g_ref,
             ho_ref, hn_ref, z_ref, mix_ref, cv_ref):
        i = pl.program_id(0)
        g = g_ref[...]
        h = h_ref[...]
        hnb = (h * _rms_r(h) * g).astype(bf16)
        hn_ref[...] = hnb
        zb = _mm_nt(hnb, win_ref[...]).astype(bf16)
        z_ref[...] = zb
        hp = hp_ref[...]
        zp = _mm_nt((hp * _rms_r(hp) * g).astype(bf16), win_ref[...]).astype(bf16).astype(f32)
        z = zb.astype(f32)
        ze = jnp.concatenate([jnp.where(i > 0, zp, 0.0), z], axis=0)
        hc = ze[:, :w] * _sigmoid(ze[:, w:2 * w])
        cv = _conv_taps(_Shifted(hc, _down, C_KERNEL - 1), cw_ref, C_KERNEL, CONV_HALO, tile) + cb_ref[...]
        cv_ref[...] = cv
        ln, _, _ = _ln_fwd(cv, clg_ref[...], clb_ref[...])
        yc = ln * _sigmoid(ln)
        p = ze[:, 3 * w:4 * w] * ze[:, 4 * w:]
        yd = z[:, 2 * w:3 * w] * _conv_taps(_Shifted(p, _down, D_KERNEL - 1), dw_ref, D_KERNEL, CONV_HALO, tile)
        mix = jnp.concatenate([yc, yd], axis=1).astype(bf16)
        mix_ref[...] = mix
        ho_ref[...] = h + _mm(mix, wout_ref[...])

    row = lambda cols: pl.BlockSpec((tile, cols), lambda i: (i, 0))
    return pl.pallas_call(
        body, name="odd_fwd", grid=(n_tiles,),
        in_specs=[row(D_MODEL), _prev_halo(tile, CONV_HALO, D_MODEL), _const(w_in.shape, 1), _const(w_out.shape, 1),
                  _const(cw.shape, 1), _const(cb.shape, 1), _const(clg.shape, 1), _const(clb.shape, 1),
                  _const(dw.shape, 1), _const(gm.shape, 1)],
        out_specs=[row(D_MODEL), row(D_MODEL), row(5 * w), row(D_MODEL), row(w)],
        out_shape=[jax.ShapeDtypeStruct((seq, D_MODEL), f32), jax.ShapeDtypeStruct((seq, D_MODEL), bf16),
                   jax.ShapeDtypeStruct((seq, 5 * w), bf16), jax.ShapeDtypeStruct((seq, D_MODEL), bf16),
                   jax.ShapeDtypeStruct((seq, w), f32)],
        compiler_params=_params(1),
    )(h, h, w_in, w_out, cw, cb, clg, clb, dw, gm)


def _odd_bwd(dh, h, z, cv, w_in, w_out, cw, cb, clg, clb, dw, gm, jobs=()):
    seq = h.shape[0]
    tile = min(MIX_TILE, seq)
    n_tiles = seq // tile
    w = A_WIDTH
    halo = CONV_HALO

    def body(dh_ref, dhx_ref, h_ref, z_ref, zp_ref, zx_ref, cv_ref, cvx_ref, win_ref, wout_ref, cw_ref, cb_ref,
             clg_ref, clb_ref, dw_ref, g_ref,
             dhi_ref, dz_ref, dcw_ref, dcb_ref, dclg_ref, dclb_ref, ddw_ref, dg_ref):
        i = pl.program_id(0)

        @pl.when(i == 0)
        def _():
            for ref in (dcw_ref, dcb_ref, dclg_ref, dclb_ref, ddw_ref, dg_ref):
                ref[...] = jnp.zeros_like(ref)

        dh = dh_ref[...]
        dhe = jnp.concatenate([dh, jnp.where(i < n_tiles - 1, dhx_ref[...], 0.0)], axis=0)
        dmix = _mm_nt(dhe.astype(bf16), wout_ref[...])
        ze = jnp.concatenate([jnp.where(i > 0, zp_ref[...].astype(f32), 0.0), z_ref[...].astype(f32),
                              zx_ref[...].astype(f32)], axis=0)

        sg = _sigmoid(ze[:, w:2 * w])
        ca = ze[:, :w]
        hc = ca * sg
        hcs = _Shifted(hc, _down, C_KERNEL - 1)
        cv = jnp.concatenate([cv_ref[...], cvx_ref[...]], axis=0)
        clg = clg_ref[...]
        ln, xh, r = _ln_fwd(cv, clg, clb_ref[...])
        sl = _sigmoid(ln)
        dln = dmix[:, :w] * (sl * (1.0 + ln * (1.0 - sl)))
        dclg_ref[...] += jnp.sum((dln * xh)[:tile], axis=0, keepdims=True)
        dclb_ref[...] += jnp.sum(dln[:tile], axis=0, keepdims=True)
        dcv = _ln_bwd(dln, xh, r, clg)
        dcb_ref[...] += jnp.sum(dcv[:tile], axis=0, keepdims=True)
        dcvs = _Shifted(dcv, _up, C_KERNEL - 1)
        dhc = None
        for j in range(C_KERNEL):
            k = C_KERNEL - 1 - j
            dcw_ref[pl.ds(j, 1), :] += jnp.sum(dcv[:tile] * hcs.rows(k, halo, tile), axis=0, keepdims=True)
            term = cw_ref[pl.ds(j, 1), :] * dcvs.rows(k, 0, tile)
            dhc = term if dhc is None else dhc + term
        sgt = sg[halo:halo + tile]
        cat = ca[halo:halo + tile]
        dca = dhc * sgt
        dcg = dhc * cat * sgt * (1.0 - sgt)

        dcgv = ze[:, 3 * w:4 * w]
        dxin = ze[:, 4 * w:]
        p = dcgv * dxin
        ps = _Shifted(p, _down, D_KERNEL - 1)
        q = _conv_taps(ps, dw_ref, D_KERNEL, halo, tile)
        dyd = dmix[:, w:]
        dq = dyd * ze[halo:, 2 * w:3 * w]
        ddbg = dyd[:tile] * q
        dqs = _Shifted(dq, _up, D_KERNEL - 1)
        dp = None
        for j in range(D_KERNEL):
            k = D_KERNEL - 1 - j
            ddw_ref[pl.ds(j, 1), :] += jnp.sum(dq[:tile] * ps.rows(k, halo, tile), axis=0, keepdims=True)
            term = dw_ref[pl.ds(j, 1), :] * dqs.rows(k, 0, tile)
            dp = term if dp is None else dp + term
        ddcg = dp * dxin[halo:halo + tile]
        ddxin = dp * dcgv[halo:halo + tile]

        dzf = jnp.concatenate([dca, dcg, ddbg, ddcg, ddxin], axis=1).astype(bf16)
        dz_ref[...] = dzf
        dhn = _mm(dzf, win_ref[...])
        dhr, dg = _rms_bwd(dhn, h_ref[...], g_ref[...])
        dhi_ref[...] = dh + dhr
        dg_ref[...] += dg

    row = lambda cols: pl.BlockSpec((tile, cols), lambda i: (i, 0))
    small = [cw.shape, cb.shape, clg.shape, clb.shape, dw.shape, gm.shape]
    return _launch(
        body, name="odd_bwd", grid=(n_tiles,), jobs=jobs,
        in_specs=[row(D_MODEL), _next_halo(tile, halo, D_MODEL, seq), row(D_MODEL), row(5 * w),
                  _prev_halo(tile, halo, 5 * w), _next_halo(tile, halo, 5 * w, seq),
                  row(w), _next_halo(tile, halo, w, seq),
                  _const(w_in.shape, 1), _const(w_out.shape, 1), _const(cw.shape, 1), _const(cb.shape, 1),
                  _const(clg.shape, 1), _const(clb.shape, 1), _const(dw.shape, 1), _const(gm.shape, 1)],
        out_specs=[row(D_MODEL), row(5 * w)] + [_const(s, 1) for s in small],
        out_shape=[jax.ShapeDtypeStruct((seq, D_MODEL), f32), jax.ShapeDtypeStruct((seq, 5 * w), bf16)]
                  + [jax.ShapeDtypeStruct(s, f32) for s in small],
        args=(dh, dh, h, z, z, z, cv, cv, w_in, w_out, cw, cb, clg, clb, dw, gm))


def _ffn_chunks():
    assert sum(FFN_CHUNKS) == D_FF
    start = 0
    for size in FFN_CHUNKS:
        yield slice(start, start + size)
        start += size


def _ffn_fwd(h, wg, wu, wd, gm, jobs=(), head=None):
    seq = h.shape[0]
    tile = min(FFN_TILE, seq)

    def body(h_ref, g_ref, wg_ref, wu_ref, wd_ref, *refs):
        if head is None:
            ho_ref, hn_ref, gate_ref, up_ref = refs
        else:
            t_ref, gf_ref, ho_ref, hn_ref, gate_ref, up_ref, loss_ref, dgf_ref = refs
        h = h_ref[...]
        hnb = (h * _rms_r(h) * g_ref[...]).astype(bf16)
        hn_ref[...] = hnb
        acc = None
        for rows in _ffn_chunks():
            gb = _mm_nt(hnb, wg_ref[rows, :]).astype(bf16)
            ub = _mm_nt(hnb, wu_ref[rows, :]).astype(bf16)
            gate_ref[:, rows] = gb
            up_ref[:, rows] = ub
            gf = gb.astype(f32)
            act = gf * _sigmoid(gf) * ub.astype(f32)
            part = _mm(act.astype(bf16), wd_ref[rows, :])
            acc = part if acc is None else acc + part
        ho = h + acc
        if head is None:
            ho_ref[...] = ho
            return

        @pl.when(pl.program_id(0) == 0)
        def _():
            loss_ref[...] = jnp.zeros_like(loss_ref)
            dgf_ref[...] = jnp.zeros_like(dgf_ref)

        g_final = gf_ref[...]
        err = ho * _rms_r(ho) * g_final - t_ref[...]
        loss_ref[...] += (0.5 / D_MODEL) * jnp.sum(jnp.sum(err * err, axis=1, keepdims=True), axis=0, keepdims=True)
        dho, dg = _rms_bwd(err * (1.0 / D_MODEL), ho, g_final)
        ho_ref[...] = dho
        dgf_ref[...] += dg

    row = pl.BlockSpec((tile, D_MODEL), lambda i: (i, 0))
    wide = pl.BlockSpec((tile, D_FF), lambda i: (i, 0))
    in_specs = [row, _const(gm.shape, 1), _const(wg.shape, 1), _const(wu.shape, 1), _const(wd.shape, 1)]
    out_specs = [row, row, wide, wide]
    out_shape = [jax.ShapeDtypeStruct((seq, D_MODEL), f32), jax.ShapeDtypeStruct((seq, D_MODEL), bf16),
                 jax.ShapeDtypeStruct((seq, D_FF), bf16), jax.ShapeDtypeStruct((seq, D_FF), bf16)]
    args = (h, gm, wg, wu, wd)
    if head is not None:
        target, g_final = head
        in_specs += [row, _const(g_final.shape, 1)]
        out_specs += [_const((1, 1), 1), _const(g_final.shape, 1)]
        out_shape += [jax.ShapeDtypeStruct((1, 1), f32), jax.ShapeDtypeStruct(g_final.shape, f32)]
        args += (target, g_final)
    return _launch(
        body, name="ffn_fwd" if head is None else "ffn_fwd_loss", grid=(seq // tile,), jobs=jobs,
        in_specs=in_specs, out_specs=out_specs, out_shape=out_shape, args=args)


def _ffn_bwd(dh, h, gate, up, wg, wu, wd, gm, jobs=()):
    seq = h.shape[0]
    tile = min(FFN_TILE, seq)
    n_tiles = seq // tile

    def body(dh_ref, h_ref, g_ref, gate_ref, up_ref, wg_ref, wu_ref, wd_ref,
             dhi_ref, dgate_ref, dup_ref, act_ref, dg_ref):
        @pl.when(pl.program_id(0) == 0)
        def _():
            dg_ref[...] = jnp.zeros_like(dg_ref)

        dh = dh_ref[...]
        dhb = dh.astype(bf16)
        acc = None
        for rows in _ffn_chunks():
            dact = _mm_nt(dhb, wd_ref[rows, :])
            gf = gate_ref[:, rows].astype(f32)
            uf = up_ref[:, rows].astype(f32)
            s = _sigmoid(gf)
            silu = gf * s
            act_ref[:, rows] = (silu * uf).astype(bf16)
            dgb = (dact * uf * (s * (1.0 + gf * (1.0 - s)))).astype(bf16)
            dub = (dact * silu).astype(bf16)
            dgate_ref[:, rows] = dgb
            dup_ref[:, rows] = dub
            part = _mm(dgb, wg_ref[rows, :]) + _mm(dub, wu_ref[rows, :])
            acc = part if acc is None else acc + part
        dhr, dg = _rms_bwd(acc, h_ref[...], g_ref[...])
        dhi_ref[...] = dh + dhr
        dg_ref[...] += dg

    row = pl.BlockSpec((tile, D_MODEL), lambda i: (i, 0))
    wide = pl.BlockSpec((tile, D_FF), lambda i: (i, 0))
    return _launch(
        body, name="ffn_bwd", grid=(n_tiles,), jobs=jobs,
        in_specs=[row, row, _const(gm.shape, 1), wide, wide, _const(wg.shape, 1), _const(wu.shape, 1), _const(wd.shape, 1)],
        out_specs=[row, wide, wide, wide, _const(gm.shape, 1)],
        out_shape=[jax.ShapeDtypeStruct((seq, D_MODEL), f32), jax.ShapeDtypeStruct((seq, D_FF), bf16),
                   jax.ShapeDtypeStruct((seq, D_FF), bf16), jax.ShapeDtypeStruct((seq, D_FF), bf16),
                   jax.ShapeDtypeStruct(gm.shape, f32)],
        args=(dh, h, gm, gate, up, wg, wu, wd))


def _weight_grads(pairs, name, jobs=()):
    seq, m = pairs[0][0].shape
    tk = min(DW_TK, seq)
    tm = m if m <= DW_TM else m // 2
    n_k = seq // tk
    n_pairs = len(pairs)

    def body(*refs):
        x_refs = refs[0:2 * n_pairs:2]
        y_refs = refs[1:2 * n_pairs:2]
        o_refs = refs[2 * n_pairs:3 * n_pairs]
        acc_refs = refs[3 * n_pairs:]
        k = pl.program_id(1)
        @pl.when(k == 0)
        def _():
            for acc_ref in acc_refs:
                acc_ref[...] = jnp.zeros_like(acc_ref)

        for x_ref, y_ref, acc_ref in zip(x_refs, y_refs, acc_refs):
            acc_ref[...] += _mm_tn(x_ref[...].astype(bf16), y_ref[...].astype(bf16))

        @pl.when(k == n_k - 1)
        def _():
            for o_ref, acc_ref in zip(o_refs, acc_refs):
                o_ref[...] = acc_ref[...].astype(bf16)

    in_specs = []
    for _ in pairs:
        in_specs += [pl.BlockSpec((tk, tm), lambda j, k: (k, j)), pl.BlockSpec((tk, D_MODEL), lambda j, k: (k, 0))]
    return _launch(
        body, name=name, grid=(m // tm, n_k), jobs=jobs,
        in_specs=in_specs,
        out_specs=[pl.BlockSpec((tm, D_MODEL), lambda j, k: (j, 0))] * n_pairs,
        out_shape=[jax.ShapeDtypeStruct((m, D_MODEL), bf16)] * n_pairs,
        scratch=[pltpu.VMEM((tm, D_MODEL), f32)] * n_pairs,
        args=[a for pair in pairs for a in pair])


def _row_tile(rows, limit=512):
    best = rows
    for t in range(8, min(rows, limit) + 1, 8):
        if rows % t == 0:
            best = t
    return best if rows > limit else rows


def _adamw(w, g, m, v, name):
    rows, cols = w.shape
    tr = _row_tile(rows)

    def body(w_ref, g_ref, m_ref, v_ref, d_ref, mo_ref, vo_ref):
        g = g_ref[...]
        m2 = ADAM_B1 * m_ref[...] + (1.0 - ADAM_B1) * g
        v2 = ADAM_B2 * v_ref[...] + (1.0 - ADAM_B2) * (g * g)
        m_hat = m2 / (1.0 - ADAM_B1 ** ADAM_STEP)
        v_hat = v2 / (1.0 - ADAM_B2 ** ADAM_STEP)
        d_ref[...] = -ADAM_LR * (m_hat / (jnp.sqrt(v_hat) + ADAM_EPS) + ADAM_WD * w_ref[...])
        mo_ref[...] = m2
        vo_ref[...] = v2

    spec = pl.BlockSpec((tr, cols), lambda i: (i, 0))
    return pl.pallas_call(
        body, name=name, grid=(rows // tr,),
        in_specs=[spec] * 4, out_specs=[spec] * 3,
        out_shape=[jax.ShapeDtypeStruct((rows, cols), f32)] * 3,
        compiler_params=_params(1),
    )(w, g, m, v)


def _sum_leading(x, name):
    n, rows, cols = x.shape
    tr = _row_tile(rows)

    def body(x_ref, o_ref):
        acc = x_ref[0].astype(f32)
        for k in range(1, n):
            acc = acc + x_ref[k].astype(f32)
        o_ref[...] = acc

    return pl.pallas_call(
        body, name=name, grid=(rows // tr,),
        in_specs=[pl.BlockSpec((n, tr, cols), lambda i: (0, i, 0))],
        out_specs=pl.BlockSpec((tr, cols), lambda i: (i, 0)),
        out_shape=jax.ShapeDtypeStruct((rows, cols), f32),
        compiler_params=_params(1),
    )(x)


def _pair_sum(gs, recvs, c_idx, name):
    n = len(gs)

    def body(c_ref, *refs):
        for g_ref, r_ref, o_ref in zip(refs[:n], refs[n:2 * n], refs[2 * n:]):
            o_ref[...] = (g_ref[...].astype(f32) + r_ref[...].astype(f32)).astype(o_ref.dtype)

    own = [pl.BlockSpec((1,) + g.shape[1:], lambda k, c: (2 * k + c[0], 0, 0)) for g in gs]
    by_chip = [pl.BlockSpec((1,) + g.shape[1:], lambda k, c: (k, 0, 0)) for g in gs]
    return list(pl.pallas_call(
        body, name=name,
        grid_spec=pltpu.PrefetchScalarGridSpec(num_scalar_prefetch=1, grid=(N_CHIP,),
                                               in_specs=own + by_chip, out_specs=by_chip),
        out_shape=[jax.ShapeDtypeStruct((N_CHIP,) + g.shape[1:], g.dtype) for g in gs],
        compiler_params=_params(1),
    )(c_idx, *gs, *recvs))


def _chip_sum(parts, name):
    n = len(parts)

    def body(*refs):
        for x_ref, o_ref in zip(refs[:n], refs[n:]):
            acc = x_ref[0].astype(f32)
            for k in range(1, N_CHIP):
                acc = acc + x_ref[k].astype(f32)
            o_ref[...] = acc

    return list(pl.pallas_call(
        body, name=name, grid=(1,),
        in_specs=[pl.BlockSpec(p.shape, lambda i: (0, 0, 0)) for p in parts],
        out_specs=[pl.BlockSpec(p.shape[1:], lambda i: (0, 0)) for p in parts],
        out_shape=[jax.ShapeDtypeStruct(p.shape[1:], f32) for p in parts],
        compiler_params=_params(1),
    )(*parts))


def _pack_rows(w):
    return w.reshape(N_DEV, -1, D_MODEL)


def kernel(x, even_w_in, even_w_out, a_w_s, a_b_s, a_ln_g, a_ln_b, b_w_pool, b_scale, odd_w_in, odd_w_out, c_w_dw, c_b_dw, c_ln_g, c_ln_b, d_w_dw, norm_mix_g, norm_ffn_g, ffn_w_gate, ffn_w_up, ffn_w_down, final_norm_g, loss_target, m_even_w_in, m_even_w_out, m_a_w_s, m_a_b_s, m_a_ln_g, m_a_ln_b, m_b_w_pool, m_b_scale, m_odd_w_in, m_odd_w_out, m_c_w_dw, m_c_b_dw, m_c_ln_g, m_c_ln_b, m_d_w_dw, m_norm_mix_g, m_norm_ffn_g, m_ffn_w_gate, m_ffn_w_up, m_ffn_w_down, m_final_norm_g, v_even_w_in, v_even_w_out, v_a_w_s, v_a_b_s, v_a_ln_g, v_a_ln_b, v_b_w_pool, v_b_scale, v_odd_w_in, v_odd_w_out, v_c_w_dw, v_c_b_dw, v_c_ln_g, v_c_ln_b, v_d_w_dw, v_norm_mix_g, v_norm_ffn_g, v_ffn_w_gate, v_ffn_w_up, v_ffn_w_down, v_final_norm_g):
    weights = dict(even_w_in=even_w_in, even_w_out=even_w_out, a_w_s=a_w_s, a_b_s=a_b_s, a_ln_g=a_ln_g, a_ln_b=a_ln_b,
                   b_w_pool=b_w_pool, b_scale=b_scale, odd_w_in=odd_w_in, odd_w_out=odd_w_out, c_w_dw=c_w_dw,
                   c_b_dw=c_b_dw, c_ln_g=c_ln_g, c_ln_b=c_ln_b, d_w_dw=d_w_dw, norm_mix_g=norm_mix_g,
                   norm_ffn_g=norm_ffn_g, ffn_w_gate=ffn_w_gate, ffn_w_up=ffn_w_up, ffn_w_down=ffn_w_down,
                   final_norm_g=final_norm_g)
    m_in = dict(even_w_in=m_even_w_in, even_w_out=m_even_w_out, a_w_s=m_a_w_s, a_b_s=m_a_b_s, a_ln_g=m_a_ln_g,
                a_ln_b=m_a_ln_b, b_w_pool=m_b_w_pool, b_scale=m_b_scale, odd_w_in=m_odd_w_in, odd_w_out=m_odd_w_out,
                c_w_dw=m_c_w_dw, c_b_dw=m_c_b_dw, c_ln_g=m_c_ln_g, c_ln_b=m_c_ln_b, d_w_dw=m_d_w_dw,
                norm_mix_g=m_norm_mix_g, norm_ffn_g=m_norm_ffn_g, ffn_w_gate=m_ffn_w_gate, ffn_w_up=m_ffn_w_up,
                ffn_w_down=m_ffn_w_down, final_norm_g=m_final_norm_g)
    v_in = dict(even_w_in=v_even_w_in, even_w_out=v_even_w_out, a_w_s=v_a_w_s, a_b_s=v_a_b_s, a_ln_g=v_a_ln_g,
                a_ln_b=v_a_ln_b, b_w_pool=v_b_w_pool, b_scale=v_b_scale, odd_w_in=v_odd_w_in, odd_w_out=v_odd_w_out,
                c_w_dw=v_c_w_dw, c_b_dw=v_c_b_dw, c_ln_g=v_c_ln_g, c_ln_b=v_c_ln_b, d_w_dw=v_d_w_dw,
                norm_mix_g=v_norm_mix_g, norm_ffn_g=v_norm_ffn_g, ffn_w_gate=v_ffn_w_gate, ffn_w_up=v_ffn_w_up,
                ffn_w_down=v_ffn_w_down, final_norm_g=v_final_norm_g)
    names = list(weights)

    group_parts = {
        "even": [even_w_in[0].T, even_w_out[0]],
        "ffn0": [ffn_w_gate[0].T, ffn_w_up[0].T, ffn_w_down[0]],
        "odd": [odd_w_in[0].T, odd_w_out[0]],
        "ffn1": [ffn_w_gate[1].T, ffn_w_up[1].T, ffn_w_down[1]],
    }

    def gather_jobs(*groups):
        return [_all_gather_job(p.astype(bf16)) for k in groups for p in group_parts[k]]

    def whole(gathered):
        return [g.reshape(-1, D_MODEL) for g in gathered]

    conv_names = ["c_w_dw", "c_b_dw", "c_ln_g", "c_ln_b", "d_w_dw"]
    conv_rows = [C_KERNEL, 1, 1, 1, D_KERNEL]
    conv_local = jnp.concatenate([weights[n].reshape(r, -1) for n, r in zip(conv_names, conv_rows)]
                                 + [jnp.zeros((3, c_b_dw.shape[-1]), f32)], axis=0)
    *even_gathered, conv_all = _run_jobs(gather_jobs("even") + [_all_gather_job(conv_local)], "gather_even_conv")
    w_in_e, w_out_e = whole(even_gathered)
    conv_all = conv_all.transpose(1, 0, 2).reshape(conv_local.shape[0], -1)
    conv_offs = [sum(conv_rows[:k]) for k in range(len(conv_rows) + 1)]
    cw, cb, clg, clb, dw = [conv_all[conv_offs[k]:conv_offs[k + 1]] for k in range(len(conv_rows))]

    ws, bst = a_w_s[0], a_b_s[0].T
    lng, lnb, wp, sc = a_ln_g, a_ln_b, b_w_pool[0], b_scale
    gmix = [norm_mix_g[l:l + 1] for l in range(2)]
    gffn = [norm_ffn_g[l:l + 1] for l in range(2)]
    gfin = final_norm_g.reshape(1, D_MODEL)

    h0 = x[0]
    h1, hn_e, za, pooled, mix_e, *ffn0_gathered = _even_fwd(
        h0, w_in_e, w_out_e, ws, bst, lng, lnb, wp, sc, gmix[0], jobs=gather_jobs("ffn0"))
    w_gate0, w_up0, w_down0 = whole(ffn0_gathered)
    h2, hn_f0, gate0, up0, *rest_gathered = _ffn_fwd(h1, w_gate0, w_up0, w_down0, gffn[0],
                                                     jobs=gather_jobs("odd", "ffn1"))
    w_in_o, w_out_o, w_gate1, w_up1, w_down1 = whole(rest_gathered)
    h3, hn_o, z_o, mix_o, cv_o = _odd_fwd(h2, w_in_o, w_out_o, cw, cb, clg, clb, dw, gmix[1])
    dh4, hn_f1, gate1, up1, loss_local, g_final = _ffn_fwd(h3, w_gate1, w_up1, w_down1, gffn[1],
                                                           head=(loss_target[0], gfin))

    c_idx = lax.axis_index("c").astype(jnp.int32).reshape(1)

    def weight_grad(x, y, name, jobs=()):
        g, *job_results = _weight_grads([(x, y)], name, jobs=jobs)
        return [_pack_rows(g)] + job_results

    def siblings(parts):
        return [_sibling_exchange_job(p) for p in parts]

    def chips(pairs):
        return [_chip_exchange_job(p) for p in pairs]

    dh3, dgate1, dup1, act1, g_ffn1 = _ffn_bwd(dh4, h3, gate1, up1, w_gate1, w_up1, w_down1, gffn[1])
    part_ffn1 = (weight_grad(dgate1, hn_f1, "dw_gate1") + weight_grad(dup1, hn_f1, "dw_up1")
                 + weight_grad(act1, dh4, "dw_down1"))
    dh2, dz_o, g_cw, g_cb, g_clg, g_clb, g_dw, g_mix1, *recv_ffn1 = _odd_bwd(
        dh3, h2, z_o, cv_o, w_in_o, w_out_o, cw, cb, clg, clb, dw, gmix[1], jobs=siblings(part_ffn1))
    pair_ffn1 = _pair_sum(part_ffn1, recv_ffn1, c_idx, "pair_sum_ffn1")
    part_odd = weight_grad(dz_o, hn_o, "dw_odd_in") + weight_grad(mix_o, dh3, "dw_odd_out")
    dh1, dgate0, dup0, act0, g_ffn0, *exchanged = _ffn_bwd(
        dh2, h1, gate0, up0, w_gate0, w_up0, w_down0, gffn[0], jobs=chips(pair_ffn1) + siblings(part_odd))
    chips_ffn1, recv_odd = exchanged[:3], exchanged[3:]
    pair_odd = _pair_sum(part_odd, recv_odd, c_idx, "pair_sum_odd")
    dw_gate0, *chips_odd = weight_grad(dgate0, hn_f0, "dw_gate0", jobs=chips(pair_odd))
    part_ffn0 = [dw_gate0] + weight_grad(dup0, hn_f0, "dw_up0") + weight_grad(act0, dh2, "dw_down0")
    part_even_out, *recv_ffn0 = weight_grad(mix_e, dh1, "dw_even_out", jobs=siblings(part_ffn0))
    pair_ffn0 = _pair_sum(part_ffn0, recv_ffn0, c_idx, "pair_sum_ffn0")
    dh0, dz_e, g_ws, g_bs, g_lng, g_lnb, g_wp, g_sc, g_mix0, *exchanged = _even_bwd(
        dh1, h0, za, pooled, w_in_e, w_out_e, ws, bst, lng, lnb, wp, sc, gmix[0],
        jobs=chips(pair_ffn0) + siblings([part_even_out]))
    chips_ffn0, recv_even_out = exchanged[:3], exchanged[3:]
    pair_even_out = _pair_sum([part_even_out], recv_even_out, c_idx, "pair_sum_even_out")

    lanes = HEAD
    small = [("a_w_s", g_ws), ("a_b_s", g_bs), ("a_ln_g", g_lng), ("a_ln_b", g_lnb), ("b_w_pool", g_wp),
             ("b_scale", g_sc), ("norm_mix_g", jnp.concatenate([g_mix0, g_mix1], axis=0)),
             ("norm_ffn_g", jnp.concatenate([g_ffn0, g_ffn1], axis=0)), ("final_norm_g", g_final),
             ("c_w_dw", g_cw), ("c_b_dw", g_cb), ("c_ln_g", g_clg), ("c_ln_b", g_clb), ("d_w_dw", g_dw)]
    small_rows = [-(-g.size // (8 * lanes)) * 8 for _, g in small]
    small_offs = [sum(small_rows[:k]) for k in range(len(small) + 1)]
    pad_rows = -small_offs[-1] % 256
    small_buf = jnp.concatenate(
        [jnp.pad(g.reshape(-1), (0, r * lanes - g.size)).reshape(r, lanes) for (_, g), r in zip(small, small_rows)]
        + [jnp.zeros((pad_rows, lanes), f32)], axis=0)
    part_even_in, small_all, chips_even_out = weight_grad(
        dz_e, hn_e, "dw_even_in", jobs=[_all_gather_job(small_buf)] + chips(pair_even_out))
    small_sum = _sum_leading(small_all, "small_grad_sum")
    recv_even_in = _run_jobs(siblings([part_even_in]), "sibling_exchange_even_in")
    chips_even_in = _run_jobs(chips(_pair_sum([part_even_in], recv_even_in, c_idx, "pair_sum_even_in")),
                              "chip_exchange_even_in")
    grads = {}
    for k, (n, g) in enumerate(small):
        grads[n] = small_sum[small_offs[k]:small_offs[k + 1]].reshape(-1)[:g.size].reshape(g.shape)
    me = 4 * lax.axis_index("x") + 2 * lax.axis_index("y") + lax.axis_index("c")
    shard = c_b_dw.shape[-1]
    for n in conv_names:
        grads[n] = lax.dynamic_slice_in_dim(grads[n], me * shard, shard, axis=1)

    col_sharded = ("even_w_in", "odd_w_in", "ffn_w_gate", "ffn_w_up")

    def rows_view(n, a):
        return jnp.swapaxes(a, -1, -2) if n in col_sharded else a

    grads["even_w_in"], = _chip_sum(chips_even_in, "chip_sum_even_in")
    grads["even_w_out"], = _chip_sum([chips_even_out], "chip_sum_even_out")
    grads["odd_w_in"], grads["odd_w_out"] = _chip_sum(chips_odd, "chip_sum_odd")
    per_layer = [_chip_sum(chips_ffn0, "chip_sum_ffn0"), _chip_sum(chips_ffn1, "chip_sum_ffn1")]
    for k, n in enumerate(["ffn_w_gate", "ffn_w_up", "ffn_w_down"]):
        grads[n] = jnp.stack([per_layer[l][k] for l in range(2)])
    grads = {n: grads[n].reshape(rows_view(n, weights[n]).shape) for n in names}

    delta, new_m, new_v = {}, {}, {}
    for n in names:
        w_rows = rows_view(n, weights[n])
        view = (-1, w_rows.shape[-1])
        outs = _adamw(w_rows.reshape(view), grads[n].reshape(view), rows_view(n, m_in[n]).reshape(view),
                      rows_view(n, v_in[n]).reshape(view), "adamw_" + n)
        delta[n], new_m[n], new_v[n] = [rows_view(n, o.reshape(w_rows.shape)) for o in outs]
        grads[n] = rows_view(n, grads[n])

    loss = lax.psum(loss_local[0, 0], ("x", "y", "c"))
    return (loss, dh0[None], *[grads[n] for n in names], *[delta[n] for n in names],
            *[new_m[n] for n in names], *[new_v[n] for n in names])
```

```python
import jax
import jax.numpy as jnp
from jax import lax
from jax.experimental import pallas as pl
from jax.experimental.pallas import tpu as pltpu

f32 = jnp.float32
bf16 = jnp.bfloat16

EPS = 1e-6
D_MODEL = 1024
A_WIDTH = 512
HEAD = 128
N_HEADS = 4
CHUNK = 64
POOL_WINDOWS = (2, 4, 8, 16)
POOL_HALO = 16
C_KERNEL = 31
D_KERNEL = 3
CONV_HALO = 32
D_FF = 2816
N_DEV = 8
N_CHIP = 4

ADAM_LR = 0.001
ADAM_B1 = 0.9
ADAM_B2 = 0.999
ADAM_EPS = 1e-08
ADAM_WD = 0.01
ADAM_STEP = 10

MIX_TILE = 512
FFN_TILE = 256
FFN_CHUNKS = (1536, 1280)
DW_TK = 2048
DW_TM = 1536
MIDDLE_AT, MIDDLE_OF = 7, 8
VMEM_LIMIT = 56 * 1024 * 1024

MESH = pl.DeviceIdType.MESH
ANY = pl.BlockSpec(memory_space=pl.ANY)


def _params(n_axes):
    return pltpu.CompilerParams(dimension_semantics=("arbitrary",) * n_axes, vmem_limit_bytes=VMEM_LIMIT)


def _mm(a, b):
    return jnp.dot(a, b, preferred_element_type=f32)


def _mm_nt(a, b):
    return lax.dot_general(a, b, (((1,), (1,)), ((), ())), preferred_element_type=f32)


def _mm_tn(a, b):
    return lax.dot_general(a, b, (((0,), (0,)), ((), ())), preferred_element_type=f32)


def _sigmoid(x):
    return 1.0 / (1.0 + jnp.exp(-x))


def _rms_r(h):
    return lax.rsqrt(jnp.mean(h * h, axis=-1, keepdims=True) + EPS)


def _rms_bwd(dy, h, g):
    r = _rms_r(h)
    xh = h * r
    dxh = dy * g
    dh = r * (dxh - xh * jnp.mean(dxh * xh, axis=-1, keepdims=True))
    return dh, jnp.sum(dy * xh, axis=0, keepdims=True)


def _ln_fwd(x, g, b):
    mu = jnp.mean(x, axis=-1, keepdims=True)
    xc = x - mu
    r = lax.rsqrt(jnp.mean(xc * xc, axis=-1, keepdims=True) + EPS)
    xh = xc * r
    return xh * g + b, xh, r


def _ln_bwd(dy, xh, r, g):
    dxh = dy * g
    return r * (dxh - jnp.mean(dxh, axis=-1, keepdims=True) - xh * jnp.mean(dxh * xh, axis=-1, keepdims=True))


_GELU_C = 0.7978845608028654
_GELU_A = 0.044715


def _gelu(x):
    th = jnp.tanh(x * (_GELU_C + (_GELU_C * _GELU_A) * (x * x)))
    half = 0.5 * x
    return half + half * th, th


def _gelu_grad(x, th):
    return 0.5 + 0.5 * th + (1.0 - th * th) * (x * (0.5 * _GELU_C + (1.5 * _GELU_C * _GELU_A) * (x * x)))


def _down(x, k):
    return x if k == 0 else pltpu.roll(x, k, 0)


def _up(x, k):
    return x if k == 0 else pltpu.roll(x, x.shape[0] - k, 0)


def _window_sum(x, win, shift):
    s = x
    step = 1
    while step < win:
        s = s + shift(s, step)
        step *= 2
    return s


def _inv_count(t0, rows, win):
    t = t0 + lax.broadcasted_iota(jnp.int32, (rows, 1), 0)
    return 1.0 / jnp.minimum(t + 1, win).astype(f32)


def _chunk_mask():
    i = lax.broadcasted_iota(jnp.int32, (HEAD, HEAD), 0)
    j = lax.broadcasted_iota(jnp.int32, (HEAD, HEAD), 1)
    return jnp.logical_or(i >= CHUNK, j < CHUNK)


def _const(shape, n_axes):
    zeros = (0,) * len(shape)
    if n_axes == 1:
        return pl.BlockSpec(shape, lambda i: zeros)
    return pl.BlockSpec(shape, lambda i, j: zeros)


def _prev_halo(tile, halo, cols):
    return pl.BlockSpec((halo, cols), lambda i: (jnp.maximum(i * (tile // halo) - 1, 0), 0))


def _next_halo(tile, halo, cols, seq):
    return pl.BlockSpec((halo, cols), lambda i: (jnp.minimum((i + 1) * (tile // halo), seq // halo - 1), 0))


class _Job:
    def __init__(self, inputs, out_shape, sems, hooks):
        self.inputs, self.out_shape, self.sems, self.hooks = inputs, out_shape, sems, hooks


def _position():
    return lax.axis_index("x"), lax.axis_index("y"), lax.axis_index("c")


def _all_gather_job(block):
    rows, cols = block.shape

    def hooks(ins, outs, sems):
        (x_ref,), (out_ref,), (send_sems, recv_sems, local_sem) = ins, outs, sems
        x, y, c = _position()
        me, sibling = (x, y, c), (x, y, 1 - c)
        chips = [(1 - x, y), (x, 1 - y), (1 - x, 1 - y)]

        def slot(px, py, pc):
            return out_ref.at[4 * px + 2 * py + pc]

        def copy(k, block_of, to, src=None):
            return pltpu.make_async_remote_copy(
                src_ref=slot(*block_of) if src is None else src, dst_ref=slot(*block_of),
                send_sem=send_sems.at[k], recv_sem=recv_sems.at[k], device_id=to, device_id_type=MESH)

        mine = pltpu.make_async_copy(x_ref, slot(*me), local_sem)
        first = [copy(0, me, sibling, src=x_ref)]
        first += [copy(1 + j, me, (*chip, c), src=x_ref) for j, chip in enumerate(chips)]
        passed = [copy(4 + j, (*chip, c), sibling) for j, chip in enumerate(chips)]

        def start():
            mine.start()
            for cp in first:
                cp.start()

        def middle():
            for j, chip in enumerate(chips):
                copy(1 + j, (*chip, c), me).wait_recv()
                passed[j].start()

        def finish():
            copy(0, sibling, me).wait_recv()
            for j, chip in enumerate(chips):
                copy(4 + j, (*chip, 1 - c), me).wait_recv()
            for cp in first + passed:
                cp.wait_send()
            mine.wait()

        return start, middle, finish

    return _Job([block], [jax.ShapeDtypeStruct((N_DEV, rows, cols), block.dtype)],
                [pltpu.SemaphoreType.DMA((7,)), pltpu.SemaphoreType.DMA((7,)), pltpu.SemaphoreType.DMA], hooks)


def _sibling_exchange_job(g):
    _, rows, cols = g.shape

    def hooks(ins, outs, sems):
        (g_ref,), (recv_ref,), (send_sems, recv_sems) = ins, outs, sems
        x, y, c = _position()
        copies = [pltpu.make_async_remote_copy(
            src_ref=g_ref.at[2 * k + (1 - c)], dst_ref=recv_ref.at[k], send_sem=send_sems.at[k],
            recv_sem=recv_sems.at[k], device_id=(x, y, 1 - c), device_id_type=MESH) for k in range(N_CHIP)]

        def start():
            for cp in copies:
                cp.start()

        def finish():
            for cp in copies:
                cp.wait()

        return start, lambda: None, finish

    return _Job([g], [jax.ShapeDtypeStruct((N_CHIP, rows, cols), g.dtype)],
                [pltpu.SemaphoreType.DMA((N_CHIP,)), pltpu.SemaphoreType.DMA((N_CHIP,))], hooks)


def _chip_exchange_job(p):
    _, rows, cols = p.shape

    def hooks(ins, outs, sems):
        (p_ref,), (recv_ref,), (send_sems, recv_sems, local_sem) = ins, outs, sems
        x, y, c = _position()
        k_me = 2 * x + y
        mine = pltpu.make_async_copy(p_ref.at[k_me], recv_ref.at[k_me], local_sem)
        copies = [pltpu.make_async_remote_copy(
            src_ref=p_ref.at[2 * px + py], dst_ref=recv_ref.at[k_me], send_sem=send_sems.at[j],
            recv_sem=recv_sems.at[j], device_id=(px, py, c), device_id_type=MESH)
            for j, (px, py) in enumerate([(1 - x, y), (x, 1 - y), (1 - x, 1 - y)])]

        def start():
            mine.start()
            for cp in copies:
                cp.start()

        def finish():
            for cp in copies:
                cp.wait()
            mine.wait()

        return start, lambda: None, finish

    return _Job([p], [jax.ShapeDtypeStruct((N_CHIP, rows, cols), p.dtype)],
                [pltpu.SemaphoreType.DMA((3,)), pltpu.SemaphoreType.DMA((3,)), pltpu.SemaphoreType.DMA], hooks)


def _job_hooks(jobs, ins, outs, sems):
    hooks = []
    for job in jobs:
        n_in, n_out, n_sem = len(job.inputs), len(job.out_shape), len(job.sems)
        hooks.append(job.hooks(ins[:n_in], outs[:n_out], sems[:n_sem]))
        ins, outs, sems = ins[n_in:], outs[n_out:], sems[n_sem:]
    return hooks


def _run_jobs(jobs, name):
    n_in = sum(len(job.inputs) for job in jobs)
    n_out = sum(len(job.out_shape) for job in jobs)

    def body(*refs):
        hooks = _job_hooks(jobs, refs[:n_in], refs[n_in:n_in + n_out], refs[n_in + n_out:])
        for phase in range(3):
            for h in hooks:
                h[phase]()

    return list(pl.pallas_call(
        body, name=name, in_specs=[ANY] * n_in, out_specs=[ANY] * n_out,
        out_shape=[s for job in jobs for s in job.out_shape],
        scratch_shapes=[s for job in jobs for s in job.sems],
    )(*[a for job in jobs for a in job.inputs]))


def _launch(body, *, name, grid, in_specs, out_specs, out_shape, args, scratch=(), jobs=()):
    in_specs, out_specs, out_shape, scratch = list(in_specs), list(out_specs), list(out_shape), list(scratch)
    if not jobs:
        return list(pl.pallas_call(body, name=name, grid=grid, in_specs=in_specs, out_specs=out_specs,
                                   out_shape=out_shape, scratch_shapes=scratch,
                                   compiler_params=_params(len(grid)))(*args))
    n_in, n_out, n_sc = len(in_specs), len(out_specs), len(scratch)
    j_in = [a for job in jobs for a in job.inputs]
    j_out = [s for job in jobs for s in job.out_shape]
    j_sems = [s for job in jobs for s in job.sems]
    n_steps = 1
    for g in grid:
        n_steps *= g

    def wrapped(*refs):
        ins, refs = refs[:n_in], refs[n_in:]
        jins, refs = refs[:len(j_in)], refs[len(j_in):]
        outs, refs = refs[:n_out], refs[n_out:]
        jouts, refs = refs[:len(j_out)], refs[len(j_out):]
        sc, jsems = refs[:n_sc], refs[n_sc:]
        step = pl.program_id(0)
        for axis in range(1, len(grid)):
            step = step * grid[axis] + pl.program_id(axis)
        hooks = _job_hooks(jobs, jins, jouts, jsems)

        @pl.when(step == 0)
        def _():
            for h in hooks:
                h[0]()

        body(*ins, *outs, *sc)

        @pl.when(step == (MIDDLE_AT * n_steps) // MIDDLE_OF)
        def _():
            for h in hooks:
                h[1]()

        @pl.when(step == n_steps - 1)
        def _():
            for h in hooks:
                h[2]()

    return list(pl.pallas_call(
        wrapped, name=name, grid=grid, in_specs=in_specs + [ANY] * len(j_in), out_specs=out_specs + [ANY] * len(j_out),
        out_shape=out_shape + j_out, scratch_shapes=scratch + j_sems, compiler_params=_params(len(grid)),
    )(*args, *j_in))


def _gmlp_gate(vnb, wsm, bst, tile):
    rows = []
    for n in range(tile // HEAD):
        cols = []
        for hh in range(N_HEADS):
            blk = vnb[n * HEAD:(n + 1) * HEAD, hh * HEAD:(hh + 1) * HEAD]
            cols.append(_mm(wsm[hh], blk) + bst[:, hh:hh + 1])
        rows.append(jnp.concatenate(cols, axis=1))
    return jnp.concatenate(rows, axis=0)


def _even_fwd(h, w_in, w_out, ws, bst, lng, lnb, wp, sc, gm, jobs=()):
    seq = h.shape[0]
    tile = min(MIX_TILE, seq)
    n_tiles = seq // tile

    def body(h_ref, hp_ref, win_ref, wout_ref, ws_ref, bst_ref, lng_ref, lnb_ref, wp_ref, sc_ref, g_ref,
             ho_ref, hn_ref, za_ref, pool_ref, mix_ref):
        i = pl.program_id(0)
        g = g_ref[...]
        h = h_ref[...]
        hnb = (h * _rms_r(h) * g).astype(bf16)
        hn_ref[...] = hnb
        z = _mm_nt(hnb, win_ref[...])
        zab = z[:, :2 * A_WIDTH].astype(bf16)
        za_ref[...] = zab
        hp = hp_ref[...]
        zbp = _mm_nt((hp * _rms_r(hp) * g).astype(bf16), win_ref[2 * A_WIDTH:, :])
        zbe = jnp.concatenate([jnp.where(i > 0, zbp, 0.0), z[:, 2 * A_WIDTH:]], axis=0)
        pooled = []
        for gi, win in enumerate(POOL_WINDOWS):
            xg = zbe[:, gi * HEAD:(gi + 1) * HEAD]
            s = _window_sum(xg, win, _down)
            pooled.append(s[POOL_HALO:] * _inv_count(i * tile, tile, win) - xg[POOL_HALO:])
        plb = jnp.concatenate(pooled, axis=1).astype(bf16)
        pool_ref[...] = plb

        ga, _ = _gelu(zab.astype(f32))
        vn, _, _ = _ln_fwd(ga[:, A_WIDTH:], lng_ref[...], lnb_ref[...])
        mask = _chunk_mask()
        wsm = [jnp.where(mask, ws_ref[hh], 0.0).astype(bf16) for hh in range(N_HEADS)]
        ya = ga[:, :A_WIDTH] * _gmlp_gate(vn.astype(bf16), wsm, bst_ref[...], tile)
        yb = jnp.concatenate([_mm(plb[:, gi * HEAD:(gi + 1) * HEAD], wp_ref[gi].astype(bf16))
                              for gi in range(len(POOL_WINDOWS))], axis=1) * sc_ref[...]
        mix = jnp.concatenate([ya, yb], axis=1).astype(bf16)
        mix_ref[...] = mix
        ho_ref[...] = h + _mm(mix, wout_ref[...])

    row = lambda cols: pl.BlockSpec((tile, cols), lambda i: (i, 0))
    return _launch(
        body, name="even_fwd", grid=(n_tiles,), jobs=jobs,
        in_specs=[row(D_MODEL), _prev_halo(tile, POOL_HALO, D_MODEL), _const(w_in.shape, 1), _const(w_out.shape, 1),
                  _const(ws.shape, 1), _const(bst.shape, 1), _const(lng.shape, 1), _const(lnb.shape, 1),
                  _const(wp.shape, 1), _const(sc.shape, 1), _const(gm.shape, 1)],
        out_specs=[row(D_MODEL), row(D_MODEL), row(2 * A_WIDTH), row(A_WIDTH), row(D_MODEL)],
        out_shape=[jax.ShapeDtypeStruct((seq, D_MODEL), f32), jax.ShapeDtypeStruct((seq, D_MODEL), bf16),
                   jax.ShapeDtypeStruct((seq, 2 * A_WIDTH), bf16), jax.ShapeDtypeStruct((seq, A_WIDTH), bf16),
                   jax.ShapeDtypeStruct((seq, D_MODEL), bf16)],
        args=(h, h, w_in, w_out, ws, bst, lng, lnb, wp, sc, gm))


def _even_bwd(dh, h, za, pooled, w_in, w_out, ws, bst, lng, lnb, wp, sc, gm, jobs=()):
    seq = h.shape[0]
    tile = min(MIX_TILE, seq)
    n_tiles = seq // tile
    n_groups = len(POOL_WINDOWS)

    def body(dh_ref, dhx_ref, h_ref, za_ref, pool_ref, win_ref, wout_ref, ws_ref, bst_ref, lng_ref, lnb_ref,
             wp_ref, sc_ref, g_ref,
             dhi_ref, dz_ref, dws_ref, dbs_ref, dlng_ref, dlnb_ref, dwp_ref, dsc_ref, dg_ref):
        i = pl.program_id(0)

        @pl.when(i == 0)
        def _():
            for ref in (dws_ref, dbs_ref, dlng_ref, dlnb_ref, dwp_ref, dsc_ref, dg_ref):
                ref[...] = jnp.zeros_like(ref)

        dh = dh_ref[...]
        dmix = _mm_nt(dh.astype(bf16), wout_ref[...])
        dya = dmix[:, :A_WIDTH]
        dyb = dmix[:, A_WIDTH:]
        dybx = _mm_nt(dhx_ref[...].astype(bf16), wout_ref[A_WIDTH:, :])
        dybx = jnp.where(i < n_tiles - 1, dybx, 0.0)

        za = za_ref[...].astype(f32)
        ga, th = _gelu(za)
        u = ga[:, :A_WIDTH]
        lng = lng_ref[...]
        vn, vh, r = _ln_fwd(ga[:, A_WIDTH:], lng, lnb_ref[...])
        vnb = vn.astype(bf16)
        mask = _chunk_mask()
        wsf = [jnp.where(mask, ws_ref[hh], 0.0) for hh in range(N_HEADS)]
        sv = _gmlp_gate(vnb, [w.astype(bf16) for w in wsf], bst_ref[...], tile)
        du = dya * sv
        dsvb = (dya * u).astype(bf16)
        wst = [w.T.astype(bf16) for w in wsf]
        ones = jnp.ones((8, HEAD), bf16)
        dws = [jnp.zeros((HEAD, HEAD), f32) for _ in range(N_HEADS)]
        dbs = [jnp.zeros((8, HEAD), f32) for _ in range(N_HEADS)]
        rows = []
        for n in range(tile // HEAD):
            cols = []
            for hh in range(N_HEADS):
                blk = dsvb[n * HEAD:(n + 1) * HEAD, hh * HEAD:(hh + 1) * HEAD]
                cols.append(_mm(wst[hh], blk))
                dws[hh] = dws[hh] + _mm_nt(blk, vnb[n * HEAD:(n + 1) * HEAD, hh * HEAD:(hh + 1) * HEAD])
                dbs[hh] = dbs[hh] + _mm_nt(ones, blk)
            rows.append(jnp.concatenate(cols, axis=1))
        dvn = jnp.concatenate(rows, axis=0)
        for hh in range(N_HEADS):
            dws_ref[hh] += jnp.where(mask, dws[hh], 0.0)
            dbs_ref[pl.ds(hh, 1), :] += dbs[hh][0:1, :]
        dlng_ref[...] += jnp.sum(dvn * vh, axis=0, keepdims=True)
        dlnb_ref[...] += jnp.sum(dvn, axis=0, keepdims=True)
        dv = _ln_bwd(dvn, vh, r, lng)
        dza = jnp.concatenate([du, dv], axis=1) * _gelu_grad(za, th)

        plb = pool_ref[...]
        sc = sc_ref[...]
        dzb = []
        dsc = []
        for gi, win in enumerate(POOL_WINDOWS):
            cs = slice(gi * HEAD, (gi + 1) * HEAD)
            wpb = wp_ref[gi].astype(bf16)
            dsc.append(jnp.sum(dyb[:, cs] * _mm(plb[:, cs], wpb), axis=0, keepdims=True))
            dpre = (dyb[:, cs] * sc[:, cs]).astype(bf16)
            dprex = (dybx[:, cs] * sc[:, cs]).astype(bf16)
            dwp_ref[gi] += _mm_tn(plb[:, cs], dpre)
            dpl = _mm_nt(dpre, wpb)
            dple = jnp.concatenate([dpl, _mm_nt(dprex, wpb)], axis=0)
            q = dple * _inv_count(i * tile, tile + POOL_HALO, win)
            dzb.append(_window_sum(q, win, _up)[:tile] - dpl)
        dsc_ref[...] += jnp.concatenate(dsc, axis=1)

        dzf = jnp.concatenate([dza] + dzb, axis=1).astype(bf16)
        dz_ref[...] = dzf
        dhn = _mm(dzf, win_ref[...])
        dhr, dg = _rms_bwd(dhn, h_ref[...], g_ref[...])
        dhi_ref[...] = dh + dhr
        dg_ref[...] += dg

    row = lambda cols: pl.BlockSpec((tile, cols), lambda i: (i, 0))
    small = [ws.shape, (N_HEADS, HEAD), lng.shape, lnb.shape, wp.shape, sc.shape, gm.shape]
    return _launch(
        body, name="even_bwd", grid=(n_tiles,), jobs=jobs,
        in_specs=[row(D_MODEL), _next_halo(tile, POOL_HALO, D_MODEL, seq), row(D_MODEL), row(2 * A_WIDTH), row(A_WIDTH),
                  _const(w_in.shape, 1), _const(w_out.shape, 1), _const(ws.shape, 1), _const(bst.shape, 1),
                  _const(lng.shape, 1), _const(lnb.shape, 1), _const(wp.shape, 1), _const(sc.shape, 1), _const(gm.shape, 1)],
        out_specs=[row(D_MODEL), row(3 * A_WIDTH)] + [_const(s, 1) for s in small],
        out_shape=[jax.ShapeDtypeStruct((seq, D_MODEL), f32), jax.ShapeDtypeStruct((seq, 3 * A_WIDTH), bf16)]
                  + [jax.ShapeDtypeStruct(s, f32) for s in small],
        args=(dh, dh, h, za, pooled, w_in, w_out, ws, bst, lng, lnb, wp, sc, gm))


SUBLANES = 8


class _Shifted:
    def __init__(self, x, shift, max_shift):
        self.rolled = [shift(x, b) for b in range(min(SUBLANES, max_shift + 1))]
        self.back = shift is _down

    def rows(self, k, start, count):
        whole = k - k % SUBLANES
        lo = start - whole if self.back else start + whole
        return self.rolled[k % SUBLANES][lo:lo + count]


def _conv_taps(xs, w_ref, n_taps, halo, rows):
    acc = None
    for j in range(n_taps):
        term = w_ref[pl.ds(j, 1), :] * xs.rows(n_taps - 1 - j, halo, rows)
        acc = term if acc is None else acc + term
    return acc


def _odd_fwd(h, w_in, w_out, cw, cb, clg, clb, dw, gm):
    seq = h.shape[0]
    tile = min(MIX_TILE, seq)
    n_tiles = seq // tile
    w = A_WIDTH

    def body(h_ref, hp_ref, win_ref, wout_ref, cw_ref, cb_ref, clg_ref, clb_ref, dw_ref, g_ref,
             ho_ref, hn_ref, z_ref, mix_ref, cv_ref):
        i = pl.program_id(0)
        g = g_ref[...]
        h = h_ref[...]
        hnb = (h * _rms_r(h) * g).astype(bf16)
        hn_ref[...] = hnb
        zb = _mm_nt(hnb, win_ref[...]).astype(bf16)
        z_ref[...] = zb
        hp = hp_ref[...]
        zp = _mm_nt((hp * _rms_r(hp) * g).astype(bf16), win_ref[...]).astype(bf16).astype(f32)
        z = zb.astype(f32)
        ze = jnp.concatenate([jnp.where(i > 0, zp, 0.0), z], axis=0)
        hc = ze[:, :w] * _sigmoid(ze[:, w:2 * w])
        cv = _conv_taps(_Shifted(hc, _down, C_KERNEL - 1), cw_ref, C_KERNEL, CONV_HALO, tile) + cb_ref[...]
        cv_ref[...] = cv
        ln, _, _ = _ln_fwd(cv, clg_ref[...], clb_ref[...])
        yc = ln * _sigmoid(ln)
        p = ze[:, 3 * w:4 * w] * ze[:, 4 * w:]
        yd = z[:, 2 * w:3 * w] * _conv_taps(_Shifted(p, _down, D_KERNEL - 1), dw_ref, D_KERNEL, CONV_HALO, tile)
        mix = jnp.concatenate([yc, yd], axis=1).astype(bf16)
        mix_ref[...] = mix
        ho_ref[...] = h + _mm(mix, wout_ref[...])

    row = lambda cols: pl.BlockSpec((tile, cols), lambda i: (i, 0))
    return pl.pallas_call(
        body, name="odd_fwd", grid=(n_tiles,),
        in_specs=[row(D_MODEL), _prev_halo(tile, CONV_HALO, D_MODEL), _const(w_in.shape, 1), _const(w_out.shape, 1),
                  _const(cw.shape, 1), _const(cb.shape, 1), _const(clg.shape, 1), _const(clb.shape, 1),
                  _const(dw.shape, 1), _const(gm.shape, 1)],
        out_specs=[row(D_MODEL), row(D_MODEL), row(5 * w), row(D_MODEL), row(w)],
        out_shape=[jax.ShapeDtypeStruct((seq, D_MODEL), f32), jax.ShapeDtypeStruct((seq, D_MODEL), bf16),
                   jax.ShapeDtypeStruct((seq, 5 * w), bf16), jax.ShapeDtypeStruct((seq, D_MODEL), bf16),
                   jax.ShapeDtypeStruct((seq, w), f32)],
        compiler_params=_params(1),
    )(h, h, w_in, w_out, cw, cb, clg, clb, dw, gm)


def _odd_bwd(dh, h, z, cv, w_in, w_out, cw, clg, clb, dw, gm, jobs=()):
    seq = h.shape[0]
    tile = min(MIX_TILE, seq)
    n_tiles = seq // tile
    w = A_WIDTH
    halo = CONV_HALO

    def body(dh_ref, dhx_ref, h_ref, z_ref, zp_ref, zx_ref, cv_ref, cvx_ref, win_ref, wout_ref, cw_ref,
             clg_ref, clb_ref, dw_ref, g_ref,
             dhi_ref, dz_ref, dcw_ref, dcb_ref, dclg_ref, dclb_ref, ddw_ref, dg_ref):
        i = pl.program_id(0)

        @pl.when(i == 0)
        def _():
            for ref in (dcw_ref, dcb_ref, dclg_ref, dclb_ref, ddw_ref, dg_ref):
                ref[...] = jnp.zeros_like(ref)

        dh = dh_ref[...]
        dhe = jnp.concatenate([dh, jnp.where(i < n_tiles - 1, dhx_ref[...], 0.0)], axis=0)
        dmix = _mm_nt(dhe.astype(bf16), wout_ref[...])
        ze = jnp.concatenate([jnp.where(i > 0, zp_ref[...].astype(f32), 0.0), z_ref[...].astype(f32),
                              zx_ref[...].astype(f32)], axis=0)

        sg = _sigmoid(ze[:, w:2 * w])
        ca = ze[:, :w]
        hc = ca * sg
        hcs = _Shifted(hc, _down, C_KERNEL - 1)
        cv = jnp.concatenate([cv_ref[...], cvx_ref[...]], axis=0)
        clg = clg_ref[...]
        ln, xh, r = _ln_fwd(cv, clg, clb_ref[...])
        sl = _sigmoid(ln)
        dln = dmix[:, :w] * (sl * (1.0 + ln * (1.0 - sl)))
        dclg_ref[...] += jnp.sum((dln * xh)[:tile], axis=0, keepdims=True)
        dclb_ref[...] += jnp.sum(dln[:tile], axis=0, keepdims=True)
        dcv = _ln_bwd(dln, xh, r, clg)
        dcb_ref[...] += jnp.sum(dcv[:tile], axis=0, keepdims=True)
        dcvs = _Shifted(dcv, _up, C_KERNEL - 1)
        dhc = None
        for j in range(C_KERNEL):
            k = C_KERNEL - 1 - j
            dcw_ref[pl.ds(j, 1), :] += jnp.sum(dcv[:tile] * hcs.rows(k, halo, tile), axis=0, keepdims=True)
            term = cw_ref[pl.ds(j, 1), :] * dcvs.rows(k, 0, tile)
            dhc = term if dhc is None else dhc + term
        sgt = sg[halo:halo + tile]
        cat = ca[halo:halo + tile]
        dca = dhc * sgt
        dcg = dhc * cat * sgt * (1.0 - sgt)

        dcgv = ze[:, 3 * w:4 * w]
        dxin = ze[:, 4 * w:]
        p = dcgv * dxin
        ps = _Shifted(p, _down, D_KERNEL - 1)
        q = _conv_taps(ps, dw_ref, D_KERNEL, halo, tile)
        dyd = dmix[:, w:]
        dq = dyd * ze[halo:, 2 * w:3 * w]
        ddbg = dyd[:tile] * q
        dqs = _Shifted(dq, _up, D_KERNEL - 1)
        dp = None
        for j in range(D_KERNEL):
            k = D_KERNEL - 1 - j
            ddw_ref[pl.ds(j, 1), :] += jnp.sum(dq[:tile] * ps.rows(k, halo, tile), axis=0, keepdims=True)
            term = dw_ref[pl.ds(j, 1), :] * dqs.rows(k, 0, tile)
            dp = term if dp is None else dp + term
        ddcg = dp * dxin[halo:halo + tile]
        ddxin = dp * dcgv[halo:halo + tile]

        dzf = jnp.concatenate([dca, dcg, ddbg, ddcg, ddxin], axis=1).astype(bf16)
        dz_ref[...] = dzf
        dhn = _mm(dzf, win_ref[...])
        dhr, dg = _rms_bwd(dhn, h_ref[...], g_ref[...])
        dhi_ref[...] = dh + dhr
        dg_ref[...] += dg

    row = lambda cols: pl.BlockSpec((tile, cols), lambda i: (i, 0))
    small = [cw.shape, clg.shape, clg.shape, clb.shape, dw.shape, gm.shape]
    return _launch(
        body, name="odd_bwd", grid=(n_tiles,), jobs=jobs,
        in_specs=[row(D_MODEL), _next_halo(tile, halo, D_MODEL, seq), row(D_MODEL), row(5 * w),
                  _prev_halo(tile, halo, 5 * w), _next_halo(tile, halo, 5 * w, seq),
                  row(w), _next_halo(tile, halo, w, seq),
                  _const(w_in.shape, 1), _const(w_out.shape, 1), _const(cw.shape, 1),
                  _const(clg.shape, 1), _const(clb.shape, 1), _const(dw.shape, 1), _const(gm.shape, 1)],
        out_specs=[row(D_MODEL), row(5 * w)] + [_const(s, 1) for s in small],
        out_shape=[jax.ShapeDtypeStruct((seq, D_MODEL), f32), jax.ShapeDtypeStruct((seq, 5 * w), bf16)]
                  + [jax.ShapeDtypeStruct(s, f32) for s in small],
        args=(dh, dh, h, z, z, z, cv, cv, w_in, w_out, cw, clg, clb, dw, gm))


def _ffn_chunks():
    assert sum(FFN_CHUNKS) == D_FF
    start = 0
    for size in FFN_CHUNKS:
        yield slice(start, start + size)
        start += size


def _ffn_fwd(h, wg, wu, wd, gm, jobs=(), head=None):
    seq = h.shape[0]
    tile = min(FFN_TILE, seq)

    def body(h_ref, g_ref, wg_ref, wu_ref, wd_ref, *refs):
        if head is None:
            ho_ref, hn_ref, gate_ref, up_ref = refs
        else:
            t_ref, gf_ref, ho_ref, hn_ref, gate_ref, up_ref, loss_ref, dgf_ref = refs
        h = h_ref[...]
        hnb = (h * _rms_r(h) * g_ref[...]).astype(bf16)
        hn_ref[...] = hnb
        acc = None
        for rows in _ffn_chunks():
            gb = _mm_nt(hnb, wg_ref[rows, :]).astype(bf16)
            ub = _mm_nt(hnb, wu_ref[rows, :]).astype(bf16)
            gate_ref[:, rows] = gb
            up_ref[:, rows] = ub
            gf = gb.astype(f32)
            act = gf * _sigmoid(gf) * ub.astype(f32)
            part = _mm(act.astype(bf16), wd_ref[rows, :])
            acc = part if acc is None else acc + part
        ho = h + acc
        if head is None:
            ho_ref[...] = ho
            return

        @pl.when(pl.program_id(0) == 0)
        def _():
            loss_ref[...] = jnp.zeros_like(loss_ref)
            dgf_ref[...] = jnp.zeros_like(dgf_ref)

        g_final = gf_ref[...]
        err = ho * _rms_r(ho) * g_final - t_ref[...]
        loss_ref[...] += (0.5 / D_MODEL) * jnp.sum(jnp.sum(err * err, axis=1, keepdims=True), axis=0, keepdims=True)
        dho, dg = _rms_bwd(err * (1.0 / D_MODEL), ho, g_final)
        ho_ref[...] = dho
        dgf_ref[...] += dg

    row = pl.BlockSpec((tile, D_MODEL), lambda i: (i, 0))
    wide = pl.BlockSpec((tile, D_FF), lambda i: (i, 0))
    in_specs = [row, _const(gm.shape, 1), _const(wg.shape, 1), _const(wu.shape, 1), _const(wd.shape, 1)]
    out_specs = [row, row, wide, wide]
    out_shape = [jax.ShapeDtypeStruct((seq, D_MODEL), f32), jax.ShapeDtypeStruct((seq, D_MODEL), bf16),
                 jax.ShapeDtypeStruct((seq, D_FF), bf16), jax.ShapeDtypeStruct((seq, D_FF), bf16)]
    args = (h, gm, wg, wu, wd)
    if head is not None:
        target, g_final = head
        in_specs += [row, _const(g_final.shape, 1)]
        out_specs += [_const((1, 1), 1), _const(g_final.shape, 1)]
        out_shape += [jax.ShapeDtypeStruct((1, 1), f32), jax.ShapeDtypeStruct(g_final.shape, f32)]
        args += (target, g_final)
    return _launch(
        body, name="ffn_fwd" if head is None else "ffn_fwd_loss", grid=(seq // tile,), jobs=jobs,
        in_specs=in_specs, out_specs=out_specs, out_shape=out_shape, args=args)


def _ffn_bwd(dh, h, gate, up, wg, wu, wd, gm, jobs=()):
    seq = h.shape[0]
    tile = min(FFN_TILE, seq)
    n_tiles = seq // tile

    def body(dh_ref, h_ref, g_ref, gate_ref, up_ref, wg_ref, wu_ref, wd_ref,
             dhi_ref, dgate_ref, dup_ref, act_ref, dg_ref):
        @pl.when(pl.program_id(0) == 0)
        def _():
            dg_ref[...] = jnp.zeros_like(dg_ref)

        dh = dh_ref[...]
        dhb = dh.astype(bf16)
        acc = None
        for rows in _ffn_chunks():
            dact = _mm_nt(dhb, wd_ref[rows, :])
            gf = gate_ref[:, rows].astype(f32)
            uf = up_ref[:, rows].astype(f32)
            s = _sigmoid(gf)
            silu = gf * s
            act_ref[:, rows] = (silu * uf).astype(bf16)
            dgb = (dact * uf * (s * (1.0 + gf * (1.0 - s)))).astype(bf16)
            dub = (dact * silu).astype(bf16)
            dgate_ref[:, rows] = dgb
            dup_ref[:, rows] = dub
            part = _mm(dgb, wg_ref[rows, :]) + _mm(dub, wu_ref[rows, :])
            acc = part if acc is None else acc + part
        dhr, dg = _rms_bwd(acc, h_ref[...], g_ref[...])
        dhi_ref[...] = dh + dhr
        dg_ref[...] += dg

    row = pl.BlockSpec((tile, D_MODEL), lambda i: (i, 0))
    wide = pl.BlockSpec((tile, D_FF), lambda i: (i, 0))
    return _launch(
        body, name="ffn_bwd", grid=(n_tiles,), jobs=jobs,
        in_specs=[row, row, _const(gm.shape, 1), wide, wide, _const(wg.shape, 1), _const(wu.shape, 1), _const(wd.shape, 1)],
        out_specs=[row, wide, wide, wide, _const(gm.shape, 1)],
        out_shape=[jax.ShapeDtypeStruct((seq, D_MODEL), f32), jax.ShapeDtypeStruct((seq, D_FF), bf16),
                   jax.ShapeDtypeStruct((seq, D_FF), bf16), jax.ShapeDtypeStruct((seq, D_FF), bf16),
                   jax.ShapeDtypeStruct(gm.shape, f32)],
        args=(dh, h, gm, gate, up, wg, wu, wd))


def _weight_grads(pairs, name, jobs=()):
    seq, m = pairs[0][0].shape
    tk = min(DW_TK, seq)
    tm = m if m <= DW_TM else m // 2
    n_k = seq // tk
    n_pairs = len(pairs)

    def body(*refs):
        x_refs = refs[0:2 * n_pairs:2]
        y_refs = refs[1:2 * n_pairs:2]
        o_refs = refs[2 * n_pairs:3 * n_pairs]
        acc_refs = refs[3 * n_pairs:]
        k = pl.program_id(1)
        @pl.when(k == 0)
        def _():
            for acc_ref in acc_refs:
                acc_ref[...] = jnp.zeros_like(acc_ref)

        for x_ref, y_ref, acc_ref in zip(x_refs, y_refs, acc_refs):
            acc_ref[...] += _mm_tn(x_ref[...].astype(bf16), y_ref[...].astype(bf16))

        @pl.when(k == n_k - 1)
        def _():
            for o_ref, acc_ref in zip(o_refs, acc_refs):
                o_ref[...] = acc_ref[...].astype(bf16)

    in_specs = []
    for _ in pairs:
        in_specs += [pl.BlockSpec((tk, tm), lambda j, k: (k, j)), pl.BlockSpec((tk, D_MODEL), lambda j, k: (k, 0))]
    return _launch(
        body, name=name, grid=(m // tm, n_k), jobs=jobs,
        in_specs=in_specs,
        out_specs=[pl.BlockSpec((tm, D_MODEL), lambda j, k: (j, 0))] * n_pairs,
        out_shape=[jax.ShapeDtypeStruct((m, D_MODEL), bf16)] * n_pairs,
        scratch=[pltpu.VMEM((tm, D_MODEL), f32)] * n_pairs,
        args=[a for pair in pairs for a in pair])


def _row_tile(rows, limit=512):
    best = rows
    for t in range(8, min(rows, limit) + 1, 8):
        if rows % t == 0:
            best = t
    return best if rows > limit else rows


def _adam_step(w, g, m, v):
    m2 = ADAM_B1 * m + (1.0 - ADAM_B1) * g
    v2 = ADAM_B2 * v + (1.0 - ADAM_B2) * (g * g)
    m_hat = m2 / (1.0 - ADAM_B1 ** ADAM_STEP)
    v_hat = v2 / (1.0 - ADAM_B2 ** ADAM_STEP)
    return -ADAM_LR * (m_hat / (jnp.sqrt(v_hat) + ADAM_EPS) + ADAM_WD * w), m2, v2


def _adamw(w, g, m, v, name):
    rows, cols = w.shape
    tr = _row_tile(rows)

    def body(w_ref, g_ref, m_ref, v_ref, d_ref, mo_ref, vo_ref):
        d_ref[...], mo_ref[...], vo_ref[...] = _adam_step(w_ref[...], g_ref[...], m_ref[...], v_ref[...])

    spec = pl.BlockSpec((tr, cols), lambda i: (i, 0))
    return pl.pallas_call(
        body, name=name, grid=(rows // tr,),
        in_specs=[spec] * 4, out_specs=[spec] * 3,
        out_shape=[jax.ShapeDtypeStruct((rows, cols), f32)] * 3,
        compiler_params=_params(1),
    )(w, g, m, v)


def _adamw_reduced(w, parts, m, v, name):
    layers, rows, cols = w.shape

    def body(*refs):
        w_ref, m_ref, v_ref = refs[:3]
        part_refs = refs[3:3 + layers]
        g_ref, d_ref, mo_ref, vo_ref = refs[3 + layers:]
        layer = pl.program_id(0)
        for l, p_ref in enumerate(part_refs):
            @pl.when(layer == l)
            def _():
                acc = p_ref[0].astype(f32)
                for k in range(1, N_CHIP):
                    acc = acc + p_ref[k].astype(f32)
                g_ref[0] = acc

        d_ref[0], mo_ref[0], vo_ref[0] = _adam_step(w_ref[0], g_ref[0], m_ref[0], v_ref[0])

    blk = pl.BlockSpec((1, rows, cols), lambda l: (l, 0, 0))
    return pl.pallas_call(
        body, name=name, grid=(layers,),
        in_specs=[blk] * 3 + [pl.BlockSpec(p.shape, lambda l: (0, 0, 0)) for p in parts],
        out_specs=[blk] * 4,
        out_shape=[jax.ShapeDtypeStruct(w.shape, f32)] * 4,
        compiler_params=_params(1),
    )(w, m, v, *parts)


def _sum_leading(x, name):
    n, rows, cols = x.shape
    tr = _row_tile(rows)

    def body(x_ref, o_ref):
        acc = x_ref[0].astype(f32)
        for k in range(1, n):
            acc = acc + x_ref[k].astype(f32)
        o_ref[...] = acc

    return pl.pallas_call(
        body, name=name, grid=(rows // tr,),
        in_specs=[pl.BlockSpec((n, tr, cols), lambda i: (0, i, 0))],
        out_specs=pl.BlockSpec((tr, cols), lambda i: (i, 0)),
        out_shape=jax.ShapeDtypeStruct((rows, cols), f32),
        compiler_params=_params(1),
    )(x)


def _pair_sum(gs, recvs, c_idx, name):
    n = len(gs)

    def body(c_ref, *refs):
        for g_ref, r_ref, o_ref in zip(refs[:n], refs[n:2 * n], refs[2 * n:]):
            o_ref[...] = (g_ref[...].astype(f32) + r_ref[...].astype(f32)).astype(o_ref.dtype)

    own = [pl.BlockSpec((1,) + g.shape[1:], lambda k, c: (2 * k + c[0], 0, 0)) for g in gs]
    by_chip = [pl.BlockSpec((1,) + g.shape[1:], lambda k, c: (k, 0, 0)) for g in gs]
    return list(pl.pallas_call(
        body, name=name,
        grid_spec=pltpu.PrefetchScalarGridSpec(num_scalar_prefetch=1, grid=(N_CHIP,),
                                               in_specs=own + by_chip, out_specs=by_chip),
        out_shape=[jax.ShapeDtypeStruct((N_CHIP,) + g.shape[1:], g.dtype) for g in gs],
        compiler_params=_params(1),
    )(c_idx, *gs, *recvs))


def _pack_rows(w):
    return w.reshape(N_DEV, -1, D_MODEL)


def kernel(x, even_w_in, even_w_out, a_w_s, a_b_s, a_ln_g, a_ln_b, b_w_pool, b_scale, odd_w_in, odd_w_out, c_w_dw, c_b_dw, c_ln_g, c_ln_b, d_w_dw, norm_mix_g, norm_ffn_g, ffn_w_gate, ffn_w_up, ffn_w_down, final_norm_g, loss_target, m_even_w_in, m_even_w_out, m_a_w_s, m_a_b_s, m_a_ln_g, m_a_ln_b, m_b_w_pool, m_b_scale, m_odd_w_in, m_odd_w_out, m_c_w_dw, m_c_b_dw, m_c_ln_g, m_c_ln_b, m_d_w_dw, m_norm_mix_g, m_norm_ffn_g, m_ffn_w_gate, m_ffn_w_up, m_ffn_w_down, m_final_norm_g, v_even_w_in, v_even_w_out, v_a_w_s, v_a_b_s, v_a_ln_g, v_a_ln_b, v_b_w_pool, v_b_scale, v_odd_w_in, v_odd_w_out, v_c_w_dw, v_c_b_dw, v_c_ln_g, v_c_ln_b, v_d_w_dw, v_norm_mix_g, v_norm_ffn_g, v_ffn_w_gate, v_ffn_w_up, v_ffn_w_down, v_final_norm_g):
    weights = dict(even_w_in=even_w_in, even_w_out=even_w_out, a_w_s=a_w_s, a_b_s=a_b_s, a_ln_g=a_ln_g, a_ln_b=a_ln_b,
                   b_w_pool=b_w_pool, b_scale=b_scale, odd_w_in=odd_w_in, odd_w_out=odd_w_out, c_w_dw=c_w_dw,
                   c_b_dw=c_b_dw, c_ln_g=c_ln_g, c_ln_b=c_ln_b, d_w_dw=d_w_dw, norm_mix_g=norm_mix_g,
                   norm_ffn_g=norm_ffn_g, ffn_w_gate=ffn_w_gate, ffn_w_up=ffn_w_up, ffn_w_down=ffn_w_down,
                   final_norm_g=final_norm_g)
    m_in = dict(even_w_in=m_even_w_in, even_w_out=m_even_w_out, a_w_s=m_a_w_s, a_b_s=m_a_b_s, a_ln_g=m_a_ln_g,
                a_ln_b=m_a_ln_b, b_w_pool=m_b_w_pool, b_scale=m_b_scale, odd_w_in=m_odd_w_in, odd_w_out=m_odd_w_out,
                c_w_dw=m_c_w_dw, c_b_dw=m_c_b_dw, c_ln_g=m_c_ln_g, c_ln_b=m_c_ln_b, d_w_dw=m_d_w_dw,
                norm_mix_g=m_norm_mix_g, norm_ffn_g=m_norm_ffn_g, ffn_w_gate=m_ffn_w_gate, ffn_w_up=m_ffn_w_up,
                ffn_w_down=m_ffn_w_down, final_norm_g=m_final_norm_g)
    v_in = dict(even_w_in=v_even_w_in, even_w_out=v_even_w_out, a_w_s=v_a_w_s, a_b_s=v_a_b_s, a_ln_g=v_a_ln_g,
                a_ln_b=v_a_ln_b, b_w_pool=v_b_w_pool, b_scale=v_b_scale, odd_w_in=v_odd_w_in, odd_w_out=v_odd_w_out,
                c_w_dw=v_c_w_dw, c_b_dw=v_c_b_dw, c_ln_g=v_c_ln_g, c_ln_b=v_c_ln_b, d_w_dw=v_d_w_dw,
                norm_mix_g=v_norm_mix_g, norm_ffn_g=v_norm_ffn_g, ffn_w_gate=v_ffn_w_gate, ffn_w_up=v_ffn_w_up,
                ffn_w_down=v_ffn_w_down, final_norm_g=v_final_norm_g)
    names = list(weights)

    group_parts = {
        "even": [even_w_in[0].T, even_w_out[0]],
        "ffn0": [ffn_w_gate[0].T, ffn_w_up[0].T, ffn_w_down[0]],
        "odd": [odd_w_in[0].T, odd_w_out[0]],
        "ffn1": [ffn_w_gate[1].T, ffn_w_up[1].T, ffn_w_down[1]],
    }

    def gather_jobs(*groups):
        return [_all_gather_job(p.astype(bf16)) for k in groups for p in group_parts[k]]

    def whole(gathered):
        return [g.reshape(-1, D_MODEL) for g in gathered]

    conv_names = ["c_w_dw", "c_b_dw", "c_ln_g", "c_ln_b", "d_w_dw"]
    conv_rows = [C_KERNEL, 1, 1, 1, D_KERNEL]
    conv_local = jnp.concatenate([weights[n].reshape(r, -1) for n, r in zip(conv_names, conv_rows)]
                                 + [jnp.zeros((3, c_b_dw.shape[-1]), f32)], axis=0)
    *even_gathered, conv_all = _run_jobs(gather_jobs("even") + [_all_gather_job(conv_local)], "gather_even_conv")
    w_in_e, w_out_e = whole(even_gathered)
    conv_all = conv_all.transpose(1, 0, 2).reshape(conv_local.shape[0], -1)
    conv_offs = [sum(conv_rows[:k]) for k in range(len(conv_rows) + 1)]
    cw, cb, clg, clb, dw = [conv_all[conv_offs[k]:conv_offs[k + 1]] for k in range(len(conv_rows))]

    ws, bst = a_w_s[0], a_b_s[0].T
    lng, lnb, wp, sc = a_ln_g, a_ln_b, b_w_pool[0], b_scale
    gmix = [norm_mix_g[l:l + 1] for l in range(2)]
    gffn = [norm_ffn_g[l:l + 1] for l in range(2)]
    gfin = final_norm_g.reshape(1, D_MODEL)

    h0 = x[0]
    h1, hn_e, za, pooled, mix_e, *ffn0_gathered = _even_fwd(
        h0, w_in_e, w_out_e, ws, bst, lng, lnb, wp, sc, gmix[0], jobs=gather_jobs("ffn0"))
    w_gate0, w_up0, w_down0 = whole(ffn0_gathered)
    h2, hn_f0, gate0, up0, *rest_gathered = _ffn_fwd(h1, w_gate0, w_up0, w_down0, gffn[0],
                                                     jobs=gather_jobs("odd", "ffn1"))
    w_in_o, w_out_o, w_gate1, w_up1, w_down1 = whole(rest_gathered)
    h3, hn_o, z_o, mix_o, cv_o = _odd_fwd(h2, w_in_o, w_out_o, cw, cb, clg, clb, dw, gmix[1])
    dh4, hn_f1, gate1, up1, loss_local, g_final = _ffn_fwd(h3, w_gate1, w_up1, w_down1, gffn[1],
                                                           head=(loss_target[0], gfin))

    c_idx = lax.axis_index("c").astype(jnp.int32).reshape(1)

    def weight_grad(x, y, name, jobs=()):
        g, *job_results = _weight_grads([(x, y)], name, jobs=jobs)
        return [_pack_rows(g)] + job_results

    def siblings(parts):
        return [_sibling_exchange_job(p) for p in parts]

    def chips(pairs):
        return [_chip_exchange_job(p) for p in pairs]

    dh3, dgate1, dup1, act1, g_ffn1 = _ffn_bwd(dh4, h3, gate1, up1, w_gate1, w_up1, w_down1, gffn[1])
    part_ffn1 = (weight_grad(dgate1, hn_f1, "dw_gate1") + weight_grad(dup1, hn_f1, "dw_up1")
                 + weight_grad(act1, dh4, "dw_down1"))
    dh2, dz_o, g_cw, g_cb, g_clg, g_clb, g_dw, g_mix1, *recv_ffn1 = _odd_bwd(
        dh3, h2, z_o, cv_o, w_in_o, w_out_o, cw, clg, clb, dw, gmix[1], jobs=siblings(part_ffn1))
    pair_ffn1 = _pair_sum(part_ffn1, recv_ffn1, c_idx, "pair_sum_ffn1")
    part_odd = weight_grad(dz_o, hn_o, "dw_odd_in") + weight_grad(mix_o, dh3, "dw_odd_out")
    dh1, dgate0, dup0, act0, g_ffn0, *exchanged = _ffn_bwd(
        dh2, h1, gate0, up0, w_gate0, w_up0, w_down0, gffn[0], jobs=chips(pair_ffn1) + siblings(part_odd))
    chips_ffn1, recv_odd = exchanged[:3], exchanged[3:]
    pair_odd = _pair_sum(part_odd, recv_odd, c_idx, "pair_sum_odd")
    dw_gate0, *chips_odd = weight_grad(dgate0, hn_f0, "dw_gate0", jobs=chips(pair_odd))
    part_ffn0 = [dw_gate0] + weight_grad(dup0, hn_f0, "dw_up0") + weight_grad(act0, dh2, "dw_down0")
    part_even_out, *recv_ffn0 = weight_grad(mix_e, dh1, "dw_even_out", jobs=siblings(part_ffn0))
    pair_ffn0 = _pair_sum(part_ffn0, recv_ffn0, c_idx, "pair_sum_ffn0")
    dh0, dz_e, g_ws, g_bs, g_lng, g_lnb, g_wp, g_sc, g_mix0, *exchanged = _even_bwd(
        dh1, h0, za, pooled, w_in_e, w_out_e, ws, bst, lng, lnb, wp, sc, gmix[0],
        jobs=chips(pair_ffn0) + siblings([part_even_out]))
    chips_ffn0, recv_even_out = exchanged[:3], exchanged[3:]
    pair_even_out = _pair_sum([part_even_out], recv_even_out, c_idx, "pair_sum_even_out")

    lanes = HEAD
    small = [("a_w_s", g_ws), ("a_b_s", g_bs), ("a_ln_g", g_lng), ("a_ln_b", g_lnb), ("b_w_pool", g_wp),
             ("b_scale", g_sc), ("norm_mix_g", jnp.concatenate([g_mix0, g_mix1], axis=0)),
             ("norm_ffn_g", jnp.concatenate([g_ffn0, g_ffn1], axis=0)), ("final_norm_g", g_final),
             ("c_w_dw", g_cw), ("c_b_dw", g_cb), ("c_ln_g", g_clg), ("c_ln_b", g_clb), ("d_w_dw", g_dw),
             ("loss", loss_local)]
    small_rows = [-(-g.size // (8 * lanes)) * 8 for _, g in small]
    small_offs = [sum(small_rows[:k]) for k in range(len(small) + 1)]
    pad_rows = -small_offs[-1] % 256
    small_buf = jnp.concatenate(
        [jnp.pad(g.reshape(-1), (0, r * lanes - g.size)).reshape(r, lanes) for (_, g), r in zip(small, small_rows)]
        + [jnp.zeros((pad_rows, lanes), f32)], axis=0)
    part_even_in, small_all, chips_even_out = weight_grad(
        dz_e, hn_e, "dw_even_in", jobs=[_all_gather_job(small_buf)] + chips(pair_even_out))
    small_sum = _sum_leading(small_all, "small_grad_sum")
    recv_even_in = _run_jobs(siblings([part_even_in]), "sibling_exchange_even_in")
    chips_even_in = _run_jobs(chips(_pair_sum([part_even_in], recv_even_in, c_idx, "pair_sum_even_in")),
                              "chip_exchange_even_in")
    grads = {}
    for k, (n, g) in enumerate(small):
        grads[n] = small_sum[small_offs[k]:small_offs[k + 1]].reshape(-1)[:g.size].reshape(g.shape)
    me = 4 * lax.axis_index("x") + 2 * lax.axis_index("y") + lax.axis_index("c")
    shard = c_b_dw.shape[-1]
    for n in conv_names:
        grads[n] = lax.dynamic_slice_in_dim(grads[n], me * shard, shard, axis=1)

    col_sharded = ("even_w_in", "odd_w_in", "ffn_w_gate", "ffn_w_up")

    def rows_view(n, a):
        return jnp.swapaxes(a, -1, -2) if n in col_sharded else a

    loss = grads.pop("loss")[0, 0]
    chip_parts = {"even_w_in": chips_even_in, "even_w_out": [chips_even_out],
                  "odd_w_in": chips_odd[:1], "odd_w_out": chips_odd[1:]}
    for k, n in enumerate(["ffn_w_gate", "ffn_w_up", "ffn_w_down"]):
        chip_parts[n] = [chips_ffn0[k], chips_ffn1[k]]

    delta, new_m, new_v = {}, {}, {}
    for n in names:
        w_rows, m_rows, v_rows = [rows_view(n, a) for a in (weights[n], m_in[n], v_in[n])]
        if n in chip_parts:
            outs = _adamw_reduced(w_rows, chip_parts[n], m_rows, v_rows, "adamw_" + n)
        else:
            view = (-1, w_rows.shape[-1])
            outs = [grads[n], *_adamw(w_rows.reshape(view), grads[n].reshape(view), m_rows.reshape(view),
                                      v_rows.reshape(view), "adamw_" + n)]
        grads[n], delta[n], new_m[n], new_v[n] = [rows_view(n, o.reshape(w_rows.shape)) for o in outs]

    return (loss, dh0[None], *[grads[n] for n in names], *[delta[n] for n in names],
            *[new_m[n] for n in names], *[new_v[n] for n in names])
```

```python
import jax
import jax.numpy as jnp
from jax import lax
from jax.experimental import pallas as pl
from jax.experimental.pallas import tpu as pltpu

f32 = jnp.float32
bf16 = jnp.bfloat16

EPS = 1e-6
D_MODEL = 1024
A_WIDTH = 512
HEAD = 128
N_HEADS = 4
CHUNK = 64
POOL_WINDOWS = (2, 4, 8, 16)
POOL_HALO = 16
C_KERNEL = 31
D_KERNEL = 3
CONV_HALO = 32
D_FF = 2816
N_DEV = 8
N_CHIP = 4

ADAM_LR = 0.001
ADAM_B1 = 0.9
ADAM_B2 = 0.999
ADAM_EPS = 1e-08
ADAM_WD = 0.01
ADAM_STEP = 10

MIX_TILE = 512
ODD_FWD_TILE = 1024
FFN_TILE = 256
FFN_CHUNKS = (1536, 1280)
DW_TK = 2048
DW_TM = 1536
MIDDLE_AT, MIDDLE_OF = 7, 8
VMEM_LIMIT = 56 * 1024 * 1024

MESH = pl.DeviceIdType.MESH
ANY = pl.BlockSpec(memory_space=pl.ANY)


def _params(n_axes):
    return pltpu.CompilerParams(dimension_semantics=("arbitrary",) * n_axes, vmem_limit_bytes=VMEM_LIMIT)


def _mm(a, b):
    return jnp.dot(a, b, preferred_element_type=f32)


def _mm_nt(a, b):
    return lax.dot_general(a, b, (((1,), (1,)), ((), ())), preferred_element_type=f32)


def _mm_tn(a, b):
    return lax.dot_general(a, b, (((0,), (0,)), ((), ())), preferred_element_type=f32)


def _sigmoid(x):
    return 1.0 / (1.0 + jnp.exp(-x))


def _rms_r(h):
    return lax.rsqrt(jnp.mean(h * h, axis=-1, keepdims=True) + EPS)


def _rms_bwd(dy, h, g):
    r = _rms_r(h)
    xh = h * r
    dxh = dy * g
    dh = r * (dxh - xh * jnp.mean(dxh * xh, axis=-1, keepdims=True))
    return dh, jnp.sum(dy * xh, axis=0, keepdims=True)


def _ln_fwd(x, g, b):
    mu = jnp.mean(x, axis=-1, keepdims=True)
    xc = x - mu
    r = lax.rsqrt(jnp.mean(xc * xc, axis=-1, keepdims=True) + EPS)
    xh = xc * r
    return xh * g + b, xh, r


def _ln_bwd(dy, xh, r, g):
    dxh = dy * g
    return r * (dxh - jnp.mean(dxh, axis=-1, keepdims=True) - xh * jnp.mean(dxh * xh, axis=-1, keepdims=True))


_GELU_C = 0.7978845608028654
_GELU_A = 0.044715


def _gelu(x):
    th = jnp.tanh(x * (_GELU_C + (_GELU_C * _GELU_A) * (x * x)))
    half = 0.5 * x
    return half + half * th, th


def _gelu_grad(x, th):
    return 0.5 + 0.5 * th + (1.0 - th * th) * (x * (0.5 * _GELU_C + (1.5 * _GELU_C * _GELU_A) * (x * x)))


def _down(x, k):
    return x if k == 0 else pltpu.roll(x, k, 0)


def _up(x, k):
    return x if k == 0 else pltpu.roll(x, x.shape[0] - k, 0)


def _window_sum(x, win, shift):
    s = x
    step = 1
    while step < win:
        s = s + shift(s, step)
        step *= 2
    return s


def _inv_count(t0, rows, win):
    t = t0 + lax.broadcasted_iota(jnp.int32, (rows, 1), 0)
    return 1.0 / jnp.minimum(t + 1, win).astype(f32)


def _chunk_mask():
    i = lax.broadcasted_iota(jnp.int32, (HEAD, HEAD), 0)
    j = lax.broadcasted_iota(jnp.int32, (HEAD, HEAD), 1)
    return jnp.logical_or(i >= CHUNK, j < CHUNK)


def _const(shape, n_axes):
    zeros = (0,) * len(shape)
    if n_axes == 1:
        return pl.BlockSpec(shape, lambda i: zeros)
    return pl.BlockSpec(shape, lambda i, j: zeros)


def _prev_halo(tile, halo, cols):
    return pl.BlockSpec((halo, cols), lambda i: (jnp.maximum(i * (tile // halo) - 1, 0), 0))


def _next_halo(tile, halo, cols, seq):
    return pl.BlockSpec((halo, cols), lambda i: (jnp.minimum((i + 1) * (tile // halo), seq // halo - 1), 0))


class _Job:
    def __init__(self, inputs, out_shape, sems, hooks):
        self.inputs, self.out_shape, self.sems, self.hooks = inputs, out_shape, sems, hooks


def _position():
    return lax.axis_index("x"), lax.axis_index("y"), lax.axis_index("c")


def _all_gather_job(block):
    rows, cols = block.shape

    def hooks(ins, outs, sems):
        (x_ref,), (out_ref,), (send_sems, recv_sems, local_sem) = ins, outs, sems
        x, y, c = _position()
        me, sibling = (x, y, c), (x, y, 1 - c)
        chips = [(1 - x, y), (x, 1 - y), (1 - x, 1 - y)]

        def slot(px, py, pc):
            return out_ref.at[4 * px + 2 * py + pc]

        def copy(k, block_of, to, src=None):
            return pltpu.make_async_remote_copy(
                src_ref=slot(*block_of) if src is None else src, dst_ref=slot(*block_of),
                send_sem=send_sems.at[k], recv_sem=recv_sems.at[k], device_id=to, device_id_type=MESH)

        mine = pltpu.make_async_copy(x_ref, slot(*me), local_sem)
        first = [copy(0, me, sibling, src=x_ref)]
        first += [copy(1 + j, me, (*chip, c), src=x_ref) for j, chip in enumerate(chips)]
        passed = [copy(4 + j, (*chip, c), sibling) for j, chip in enumerate(chips)]

        def start():
            mine.start()
            for cp in first:
                cp.start()

        def middle():
            for j, chip in enumerate(chips):
                copy(1 + j, (*chip, c), me).wait_recv()
                passed[j].start()

        def finish():
            copy(0, sibling, me).wait_recv()
            for j, chip in enumerate(chips):
                copy(4 + j, (*chip, 1 - c), me).wait_recv()
            for cp in first + passed:
                cp.wait_send()
            mine.wait()

        return start, middle, finish

    return _Job([block], [jax.ShapeDtypeStruct((N_DEV, rows, cols), block.dtype)],
                [pltpu.SemaphoreType.DMA((7,)), pltpu.SemaphoreType.DMA((7,)), pltpu.SemaphoreType.DMA], hooks)


def _sibling_exchange_job(g):
    _, rows, cols = g.shape

    def hooks(ins, outs, sems):
        (g_ref,), (recv_ref,), (send_sems, recv_sems) = ins, outs, sems
        x, y, c = _position()
        copies = [pltpu.make_async_remote_copy(
            src_ref=g_ref.at[2 * k + (1 - c)], dst_ref=recv_ref.at[k], send_sem=send_sems.at[k],
            recv_sem=recv_sems.at[k], device_id=(x, y, 1 - c), device_id_type=MESH) for k in range(N_CHIP)]

        def start():
            for cp in copies:
                cp.start()

        def finish():
            for cp in copies:
                cp.wait()

        return start, lambda: None, finish

    return _Job([g], [jax.ShapeDtypeStruct((N_CHIP, rows, cols), g.dtype)],
                [pltpu.SemaphoreType.DMA((N_CHIP,)), pltpu.SemaphoreType.DMA((N_CHIP,))], hooks)


def _chip_exchange_job(p):
    _, rows, cols = p.shape

    def hooks(ins, outs, sems):
        (p_ref,), (recv_ref,), (send_sems, recv_sems, local_sem) = ins, outs, sems
        x, y, c = _position()
        k_me = 2 * x + y
        mine = pltpu.make_async_copy(p_ref.at[k_me], recv_ref.at[k_me], local_sem)
        copies = [pltpu.make_async_remote_copy(
            src_ref=p_ref.at[2 * px + py], dst_ref=recv_ref.at[k_me], send_sem=send_sems.at[j],
            recv_sem=recv_sems.at[j], device_id=(px, py, c), device_id_type=MESH)
            for j, (px, py) in enumerate([(1 - x, y), (x, 1 - y), (1 - x, 1 - y)])]

        def start():
            mine.start()
            for cp in copies:
                cp.start()

        def finish():
            for cp in copies:
                cp.wait()
            mine.wait()

        return start, lambda: None, finish

    return _Job([p], [jax.ShapeDtypeStruct((N_CHIP, rows, cols), p.dtype)],
                [pltpu.SemaphoreType.DMA((3,)), pltpu.SemaphoreType.DMA((3,)), pltpu.SemaphoreType.DMA], hooks)


def _job_hooks(jobs, ins, outs, sems):
    hooks = []
    for job in jobs:
        n_in, n_out, n_sem = len(job.inputs), len(job.out_shape), len(job.sems)
        hooks.append(job.hooks(ins[:n_in], outs[:n_out], sems[:n_sem]))
        ins, outs, sems = ins[n_in:], outs[n_out:], sems[n_sem:]
    return hooks


def _run_jobs(jobs, name):
    n_in = sum(len(job.inputs) for job in jobs)
    n_out = sum(len(job.out_shape) for job in jobs)

    def body(*refs):
        hooks = _job_hooks(jobs, refs[:n_in], refs[n_in:n_in + n_out], refs[n_in + n_out:])
        for phase in range(3):
            for h in hooks:
                h[phase]()

    return list(pl.pallas_call(
        body, name=name, in_specs=[ANY] * n_in, out_specs=[ANY] * n_out,
        out_shape=[s for job in jobs for s in job.out_shape],
        scratch_shapes=[s for job in jobs for s in job.sems],
    )(*[a for job in jobs for a in job.inputs]))


def _launch(body, *, name, grid, in_specs, out_specs, out_shape, args, scratch=(), jobs=()):
    in_specs, out_specs, out_shape, scratch = list(in_specs), list(out_specs), list(out_shape), list(scratch)
    if not jobs:
        return list(pl.pallas_call(body, name=name, grid=grid, in_specs=in_specs, out_specs=out_specs,
                                   out_shape=out_shape, scratch_shapes=scratch,
                                   compiler_params=_params(len(grid)))(*args))
    n_in, n_out, n_sc = len(in_specs), len(out_specs), len(scratch)
    j_in = [a for job in jobs for a in job.inputs]
    j_out = [s for job in jobs for s in job.out_shape]
    j_sems = [s for job in jobs for s in job.sems]
    n_steps = 1
    for g in grid:
        n_steps *= g

    def wrapped(*refs):
        ins, refs = refs[:n_in], refs[n_in:]
        jins, refs = refs[:len(j_in)], refs[len(j_in):]
        outs, refs = refs[:n_out], refs[n_out:]
        jouts, refs = refs[:len(j_out)], refs[len(j_out):]
        sc, jsems = refs[:n_sc], refs[n_sc:]
        step = pl.program_id(0)
        for axis in range(1, len(grid)):
            step = step * grid[axis] + pl.program_id(axis)
        hooks = _job_hooks(jobs, jins, jouts, jsems)

        @pl.when(step == 0)
        def _():
            for h in hooks:
                h[0]()

        body(*ins, *outs, *sc)

        for k, h in enumerate(hooks):
            pl.when(step == ((k + 1) * MIDDLE_AT * n_steps) // (len(hooks) * MIDDLE_OF))(h[1])

        @pl.when(step == n_steps - 1)
        def _():
            for h in hooks:
                h[2]()

    return list(pl.pallas_call(
        wrapped, name=name, grid=grid, in_specs=in_specs + [ANY] * len(j_in), out_specs=out_specs + [ANY] * len(j_out),
        out_shape=out_shape + j_out, scratch_shapes=scratch + j_sems, compiler_params=_params(len(grid)),
    )(*args, *j_in))


def _gmlp_gate(vnb, wsm, bst, tile):
    rows = []
    for n in range(tile // HEAD):
        cols = []
        for hh in range(N_HEADS):
            blk = vnb[n * HEAD:(n + 1) * HEAD, hh * HEAD:(hh + 1) * HEAD]
            cols.append(_mm(wsm[hh], blk) + bst[:, hh:hh + 1])
        rows.append(jnp.concatenate(cols, axis=1))
    return jnp.concatenate(rows, axis=0)


def _even_fwd(h, w_in, w_out, ws, bst, lng, lnb, wp, sc, gm, jobs=()):
    seq = h.shape[0]
    tile = min(MIX_TILE, seq)
    n_tiles = seq // tile

    def body(h_ref, hp_ref, win_ref, wout_ref, ws_ref, bst_ref, lng_ref, lnb_ref, wp_ref, sc_ref, g_ref,
             ho_ref, hn_ref, za_ref, pool_ref, mix_ref):
        i = pl.program_id(0)
        g = g_ref[...]
        h = h_ref[...]
        hnb = (h * _rms_r(h) * g).astype(bf16)
        hn_ref[...] = hnb
        z = _mm_nt(hnb, win_ref[...])
        zab = z[:, :2 * A_WIDTH].astype(bf16)
        za_ref[...] = zab
        hp = hp_ref[...]
        zbp = _mm_nt((hp * _rms_r(hp) * g).astype(bf16), win_ref[2 * A_WIDTH:, :])
        zbe = jnp.concatenate([jnp.where(i > 0, zbp, 0.0), z[:, 2 * A_WIDTH:]], axis=0)
        pooled = []
        for gi, win in enumerate(POOL_WINDOWS):
            xg = zbe[:, gi * HEAD:(gi + 1) * HEAD]
            s = _window_sum(xg, win, _down)
            pooled.append(s[POOL_HALO:] * _inv_count(i * tile, tile, win) - xg[POOL_HALO:])
        plb = jnp.concatenate(pooled, axis=1).astype(bf16)
        pool_ref[...] = plb

        ga, _ = _gelu(zab.astype(f32))
        vn, _, _ = _ln_fwd(ga[:, A_WIDTH:], lng_ref[...], lnb_ref[...])
        mask = _chunk_mask()
        wsm = [jnp.where(mask, ws_ref[hh], 0.0).astype(bf16) for hh in range(N_HEADS)]
        ya = ga[:, :A_WIDTH] * _gmlp_gate(vn.astype(bf16), wsm, bst_ref[...], tile)
        yb = jnp.concatenate([_mm(plb[:, gi * HEAD:(gi + 1) * HEAD], wp_ref[gi].astype(bf16))
                              for gi in range(len(POOL_WINDOWS))], axis=1) * sc_ref[...]
        mix = jnp.concatenate([ya, yb], axis=1).astype(bf16)
        mix_ref[...] = mix
        ho_ref[...] = h + _mm(mix, wout_ref[...])

    row = lambda cols: pl.BlockSpec((tile, cols), lambda i: (i, 0))
    return _launch(
        body, name="even_fwd", grid=(n_tiles,), jobs=jobs,
        in_specs=[row(D_MODEL), _prev_halo(tile, POOL_HALO, D_MODEL), _const(w_in.shape, 1), _const(w_out.shape, 1),
                  _const(ws.shape, 1), _const(bst.shape, 1), _const(lng.shape, 1), _const(lnb.shape, 1),
                  _const(wp.shape, 1), _const(sc.shape, 1), _const(gm.shape, 1)],
        out_specs=[row(D_MODEL), row(D_MODEL), row(2 * A_WIDTH), row(A_WIDTH), row(D_MODEL)],
        out_shape=[jax.ShapeDtypeStruct((seq, D_MODEL), f32), jax.ShapeDtypeStruct((seq, D_MODEL), bf16),
                   jax.ShapeDtypeStruct((seq, 2 * A_WIDTH), bf16), jax.ShapeDtypeStruct((seq, A_WIDTH), bf16),
                   jax.ShapeDtypeStruct((seq, D_MODEL), bf16)],
        args=(h, h, w_in, w_out, ws, bst, lng, lnb, wp, sc, gm))


def _even_bwd(dh, h, za, pooled, w_in, w_out, ws, bst, lng, lnb, wp, sc, gm, jobs=()):
    seq = h.shape[0]
    tile = min(MIX_TILE, seq)
    n_tiles = seq // tile
    n_groups = len(POOL_WINDOWS)

    def body(dh_ref, dhx_ref, h_ref, za_ref, pool_ref, win_ref, wout_ref, ws_ref, bst_ref, lng_ref, lnb_ref,
             wp_ref, sc_ref, g_ref,
             dhi_ref, dz_ref, dws_ref, dbs_ref, dlng_ref, dlnb_ref, dwp_ref, dsc_ref, dg_ref):
        i = pl.program_id(0)

        @pl.when(i == 0)
        def _():
            for ref in (dws_ref, dbs_ref, dlng_ref, dlnb_ref, dwp_ref, dsc_ref, dg_ref):
                ref[...] = jnp.zeros_like(ref)

        dh = dh_ref[...]
        dmix = _mm_nt(dh.astype(bf16), wout_ref[...])
        dya = dmix[:, :A_WIDTH]
        dyb = dmix[:, A_WIDTH:]
        dybx = _mm_nt(dhx_ref[...].astype(bf16), wout_ref[A_WIDTH:, :])
        dybx = jnp.where(i < n_tiles - 1, dybx, 0.0)

        za = za_ref[...].astype(f32)
        ga, th = _gelu(za)
        u = ga[:, :A_WIDTH]
        lng = lng_ref[...]
        vn, vh, r = _ln_fwd(ga[:, A_WIDTH:], lng, lnb_ref[...])
        vnb = vn.astype(bf16)
        mask = _chunk_mask()
        wsf = [jnp.where(mask, ws_ref[hh], 0.0) for hh in range(N_HEADS)]
        sv = _gmlp_gate(vnb, [w.astype(bf16) for w in wsf], bst_ref[...], tile)
        du = dya * sv
        dsvb = (dya * u).astype(bf16)
        wst = [w.T.astype(bf16) for w in wsf]
        ones = jnp.ones((8, HEAD), bf16)
        dws = [jnp.zeros((HEAD, HEAD), f32) for _ in range(N_HEADS)]
        dbs = [jnp.zeros((8, HEAD), f32) for _ in range(N_HEADS)]
        rows = []
        for n in range(tile // HEAD):
            cols = []
            for hh in range(N_HEADS):
                blk = dsvb[n * HEAD:(n + 1) * HEAD, hh * HEAD:(hh + 1) * HEAD]
                cols.append(_mm(wst[hh], blk))
                dws[hh] = dws[hh] + _mm_nt(blk, vnb[n * HEAD:(n + 1) * HEAD, hh * HEAD:(hh + 1) * HEAD])
                dbs[hh] = dbs[hh] + _mm_nt(ones, blk)
            rows.append(jnp.concatenate(cols, axis=1))
        dvn = jnp.concatenate(rows, axis=0)
        for hh in range(N_HEADS):
            dws_ref[hh] += jnp.where(mask, dws[hh], 0.0)
            dbs_ref[pl.ds(hh, 1), :] += dbs[hh][0:1, :]
        dlng_ref[...] += jnp.sum(dvn * vh, axis=0, keepdims=True)
        dlnb_ref[...] += jnp.sum(dvn, axis=0, keepdims=True)
        dv = _ln_bwd(dvn, vh, r, lng)
        dza = jnp.concatenate([du, dv], axis=1) * _gelu_grad(za, th)

        plb = pool_ref[...]
        sc = sc_ref[...]
        dzb = []
        dsc = []
        for gi, win in enumerate(POOL_WINDOWS):
            cs = slice(gi * HEAD, (gi + 1) * HEAD)
            wpb = wp_ref[gi].astype(bf16)
            dsc.append(jnp.sum(dyb[:, cs] * _mm(plb[:, cs], wpb), axis=0, keepdims=True))
            dpre = (dyb[:, cs] * sc[:, cs]).astype(bf16)
            dprex = (dybx[:, cs] * sc[:, cs]).astype(bf16)
            dwp_ref[gi] += _mm_tn(plb[:, cs], dpre)
            dpl = _mm_nt(dpre, wpb)
            dple = jnp.concatenate([dpl, _mm_nt(dprex, wpb)], axis=0)
            q = dple * _inv_count(i * tile, tile + POOL_HALO, win)
            dzb.append(_window_sum(q, win, _up)[:tile] - dpl)
        dsc_ref[...] += jnp.concatenate(dsc, axis=1)

        dzf = jnp.concatenate([dza] + dzb, axis=1).astype(bf16)
        dz_ref[...] = dzf
        dhn = _mm(dzf, win_ref[...])
        dhr, dg = _rms_bwd(dhn, h_ref[...], g_ref[...])
        dhi_ref[...] = dh + dhr
        dg_ref[...] += dg

    row = lambda cols: pl.BlockSpec((tile, cols), lambda i: (i, 0))
    small = [ws.shape, (N_HEADS, HEAD), lng.shape, lnb.shape, wp.shape, sc.shape, gm.shape]
    return _launch(
        body, name="even_bwd", grid=(n_tiles,), jobs=jobs,
        in_specs=[row(D_MODEL), _next_halo(tile, POOL_HALO, D_MODEL, seq), row(D_MODEL), row(2 * A_WIDTH), row(A_WIDTH),
                  _const(w_in.shape, 1), _const(w_out.shape, 1), _const(ws.shape, 1), _const(bst.shape, 1),
                  _const(lng.shape, 1), _const(lnb.shape, 1), _const(wp.shape, 1), _const(sc.shape, 1), _const(gm.shape, 1)],
        out_specs=[row(D_MODEL), row(3 * A_WIDTH)] + [_const(s, 1) for s in small],
        out_shape=[jax.ShapeDtypeStruct((seq, D_MODEL), f32), jax.ShapeDtypeStruct((seq, 3 * A_WIDTH), bf16)]
                  + [jax.ShapeDtypeStruct(s, f32) for s in small],
        args=(dh, dh, h, za, pooled, w_in, w_out, ws, bst, lng, lnb, wp, sc, gm))


SUBLANES = 8


class _Shifted:
    def __init__(self, x, shift, max_shift):
        self.rolled = [shift(x, b) for b in range(min(SUBLANES, max_shift + 1))]
        self.back = shift is _down

    def rows(self, k, start, count):
        whole = k - k % SUBLANES
        lo = start - whole if self.back else start + whole
        return self.rolled[k % SUBLANES][lo:lo + count]


def _conv_taps(xs, w_ref, n_taps, halo, rows):
    acc = None
    for j in range(n_taps):
        term = w_ref[pl.ds(j, 1), :] * xs.rows(n_taps - 1 - j, halo, rows)
        acc = term if acc is None else acc + term
    return acc


def _odd_fwd(h, w_in, w_out, cw, cb, clg, clb, dw, gm):
    seq = h.shape[0]
    tile = min(ODD_FWD_TILE, seq)
    n_tiles = seq // tile
    w = A_WIDTH

    def body(h_ref, hp_ref, win_ref, wout_ref, cw_ref, cb_ref, clg_ref, clb_ref, dw_ref, g_ref,
             ho_ref, hn_ref, z_ref, mix_ref, cv_ref):
        i = pl.program_id(0)
        g = g_ref[...]
        h = h_ref[...]
        hnb = (h * _rms_r(h) * g).astype(bf16)
        hn_ref[...] = hnb
        zb = _mm_nt(hnb, win_ref[...]).astype(bf16)
        z_ref[...] = zb
        hp = hp_ref[...]
        zp = _mm_nt((hp * _rms_r(hp) * g).astype(bf16), win_ref[...]).astype(bf16).astype(f32)
        z = zb.astype(f32)
        ze = jnp.concatenate([jnp.where(i > 0, zp, 0.0), z], axis=0)
        hc = ze[:, :w] * _sigmoid(ze[:, w:2 * w])
        cv = _conv_taps(_Shifted(hc, _down, C_KERNEL - 1), cw_ref, C_KERNEL, CONV_HALO, tile) + cb_ref[...]
        cv_ref[...] = cv
        ln, _, _ = _ln_fwd(cv, clg_ref[...], clb_ref[...])
        yc = ln * _sigmoid(ln)
        p = ze[:, 3 * w:4 * w] * ze[:, 4 * w:]
        yd = z[:, 2 * w:3 * w] * _conv_taps(_Shifted(p, _down, D_KERNEL - 1), dw_ref, D_KERNEL, CONV_HALO, tile)
        mix = jnp.concatenate([yc, yd], axis=1).astype(bf16)
        mix_ref[...] = mix
        ho_ref[...] = h + _mm(mix, wout_ref[...])

    row = lambda cols: pl.BlockSpec((tile, cols), lambda i: (i, 0))
    return pl.pallas_call(
        body, name="odd_fwd", grid=(n_tiles,),
        in_specs=[row(D_MODEL), _prev_halo(tile, CONV_HALO, D_MODEL), _const(w_in.shape, 1), _const(w_out.shape, 1),
                  _const(cw.shape, 1), _const(cb.shape, 1), _const(clg.shape, 1), _const(clb.shape, 1),
                  _const(dw.shape, 1), _const(gm.shape, 1)],
        out_specs=[row(D_MODEL), row(D_MODEL), row(5 * w), row(D_MODEL), row(w)],
        out_shape=[jax.ShapeDtypeStruct((seq, D_MODEL), f32), jax.ShapeDtypeStruct((seq, D_MODEL), bf16),
                   jax.ShapeDtypeStruct((seq, 5 * w), bf16), jax.ShapeDtypeStruct((seq, D_MODEL), bf16),
                   jax.ShapeDtypeStruct((seq, w), f32)],
        compiler_params=_params(1),
    )(h, h, w_in, w_out, cw, cb, clg, clb, dw, gm)


def _odd_bwd(dh, h, z, cv, w_in, w_out, cw, clg, clb, dw, gm, jobs=()):
    seq = h.shape[0]
    tile = min(MIX_TILE, seq)
    n_tiles = seq // tile
    w = A_WIDTH
    halo = CONV_HALO

    def body(dh_ref, dhx_ref, h_ref, z_ref, zp_ref, zx_ref, cv_ref, cvx_ref, win_ref, wout_ref, cw_ref,
             clg_ref, clb_ref, dw_ref, g_ref,
             dhi_ref, dz_ref, dcw_ref, dcb_ref, dclg_ref, dclb_ref, ddw_ref, dg_ref):
        i = pl.program_id(0)

        @pl.when(i == 0)
        def _():
            for ref in (dcw_ref, dcb_ref, dclg_ref, dclb_ref, ddw_ref, dg_ref):
                ref[...] = jnp.zeros_like(ref)

        dh = dh_ref[...]
        dhe = jnp.concatenate([dh, jnp.where(i < n_tiles - 1, dhx_ref[...], 0.0)], axis=0)
        dmix = _mm_nt(dhe.astype(bf16), wout_ref[...])
        ze = jnp.concatenate([jnp.where(i > 0, zp_ref[...].astype(f32), 0.0), z_ref[...].astype(f32),
                              zx_ref[...].astype(f32)], axis=0)

        sg = _sigmoid(ze[:, w:2 * w])
        ca = ze[:, :w]
        hc = ca * sg
        hcs = _Shifted(hc, _down, C_KERNEL - 1)
        cv = jnp.concatenate([cv_ref[...], cvx_ref[...]], axis=0)
        clg = clg_ref[...]
        ln, xh, r = _ln_fwd(cv, clg, clb_ref[...])
        sl = _sigmoid(ln)
        dln = dmix[:, :w] * (sl * (1.0 + ln * (1.0 - sl)))
        dclg_ref[...] += jnp.sum((dln * xh)[:tile], axis=0, keepdims=True)
        dclb_ref[...] += jnp.sum(dln[:tile], axis=0, keepdims=True)
        dcv = _ln_bwd(dln, xh, r, clg)
        dcb_ref[...] += jnp.sum(dcv[:tile], axis=0, keepdims=True)
        dcvs = _Shifted(dcv, _up, C_KERNEL - 1)
        dhc = None
        for j in range(C_KERNEL):
            k = C_KERNEL - 1 - j
            dcw_ref[pl.ds(j, 1), :] += jnp.sum(dcv[:tile] * hcs.rows(k, halo, tile), axis=0, keepdims=True)
            term = cw_ref[pl.ds(j, 1), :] * dcvs.rows(k, 0, tile)
            dhc = term if dhc is None else dhc + term
        sgt = sg[halo:halo + tile]
        cat = ca[halo:halo + tile]
        dca = dhc * sgt
        dcg = dhc * cat * sgt * (1.0 - sgt)

        dcgv = ze[:, 3 * w:4 * w]
        dxin = ze[:, 4 * w:]
        p = dcgv * dxin
        ps = _Shifted(p, _down, D_KERNEL - 1)
        q = _conv_taps(ps, dw_ref, D_KERNEL, halo, tile)
        dyd = dmix[:, w:]
        dq = dyd * ze[halo:, 2 * w:3 * w]
        ddbg = dyd[:tile] * q
        dqs = _Shifted(dq, _up, D_KERNEL - 1)
        dp = None
        for j in range(D_KERNEL):
            k = D_KERNEL - 1 - j
            ddw_ref[pl.ds(j, 1), :] += jnp.sum(dq[:tile] * ps.rows(k, halo, tile), axis=0, keepdims=True)
            term = dw_ref[pl.ds(j, 1), :] * dqs.rows(k, 0, tile)
            dp = term if dp is None else dp + term
        ddcg = dp * dxin[halo:halo + tile]
        ddxin = dp * dcgv[halo:halo + tile]

        dzf = jnp.concatenate([dca, dcg, ddbg, ddcg, ddxin], axis=1).astype(bf16)
        dz_ref[...] = dzf
        dhn = _mm(dzf, win_ref[...])
        dhr, dg = _rms_bwd(dhn, h_ref[...], g_ref[...])
        dhi_ref[...] = dh + dhr
        dg_ref[...] += dg

    row = lambda cols: pl.BlockSpec((tile, cols), lambda i: (i, 0))
    small = [cw.shape, clg.shape, clg.shape, clb.shape, dw.shape, gm.shape]
    return _launch(
        body, name="odd_bwd", grid=(n_tiles,), jobs=jobs,
        in_specs=[row(D_MODEL), _next_halo(tile, halo, D_MODEL, seq), row(D_MODEL), row(5 * w),
                  _prev_halo(tile, halo, 5 * w), _next_halo(tile, halo, 5 * w, seq),
                  row(w), _next_halo(tile, halo, w, seq),
                  _const(w_in.shape, 1), _const(w_out.shape, 1), _const(cw.shape, 1),
                  _const(clg.shape, 1), _const(clb.shape, 1), _const(dw.shape, 1), _const(gm.shape, 1)],
        out_specs=[row(D_MODEL), row(5 * w)] + [_const(s, 1) for s in small],
        out_shape=[jax.ShapeDtypeStruct((seq, D_MODEL), f32), jax.ShapeDtypeStruct((seq, 5 * w), bf16)]
                  + [jax.ShapeDtypeStruct(s, f32) for s in small],
        args=(dh, dh, h, z, z, z, cv, cv, w_in, w_out, cw, clg, clb, dw, gm))


def _ffn_chunks():
    assert sum(FFN_CHUNKS) == D_FF
    start = 0
    for size in FFN_CHUNKS:
        yield slice(start, start + size)
        start += size


def _ffn_fwd(h, wg, wu, wd, gm, jobs=(), head=None):
    seq = h.shape[0]
    tile = min(FFN_TILE, seq)

    def body(h_ref, g_ref, wg_ref, wu_ref, wd_ref, *refs):
        if head is None:
            ho_ref, hn_ref, gate_ref, up_ref = refs
        else:
            t_ref, gf_ref, ho_ref, hn_ref, gate_ref, up_ref, loss_ref, dgf_ref = refs
        h = h_ref[...]
        hnb = (h * _rms_r(h) * g_ref[...]).astype(bf16)
        hn_ref[...] = hnb
        acc = None
        for rows in _ffn_chunks():
            gb = _mm_nt(hnb, wg_ref[rows, :]).astype(bf16)
            ub = _mm_nt(hnb, wu_ref[rows, :]).astype(bf16)
            gate_ref[:, rows] = gb
            up_ref[:, rows] = ub
            gf = gb.astype(f32)
            act = gf * _sigmoid(gf) * ub.astype(f32)
            part = _mm(act.astype(bf16), wd_ref[rows, :])
            acc = part if acc is None else acc + part
        ho = h + acc
        if head is None:
            ho_ref[...] = ho
            return

        @pl.when(pl.program_id(0) == 0)
        def _():
            loss_ref[...] = jnp.zeros_like(loss_ref)
            dgf_ref[...] = jnp.zeros_like(dgf_ref)

        g_final = gf_ref[...]
        err = ho * _rms_r(ho) * g_final - t_ref[...]
        loss_ref[...] += (0.5 / D_MODEL) * jnp.sum(jnp.sum(err * err, axis=1, keepdims=True), axis=0, keepdims=True)
        dho, dg = _rms_bwd(err * (1.0 / D_MODEL), ho, g_final)
        ho_ref[...] = dho
        dgf_ref[...] += dg

    row = pl.BlockSpec((tile, D_MODEL), lambda i: (i, 0))
    wide = pl.BlockSpec((tile, D_FF), lambda i: (i, 0))
    in_specs = [row, _const(gm.shape, 1), _const(wg.shape, 1), _const(wu.shape, 1), _const(wd.shape, 1)]
    out_specs = [row, row, wide, wide]
    out_shape = [jax.ShapeDtypeStruct((seq, D_MODEL), f32), jax.ShapeDtypeStruct((seq, D_MODEL), bf16),
                 jax.ShapeDtypeStruct((seq, D_FF), bf16), jax.ShapeDtypeStruct((seq, D_FF), bf16)]
    args = (h, gm, wg, wu, wd)
    if head is not None:
        target, g_final = head
        in_specs += [row, _const(g_final.shape, 1)]
        out_specs += [_const((1, 1), 1), _const(g_final.shape, 1)]
        out_shape += [jax.ShapeDtypeStruct((1, 1), f32), jax.ShapeDtypeStruct(g_final.shape, f32)]
        args += (target, g_final)
    return _launch(
        body, name="ffn_fwd" if head is None else "ffn_fwd_loss", grid=(seq // tile,), jobs=jobs,
        in_specs=in_specs, out_specs=out_specs, out_shape=out_shape, args=args)


def _ffn_bwd(dh, h, gate, up, wg, wu, wd, gm, jobs=()):
    seq = h.shape[0]
    tile = min(FFN_TILE, seq)
    n_tiles = seq // tile

    def body(dh_ref, h_ref, g_ref, gate_ref, up_ref, wg_ref, wu_ref, wd_ref,
             dhi_ref, dgate_ref, dup_ref, act_ref, dg_ref):
        @pl.when(pl.program_id(0) == 0)
        def _():
            dg_ref[...] = jnp.zeros_like(dg_ref)

        dh = dh_ref[...]
        dhb = dh.astype(bf16)
        acc = None
        for rows in _ffn_chunks():
            dact = _mm_nt(dhb, wd_ref[rows, :])
            gf = gate_ref[:, rows].astype(f32)
            uf = up_ref[:, rows].astype(f32)
            s = _sigmoid(gf)
            silu = gf * s
            act_ref[:, rows] = (silu * uf).astype(bf16)
            dgb = (dact * uf * (s * (1.0 + gf * (1.0 - s)))).astype(bf16)
            dub = (dact * silu).astype(bf16)
            dgate_ref[:, rows] = dgb
            dup_ref[:, rows] = dub
            part = _mm(dgb, wg_ref[rows, :]) + _mm(dub, wu_ref[rows, :])
            acc = part if acc is None else acc + part
        dhr, dg = _rms_bwd(acc, h_ref[...], g_ref[...])
        dhi_ref[...] = dh + dhr
        dg_ref[...] += dg

    row = pl.BlockSpec((tile, D_MODEL), lambda i: (i, 0))
    wide = pl.BlockSpec((tile, D_FF), lambda i: (i, 0))
    return _launch(
        body, name="ffn_bwd", grid=(n_tiles,), jobs=jobs,
        in_specs=[row, row, _const(gm.shape, 1), wide, wide, _const(wg.shape, 1), _const(wu.shape, 1), _const(wd.shape, 1)],
        out_specs=[row, wide, wide, wide, _const(gm.shape, 1)],
        out_shape=[jax.ShapeDtypeStruct((seq, D_MODEL), f32), jax.ShapeDtypeStruct((seq, D_FF), bf16),
                   jax.ShapeDtypeStruct((seq, D_FF), bf16), jax.ShapeDtypeStruct((seq, D_FF), bf16),
                   jax.ShapeDtypeStruct(gm.shape, f32)],
        args=(dh, h, gm, gate, up, wg, wu, wd))


def _weight_grads(pairs, name, jobs=()):
    seq, m = pairs[0][0].shape
    tk = min(DW_TK, seq)
    tm = m if m <= DW_TM else m // 2
    n_k = seq // tk
    n_pairs = len(pairs)

    def body(*refs):
        x_refs = refs[0:2 * n_pairs:2]
        y_refs = refs[1:2 * n_pairs:2]
        o_refs = refs[2 * n_pairs:3 * n_pairs]
        acc_refs = refs[3 * n_pairs:]
        k = pl.program_id(1)
        @pl.when(k == 0)
        def _():
            for acc_ref in acc_refs:
                acc_ref[...] = jnp.zeros_like(acc_ref)

        for x_ref, y_ref, acc_ref in zip(x_refs, y_refs, acc_refs):
            acc_ref[...] += _mm_tn(x_ref[...].astype(bf16), y_ref[...].astype(bf16))

        @pl.when(k == n_k - 1)
        def _():
            for o_ref, acc_ref in zip(o_refs, acc_refs):
                o_ref[...] = acc_ref[...].astype(bf16)

    in_specs = []
    for _ in pairs:
        in_specs += [pl.BlockSpec((tk, tm), lambda j, k: (k, j)), pl.BlockSpec((tk, D_MODEL), lambda j, k: (k, 0))]
    return _launch(
        body, name=name, grid=(m // tm, n_k), jobs=jobs,
        in_specs=in_specs,
        out_specs=[pl.BlockSpec((tm, D_MODEL), lambda j, k: (j, 0))] * n_pairs,
        out_shape=[jax.ShapeDtypeStruct((m, D_MODEL), bf16)] * n_pairs,
        scratch=[pltpu.VMEM((tm, D_MODEL), f32)] * n_pairs,
        args=[a for pair in pairs for a in pair])


def _row_tile(rows, limit=512):
    best = rows
    for t in range(8, min(rows, limit) + 1, 8):
        if rows % t == 0:
            best = t
    return best if rows > limit else rows


def _adam_step(w, g, m, v):
    m2 = ADAM_B1 * m + (1.0 - ADAM_B1) * g
    v2 = ADAM_B2 * v + (1.0 - ADAM_B2) * (g * g)
    m_hat = m2 / (1.0 - ADAM_B1 ** ADAM_STEP)
    v_hat = v2 / (1.0 - ADAM_B2 ** ADAM_STEP)
    return -ADAM_LR * (m_hat / (jnp.sqrt(v_hat) + ADAM_EPS) + ADAM_WD * w), m2, v2


def _adamw(w, g, m, v, name):
    rows, cols = w.shape
    tr = _row_tile(rows)

    def body(w_ref, g_ref, m_ref, v_ref, d_ref, mo_ref, vo_ref):
        d_ref[...], mo_ref[...], vo_ref[...] = _adam_step(w_ref[...], g_ref[...], m_ref[...], v_ref[...])

    spec = pl.BlockSpec((tr, cols), lambda i: (i, 0))
    return pl.pallas_call(
        body, name=name, grid=(rows // tr,),
        in_specs=[spec] * 4, out_specs=[spec] * 3,
        out_shape=[jax.ShapeDtypeStruct((rows, cols), f32)] * 3,
        compiler_params=_params(1),
    )(w, g, m, v)


def _adamw_reduced(w, parts, m, v, name):
    layers, rows, cols = w.shape

    def body(*refs):
        w_ref, m_ref, v_ref = refs[:3]
        part_refs = refs[3:3 + layers]
        g_ref, d_ref, mo_ref, vo_ref = refs[3 + layers:]
        layer = pl.program_id(0)
        for l, p_ref in enumerate(part_refs):
            @pl.when(layer == l)
            def _():
                acc = p_ref[0].astype(f32)
                for k in range(1, N_CHIP):
                    acc = acc + p_ref[k].astype(f32)
                g_ref[0] = acc

        d_ref[0], mo_ref[0], vo_ref[0] = _adam_step(w_ref[0], g_ref[0], m_ref[0], v_ref[0])

    blk = pl.BlockSpec((1, rows, cols), lambda l: (l, 0, 0))
    return pl.pallas_call(
        body, name=name, grid=(layers,),
        in_specs=[blk] * 3 + [pl.BlockSpec(p.shape, lambda l: (0, 0, 0)) for p in parts],
        out_specs=[blk] * 4,
        out_shape=[jax.ShapeDtypeStruct(w.shape, f32)] * 4,
        compiler_params=_params(1),
    )(w, m, v, *parts)


def _sum_leading(x, name):
    n, rows, cols = x.shape
    tr = _row_tile(rows)

    def body(x_ref, o_ref):
        acc = x_ref[0].astype(f32)
        for k in range(1, n):
            acc = acc + x_ref[k].astype(f32)
        o_ref[...] = acc

    return pl.pallas_call(
        body, name=name, grid=(rows // tr,),
        in_specs=[pl.BlockSpec((n, tr, cols), lambda i: (0, i, 0))],
        out_specs=pl.BlockSpec((tr, cols), lambda i: (i, 0)),
        out_shape=jax.ShapeDtypeStruct((rows, cols), f32),
        compiler_params=_params(1),
    )(x)


def _pair_sum(gs, recvs, c_idx, name):
    n = len(gs)

    def body(c_ref, *refs):
        for g_ref, r_ref, o_ref in zip(refs[:n], refs[n:2 * n], refs[2 * n:]):
            o_ref[...] = (g_ref[...].astype(f32) + r_ref[...].astype(f32)).astype(o_ref.dtype)

    own = [pl.BlockSpec((1,) + g.shape[1:], lambda k, c: (2 * k + c[0], 0, 0)) for g in gs]
    by_chip = [pl.BlockSpec((1,) + g.shape[1:], lambda k, c: (k, 0, 0)) for g in gs]
    return list(pl.pallas_call(
        body, name=name,
        grid_spec=pltpu.PrefetchScalarGridSpec(num_scalar_prefetch=1, grid=(N_CHIP,),
                                               in_specs=own + by_chip, out_specs=by_chip),
        out_shape=[jax.ShapeDtypeStruct((N_CHIP,) + g.shape[1:], g.dtype) for g in gs],
        compiler_params=_params(1),
    )(c_idx, *gs, *recvs))


def _pack_rows(w):
    return w.reshape(N_DEV, -1, D_MODEL)


def kernel(x, even_w_in, even_w_out, a_w_s, a_b_s, a_ln_g, a_ln_b, b_w_pool, b_scale, odd_w_in, odd_w_out, c_w_dw, c_b_dw, c_ln_g, c_ln_b, d_w_dw, norm_mix_g, norm_ffn_g, ffn_w_gate, ffn_w_up, ffn_w_down, final_norm_g, loss_target, m_even_w_in, m_even_w_out, m_a_w_s, m_a_b_s, m_a_ln_g, m_a_ln_b, m_b_w_pool, m_b_scale, m_odd_w_in, m_odd_w_out, m_c_w_dw, m_c_b_dw, m_c_ln_g, m_c_ln_b, m_d_w_dw, m_norm_mix_g, m_norm_ffn_g, m_ffn_w_gate, m_ffn_w_up, m_ffn_w_down, m_final_norm_g, v_even_w_in, v_even_w_out, v_a_w_s, v_a_b_s, v_a_ln_g, v_a_ln_b, v_b_w_pool, v_b_scale, v_odd_w_in, v_odd_w_out, v_c_w_dw, v_c_b_dw, v_c_ln_g, v_c_ln_b, v_d_w_dw, v_norm_mix_g, v_norm_ffn_g, v_ffn_w_gate, v_ffn_w_up, v_ffn_w_down, v_final_norm_g):
    weights = dict(even_w_in=even_w_in, even_w_out=even_w_out, a_w_s=a_w_s, a_b_s=a_b_s, a_ln_g=a_ln_g, a_ln_b=a_ln_b,
                   b_w_pool=b_w_pool, b_scale=b_scale, odd_w_in=odd_w_in, odd_w_out=odd_w_out, c_w_dw=c_w_dw,
                   c_b_dw=c_b_dw, c_ln_g=c_ln_g, c_ln_b=c_ln_b, d_w_dw=d_w_dw, norm_mix_g=norm_mix_g,
                   norm_ffn_g=norm_ffn_g, ffn_w_gate=ffn_w_gate, ffn_w_up=ffn_w_up, ffn_w_down=ffn_w_down,
                   final_norm_g=final_norm_g)
    m_in = dict(even_w_in=m_even_w_in, even_w_out=m_even_w_out, a_w_s=m_a_w_s, a_b_s=m_a_b_s, a_ln_g=m_a_ln_g,
                a_ln_b=m_a_ln_b, b_w_pool=m_b_w_pool, b_scale=m_b_scale, odd_w_in=m_odd_w_in, odd_w_out=m_odd_w_out,
                c_w_dw=m_c_w_dw, c_b_dw=m_c_b_dw, c_ln_g=m_c_ln_g, c_ln_b=m_c_ln_b, d_w_dw=m_d_w_dw,
                norm_mix_g=m_norm_mix_g, norm_ffn_g=m_norm_ffn_g, ffn_w_gate=m_ffn_w_gate, ffn_w_up=m_ffn_w_up,
                ffn_w_down=m_ffn_w_down, final_norm_g=m_final_norm_g)
    v_in = dict(even_w_in=v_even_w_in, even_w_out=v_even_w_out, a_w_s=v_a_w_s, a_b_s=v_a_b_s, a_ln_g=v_a_ln_g,
                a_ln_b=v_a_ln_b, b_w_pool=v_b_w_pool, b_scale=v_b_scale, odd_w_in=v_odd_w_in, odd_w_out=v_odd_w_out,
                c_w_dw=v_c_w_dw, c_b_dw=v_c_b_dw, c_ln_g=v_c_ln_g, c_ln_b=v_c_ln_b, d_w_dw=v_d_w_dw,
                norm_mix_g=v_norm_mix_g, norm_ffn_g=v_norm_ffn_g, ffn_w_gate=v_ffn_w_gate, ffn_w_up=v_ffn_w_up,
                ffn_w_down=v_ffn_w_down, final_norm_g=v_final_norm_g)
    names = list(weights)

    group_parts = {
        "even": [even_w_in[0].T, even_w_out[0]],
        "ffn0": [ffn_w_gate[0].T, ffn_w_up[0].T, ffn_w_down[0]],
        "odd": [odd_w_in[0].T, odd_w_out[0]],
        "ffn1": [ffn_w_gate[1].T, ffn_w_up[1].T, ffn_w_down[1]],
    }

    def gather_jobs(*groups):
        return [_all_gather_job(p.astype(bf16)) for k in groups for p in group_parts[k]]

    def whole(gathered):
        return [g.reshape(-1, D_MODEL) for g in gathered]

    conv_names = ["c_w_dw", "c_b_dw", "c_ln_g", "c_ln_b", "d_w_dw"]
    conv_rows = [C_KERNEL, 1, 1, 1, D_KERNEL]
    conv_local = jnp.concatenate([weights[n].reshape(r, -1) for n, r in zip(conv_names, conv_rows)]
                                 + [jnp.zeros((3, c_b_dw.shape[-1]), f32)], axis=0)
    *even_gathered, conv_all = _run_jobs(gather_jobs("even") + [_all_gather_job(conv_local)], "gather_even_conv")
    w_in_e, w_out_e = whole(even_gathered)
    conv_all = conv_all.transpose(1, 0, 2).reshape(conv_local.shape[0], -1)
    conv_offs = [sum(conv_rows[:k]) for k in range(len(conv_rows) + 1)]
    cw, cb, clg, clb, dw = [conv_all[conv_offs[k]:conv_offs[k + 1]] for k in range(len(conv_rows))]

    ws, bst = a_w_s[0], a_b_s[0].T
    lng, lnb, wp, sc = a_ln_g, a_ln_b, b_w_pool[0], b_scale
    gmix = [norm_mix_g[l:l + 1] for l in range(2)]
    gffn = [norm_ffn_g[l:l + 1] for l in range(2)]
    gfin = final_norm_g.reshape(1, D_MODEL)

    h0 = x[0]
    h1, hn_e, za, pooled, mix_e, *ffn0_gathered = _even_fwd(
        h0, w_in_e, w_out_e, ws, bst, lng, lnb, wp, sc, gmix[0], jobs=gather_jobs("ffn0"))
    w_gate0, w_up0, w_down0 = whole(ffn0_gathered)
    h2, hn_f0, gate0, up0, *rest_gathered = _ffn_fwd(h1, w_gate0, w_up0, w_down0, gffn[0],
                                                     jobs=gather_jobs("odd", "ffn1"))
    w_in_o, w_out_o, w_gate1, w_up1, w_down1 = whole(rest_gathered)
    h3, hn_o, z_o, mix_o, cv_o = _odd_fwd(h2, w_in_o, w_out_o, cw, cb, clg, clb, dw, gmix[1])
    dh4, hn_f1, gate1, up1, loss_local, g_final = _ffn_fwd(h3, w_gate1, w_up1, w_down1, gffn[1],
                                                           head=(loss_target[0], gfin))

    c_idx = lax.axis_index("c").astype(jnp.int32).reshape(1)

    def weight_grad(x, y, name, jobs=()):
        g, *job_results = _weight_grads([(x, y)], name, jobs=jobs)
        return [_pack_rows(g)] + job_results

    def siblings(parts):
        return [_sibling_exchange_job(p) for p in parts]

    def chips(pairs):
        return [_chip_exchange_job(p) for p in pairs]

    dh3, dgate1, dup1, act1, g_ffn1 = _ffn_bwd(dh4, h3, gate1, up1, w_gate1, w_up1, w_down1, gffn[1])
    part_ffn1 = (weight_grad(dgate1, hn_f1, "dw_gate1") + weight_grad(dup1, hn_f1, "dw_up1")
                 + weight_grad(act1, dh4, "dw_down1"))
    dh2, dz_o, g_cw, g_cb, g_clg, g_clb, g_dw, g_mix1, *recv_ffn1 = _odd_bwd(
        dh3, h2, z_o, cv_o, w_in_o, w_out_o, cw, clg, clb, dw, gmix[1], jobs=siblings(part_ffn1))
    pair_ffn1 = _pair_sum(part_ffn1, recv_ffn1, c_idx, "pair_sum_ffn1")
    part_odd = weight_grad(dz_o, hn_o, "dw_odd_in") + weight_grad(mix_o, dh3, "dw_odd_out")
    dh1, dgate0, dup0, act0, g_ffn0, *exchanged = _ffn_bwd(
        dh2, h1, gate0, up0, w_gate0, w_up0, w_down0, gffn[0], jobs=chips(pair_ffn1) + siblings(part_odd))
    chips_ffn1, recv_odd = exchanged[:3], exchanged[3:]
    pair_odd = _pair_sum(part_odd, recv_odd, c_idx, "pair_sum_odd")
    dw_gate0, *chips_odd = weight_grad(dgate0, hn_f0, "dw_gate0", jobs=chips(pair_odd))
    part_ffn0 = [dw_gate0] + weight_grad(dup0, hn_f0, "dw_up0") + weight_grad(act0, dh2, "dw_down0")
    part_even_out, *recv_ffn0 = weight_grad(mix_e, dh1, "dw_even_out", jobs=siblings(part_ffn0))
    pair_ffn0 = _pair_sum(part_ffn0, recv_ffn0, c_idx, "pair_sum_ffn0")
    dh0, dz_e, g_ws, g_bs, g_lng, g_lnb, g_wp, g_sc, g_mix0, *exchanged = _even_bwd(
        dh1, h0, za, pooled, w_in_e, w_out_e, ws, bst, lng, lnb, wp, sc, gmix[0],
        jobs=chips(pair_ffn0) + siblings([part_even_out]))
    chips_ffn0, recv_even_out = exchanged[:3], exchanged[3:]
    pair_even_out = _pair_sum([part_even_out], recv_even_out, c_idx, "pair_sum_even_out")

    lanes = HEAD
    small = [("a_w_s", g_ws), ("a_b_s", g_bs), ("a_ln_g", g_lng), ("a_ln_b", g_lnb), ("b_w_pool", g_wp),
             ("b_scale", g_sc), ("norm_mix_g", jnp.concatenate([g_mix0, g_mix1], axis=0)),
             ("norm_ffn_g", jnp.concatenate([g_ffn0, g_ffn1], axis=0)), ("final_norm_g", g_final),
             ("c_w_dw", g_cw), ("c_b_dw", g_cb), ("c_ln_g", g_clg), ("c_ln_b", g_clb), ("d_w_dw", g_dw),
             ("loss", loss_local)]
    small_rows = [-(-g.size // (8 * lanes)) * 8 for _, g in small]
    small_offs = [sum(small_rows[:k]) for k in range(len(small) + 1)]
    pad_rows = -small_offs[-1] % 256
    small_buf = jnp.concatenate(
        [jnp.pad(g.reshape(-1), (0, r * lanes - g.size)).reshape(r, lanes) for (_, g), r in zip(small, small_rows)]
        + [jnp.zeros((pad_rows, lanes), f32)], axis=0)
    part_even_in, small_all, chips_even_out = weight_grad(
        dz_e, hn_e, "dw_even_in", jobs=[_all_gather_job(small_buf)] + chips(pair_even_out))
    small_sum = _sum_leading(small_all, "small_grad_sum")
    recv_even_in = _run_jobs(siblings([part_even_in]), "sibling_exchange_even_in")
    chips_even_in = _run_jobs(chips(_pair_sum([part_even_in], recv_even_in, c_idx, "pair_sum_even_in")),
                              "chip_exchange_even_in")
    grads = {}
    for k, (n, g) in enumerate(small):
        grads[n] = small_sum[small_offs[k]:small_offs[k + 1]].reshape(-1)[:g.size].reshape(g.shape)
    me = 4 * lax.axis_index("x") + 2 * lax.axis_index("y") + lax.axis_index("c")
    shard = c_b_dw.shape[-1]
    for n in conv_names:
        grads[n] = lax.dynamic_slice_in_dim(grads[n], me * shard, shard, axis=1)

    col_sharded = ("even_w_in", "odd_w_in", "ffn_w_gate", "ffn_w_up")

    def rows_view(n, a):
        return jnp.swapaxes(a, -1, -2) if n in col_sharded else a

    loss = grads.pop("loss")[0, 0]
    chip_parts = {"even_w_in": chips_even_in, "even_w_out": [chips_even_out],
                  "odd_w_in": chips_odd[:1], "odd_w_out": chips_odd[1:]}
    for k, n in enumerate(["ffn_w_gate", "ffn_w_up", "ffn_w_down"]):
        chip_parts[n] = [chips_ffn0[k], chips_ffn1[k]]

    delta, new_m, new_v = {}, {}, {}
    for n in names:
        w_rows, m_rows, v_rows = [rows_view(n, a) for a in (weights[n], m_in[n], v_in[n])]
        if n in chip_parts:
            outs = _adamw_reduced(w_rows, chip_parts[n], m_rows, v_rows, "adamw_" + n)
        else:
            view = (-1, w_rows.shape[-1])
            outs = [grads[n], *_adamw(w_rows.reshape(view), grads[n].reshape(view), m_rows.reshape(view),
                                      v_rows.reshape(view), "adamw_" + n)]
        grads[n], delta[n], new_m[n], new_v[n] = [rows_view(n, o.reshape(w_rows.shape)) for o in outs]

    return (loss, dh0[None], *[grads[n] for n in names], *[delta[n] for n in names],
            *[new_m[n] for n in names], *[new_v[n] for n in names])
```

```python
import jax
import jax.numpy as jnp
from jax import lax
from jax.experimental import pallas as pl
from jax.experimental.pallas import tpu as pltpu

f32 = jnp.float32
bf16 = jnp.bfloat16

EPS = 1e-6
D_MODEL = 1024
A_WIDTH = 512
HEAD = 128
N_HEADS = 4
CHUNK = 64
POOL_WINDOWS = (2, 4, 8, 16)
POOL_HALO = 16
C_KERNEL = 31
D_KERNEL = 3
CONV_HALO = 32
D_FF = 2816
N_DEV = 8
N_CHIP = 4

ADAM_LR = 0.001
ADAM_B1 = 0.9
ADAM_B2 = 0.999
ADAM_EPS = 1e-08
ADAM_WD = 0.01
ADAM_STEP = 10

MIX_TILE = 512
ODD_FWD_TILE = 1024
FFN_TILE = 256
FFN_CHUNKS = (1536, 1280)
DW_TK = 2048
DW_TM = 1536
MIDDLE_AT, MIDDLE_OF = 7, 8
VMEM_LIMIT = 56 * 1024 * 1024

MESH = pl.DeviceIdType.MESH
ANY = pl.BlockSpec(memory_space=pl.ANY)


def _params(n_axes):
    return pltpu.CompilerParams(dimension_semantics=("arbitrary",) * n_axes, vmem_limit_bytes=VMEM_LIMIT)


def _mm(a, b):
    return jnp.dot(a, b, preferred_element_type=f32)


def _mm_nt(a, b):
    return lax.dot_general(a, b, (((1,), (1,)), ((), ())), preferred_element_type=f32)


def _mm_tn(a, b):
    return lax.dot_general(a, b, (((0,), (0,)), ((), ())), preferred_element_type=f32)


def _sigmoid(x):
    return 1.0 / (1.0 + jnp.exp(-x))


def _rms_r(h):
    return lax.rsqrt(jnp.mean(h * h, axis=-1, keepdims=True) + EPS)


def _rms_bwd(dy, h, g):
    r = _rms_r(h)
    xh = h * r
    dxh = dy * g
    dh = r * (dxh - xh * jnp.mean(dxh * xh, axis=-1, keepdims=True))
    return dh, jnp.sum(dy * xh, axis=0, keepdims=True)


def _ln_fwd(x, g, b):
    mu = jnp.mean(x, axis=-1, keepdims=True)
    xc = x - mu
    r = lax.rsqrt(jnp.mean(xc * xc, axis=-1, keepdims=True) + EPS)
    xh = xc * r
    return xh * g + b, xh, r


def _ln_bwd(dy, xh, r, g):
    dxh = dy * g
    return r * (dxh - jnp.mean(dxh, axis=-1, keepdims=True) - xh * jnp.mean(dxh * xh, axis=-1, keepdims=True))


_GELU_C = 0.7978845608028654
_GELU_A = 0.044715


def _gelu(x):
    th = jnp.tanh(x * (_GELU_C + (_GELU_C * _GELU_A) * (x * x)))
    half = 0.5 * x
    return half + half * th, th


def _gelu_grad(x, th):
    return 0.5 + 0.5 * th + (1.0 - th * th) * (x * (0.5 * _GELU_C + (1.5 * _GELU_C * _GELU_A) * (x * x)))


def _down(x, k):
    return x if k == 0 else pltpu.roll(x, k, 0)


def _up(x, k):
    return x if k == 0 else pltpu.roll(x, x.shape[0] - k, 0)


def _window_sum(x, win, shift):
    s = x
    step = 1
    while step < win:
        s = s + shift(s, step)
        step *= 2
    return s


def _inv_count(t0, rows, win):
    t = t0 + lax.broadcasted_iota(jnp.int32, (rows, 1), 0)
    return 1.0 / jnp.minimum(t + 1, win).astype(f32)


def _chunk_mask():
    i = lax.broadcasted_iota(jnp.int32, (HEAD, HEAD), 0)
    j = lax.broadcasted_iota(jnp.int32, (HEAD, HEAD), 1)
    return jnp.logical_or(i >= CHUNK, j < CHUNK)


def _const(shape, n_axes):
    zeros = (0,) * len(shape)
    if n_axes == 1:
        return pl.BlockSpec(shape, lambda i: zeros)
    return pl.BlockSpec(shape, lambda i, j: zeros)


def _prev_halo(tile, halo, cols):
    return pl.BlockSpec((halo, cols), lambda i: (jnp.maximum(i * (tile // halo) - 1, 0), 0))


def _next_halo(tile, halo, cols, seq):
    return pl.BlockSpec((halo, cols), lambda i: (jnp.minimum((i + 1) * (tile // halo), seq // halo - 1), 0))


class _Job:
    def __init__(self, inputs, out_shape, sems, hooks):
        self.inputs, self.out_shape, self.sems, self.hooks = inputs, out_shape, sems, hooks


def _position():
    return lax.axis_index("x"), lax.axis_index("y"), lax.axis_index("c")


def _all_gather_job(block):
    rows, cols = block.shape

    def hooks(ins, outs, sems):
        (x_ref,), (out_ref,), (send_sems, recv_sems, local_sem) = ins, outs, sems
        x, y, c = _position()
        me, sibling = (x, y, c), (x, y, 1 - c)
        chips = [(1 - x, y), (x, 1 - y), (1 - x, 1 - y)]

        def slot(px, py, pc):
            return out_ref.at[4 * px + 2 * py + pc]

        def copy(k, block_of, to, src=None):
            return pltpu.make_async_remote_copy(
                src_ref=slot(*block_of) if src is None else src, dst_ref=slot(*block_of),
                send_sem=send_sems.at[k], recv_sem=recv_sems.at[k], device_id=to, device_id_type=MESH)

        mine = pltpu.make_async_copy(x_ref, slot(*me), local_sem)
        first = [copy(0, me, sibling, src=x_ref)]
        first += [copy(1 + j, me, (*chip, c), src=x_ref) for j, chip in enumerate(chips)]
        passed = [copy(4 + j, (*chip, c), sibling) for j, chip in enumerate(chips)]

        def start():
            mine.start()
            for cp in first:
                cp.start()

        def middle():
            for j, chip in enumerate(chips):
                copy(1 + j, (*chip, c), me).wait_recv()
                passed[j].start()

        def finish():
            copy(0, sibling, me).wait_recv()
            for j, chip in enumerate(chips):
                copy(4 + j, (*chip, 1 - c), me).wait_recv()
            for cp in first + passed:
                cp.wait_send()
            mine.wait()

        return start, middle, finish

    return _Job([block], [jax.ShapeDtypeStruct((N_DEV, rows, cols), block.dtype)],
                [pltpu.SemaphoreType.DMA((7,)), pltpu.SemaphoreType.DMA((7,)), pltpu.SemaphoreType.DMA], hooks)


def _sibling_exchange_job(g):
    _, rows, cols = g.shape

    def hooks(ins, outs, sems):
        (g_ref,), (recv_ref,), (send_sems, recv_sems) = ins, outs, sems
        x, y, c = _position()
        copies = [pltpu.make_async_remote_copy(
            src_ref=g_ref.at[2 * k + (1 - c)], dst_ref=recv_ref.at[k], send_sem=send_sems.at[k],
            recv_sem=recv_sems.at[k], device_id=(x, y, 1 - c), device_id_type=MESH) for k in range(N_CHIP)]

        def start():
            for cp in copies:
                cp.start()

        def finish():
            for cp in copies:
                cp.wait()

        return start, lambda: None, finish

    return _Job([g], [jax.ShapeDtypeStruct((N_CHIP, rows, cols), g.dtype)],
                [pltpu.SemaphoreType.DMA((N_CHIP,)), pltpu.SemaphoreType.DMA((N_CHIP,))], hooks)


def _chip_exchange_job(p):
    _, rows, cols = p.shape

    def hooks(ins, outs, sems):
        (p_ref,), (recv_ref,), (send_sems, recv_sems, local_sem) = ins, outs, sems
        x, y, c = _position()
        k_me = 2 * x + y
        mine = pltpu.make_async_copy(p_ref.at[k_me], recv_ref.at[k_me], local_sem)
        copies = [pltpu.make_async_remote_copy(
            src_ref=p_ref.at[2 * px + py], dst_ref=recv_ref.at[k_me], send_sem=send_sems.at[j],
            recv_sem=recv_sems.at[j], device_id=(px, py, c), device_id_type=MESH)
            for j, (px, py) in enumerate([(1 - x, y), (x, 1 - y), (1 - x, 1 - y)])]

        def start():
            mine.start()
            for cp in copies:
                cp.start()

        def finish():
            for cp in copies:
                cp.wait()
            mine.wait()

        return start, lambda: None, finish

    return _Job([p], [jax.ShapeDtypeStruct((N_CHIP, rows, cols), p.dtype)],
                [pltpu.SemaphoreType.DMA((3,)), pltpu.SemaphoreType.DMA((3,)), pltpu.SemaphoreType.DMA], hooks)


def _job_hooks(jobs, ins, outs, sems):
    hooks = []
    for job in jobs:
        n_in, n_out, n_sem = len(job.inputs), len(job.out_shape), len(job.sems)
        hooks.append(job.hooks(ins[:n_in], outs[:n_out], sems[:n_sem]))
        ins, outs, sems = ins[n_in:], outs[n_out:], sems[n_sem:]
    return hooks


def _run_jobs(jobs, name):
    n_in = sum(len(job.inputs) for job in jobs)
    n_out = sum(len(job.out_shape) for job in jobs)

    def body(*refs):
        hooks = _job_hooks(jobs, refs[:n_in], refs[n_in:n_in + n_out], refs[n_in + n_out:])
        for phase in range(3):
            for h in hooks:
                h[phase]()

    return list(pl.pallas_call(
        body, name=name, in_specs=[ANY] * n_in, out_specs=[ANY] * n_out,
        out_shape=[s for job in jobs for s in job.out_shape],
        scratch_shapes=[s for job in jobs for s in job.sems],
    )(*[a for job in jobs for a in job.inputs]))


def _launch(body, *, name, grid, in_specs, out_specs, out_shape, args, scratch=(), jobs=()):
    in_specs, out_specs, out_shape, scratch = list(in_specs), list(out_specs), list(out_shape), list(scratch)
    if not jobs:
        return list(pl.pallas_call(body, name=name, grid=grid, in_specs=in_specs, out_specs=out_specs,
                                   out_shape=out_shape, scratch_shapes=scratch,
                                   compiler_params=_params(len(grid)))(*args))
    n_in, n_out, n_sc = len(in_specs), len(out_specs), len(scratch)
    j_in = [a for job in jobs for a in job.inputs]
    j_out = [s for job in jobs for s in job.out_shape]
    j_sems = [s for job in jobs for s in job.sems]
    n_steps = 1
    for g in grid:
        n_steps *= g

    def wrapped(*refs):
        ins, refs = refs[:n_in], refs[n_in:]
        jins, refs = refs[:len(j_in)], refs[len(j_in):]
        outs, refs = refs[:n_out], refs[n_out:]
        jouts, refs = refs[:len(j_out)], refs[len(j_out):]
        sc, jsems = refs[:n_sc], refs[n_sc:]
        step = pl.program_id(0)
        for axis in range(1, len(grid)):
            step = step * grid[axis] + pl.program_id(axis)
        hooks = _job_hooks(jobs, jins, jouts, jsems)

        @pl.when(step == 0)
        def _():
            for h in hooks:
                h[0]()

        body(*ins, *outs, *sc)

        @pl.when(step == (MIDDLE_AT * n_steps) // MIDDLE_OF)
        def _():
            for h in hooks:
                h[1]()

        @pl.when(step == n_steps - 1)
        def _():
            for h in hooks:
                h[2]()

    return list(pl.pallas_call(
        wrapped, name=name, grid=grid, in_specs=in_specs + [ANY] * len(j_in), out_specs=out_specs + [ANY] * len(j_out),
        out_shape=out_shape + j_out, scratch_shapes=scratch + j_sems, compiler_params=_params(len(grid)),
    )(*args, *j_in))


def _gmlp_gate(vnb, wsm, bst, tile):
    rows = []
    for n in range(tile // HEAD):
        cols = []
        for hh in range(N_HEADS):
            blk = vnb[n * HEAD:(n + 1) * HEAD, hh * HEAD:(hh + 1) * HEAD]
            cols.append(_mm(wsm[hh], blk) + bst[:, hh:hh + 1])
        rows.append(jnp.concatenate(cols, axis=1))
    return jnp.concatenate(rows, axis=0)


def _even_fwd(h, w_in, w_out, ws, bst, lng, lnb, wp, sc, gm, jobs=()):
    seq = h.shape[0]
    tile = min(MIX_TILE, seq)
    n_tiles = seq // tile

    def body(h_ref, hp_ref, win_ref, wout_ref, ws_ref, bst_ref, lng_ref, lnb_ref, wp_ref, sc_ref, g_ref,
             ho_ref, hn_ref, za_ref, pool_ref, mix_ref):
        i = pl.program_id(0)
        g = g_ref[...]
        h = h_ref[...]
        hnb = (h * _rms_r(h) * g).astype(bf16)
        hn_ref[...] = hnb
        z = _mm_nt(hnb, win_ref[...])
        zab = z[:, :2 * A_WIDTH].astype(bf16)
        za_ref[...] = zab
        hp = hp_ref[...]
        zbp = _mm_nt((hp * _rms_r(hp) * g).astype(bf16), win_ref[2 * A_WIDTH:, :])
        zbe = jnp.concatenate([jnp.where(i > 0, zbp, 0.0), z[:, 2 * A_WIDTH:]], axis=0)
        pooled = []
        for gi, win in enumerate(POOL_WINDOWS):
            xg = zbe[:, gi * HEAD:(gi + 1) * HEAD]
            s = _window_sum(xg, win, _down)
            pooled.append(s[POOL_HALO:] * _inv_count(i * tile, tile, win) - xg[POOL_HALO:])
        plb = jnp.concatenate(pooled, axis=1).astype(bf16)
        pool_ref[...] = plb

        ga, _ = _gelu(zab.astype(f32))
        vn, _, _ = _ln_fwd(ga[:, A_WIDTH:], lng_ref[...], lnb_ref[...])
        mask = _chunk_mask()
        wsm = [jnp.where(mask, ws_ref[hh], 0.0).astype(bf16) for hh in range(N_HEADS)]
        ya = ga[:, :A_WIDTH] * _gmlp_gate(vn.astype(bf16), wsm, bst_ref[...], tile)
        yb = jnp.concatenate([_mm(plb[:, gi * HEAD:(gi + 1) * HEAD], wp_ref[gi].astype(bf16))
                              for gi in range(len(POOL_WINDOWS))], axis=1) * sc_ref[...]
        mix = jnp.concatenate([ya, yb], axis=1).astype(bf16)
        mix_ref[...] = mix
        ho_ref[...] = h + _mm(mix, wout_ref[...])

    row = lambda cols: pl.BlockSpec((tile, cols), lambda i: (i, 0))
    return _launch(
        body, name="even_fwd", grid=(n_tiles,), jobs=jobs,
        in_specs=[row(D_MODEL), _prev_halo(tile, POOL_HALO, D_MODEL), _const(w_in.shape, 1), _const(w_out.shape, 1),
                  _const(ws.shape, 1), _const(bst.shape, 1), _const(lng.shape, 1), _const(lnb.shape, 1),
                  _const(wp.shape, 1), _const(sc.shape, 1), _const(gm.shape, 1)],
        out_specs=[row(D_MODEL), row(D_MODEL), row(2 * A_WIDTH), row(A_WIDTH), row(D_MODEL)],
        out_shape=[jax.ShapeDtypeStruct((seq, D_MODEL), f32), jax.ShapeDtypeStruct((seq, D_MODEL), bf16),
                   jax.ShapeDtypeStruct((seq, 2 * A_WIDTH), bf16), jax.ShapeDtypeStruct((seq, A_WIDTH), bf16),
                   jax.ShapeDtypeStruct((seq, D_MODEL), bf16)],
        args=(h, h, w_in, w_out, ws, bst, lng, lnb, wp, sc, gm))


def _even_bwd(dh, h, za, pooled, w_in, w_out, ws, bst, lng, lnb, wp, sc, gm, jobs=()):
    seq = h.shape[0]
    tile = min(MIX_TILE, seq)
    n_tiles = seq // tile
    n_groups = len(POOL_WINDOWS)

    def body(dh_ref, dhx_ref, h_ref, za_ref, pool_ref, win_ref, wout_ref, ws_ref, bst_ref, lng_ref, lnb_ref,
             wp_ref, sc_ref, g_ref,
             dhi_ref, dz_ref, dws_ref, dbs_ref, dlng_ref, dlnb_ref, dwp_ref, dsc_ref, dg_ref):
        i = pl.program_id(0)

        @pl.when(i == 0)
        def _():
            for ref in (dws_ref, dbs_ref, dlng_ref, dlnb_ref, dwp_ref, dsc_ref, dg_ref):
                ref[...] = jnp.zeros_like(ref)

        dh = dh_ref[...]
        dmix = _mm_nt(dh.astype(bf16), wout_ref[...])
        dya = dmix[:, :A_WIDTH]
        dyb = dmix[:, A_WIDTH:]
        dybx = _mm_nt(dhx_ref[...].astype(bf16), wout_ref[A_WIDTH:, :])
        dybx = jnp.where(i < n_tiles - 1, dybx, 0.0)

        za = za_ref[...].astype(f32)
        ga, th = _gelu(za)
        u = ga[:, :A_WIDTH]
        lng = lng_ref[...]
        vn, vh, r = _ln_fwd(ga[:, A_WIDTH:], lng, lnb_ref[...])
        vnb = vn.astype(bf16)
        mask = _chunk_mask()
        wsf = [jnp.where(mask, ws_ref[hh], 0.0) for hh in range(N_HEADS)]
        sv = _gmlp_gate(vnb, [w.astype(bf16) for w in wsf], bst_ref[...], tile)
        du = dya * sv
        dsvb = (dya * u).astype(bf16)
        wst = [w.T.astype(bf16) for w in wsf]
        ones = jnp.ones((8, HEAD), bf16)
        dws = [jnp.zeros((HEAD, HEAD), f32) for _ in range(N_HEADS)]
        dbs = [jnp.zeros((8, HEAD), f32) for _ in range(N_HEADS)]
        rows = []
        for n in range(tile // HEAD):
            cols = []
            for hh in range(N_HEADS):
                blk = dsvb[n * HEAD:(n + 1) * HEAD, hh * HEAD:(hh + 1) * HEAD]
                cols.append(_mm(wst[hh], blk))
                dws[hh] = dws[hh] + _mm_nt(blk, vnb[n * HEAD:(n + 1) * HEAD, hh * HEAD:(hh + 1) * HEAD])
                dbs[hh] = dbs[hh] + _mm_nt(ones, blk)
            rows.append(jnp.concatenate(cols, axis=1))
        dvn = jnp.concatenate(rows, axis=0)
        for hh in range(N_HEADS):
            dws_ref[hh] += jnp.where(mask, dws[hh], 0.0)
            dbs_ref[pl.ds(hh, 1), :] += dbs[hh][0:1, :]
        dlng_ref[...] += jnp.sum(dvn * vh, axis=0, keepdims=True)
        dlnb_ref[...] += jnp.sum(dvn, axis=0, keepdims=True)
        dv = _ln_bwd(dvn, vh, r, lng)
        dza = jnp.concatenate([du, dv], axis=1) * _gelu_grad(za, th)

        plb = pool_ref[...]
        sc = sc_ref[...]
        dzb = []
        dsc = []
        for gi, win in enumerate(POOL_WINDOWS):
            cs = slice(gi * HEAD, (gi + 1) * HEAD)
            wpb = wp_ref[gi].astype(bf16)
            dsc.append(jnp.sum(dyb[:, cs] * _mm(plb[:, cs], wpb), axis=0, keepdims=True))
            dpre = (dyb[:, cs] * sc[:, cs]).astype(bf16)
            dprex = (dybx[:, cs] * sc[:, cs]).astype(bf16)
            dwp_ref[gi] += _mm_tn(plb[:, cs], dpre)
            dpl = _mm_nt(dpre, wpb)
            dple = jnp.concatenate([dpl, _mm_nt(dprex, wpb)], axis=0)
            q = dple * _inv_count(i * tile, tile + POOL_HALO, win)
            dzb.append(_window_sum(q, win, _up)[:tile] - dpl)
        dsc_ref[...] += jnp.concatenate(dsc, axis=1)

        dzf = jnp.concatenate([dza] + dzb, axis=1).astype(bf16)
        dz_ref[...] = dzf
        dhn = _mm(dzf, win_ref[...])
        dhr, dg = _rms_bwd(dhn, h_ref[...], g_ref[...])
        dhi_ref[...] = dh + dhr
        dg_ref[...] += dg

    row = lambda cols: pl.BlockSpec((tile, cols), lambda i: (i, 0))
    small = [ws.shape, (N_HEADS, HEAD), lng.shape, lnb.shape, wp.shape, sc.shape, gm.shape]
    return _launch(
        body, name="even_bwd", grid=(n_tiles,), jobs=jobs,
        in_specs=[row(D_MODEL), _next_halo(tile, POOL_HALO, D_MODEL, seq), row(D_MODEL), row(2 * A_WIDTH), row(A_WIDTH),
                  _const(w_in.shape, 1), _const(w_out.shape, 1), _const(ws.shape, 1), _const(bst.shape, 1),
                  _const(lng.shape, 1), _const(lnb.shape, 1), _const(wp.shape, 1), _const(sc.shape, 1), _const(gm.shape, 1)],
        out_specs=[row(D_MODEL), row(3 * A_WIDTH)] + [_const(s, 1) for s in small],
        out_shape=[jax.ShapeDtypeStruct((seq, D_MODEL), f32), jax.ShapeDtypeStruct((seq, 3 * A_WIDTH), bf16)]
                  + [jax.ShapeDtypeStruct(s, f32) for s in small],
        args=(dh, dh, h, za, pooled, w_in, w_out, ws, bst, lng, lnb, wp, sc, gm))


SUBLANES = 8


class _Shifted:
    def __init__(self, x, shift, max_shift):
        self.rolled = [shift(x, b) for b in range(min(SUBLANES, max_shift + 1))]
        self.back = shift is _down

    def rows(self, k, start, count):
        whole = k - k % SUBLANES
        lo = start - whole if self.back else start + whole
        return self.rolled[k % SUBLANES][lo:lo + count]


def _conv_taps(xs, w_ref, n_taps, halo, rows):
    acc = None
    for j in range(n_taps):
        term = w_ref[pl.ds(j, 1), :] * xs.rows(n_taps - 1 - j, halo, rows)
        acc = term if acc is None else acc + term
    return acc


def _odd_fwd(h, w_in, w_out, cw, cb, clg, clb, dw, gm):
    seq = h.shape[0]
    tile = min(ODD_FWD_TILE, seq)
    n_tiles = seq // tile
    w = A_WIDTH

    def body(h_ref, hp_ref, win_ref, wout_ref, cw_ref, cb_ref, clg_ref, clb_ref, dw_ref, g_ref,
             ho_ref, hn_ref, z_ref, mix_ref, cv_ref):
        i = pl.program_id(0)
        g = g_ref[...]
        h = h_ref[...]
        hnb = (h * _rms_r(h) * g).astype(bf16)
        hn_ref[...] = hnb
        zb = _mm_nt(hnb, win_ref[...]).astype(bf16)
        z_ref[...] = zb
        hp = hp_ref[...]
        zp = _mm_nt((hp * _rms_r(hp) * g).astype(bf16), win_ref[...]).astype(bf16).astype(f32)
        z = zb.astype(f32)
        ze = jnp.concatenate([jnp.where(i > 0, zp, 0.0), z], axis=0)
        hc = ze[:, :w] * _sigmoid(ze[:, w:2 * w])
        cv = _conv_taps(_Shifted(hc, _down, C_KERNEL - 1), cw_ref, C_KERNEL, CONV_HALO, tile) + cb_ref[...]
        cv_ref[...] = cv
        ln, _, _ = _ln_fwd(cv, clg_ref[...], clb_ref[...])
        yc = ln * _sigmoid(ln)
        p = ze[:, 3 * w:4 * w] * ze[:, 4 * w:]
        yd = z[:, 2 * w:3 * w] * _conv_taps(_Shifted(p, _down, D_KERNEL - 1), dw_ref, D_KERNEL, CONV_HALO, tile)
        mix = jnp.concatenate([yc, yd], axis=1).astype(bf16)
        mix_ref[...] = mix
        ho_ref[...] = h + _mm(mix, wout_ref[...])

    row = lambda cols: pl.BlockSpec((tile, cols), lambda i: (i, 0))
    return pl.pallas_call(
        body, name="odd_fwd", grid=(n_tiles,),
        in_specs=[row(D_MODEL), _prev_halo(tile, CONV_HALO, D_MODEL), _const(w_in.shape, 1), _const(w_out.shape, 1),
                  _const(cw.shape, 1), _const(cb.shape, 1), _const(clg.shape, 1), _const(clb.shape, 1),
                  _const(dw.shape, 1), _const(gm.shape, 1)],
        out_specs=[row(D_MODEL), row(D_MODEL), row(5 * w), row(D_MODEL), row(w)],
        out_shape=[jax.ShapeDtypeStruct((seq, D_MODEL), f32), jax.ShapeDtypeStruct((seq, D_MODEL), bf16),
                   jax.ShapeDtypeStruct((seq, 5 * w), bf16), jax.ShapeDtypeStruct((seq, D_MODEL), bf16),
                   jax.ShapeDtypeStruct((seq, w), f32)],
        compiler_params=_params(1),
    )(h, h, w_in, w_out, cw, cb, clg, clb, dw, gm)


def _odd_bwd(dh, h, z, cv, w_in, w_out, cw, clg, clb, dw, gm, jobs=()):
    seq = h.shape[0]
    tile = min(MIX_TILE, seq)
    n_tiles = seq // tile
    w = A_WIDTH
    halo = CONV_HALO

    def body(dh_ref, dhx_ref, h_ref, z_ref, zp_ref, zx_ref, cv_ref, cvx_ref, win_ref, wout_ref, cw_ref,
             clg_ref, clb_ref, dw_ref, g_ref,
             dhi_ref, dz_ref, dcw_ref, dcb_ref, dclg_ref, dclb_ref, ddw_ref, dg_ref):
        i = pl.program_id(0)

        @pl.when(i == 0)
        def _():
            for ref in (dcw_ref, dcb_ref, dclg_ref, dclb_ref, ddw_ref, dg_ref):
                ref[...] = jnp.zeros_like(ref)

        dh = dh_ref[...]
        dhe = jnp.concatenate([dh, jnp.where(i < n_tiles - 1, dhx_ref[...], 0.0)], axis=0)
        dmix = _mm_nt(dhe.astype(bf16), wout_ref[...])
        ze = jnp.concatenate([jnp.where(i > 0, zp_ref[...].astype(f32), 0.0), z_ref[...].astype(f32),
                              zx_ref[...].astype(f32)], axis=0)

        sg = _sigmoid(ze[:, w:2 * w])
        ca = ze[:, :w]
        hc = ca * sg
        hcs = _Shifted(hc, _down, C_KERNEL - 1)
        cv = jnp.concatenate([cv_ref[...], cvx_ref[...]], axis=0)
        clg = clg_ref[...]
        ln, xh, r = _ln_fwd(cv, clg, clb_ref[...])
        sl = _sigmoid(ln)
        dln = dmix[:, :w] * (sl * (1.0 + ln * (1.0 - sl)))
        dclg_ref[...] += jnp.sum((dln * xh)[:tile], axis=0, keepdims=True)
        dclb_ref[...] += jnp.sum(dln[:tile], axis=0, keepdims=True)
        dcv = _ln_bwd(dln, xh, r, clg)
        dcb_ref[...] += jnp.sum(dcv[:tile], axis=0, keepdims=True)
        dcvs = _Shifted(dcv, _up, C_KERNEL - 1)
        dhc = None
        for j in range(C_KERNEL):
            k = C_KERNEL - 1 - j
            dcw_ref[pl.ds(j, 1), :] += jnp.sum(dcv[:tile] * hcs.rows(k, halo, tile), axis=0, keepdims=True)
            term = cw_ref[pl.ds(j, 1), :] * dcvs.rows(k, 0, tile)
            dhc = term if dhc is None else dhc + term
        sgt = sg[halo:halo + tile]
        cat = ca[halo:halo + tile]
        dca = dhc * sgt
        dcg = dhc * cat * sgt * (1.0 - sgt)

        dcgv = ze[:, 3 * w:4 * w]
        dxin = ze[:, 4 * w:]
        p = dcgv * dxin
        ps = _Shifted(p, _down, D_KERNEL - 1)
        q = _conv_taps(ps, dw_ref, D_KERNEL, halo, tile)
        dyd = dmix[:, w:]
        dq = dyd * ze[halo:, 2 * w:3 * w]
        ddbg = dyd[:tile] * q
        dqs = _Shifted(dq, _up, D_KERNEL - 1)
        dp = None
        for j in range(D_KERNEL):
            k = D_KERNEL - 1 - j
            ddw_ref[pl.ds(j, 1), :] += jnp.sum(dq[:tile] * ps.rows(k, halo, tile), axis=0, keepdims=True)
            term = dw_ref[pl.ds(j, 1), :] * dqs.rows(k, 0, tile)
            dp = term if dp is None else dp + term
        ddcg = dp * dxin[halo:halo + tile]
        ddxin = dp * dcgv[halo:halo + tile]

        dzf = jnp.concatenate([dca, dcg, ddbg, ddcg, ddxin], axis=1).astype(bf16)
        dz_ref[...] = dzf
        dhn = _mm(dzf, win_ref[...])
        dhr, dg = _rms_bwd(dhn, h_ref[...], g_ref[...])
        dhi_ref[...] = dh + dhr
        dg_ref[...] += dg

    row = lambda cols: pl.BlockSpec((tile, cols), lambda i: (i, 0))
    small = [cw.shape, clg.shape, clg.shape, clb.shape, dw.shape, gm.shape]
    return _launch(
        body, name="odd_bwd", grid=(n_tiles,), jobs=jobs,
        in_specs=[row(D_MODEL), _next_halo(tile, halo, D_MODEL, seq), row(D_MODEL), row(5 * w),
                  _prev_halo(tile, halo, 5 * w), _next_halo(tile, halo, 5 * w, seq),
                  row(w), _next_halo(tile, halo, w, seq),
                  _const(w_in.shape, 1), _const(w_out.shape, 1), _const(cw.shape, 1),
                  _const(clg.shape, 1), _const(clb.shape, 1), _const(dw.shape, 1), _const(gm.shape, 1)],
        out_specs=[row(D_MODEL), row(5 * w)] + [_const(s, 1) for s in small],
        out_shape=[jax.ShapeDtypeStruct((seq, D_MODEL), f32), jax.ShapeDtypeStruct((seq, 5 * w), bf16)]
                  + [jax.ShapeDtypeStruct(s, f32) for s in small],
        args=(dh, dh, h, z, z, z, cv, cv, w_in, w_out, cw, clg, clb, dw, gm))


def _ffn_chunks():
    assert sum(FFN_CHUNKS) == D_FF
    start = 0
    for size in FFN_CHUNKS:
        yield slice(start, start + size)
        start += size


def _ffn_fwd(h, wg, wu, wd, gm, jobs=(), head=None):
    seq = h.shape[0]
    tile = min(FFN_TILE, seq)

    def body(h_ref, g_ref, wg_ref, wu_ref, wd_ref, *refs):
        if head is None:
            ho_ref, hn_ref, gate_ref, up_ref = refs
        else:
            t_ref, gf_ref, ho_ref, hn_ref, gate_ref, up_ref, loss_ref, dgf_ref = refs
        h = h_ref[...]
        hnb = (h * _rms_r(h) * g_ref[...]).astype(bf16)
        hn_ref[...] = hnb
        acc = None
        for rows in _ffn_chunks():
            gb = _mm_nt(hnb, wg_ref[rows, :]).astype(bf16)
            ub = _mm_nt(hnb, wu_ref[rows, :]).astype(bf16)
            gate_ref[:, rows] = gb
            up_ref[:, rows] = ub
            gf = gb.astype(f32)
            act = gf * _sigmoid(gf) * ub.astype(f32)
            part = _mm(act.astype(bf16), wd_ref[rows, :])
            acc = part if acc is None else acc + part
        ho = h + acc
        if head is None:
            ho_ref[...] = ho
            return

        @pl.when(pl.program_id(0) == 0)
        def _():
            loss_ref[...] = jnp.zeros_like(loss_ref)
            dgf_ref[...] = jnp.zeros_like(dgf_ref)

        g_final = gf_ref[...]
        err = ho * _rms_r(ho) * g_final - t_ref[...]
        loss_ref[...] += (0.5 / D_MODEL) * jnp.sum(jnp.sum(err * err, axis=1, keepdims=True), axis=0, keepdims=True)
        dho, dg = _rms_bwd(err * (1.0 / D_MODEL), ho, g_final)
        ho_ref[...] = dho
        dgf_ref[...] += dg

    row = pl.BlockSpec((tile, D_MODEL), lambda i: (i, 0))
    wide = pl.BlockSpec((tile, D_FF), lambda i: (i, 0))
    in_specs = [row, _const(gm.shape, 1), _const(wg.shape, 1), _const(wu.shape, 1), _const(wd.shape, 1)]
    out_specs = [row, row, wide, wide]
    out_shape = [jax.ShapeDtypeStruct((seq, D_MODEL), f32), jax.ShapeDtypeStruct((seq, D_MODEL), bf16),
                 jax.ShapeDtypeStruct((seq, D_FF), bf16), jax.ShapeDtypeStruct((seq, D_FF), bf16)]
    args = (h, gm, wg, wu, wd)
    if head is not None:
        target, g_final = head
        in_specs += [row, _const(g_final.shape, 1)]
        out_specs += [_const((1, 1), 1), _const(g_final.shape, 1)]
        out_shape += [jax.ShapeDtypeStruct((1, 1), f32), jax.ShapeDtypeStruct(g_final.shape, f32)]
        args += (target, g_final)
    return _launch(
        body, name="ffn_fwd" if head is None else "ffn_fwd_loss", grid=(seq // tile,), jobs=jobs,
        in_specs=in_specs, out_specs=out_specs, out_shape=out_shape, args=args)


def _ffn_bwd(dh, h, gate, up, wg, wu, wd, gm, jobs=()):
    seq = h.shape[0]
    tile = min(FFN_TILE, seq)
    n_tiles = seq // tile

    def body(dh_ref, h_ref, g_ref, gate_ref, up_ref, wg_ref, wu_ref, wd_ref,
             dhi_ref, dgate_ref, dup_ref, act_ref, dg_ref):
        @pl.when(pl.program_id(0) == 0)
        def _():
            dg_ref[...] = jnp.zeros_like(dg_ref)

        dh = dh_ref[...]
        dhb = dh.astype(bf16)
        acc = None
        for rows in _ffn_chunks():
            dact = _mm_nt(dhb, wd_ref[rows, :])
            gf = gate_ref[:, rows].astype(f32)
            uf = up_ref[:, rows].astype(f32)
            s = _sigmoid(gf)
            silu = gf * s
            act_ref[:, rows] = (silu * uf).astype(bf16)
            dgb = (dact * uf * (s * (1.0 + gf * (1.0 - s)))).astype(bf16)
            dub = (dact * silu).astype(bf16)
            dgate_ref[:, rows] = dgb
            dup_ref[:, rows] = dub
            part = _mm(dgb, wg_ref[rows, :]) + _mm(dub, wu_ref[rows, :])
            acc = part if acc is None else acc + part
        dhr, dg = _rms_bwd(acc, h_ref[...], g_ref[...])
        dhi_ref[...] = dh + dhr
        dg_ref[...] += dg

    row = pl.BlockSpec((tile, D_MODEL), lambda i: (i, 0))
    wide = pl.BlockSpec((tile, D_FF), lambda i: (i, 0))
    return _launch(
        body, name="ffn_bwd", grid=(n_tiles,), jobs=jobs,
        in_specs=[row, row, _const(gm.shape, 1), wide, wide, _const(wg.shape, 1), _const(wu.shape, 1), _const(wd.shape, 1)],
        out_specs=[row, wide, wide, wide, _const(gm.shape, 1)],
        out_shape=[jax.ShapeDtypeStruct((seq, D_MODEL), f32), jax.ShapeDtypeStruct((seq, D_FF), bf16),
                   jax.ShapeDtypeStruct((seq, D_FF), bf16), jax.ShapeDtypeStruct((seq, D_FF), bf16),
                   jax.ShapeDtypeStruct(gm.shape, f32)],
        args=(dh, h, gm, gate, up, wg, wu, wd))


def _weight_grads(pairs, name, jobs=()):
    seq, m = pairs[0][0].shape
    tk = min(DW_TK, seq)
    tm = m if m <= DW_TM else m // 2
    n_k = seq // tk
    n_pairs = len(pairs)

    def body(*refs):
        x_refs = refs[0:2 * n_pairs:2]
        y_refs = refs[1:2 * n_pairs:2]
        o_refs = refs[2 * n_pairs:3 * n_pairs]
        acc_refs = refs[3 * n_pairs:]
        k = pl.program_id(1)
        @pl.when(k == 0)
        def _():
            for acc_ref in acc_refs:
                acc_ref[...] = jnp.zeros_like(acc_ref)

        for x_ref, y_ref, acc_ref in zip(x_refs, y_refs, acc_refs):
            acc_ref[...] += _mm_tn(x_ref[...].astype(bf16), y_ref[...].astype(bf16))

        @pl.when(k == n_k - 1)
        def _():
            for o_ref, acc_ref in zip(o_refs, acc_refs):
                o_ref[...] = acc_ref[...].astype(bf16)

    in_specs = []
    for _ in pairs:
        in_specs += [pl.BlockSpec((tk, tm), lambda j, k: (k, j)), pl.BlockSpec((tk, D_MODEL), lambda j, k: (k, 0))]
    return _launch(
        body, name=name, grid=(m // tm, n_k), jobs=jobs,
        in_specs=in_specs,
        out_specs=[pl.BlockSpec((tm, D_MODEL), lambda j, k: (j, 0))] * n_pairs,
        out_shape=[jax.ShapeDtypeStruct((m, D_MODEL), bf16)] * n_pairs,
        scratch=[pltpu.VMEM((tm, D_MODEL), f32)] * n_pairs,
        args=[a for pair in pairs for a in pair])


def _row_tile(rows, limit=512):
    best = rows
    for t in range(8, min(rows, limit) + 1, 8):
        if rows % t == 0:
            best = t
    return best if rows > limit else rows


def _adam_step(w, g, m, v):
    m2 = ADAM_B1 * m + (1.0 - ADAM_B1) * g
    v2 = ADAM_B2 * v + (1.0 - ADAM_B2) * (g * g)
    m_hat = m2 / (1.0 - ADAM_B1 ** ADAM_STEP)
    v_hat = v2 / (1.0 - ADAM_B2 ** ADAM_STEP)
    return -ADAM_LR * (m_hat / (jnp.sqrt(v_hat) + ADAM_EPS) + ADAM_WD * w), m2, v2


def _adamw(w, g, m, v, name):
    rows, cols = w.shape
    tr = _row_tile(rows)

    def body(w_ref, g_ref, m_ref, v_ref, d_ref, mo_ref, vo_ref):
        d_ref[...], mo_ref[...], vo_ref[...] = _adam_step(w_ref[...], g_ref[...], m_ref[...], v_ref[...])

    spec = pl.BlockSpec((tr, cols), lambda i: (i, 0))
    return pl.pallas_call(
        body, name=name, grid=(rows // tr,),
        in_specs=[spec] * 4, out_specs=[spec] * 3,
        out_shape=[jax.ShapeDtypeStruct((rows, cols), f32)] * 3,
        compiler_params=_params(1),
    )(w, g, m, v)


def _adamw_reduced(w, parts, m, v, name):
    layers, rows, cols = w.shape

    def body(*refs):
        w_ref, m_ref, v_ref = refs[:3]
        part_refs = refs[3:3 + layers]
        g_ref, d_ref, mo_ref, vo_ref = refs[3 + layers:]
        layer = pl.program_id(0)
        for l, p_ref in enumerate(part_refs):
            @pl.when(layer == l)
            def _():
                acc = p_ref[0].astype(f32)
                for k in range(1, N_CHIP):
                    acc = acc + p_ref[k].astype(f32)
                g_ref[0] = acc

        d_ref[0], mo_ref[0], vo_ref[0] = _adam_step(w_ref[0], g_ref[0], m_ref[0], v_ref[0])

    blk = pl.BlockSpec((1, rows, cols), lambda l: (l, 0, 0))
    return pl.pallas_call(
        body, name=name, grid=(layers,),
        in_specs=[blk] * 3 + [pl.BlockSpec(p.shape, lambda l: (0, 0, 0)) for p in parts],
        out_specs=[blk] * 4,
        out_shape=[jax.ShapeDtypeStruct(w.shape, f32)] * 4,
        compiler_params=_params(1),
    )(w, m, v, *parts)


def _sum_leading(x, name):
    n, rows, cols = x.shape
    tr = _row_tile(rows)

    def body(x_ref, o_ref):
        acc = x_ref[0].astype(f32)
        for k in range(1, n):
            acc = acc + x_ref[k].astype(f32)
        o_ref[...] = acc

    return pl.pallas_call(
        body, name=name, grid=(rows // tr,),
        in_specs=[pl.BlockSpec((n, tr, cols), lambda i: (0, i, 0))],
        out_specs=pl.BlockSpec((tr, cols), lambda i: (i, 0)),
        out_shape=jax.ShapeDtypeStruct((rows, cols), f32),
        compiler_params=_params(1),
    )(x)


def _pair_sum(gs, recvs, c_idx, name):
    n = len(gs)

    def body(c_ref, *refs):
        for g_ref, r_ref, o_ref in zip(refs[:n], refs[n:2 * n], refs[2 * n:]):
            o_ref[...] = (g_ref[...].astype(f32) + r_ref[...].astype(f32)).astype(o_ref.dtype)

    own = [pl.BlockSpec((1,) + g.shape[1:], lambda k, c: (2 * k + c[0], 0, 0)) for g in gs]
    by_chip = [pl.BlockSpec((1,) + g.shape[1:], lambda k, c: (k, 0, 0)) for g in gs]
    return list(pl.pallas_call(
        body, name=name,
        grid_spec=pltpu.PrefetchScalarGridSpec(num_scalar_prefetch=1, grid=(N_CHIP,),
                                               in_specs=own + by_chip, out_specs=by_chip),
        out_shape=[jax.ShapeDtypeStruct((N_CHIP,) + g.shape[1:], g.dtype) for g in gs],
        compiler_params=_params(1),
    )(c_idx, *gs, *recvs))


def _pack_rows(w):
    return w.reshape(N_DEV, -1, D_MODEL)


def kernel(x, even_w_in, even_w_out, a_w_s, a_b_s, a_ln_g, a_ln_b, b_w_pool, b_scale, odd_w_in, odd_w_out, c_w_dw, c_b_dw, c_ln_g, c_ln_b, d_w_dw, norm_mix_g, norm_ffn_g, ffn_w_gate, ffn_w_up, ffn_w_down, final_norm_g, loss_target, m_even_w_in, m_even_w_out, m_a_w_s, m_a_b_s, m_a_ln_g, m_a_ln_b, m_b_w_pool, m_b_scale, m_odd_w_in, m_odd_w_out, m_c_w_dw, m_c_b_dw, m_c_ln_g, m_c_ln_b, m_d_w_dw, m_norm_mix_g, m_norm_ffn_g, m_ffn_w_gate, m_ffn_w_up, m_ffn_w_down, m_final_norm_g, v_even_w_in, v_even_w_out, v_a_w_s, v_a_b_s, v_a_ln_g, v_a_ln_b, v_b_w_pool, v_b_scale, v_odd_w_in, v_odd_w_out, v_c_w_dw, v_c_b_dw, v_c_ln_g, v_c_ln_b, v_d_w_dw, v_norm_mix_g, v_norm_ffn_g, v_ffn_w_gate, v_ffn_w_up, v_ffn_w_down, v_final_norm_g):
    weights = dict(even_w_in=even_w_in, even_w_out=even_w_out, a_w_s=a_w_s, a_b_s=a_b_s, a_ln_g=a_ln_g, a_ln_b=a_ln_b,
                   b_w_pool=b_w_pool, b_scale=b_scale, odd_w_in=odd_w_in, odd_w_out=odd_w_out, c_w_dw=c_w_dw,
                   c_b_dw=c_b_dw, c_ln_g=c_ln_g, c_ln_b=c_ln_b, d_w_dw=d_w_dw, norm_mix_g=norm_mix_g,
                   norm_ffn_g=norm_ffn_g, ffn_w_gate=ffn_w_gate, ffn_w_up=ffn_w_up, ffn_w_down=ffn_w_down,
                   final_norm_g=final_norm_g)
    m_in = dict(even_w_in=m_even_w_in, even_w_out=m_even_w_out, a_w_s=m_a_w_s, a_b_s=m_a_b_s, a_ln_g=m_a_ln_g,
                a_ln_b=m_a_ln_b, b_w_pool=m_b_w_pool, b_scale=m_b_scale, odd_w_in=m_odd_w_in, odd_w_out=m_odd_w_out,
                c_w_dw=m_c_w_dw, c_b_dw=m_c_b_dw, c_ln_g=m_c_ln_g, c_ln_b=m_c_ln_b, d_w_dw=m_d_w_dw,
                norm_mix_g=m_norm_mix_g, norm_ffn_g=m_norm_ffn_g, ffn_w_gate=m_ffn_w_gate, ffn_w_up=m_ffn_w_up,
                ffn_w_down=m_ffn_w_down, final_norm_g=m_final_norm_g)
    v_in = dict(even_w_in=v_even_w_in, even_w_out=v_even_w_out, a_w_s=v_a_w_s, a_b_s=v_a_b_s, a_ln_g=v_a_ln_g,
                a_ln_b=v_a_ln_b, b_w_pool=v_b_w_pool, b_scale=v_b_scale, odd_w_in=v_odd_w_in, odd_w_out=v_odd_w_out,
                c_w_dw=v_c_w_dw, c_b_dw=v_c_b_dw, c_ln_g=v_c_ln_g, c_ln_b=v_c_ln_b, d_w_dw=v_d_w_dw,
                norm_mix_g=v_norm_mix_g, norm_ffn_g=v_norm_ffn_g, ffn_w_gate=v_ffn_w_gate, ffn_w_up=v_ffn_w_up,
                ffn_w_down=v_ffn_w_down, final_norm_g=v_final_norm_g)
    names = list(weights)

    group_parts = {
        "even": [even_w_in[0].T, even_w_out[0]],
        "ffn0": [ffn_w_gate[0].T, ffn_w_up[0].T, ffn_w_down[0]],
        "odd": [odd_w_in[0].T, odd_w_out[0]],
        "ffn1": [ffn_w_gate[1].T, ffn_w_up[1].T, ffn_w_down[1]],
    }

    def gather_jobs(*groups):
        return [_all_gather_job(p.astype(bf16)) for k in groups for p in group_parts[k]]

    def whole(gathered):
        return [g.reshape(-1, D_MODEL) for g in gathered]

    conv_names = ["c_w_dw", "c_b_dw", "c_ln_g", "c_ln_b", "d_w_dw"]
    conv_rows = [C_KERNEL, 1, 1, 1, D_KERNEL]
    conv_local = jnp.concatenate([weights[n].reshape(r, -1) for n, r in zip(conv_names, conv_rows)]
                                 + [jnp.zeros((3, c_b_dw.shape[-1]), f32)], axis=0)
    *even_gathered, conv_all = _run_jobs(gather_jobs("even") + [_all_gather_job(conv_local)], "gather_even_conv")
    w_in_e, w_out_e = whole(even_gathered)
    conv_all = conv_all.transpose(1, 0, 2).reshape(conv_local.shape[0], -1)
    conv_offs = [sum(conv_rows[:k]) for k in range(len(conv_rows) + 1)]
    cw, cb, clg, clb, dw = [conv_all[conv_offs[k]:conv_offs[k + 1]] for k in range(len(conv_rows))]

    ws, bst = a_w_s[0], a_b_s[0].T
    lng, lnb, wp, sc = a_ln_g, a_ln_b, b_w_pool[0], b_scale
    gmix = [norm_mix_g[l:l + 1] for l in range(2)]
    gffn = [norm_ffn_g[l:l + 1] for l in range(2)]
    gfin = final_norm_g.reshape(1, D_MODEL)

    h0 = x[0]
    h1, hn_e, za, pooled, mix_e, *ffn0_gathered = _even_fwd(
        h0, w_in_e, w_out_e, ws, bst, lng, lnb, wp, sc, gmix[0], jobs=gather_jobs("ffn0"))
    w_gate0, w_up0, w_down0 = whole(ffn0_gathered)
    h2, hn_f0, gate0, up0, *rest_gathered = _ffn_fwd(h1, w_gate0, w_up0, w_down0, gffn[0],
                                                     jobs=gather_jobs("odd", "ffn1"))
    w_in_o, w_out_o, w_gate1, w_up1, w_down1 = whole(rest_gathered)
    h3, hn_o, z_o, mix_o, cv_o = _odd_fwd(h2, w_in_o, w_out_o, cw, cb, clg, clb, dw, gmix[1])
    dh4, hn_f1, gate1, up1, loss_local, g_final = _ffn_fwd(h3, w_gate1, w_up1, w_down1, gffn[1],
                                                           head=(loss_target[0], gfin))

    c_idx = lax.axis_index("c").astype(jnp.int32).reshape(1)

    def weight_grad(x, y, name, jobs=()):
        g, *job_results = _weight_grads([(x, y)], name, jobs=jobs)
        return [_pack_rows(g)] + job_results

    def siblings(parts):
        return [_sibling_exchange_job(p) for p in parts]

    def chips(pairs):
        return [_chip_exchange_job(p) for p in pairs]

    dh3, dgate1, dup1, act1, g_ffn1 = _ffn_bwd(dh4, h3, gate1, up1, w_gate1, w_up1, w_down1, gffn[1])
    part_ffn1 = (weight_grad(dgate1, hn_f1, "dw_gate1") + weight_grad(dup1, hn_f1, "dw_up1")
                 + weight_grad(act1, dh4, "dw_down1"))
    dh2, dz_o, g_cw, g_cb, g_clg, g_clb, g_dw, g_mix1, *recv_ffn1 = _odd_bwd(
        dh3, h2, z_o, cv_o, w_in_o, w_out_o, cw, clg, clb, dw, gmix[1], jobs=siblings(part_ffn1))
    pair_ffn1 = _pair_sum(part_ffn1, recv_ffn1, c_idx, "pair_sum_ffn1")
    part_odd = weight_grad(dz_o, hn_o, "dw_odd_in") + weight_grad(mix_o, dh3, "dw_odd_out")
    dh1, dgate0, dup0, act0, g_ffn0, *exchanged = _ffn_bwd(
        dh2, h1, gate0, up0, w_gate0, w_up0, w_down0, gffn[0], jobs=chips(pair_ffn1) + siblings(part_odd))
    chips_ffn1, recv_odd = exchanged[:3], exchanged[3:]
    pair_odd = _pair_sum(part_odd, recv_odd, c_idx, "pair_sum_odd")
    dw_gate0, *chips_odd = weight_grad(dgate0, hn_f0, "dw_gate0", jobs=chips(pair_odd))
    part_ffn0 = [dw_gate0] + weight_grad(dup0, hn_f0, "dw_up0") + weight_grad(act0, dh2, "dw_down0")
    part_even_out, *recv_ffn0 = weight_grad(mix_e, dh1, "dw_even_out", jobs=siblings(part_ffn0))
    pair_ffn0 = _pair_sum(part_ffn0, recv_ffn0, c_idx, "pair_sum_ffn0")
    dh0, dz_e, g_ws, g_bs, g_lng, g_lnb, g_wp, g_sc, g_mix0, *exchanged = _even_bwd(
        dh1, h0, za, pooled, w_in_e, w_out_e, ws, bst, lng, lnb, wp, sc, gmix[0],
        jobs=chips(pair_ffn0) + siblings([part_even_out]))
    chips_ffn0, recv_even_out = exchanged[:3], exchanged[3:]
    pair_even_out = _pair_sum([part_even_out], recv_even_out, c_idx, "pair_sum_even_out")

    lanes = HEAD
    small = [("a_w_s", g_ws), ("a_b_s", g_bs), ("a_ln_g", g_lng), ("a_ln_b", g_lnb), ("b_w_pool", g_wp),
             ("b_scale", g_sc), ("norm_mix_g", jnp.concatenate([g_mix0, g_mix1], axis=0)),
             ("norm_ffn_g", jnp.concatenate([g_ffn0, g_ffn1], axis=0)), ("final_norm_g", g_final),
             ("c_w_dw", g_cw), ("c_b_dw", g_cb), ("c_ln_g", g_clg), ("c_ln_b", g_clb), ("d_w_dw", g_dw),
             ("loss", loss_local)]
    small_rows = [-(-g.size // (8 * lanes)) * 8 for _, g in small]
    small_offs = [sum(small_rows[:k]) for k in range(len(small) + 1)]
    pad_rows = -small_offs[-1] % 256
    small_buf = jnp.concatenate(
        [jnp.pad(g.reshape(-1), (0, r * lanes - g.size)).reshape(r, lanes) for (_, g), r in zip(small, small_rows)]
        + [jnp.zeros((pad_rows, lanes), f32)], axis=0)
    part_even_in, small_all, chips_even_out = weight_grad(
        dz_e, hn_e, "dw_even_in", jobs=[_all_gather_job(small_buf)] + chips(pair_even_out))
    small_sum = _sum_leading(small_all, "small_grad_sum")
    recv_even_in = _run_jobs(siblings([part_even_in]), "sibling_exchange_even_in")
    chips_even_in = _run_jobs(chips(_pair_sum([part_even_in], recv_even_in, c_idx, "pair_sum_even_in")),
                              "chip_exchange_even_in")
    grads = {}
    for k, (n, g) in enumerate(small):
        grads[n] = small_sum[small_offs[k]:small_offs[k + 1]].reshape(-1)[:g.size].reshape(g.shape)
    me = 4 * lax.axis_index("x") + 2 * lax.axis_index("y") + lax.axis_index("c")
    shard = c_b_dw.shape[-1]
    for n in conv_names:
        grads[n] = lax.dynamic_slice_in_dim(grads[n], me * shard, shard, axis=1)

    col_sharded = ("even_w_in", "odd_w_in", "ffn_w_gate", "ffn_w_up")

    def rows_view(n, a):
        return jnp.swapaxes(a, -1, -2) if n in col_sharded else a

    loss = grads.pop("loss")[0, 0]
    chip_parts = {"even_w_in": chips_even_in, "even_w_out": [chips_even_out],
                  "odd_w_in": chips_odd[:1], "odd_w_out": chips_odd[1:]}
    for k, n in enumerate(["ffn_w_gate", "ffn_w_up", "ffn_w_down"]):
        chip_parts[n] = [chips_ffn0[k], chips_ffn1[k]]

    delta, new_m, new_v = {}, {}, {}
    for n in names:
        w_rows, m_rows, v_rows = [rows_view(n, a) for a in (weights[n], m_in[n], v_in[n])]
        if n in chip_parts:
            outs = _adamw_reduced(w_rows, chip_parts[n], m_rows, v_rows, "adamw_" + n)
        else:
            view = (-1, w_rows.shape[-1])
            outs = [grads[n], *_adamw(w_rows.reshape(view), grads[n].reshape(view), m_rows.reshape(view),
                                      v_rows.reshape(view), "adamw_" + n)]
        grads[n], delta[n], new_m[n], new_v[n] = [rows_view(n, o.reshape(w_rows.shape)) for o in outs]

    return (loss, dh0[None], *[grads[n] for n in names], *[delta[n] for n in names],
            *[new_m[n] for n in names], *[new_v[n] for n in names])
```

```python
import jax
import jax.numpy as jnp
from jax import lax
from jax.experimental import pallas as pl
from jax.experimental.pallas import tpu as pltpu

f32 = jnp.float32
bf16 = jnp.bfloat16

EPS = 1e-6
D_MODEL = 1024
A_WIDTH = 512
HEAD = 128
N_HEADS = 4
CHUNK = 64
POOL_WINDOWS = (2, 4, 8, 16)
POOL_HALO = 16
C_KERNEL = 31
D_KERNEL = 3
CONV_HALO = 32
D_FF = 2816
N_DEV = 8
N_CHIP = 4

ADAM_LR = 0.001
ADAM_B1 = 0.9
ADAM_B2 = 0.999
ADAM_EPS = 1e-08
ADAM_WD = 0.01
ADAM_STEP = 10

MIX_TILE = 512
ODD_FWD_TILE = 1024
FFN_TILE = 256
FFN_CHUNKS = (1536, 1280)
DW_TK = 2048
DW_TM = 1536
MIDDLE_AT, MIDDLE_OF = 7, 8
VMEM_LIMIT = 56 * 1024 * 1024

MESH = pl.DeviceIdType.MESH
ANY = pl.BlockSpec(memory_space=pl.ANY)


def _params(n_axes):
    return pltpu.CompilerParams(dimension_semantics=("arbitrary",) * n_axes, vmem_limit_bytes=VMEM_LIMIT)


def _mm(a, b):
    return jnp.dot(a, b, preferred_element_type=f32)


def _mm_nt(a, b):
    return lax.dot_general(a, b, (((1,), (1,)), ((), ())), preferred_element_type=f32)


def _mm_tn(a, b):
    return lax.dot_general(a, b, (((0,), (0,)), ((), ())), preferred_element_type=f32)


def _sigmoid(x):
    return 1.0 / (1.0 + jnp.exp(-x))


def _rms_r(h):
    return lax.rsqrt(jnp.mean(h * h, axis=-1, keepdims=True) + EPS)


def _rms_bwd(dy, h, g):
    r = _rms_r(h)
    xh = h * r
    dxh = dy * g
    dh = r * (dxh - xh * jnp.mean(dxh * xh, axis=-1, keepdims=True))
    return dh, jnp.sum(dy * xh, axis=0, keepdims=True)


def _ln_fwd(x, g, b):
    mu = jnp.mean(x, axis=-1, keepdims=True)
    xc = x - mu
    r = lax.rsqrt(jnp.mean(xc * xc, axis=-1, keepdims=True) + EPS)
    xh = xc * r
    return xh * g + b, xh, r


def _ln_bwd(dy, xh, r, g):
    dxh = dy * g
    return r * (dxh - jnp.mean(dxh, axis=-1, keepdims=True) - xh * jnp.mean(dxh * xh, axis=-1, keepdims=True))


_GELU_C = 0.7978845608028654
_GELU_A = 0.044715


def _gelu(x):
    th = jnp.tanh(x * (_GELU_C + (_GELU_C * _GELU_A) * (x * x)))
    half = 0.5 * x
    return half + half * th, th


def _gelu_grad(x, th):
    return 0.5 + 0.5 * th + (1.0 - th * th) * (x * (0.5 * _GELU_C + (1.5 * _GELU_C * _GELU_A) * (x * x)))


def _down(x, k):
    return x if k == 0 else pltpu.roll(x, k, 0)


def _up(x, k):
    return x if k == 0 else pltpu.roll(x, x.shape[0] - k, 0)


def _window_sum(x, win, shift):
    s = x
    step = 1
    while step < win:
        s = s + shift(s, step)
        step *= 2
    return s


def _inv_count(t0, rows, win):
    t = t0 + lax.broadcasted_iota(jnp.int32, (rows, 1), 0)
    return 1.0 / jnp.minimum(t + 1, win).astype(f32)


def _chunk_mask():
    i = lax.broadcasted_iota(jnp.int32, (HEAD, HEAD), 0)
    j = lax.broadcasted_iota(jnp.int32, (HEAD, HEAD), 1)
    return jnp.logical_or(i >= CHUNK, j < CHUNK)


def _const(shape, n_axes):
    zeros = (0,) * len(shape)
    if n_axes == 1:
        return pl.BlockSpec(shape, lambda i: zeros)
    return pl.BlockSpec(shape, lambda i, j: zeros)


def _prev_halo(tile, halo, cols):
    return pl.BlockSpec((halo, cols), lambda i: (jnp.maximum(i * (tile // halo) - 1, 0), 0))


def _next_halo(tile, halo, cols, seq):
    return pl.BlockSpec((halo, cols), lambda i: (jnp.minimum((i + 1) * (tile // halo), seq // halo - 1), 0))


class _Job:
    def __init__(self, inputs, out_shape, sems, hooks):
        self.inputs, self.out_shape, self.sems, self.hooks = inputs, out_shape, sems, hooks


def _position():
    return lax.axis_index("x"), lax.axis_index("y"), lax.axis_index("c")


def _all_gather_job(block):
    rows, cols = block.shape

    def hooks(ins, outs, sems):
        (x_ref,), (out_ref,), (send_sems, recv_sems, local_sem) = ins, outs, sems
        x, y, c = _position()
        me, sibling = (x, y, c), (x, y, 1 - c)
        chips = [(1 - x, y), (x, 1 - y), (1 - x, 1 - y)]

        def slot(px, py, pc):
            return out_ref.at[4 * px + 2 * py + pc]

        def copy(k, block_of, to, src=None):
            return pltpu.make_async_remote_copy(
                src_ref=slot(*block_of) if src is None else src, dst_ref=slot(*block_of),
                send_sem=send_sems.at[k], recv_sem=recv_sems.at[k], device_id=to, device_id_type=MESH)

        mine = pltpu.make_async_copy(x_ref, slot(*me), local_sem)
        first = [copy(0, me, sibling, src=x_ref)]
        first += [copy(1 + j, me, (*chip, c), src=x_ref) for j, chip in enumerate(chips)]
        passed = [copy(4 + j, (*chip, c), sibling) for j, chip in enumerate(chips)]

        def start():
            mine.start()
            for cp in first:
                cp.start()

        def middle():
            for j, chip in enumerate(chips):
                copy(1 + j, (*chip, c), me).wait_recv()
                passed[j].start()

        def finish():
            copy(0, sibling, me).wait_recv()
            for j, chip in enumerate(chips):
                copy(4 + j, (*chip, 1 - c), me).wait_recv()
            for cp in first + passed:
                cp.wait_send()
            mine.wait()

        return start, middle, finish

    return _Job([block], [jax.ShapeDtypeStruct((N_DEV, rows, cols), block.dtype)],
                [pltpu.SemaphoreType.DMA((7,)), pltpu.SemaphoreType.DMA((7,)), pltpu.SemaphoreType.DMA], hooks)


def _sibling_exchange_job(g):
    _, rows, cols = g.shape

    def hooks(ins, outs, sems):
        (g_ref,), (recv_ref,), (send_sems, recv_sems) = ins, outs, sems
        x, y, c = _position()
        copies = [pltpu.make_async_remote_copy(
            src_ref=g_ref.at[2 * k + (1 - c)], dst_ref=recv_ref.at[k], send_sem=send_sems.at[k],
            recv_sem=recv_sems.at[k], device_id=(x, y, 1 - c), device_id_type=MESH) for k in range(N_CHIP)]

        def start():
            for cp in copies:
                cp.start()

        def finish():
            for cp in copies:
                cp.wait()

        return start, lambda: None, finish

    return _Job([g], [jax.ShapeDtypeStruct((N_CHIP, rows, cols), g.dtype)],
                [pltpu.SemaphoreType.DMA((N_CHIP,)), pltpu.SemaphoreType.DMA((N_CHIP,))], hooks)


def _chip_exchange_job(p):
    _, rows, cols = p.shape

    def hooks(ins, outs, sems):
        (p_ref,), (recv_ref,), (send_sems, recv_sems, local_sem) = ins, outs, sems
        x, y, c = _position()
        k_me = 2 * x + y
        mine = pltpu.make_async_copy(p_ref.at[k_me], recv_ref.at[k_me], local_sem)
        copies = [pltpu.make_async_remote_copy(
            src_ref=p_ref.at[2 * px + py], dst_ref=recv_ref.at[k_me], send_sem=send_sems.at[j],
            recv_sem=recv_sems.at[j], device_id=(px, py, c), device_id_type=MESH)
            for j, (px, py) in enumerate([(1 - x, y), (x, 1 - y), (1 - x, 1 - y)])]

        def start():
            mine.start()
            for cp in copies:
                cp.start()

        def finish():
            for cp in copies:
                cp.wait()
            mine.wait()

        return start, lambda: None, finish

    return _Job([p], [jax.ShapeDtypeStruct((N_CHIP, rows, cols), p.dtype)],
                [pltpu.SemaphoreType.DMA((3,)), pltpu.SemaphoreType.DMA((3,)), pltpu.SemaphoreType.DMA], hooks)


def _job_hooks(jobs, ins, outs, sems):
    hooks = []
    for job in jobs:
        n_in, n_out, n_sem = len(job.inputs), len(job.out_shape), len(job.sems)
        hooks.append(job.hooks(ins[:n_in], outs[:n_out], sems[:n_sem]))
        ins, outs, sems = ins[n_in:], outs[n_out:], sems[n_sem:]
    return hooks


def _run_jobs(jobs, name):
    n_in = sum(len(job.inputs) for job in jobs)
    n_out = sum(len(job.out_shape) for job in jobs)

    def body(*refs):
        hooks = _job_hooks(jobs, refs[:n_in], refs[n_in:n_in + n_out], refs[n_in + n_out:])
        for phase in range(3):
            for h in hooks:
                h[phase]()

    return list(pl.pallas_call(
        body, name=name, in_specs=[ANY] * n_in, out_specs=[ANY] * n_out,
        out_shape=[s for job in jobs for s in job.out_shape],
        scratch_shapes=[s for job in jobs for s in job.sems],
    )(*[a for job in jobs for a in job.inputs]))


def _launch(body, *, name, grid, in_specs, out_specs, out_shape, args, scratch=(), jobs=()):
    in_specs, out_specs, out_shape, scratch = list(in_specs), list(out_specs), list(out_shape), list(scratch)
    if not jobs:
        return list(pl.pallas_call(body, name=name, grid=grid, in_specs=in_specs, out_specs=out_specs,
                                   out_shape=out_shape, scratch_shapes=scratch,
                                   compiler_params=_params(len(grid)))(*args))
    n_in, n_out, n_sc = len(in_specs), len(out_specs), len(scratch)
    j_in = [a for job in jobs for a in job.inputs]
    j_out = [s for job in jobs for s in job.out_shape]
    j_sems = [s for job in jobs for s in job.sems]
    n_steps = 1
    for g in grid:
        n_steps *= g

    def wrapped(*refs):
        ins, refs = refs[:n_in], refs[n_in:]
        jins, refs = refs[:len(j_in)], refs[len(j_in):]
        outs, refs = refs[:n_out], refs[n_out:]
        jouts, refs = refs[:len(j_out)], refs[len(j_out):]
        sc, jsems = refs[:n_sc], refs[n_sc:]
        step = pl.program_id(0)
        for axis in range(1, len(grid)):
            step = step * grid[axis] + pl.program_id(axis)
        hooks = _job_hooks(jobs, jins, jouts, jsems)

        @pl.when(step == 0)
        def _():
            for h in hooks:
                h[0]()

        body(*ins, *outs, *sc)

        @pl.when(step == (MIDDLE_AT * n_steps) // MIDDLE_OF)
        def _():
            for h in hooks:
                h[1]()

        @pl.when(step == n_steps - 1)
        def _():
            for h in hooks:
                h[2]()

    return list(pl.pallas_call(
        wrapped, name=name, grid=grid, in_specs=in_specs + [ANY] * len(j_in), out_specs=out_specs + [ANY] * len(j_out),
        out_shape=out_shape + j_out, scratch_shapes=scratch + j_sems, compiler_params=_params(len(grid)),
    )(*args, *j_in))


def _gmlp_gate(vnb, wsm, bst, tile):
    rows = []
    for n in range(tile // HEAD):
        cols = []
        for hh in range(N_HEADS):
            blk = vnb[n * HEAD:(n + 1) * HEAD, hh * HEAD:(hh + 1) * HEAD]
            cols.append(_mm(wsm[hh], blk) + bst[:, hh:hh + 1])
        rows.append(jnp.concatenate(cols, axis=1))
    return jnp.concatenate(rows, axis=0)


def _even_fwd(h, w_in, w_out, ws, bst, lng, lnb, wp, sc, gm, jobs=()):
    seq = h.shape[0]
    tile = min(MIX_TILE, seq)
    n_tiles = seq // tile

    def body(h_ref, hp_ref, win_ref, wout_ref, ws_ref, bst_ref, lng_ref, lnb_ref, wp_ref, sc_ref, g_ref,
             ho_ref, hn_ref, za_ref, pool_ref, mix_ref):
        i = pl.program_id(0)
        g = g_ref[...]
        h = h_ref[...]
        hnb = (h * _rms_r(h) * g).astype(bf16)
        hn_ref[...] = hnb
        z = _mm_nt(hnb, win_ref[...])
        zab = z[:, :2 * A_WIDTH].astype(bf16)
        za_ref[...] = zab
        hp = hp_ref[...]
        zbp = _mm_nt((hp * _rms_r(hp) * g).astype(bf16), win_ref[2 * A_WIDTH:, :])
        zbe = jnp.concatenate([jnp.where(i > 0, zbp, 0.0), z[:, 2 * A_WIDTH:]], axis=0)
        pooled = []
        for gi, win in enumerate(POOL_WINDOWS):
            xg = zbe[:, gi * HEAD:(gi + 1) * HEAD]
            s = _window_sum(xg, win, _down)
            pooled.append(s[POOL_HALO:] * _inv_count(i * tile, tile, win) - xg[POOL_HALO:])
        plb = jnp.concatenate(pooled, axis=1).astype(bf16)
        pool_ref[...] = plb

        ga, _ = _gelu(zab.astype(f32))
        vn, _, _ = _ln_fwd(ga[:, A_WIDTH:], lng_ref[...], lnb_ref[...])
        mask = _chunk_mask()
        wsm = [jnp.where(mask, ws_ref[hh], 0.0).astype(bf16) for hh in range(N_HEADS)]
        ya = ga[:, :A_WIDTH] * _gmlp_gate(vn.astype(bf16), wsm, bst_ref[...], tile)
        yb = jnp.concatenate([_mm(plb[:, gi * HEAD:(gi + 1) * HEAD], wp_ref[gi].astype(bf16))
                              for gi in range(len(POOL_WINDOWS))], axis=1) * sc_ref[...]
        mix = jnp.concatenate([ya, yb], axis=1).astype(bf16)
        mix_ref[...] = mix
        ho_ref[...] = h + _mm(mix, wout_ref[...])

    row = lambda cols: pl.BlockSpec((tile, cols), lambda i: (i, 0))
    return _launch(
        body, name="even_fwd", grid=(n_tiles,), jobs=jobs,
        in_specs=[row(D_MODEL), _prev_halo(tile, POOL_HALO, D_MODEL), _const(w_in.shape, 1), _const(w_out.shape, 1),
                  _const(ws.shape, 1), _const(bst.shape, 1), _const(lng.shape, 1), _const(lnb.shape, 1),
                  _const(wp.shape, 1), _const(sc.shape, 1), _const(gm.shape, 1)],
        out_specs=[row(D_MODEL), row(D_MODEL), row(2 * A_WIDTH), row(A_WIDTH), row(D_MODEL)],
        out_shape=[jax.ShapeDtypeStruct((seq, D_MODEL), f32), jax.ShapeDtypeStruct((seq, D_MODEL), bf16),
                   jax.ShapeDtypeStruct((seq, 2 * A_WIDTH), bf16), jax.ShapeDtypeStruct((seq, A_WIDTH), bf16),
                   jax.ShapeDtypeStruct((seq, D_MODEL), bf16)],
        args=(h, h, w_in, w_out, ws, bst, lng, lnb, wp, sc, gm))


def _even_bwd(dh, h, za, pooled, w_in, w_out, ws, bst, lng, lnb, wp, sc, gm, jobs=()):
    seq = h.shape[0]
    tile = min(MIX_TILE, seq)
    n_tiles = seq // tile
    n_groups = len(POOL_WINDOWS)

    def body(dh_ref, dhx_ref, h_ref, za_ref, pool_ref, win_ref, wout_ref, ws_ref, bst_ref, lng_ref, lnb_ref,
             wp_ref, sc_ref, g_ref,
             dhi_ref, dz_ref, dws_ref, dbs_ref, dlng_ref, dlnb_ref, dwp_ref, dsc_ref, dg_ref):
        i = pl.program_id(0)

        @pl.when(i == 0)
        def _():
            for ref in (dws_ref, dbs_ref, dlng_ref, dlnb_ref, dwp_ref, dsc_ref, dg_ref):
                ref[...] = jnp.zeros_like(ref)

        dh = dh_ref[...]
        dmix = _mm_nt(dh.astype(bf16), wout_ref[...])
        dya = dmix[:, :A_WIDTH]
        dyb = dmix[:, A_WIDTH:]
        dybx = _mm_nt(dhx_ref[...].astype(bf16), wout_ref[A_WIDTH:, :])
        dybx = jnp.where(i < n_tiles - 1, dybx, 0.0)

        za = za_ref[...].astype(f32)
        ga, th = _gelu(za)
        u = ga[:, :A_WIDTH]
        lng = lng_ref[...]
        vn, vh, r = _ln_fwd(ga[:, A_WIDTH:], lng, lnb_ref[...])
        vnb = vn.astype(bf16)
        mask = _chunk_mask()
        wsf = [jnp.where(mask, ws_ref[hh], 0.0) for hh in range(N_HEADS)]
        sv = _gmlp_gate(vnb, [w.astype(bf16) for w in wsf], bst_ref[...], tile)
        du = dya * sv
        dsvb = (dya * u).astype(bf16)
        wst = [w.T.astype(bf16) for w in wsf]
        ones = jnp.ones((8, HEAD), bf16)
        dws = [jnp.zeros((HEAD, HEAD), f32) for _ in range(N_HEADS)]
        dbs = [jnp.zeros((8, HEAD), f32) for _ in range(N_HEADS)]
        rows = []
        for n in range(tile // HEAD):
            cols = []
            for hh in range(N_HEADS):
                blk = dsvb[n * HEAD:(n + 1) * HEAD, hh * HEAD:(hh + 1) * HEAD]
                cols.append(_mm(wst[hh], blk))
                dws[hh] = dws[hh] + _mm_nt(blk, vnb[n * HEAD:(n + 1) * HEAD, hh * HEAD:(hh + 1) * HEAD])
                dbs[hh] = dbs[hh] + _mm_nt(ones, blk)
            rows.append(jnp.concatenate(cols, axis=1))
        dvn = jnp.concatenate(rows, axis=0)
        for hh in range(N_HEADS):
            dws_ref[hh] += jnp.where(mask, dws[hh], 0.0)
            dbs_ref[pl.ds(hh, 1), :] += dbs[hh][0:1, :]
        dlng_ref[...] += jnp.sum(dvn * vh, axis=0, keepdims=True)
        dlnb_ref[...] += jnp.sum(dvn, axis=0, keepdims=True)
        dv = _ln_bwd(dvn, vh, r, lng)
        dza = jnp.concatenate([du, dv], axis=1) * _gelu_grad(za, th)

        plb = pool_ref[...]
        sc = sc_ref[...]
        dzb = []
        dsc = []
        for gi, win in enumerate(POOL_WINDOWS):
            cs = slice(gi * HEAD, (gi + 1) * HEAD)
            wpb = wp_ref[gi].astype(bf16)
            dsc.append(jnp.sum(dyb[:, cs] * _mm(plb[:, cs], wpb), axis=0, keepdims=True))
            dpre = (dyb[:, cs] * sc[:, cs]).astype(bf16)
            dprex = (dybx[:, cs] * sc[:, cs]).astype(bf16)
            dwp_ref[gi] += _mm_tn(plb[:, cs], dpre)
            dpl = _mm_nt(dpre, wpb)
            dple = jnp.concatenate([dpl, _mm_nt(dprex, wpb)], axis=0)
            q = dple * _inv_count(i * tile, tile + POOL_HALO, win)
            dzb.append(_window_sum(q, win, _up)[:tile] - dpl)
        dsc_ref[...] += jnp.concatenate(dsc, axis=1)

        dzf = jnp.concatenate([dza] + dzb, axis=1).astype(bf16)
        dz_ref[...] = dzf
        dhn = _mm(dzf, win_ref[...])
        dhr, dg = _rms_bwd(dhn, h_ref[...], g_ref[...])
        dhi_ref[...] = dh + dhr
        dg_ref[...] += dg

    row = lambda cols: pl.BlockSpec((tile, cols), lambda i: (i, 0))
    small = [ws.shape, (N_HEADS, HEAD), lng.shape, lnb.shape, wp.shape, sc.shape, gm.shape]
    return _launch(
        body, name="even_bwd", grid=(n_tiles,), jobs=jobs,
        in_specs=[row(D_MODEL), _next_halo(tile, POOL_HALO, D_MODEL, seq), row(D_MODEL), row(2 * A_WIDTH), row(A_WIDTH),
                  _const(w_in.shape, 1), _const(w_out.shape, 1), _const(ws.shape, 1), _const(bst.shape, 1),
                  _const(lng.shape, 1), _const(lnb.shape, 1), _const(wp.shape, 1), _const(sc.shape, 1), _const(gm.shape, 1)],
        out_specs=[row(D_MODEL), row(3 * A_WIDTH)] + [_const(s, 1) for s in small],
        out_shape=[jax.ShapeDtypeStruct((seq, D_MODEL), f32), jax.ShapeDtypeStruct((seq, 3 * A_WIDTH), bf16)]
                  + [jax.ShapeDtypeStruct(s, f32) for s in small],
        args=(dh, dh, h, za, pooled, w_in, w_out, ws, bst, lng, lnb, wp, sc, gm))


SUBLANES = 8


class _Shifted:
    def __init__(self, x, shift, max_shift):
        self.rolled = [shift(x, b) for b in range(min(SUBLANES, max_shift + 1))]
        self.back = shift is _down

    def rows(self, k, start, count):
        whole = k - k % SUBLANES
        lo = start - whole if self.back else start + whole
        return self.rolled[k % SUBLANES][lo:lo + count]


def _conv_taps(xs, w_ref, n_taps, halo, rows):
    acc = None
    for j in range(n_taps):
        term = w_ref[pl.ds(j, 1), :] * xs.rows(n_taps - 1 - j, halo, rows)
        acc = term if acc is None else acc + term
    return acc


def _odd_fwd(h, w_in, w_out, cw, cb, clg, clb, dw, gm):
    seq = h.shape[0]
    tile = min(ODD_FWD_TILE, seq)
    n_tiles = seq // tile
    w = A_WIDTH

    def body(h_ref, hp_ref, win_ref, wout_ref, cw_ref, cb_ref, clg_ref, clb_ref, dw_ref, g_ref,
             ho_ref, hn_ref, z_ref, mix_ref, cv_ref):
        i = pl.program_id(0)
        g = g_ref[...]
        h = h_ref[...]
        hnb = (h * _rms_r(h) * g).astype(bf16)
        hn_ref[...] = hnb
        zb = _mm_nt(hnb, win_ref[...]).astype(bf16)
        z_ref[...] = zb
        hp = hp_ref[...]
        zp = _mm_nt((hp * _rms_r(hp) * g).astype(bf16), win_ref[...]).astype(bf16).astype(f32)
        z = zb.astype(f32)
        ze = jnp.concatenate([jnp.where(i > 0, zp, 0.0), z], axis=0)
        hc = ze[:, :w] * _sigmoid(ze[:, w:2 * w])
        cv = _conv_taps(_Shifted(hc, _down, C_KERNEL - 1), cw_ref, C_KERNEL, CONV_HALO, tile) + cb_ref[...]
        cv_ref[...] = cv
        ln, _, _ = _ln_fwd(cv, clg_ref[...], clb_ref[...])
        yc = ln * _sigmoid(ln)
        p = ze[:, 3 * w:4 * w] * ze[:, 4 * w:]
        yd = z[:, 2 * w:3 * w] * _conv_taps(_Shifted(p, _down, D_KERNEL - 1), dw_ref, D_KERNEL, CONV_HALO, tile)
        mix = jnp.concatenate([yc, yd], axis=1).astype(bf16)
        mix_ref[...] = mix
        ho_ref[...] = h + _mm(mix, wout_ref[...])

    row = lambda cols: pl.BlockSpec((tile, cols), lambda i: (i, 0))
    return pl.pallas_call(
        body, name="odd_fwd", grid=(n_tiles,),
        in_specs=[row(D_MODEL), _prev_halo(tile, CONV_HALO, D_MODEL), _const(w_in.shape, 1), _const(w_out.shape, 1),
                  _const(cw.shape, 1), _const(cb.shape, 1), _const(clg.shape, 1), _const(clb.shape, 1),
                  _const(dw.shape, 1), _const(gm.shape, 1)],
        out_specs=[row(D_MODEL), row(D_MODEL), row(5 * w), row(D_MODEL), row(w)],
        out_shape=[jax.ShapeDtypeStruct((seq, D_MODEL), f32), jax.ShapeDtypeStruct((seq, D_MODEL), bf16),
                   jax.ShapeDtypeStruct((seq, 5 * w), bf16), jax.ShapeDtypeStruct((seq, D_MODEL), bf16),
                   jax.ShapeDtypeStruct((seq, w), f32)],
        compiler_params=_params(1),
    )(h, h, w_in, w_out, cw, cb, clg, clb, dw, gm)


def _odd_bwd(dh, h, z, cv, w_in, w_out, cw, clg, clb, dw, gm, jobs=()):
    seq = h.shape[0]
    tile = min(MIX_TILE, seq)
    n_tiles = seq // tile
    w = A_WIDTH
    halo = CONV_HALO

    def body(dh_ref, dhx_ref, h_ref, z_ref, zp_ref, zx_ref, cv_ref, cvx_ref, win_ref, wout_ref, cw_ref,
             clg_ref, clb_ref, dw_ref, g_ref,
             dhi_ref, dz_ref, dcw_ref, dcb_ref, dclg_ref, dclb_ref, ddw_ref, dg_ref):
        i = pl.program_id(0)

        @pl.when(i == 0)
        def _():
            for ref in (dcw_ref, dcb_ref, dclg_ref, dclb_ref, ddw_ref, dg_ref):
                ref[...] = jnp.zeros_like(ref)

        dh = dh_ref[...]
        dhe = jnp.concatenate([dh, jnp.where(i < n_tiles - 1, dhx_ref[...], 0.0)], axis=0)
        dmix = _mm_nt(dhe.astype(bf16), wout_ref[...])
        ze = jnp.concatenate([jnp.where(i > 0, zp_ref[...].astype(f32), 0.0), z_ref[...].astype(f32),
                              zx_ref[...].astype(f32)], axis=0)

        sg = _sigmoid(ze[:, w:2 * w])
        ca = ze[:, :w]
        hc = ca * sg
        hcs = _Shifted(hc, _down, C_KERNEL - 1)
        cv = jnp.concatenate([cv_ref[...], cvx_ref[...]], axis=0)
        clg = clg_ref[...]
        ln, xh, r = _ln_fwd(cv, clg, clb_ref[...])
        sl = _sigmoid(ln)
        dln = dmix[:, :w] * (sl * (1.0 + ln * (1.0 - sl)))
        dclg_ref[...] += jnp.sum((dln * xh)[:tile], axis=0, keepdims=True)
        dclb_ref[...] += jnp.sum(dln[:tile], axis=0, keepdims=True)
        dcv = _ln_bwd(dln, xh, r, clg)
        dcb_ref[...] += jnp.sum(dcv[:tile], axis=0, keepdims=True)
        dcvs = _Shifted(dcv, _up, C_KERNEL - 1)
        dhc = None
        for j in range(C_KERNEL):
            k = C_KERNEL - 1 - j
            dcw_ref[pl.ds(j, 1), :] += jnp.sum(dcv[:tile] * hcs.rows(k, halo, tile), axis=0, keepdims=True)
            term = cw_ref[pl.ds(j, 1), :] * dcvs.rows(k, 0, tile)
            dhc = term if dhc is None else dhc + term
        sgt = sg[halo:halo + tile]
        cat = ca[halo:halo + tile]
        dca = dhc * sgt
        dcg = dhc * cat * sgt * (1.0 - sgt)

        dcgv = ze[:, 3 * w:4 * w]
        dxin = ze[:, 4 * w:]
        p = dcgv * dxin
        ps = _Shifted(p, _down, D_KERNEL - 1)
        q = _conv_taps(ps, dw_ref, D_KERNEL, halo, tile)
        dyd = dmix[:, w:]
        dq = dyd * ze[halo:, 2 * w:3 * w]
        ddbg = dyd[:tile] * q
        dqs = _Shifted(dq, _up, D_KERNEL - 1)
        dp = None
        for j in range(D_KERNEL):
            k = D_KERNEL - 1 - j
            ddw_ref[pl.ds(j, 1), :] += jnp.sum(dq[:tile] * ps.rows(k, halo, tile), axis=0, keepdims=True)
            term = dw_ref[pl.ds(j, 1), :] * dqs.rows(k, 0, tile)
            dp = term if dp is None else dp + term
        ddcg = dp * dxin[halo:halo + tile]
        ddxin = dp * dcgv[halo:halo + tile]

        dzf = jnp.concatenate([dca, dcg, ddbg, ddcg, ddxin], axis=1).astype(bf16)
        dz_ref[...] = dzf
        dhn = _mm(dzf, win_ref[...])
        dhr, dg = _rms_bwd(dhn, h_ref[...], g_ref[...])
        dhi_ref[...] = dh + dhr
        dg_ref[...] += dg

    row = lambda cols: pl.BlockSpec((tile, cols), lambda i: (i, 0))
    small = [cw.shape, clg.shape, clg.shape, clb.shape, dw.shape, gm.shape]
    return _launch(
        body, name="odd_bwd", grid=(n_tiles,), jobs=jobs,
        in_specs=[row(D_MODEL), _next_halo(tile, halo, D_MODEL, seq), row(D_MODEL), row(5 * w),
                  _prev_halo(tile, halo, 5 * w), _next_halo(tile, halo, 5 * w, seq),
                  row(w), _next_halo(tile, halo, w, seq),
                  _const(w_in.shape, 1), _const(w_out.shape, 1), _const(cw.shape, 1),
                  _const(clg.shape, 1), _const(clb.shape, 1), _const(dw.shape, 1), _const(gm.shape, 1)],
        out_specs=[row(D_MODEL), row(5 * w)] + [_const(s, 1) for s in small],
        out_shape=[jax.ShapeDtypeStruct((seq, D_MODEL), f32), jax.ShapeDtypeStruct((seq, 5 * w), bf16)]
                  + [jax.ShapeDtypeStruct(s, f32) for s in small],
        args=(dh, dh, h, z, z, z, cv, cv, w_in, w_out, cw, clg, clb, dw, gm))


def _ffn_chunks():
    assert sum(FFN_CHUNKS) == D_FF
    start = 0
    for size in FFN_CHUNKS:
        yield slice(start, start + size)
        start += size


def _ffn_fwd(h, wg, wu, wd, gm, jobs=(), head=None):
    seq = h.shape[0]
    tile = min(FFN_TILE, seq)

    def body(h_ref, g_ref, wg_ref, wu_ref, wd_ref, *refs):
        if head is None:
            ho_ref, hn_ref, gate_ref, up_ref = refs
        else:
            t_ref, gf_ref, ho_ref, hn_ref, gate_ref, up_ref, loss_ref, dgf_ref = refs
        h = h_ref[...]
        hnb = (h * _rms_r(h) * g_ref[...]).astype(bf16)
        hn_ref[...] = hnb
        acc = None
        for rows in _ffn_chunks():
            gb = _mm_nt(hnb, wg_ref[rows, :]).astype(bf16)
            ub = _mm_nt(hnb, wu_ref[rows, :]).astype(bf16)
            gate_ref[:, rows] = gb
            up_ref[:, rows] = ub
            gf = gb.astype(f32)
            act = gf * _sigmoid(gf) * ub.astype(f32)
            part = _mm(act.astype(bf16), wd_ref[rows, :])
            acc = part if acc is None else acc + part
        ho = h + acc
        if head is None:
            ho_ref[...] = ho
            return

        @pl.when(pl.program_id(0) == 0)
        def _():
            loss_ref[...] = jnp.zeros_like(loss_ref)
            dgf_ref[...] = jnp.zeros_like(dgf_ref)

        g_final = gf_ref[...]
        err = ho * _rms_r(ho) * g_final - t_ref[...]
        loss_ref[...] += (0.5 / D_MODEL) * jnp.sum(jnp.sum(err * err, axis=1, keepdims=True), axis=0, keepdims=True)
        dho, dg = _rms_bwd(err * (1.0 / D_MODEL), ho, g_final)
        ho_ref[...] = dho
        dgf_ref[...] += dg

    row = pl.BlockSpec((tile, D_MODEL), lambda i: (i, 0))
    wide = pl.BlockSpec((tile, D_FF), lambda i: (i, 0))
    in_specs = [row, _const(gm.shape, 1), _const(wg.shape, 1), _const(wu.shape, 1), _const(wd.shape, 1)]
    out_specs = [row, row, wide, wide]
    out_shape = [jax.ShapeDtypeStruct((seq, D_MODEL), f32), jax.ShapeDtypeStruct((seq, D_MODEL), bf16),
                 jax.ShapeDtypeStruct((seq, D_FF), bf16), jax.ShapeDtypeStruct((seq, D_FF), bf16)]
    args = (h, gm, wg, wu, wd)
    if head is not None:
        target, g_final = head
        in_specs += [row, _const(g_final.shape, 1)]
        out_specs += [_const((1, 1), 1), _const(g_final.shape, 1)]
        out_shape += [jax.ShapeDtypeStruct((1, 1), f32), jax.ShapeDtypeStruct(g_final.shape, f32)]
        args += (target, g_final)
    return _launch(
        body, name="ffn_fwd" if head is None else "ffn_fwd_loss", grid=(seq // tile,), jobs=jobs,
        in_specs=in_specs, out_specs=out_specs, out_shape=out_shape, args=args)


def _ffn_bwd(dh, h, gate, up, wg, wu, wd, gm, jobs=()):
    seq = h.shape[0]
    tile = min(FFN_TILE, seq)
    n_tiles = seq // tile

    def body(dh_ref, h_ref, g_ref, gate_ref, up_ref, wg_ref, wu_ref, wd_ref,
             dhi_ref, dgate_ref, dup_ref, act_ref, dg_ref):
        @pl.when(pl.program_id(0) == 0)
        def _():
            dg_ref[...] = jnp.zeros_like(dg_ref)

        dh = dh_ref[...]
        dhb = dh.astype(bf16)
        acc = None
        for rows in _ffn_chunks():
            dact = _mm_nt(dhb, wd_ref[rows, :])
            gf = gate_ref[:, rows].astype(f32)
            uf = up_ref[:, rows].astype(f32)
            s = _sigmoid(gf)
            silu = gf * s
            act_ref[:, rows] = (silu * uf).astype(bf16)
            dgb = (dact * uf * (s * (1.0 + gf * (1.0 - s)))).astype(bf16)
            dub = (dact * silu).astype(bf16)
            dgate_ref[:, rows] = dgb
            dup_ref[:, rows] = dub
            part = _mm(dgb, wg_ref[rows, :]) + _mm(dub, wu_ref[rows, :])
            acc = part if acc is None else acc + part
        dhr, dg = _rms_bwd(acc, h_ref[...], g_ref[...])
        dhi_ref[...] = dh + dhr
        dg_ref[...] += dg

    row = pl.BlockSpec((tile, D_MODEL), lambda i: (i, 0))
    wide = pl.BlockSpec((tile, D_FF), lambda i: (i, 0))
    return _launch(
        body, name="ffn_bwd", grid=(n_tiles,), jobs=jobs,
        in_specs=[row, row, _const(gm.shape, 1), wide, wide, _const(wg.shape, 1), _const(wu.shape, 1), _const(wd.shape, 1)],
        out_specs=[row, wide, wide, wide, _const(gm.shape, 1)],
        out_shape=[jax.ShapeDtypeStruct((seq, D_MODEL), f32), jax.ShapeDtypeStruct((seq, D_FF), bf16),
                   jax.ShapeDtypeStruct((seq, D_FF), bf16), jax.ShapeDtypeStruct((seq, D_FF), bf16),
                   jax.ShapeDtypeStruct(gm.shape, f32)],
        args=(dh, h, gm, gate, up, wg, wu, wd))


def _weight_grads(pairs, name, jobs=()):
    seq, m = pairs[0][0].shape
    tk = min(DW_TK, seq)
    tm = m if m <= DW_TM else m // 2
    n_k = seq // tk
    n_pairs = len(pairs)

    def body(*refs):
        x_refs = refs[0:2 * n_pairs:2]
        y_refs = refs[1:2 * n_pairs:2]
        o_refs = refs[2 * n_pairs:3 * n_pairs]
        acc_refs = refs[3 * n_pairs:]
        k = pl.program_id(1)
        @pl.when(k == 0)
        def _():
            for acc_ref in acc_refs:
                acc_ref[...] = jnp.zeros_like(acc_ref)

        for x_ref, y_ref, acc_ref in zip(x_refs, y_refs, acc_refs):
            acc_ref[...] += _mm_tn(x_ref[...].astype(bf16), y_ref[...].astype(bf16))

        @pl.when(k == n_k - 1)
        def _():
            for o_ref, acc_ref in zip(o_refs, acc_refs):
                o_ref[...] = acc_ref[...].astype(bf16)

    in_specs = []
    for _ in pairs:
        in_specs += [pl.BlockSpec((tk, tm), lambda j, k: (k, j)), pl.BlockSpec((tk, D_MODEL), lambda j, k: (k, 0))]
    return _launch(
        body, name=name, grid=(m // tm, n_k), jobs=jobs,
        in_specs=in_specs,
        out_specs=[pl.BlockSpec((tm, D_MODEL), lambda j, k: (j, 0))] * n_pairs,
        out_shape=[jax.ShapeDtypeStruct((m, D_MODEL), bf16)] * n_pairs,
        scratch=[pltpu.VMEM((tm, D_MODEL), f32)] * n_pairs,
        args=[a for pair in pairs for a in pair])


def _row_tile(rows, limit=512):
    best = rows
    for t in range(8, min(rows, limit) + 1, 8):
        if rows % t == 0:
            best = t
    return best if rows > limit else rows


def _adam_step(w, g, m, v):
    m2 = ADAM_B1 * m + (1.0 - ADAM_B1) * g
    v2 = ADAM_B2 * v + (1.0 - ADAM_B2) * (g * g)
    m_hat = m2 / (1.0 - ADAM_B1 ** ADAM_STEP)
    v_hat = v2 / (1.0 - ADAM_B2 ** ADAM_STEP)
    return -ADAM_LR * (m_hat / (jnp.sqrt(v_hat) + ADAM_EPS) + ADAM_WD * w), m2, v2


def _adamw(w, g, m, v, name):
    rows, cols = w.shape
    tr = _row_tile(rows)

    def body(w_ref, g_ref, m_ref, v_ref, d_ref, mo_ref, vo_ref):
        d_ref[...], mo_ref[...], vo_ref[...] = _adam_step(w_ref[...], g_ref[...], m_ref[...], v_ref[...])

    spec = pl.BlockSpec((tr, cols), lambda i: (i, 0))
    return pl.pallas_call(
        body, name=name, grid=(rows // tr,),
        in_specs=[spec] * 4, out_specs=[spec] * 3,
        out_shape=[jax.ShapeDtypeStruct((rows, cols), f32)] * 3,
        compiler_params=_params(1),
    )(w, g, m, v)


def _adamw_reduced(w, parts, m, v, name, jobs=()):
    layers, rows, cols = w.shape

    def body(*refs):
        w_ref, m_ref, v_ref = refs[:3]
        part_refs = refs[3:3 + layers]
        g_ref, d_ref, mo_ref, vo_ref = refs[3 + layers:]
        layer = pl.program_id(0)
        for l, p_ref in enumerate(part_refs):
            @pl.when(layer == l)
            def _():
                acc = p_ref[0].astype(f32)
                for k in range(1, N_CHIP):
                    acc = acc + p_ref[k].astype(f32)
                g_ref[0] = acc

        d_ref[0], mo_ref[0], vo_ref[0] = _adam_step(w_ref[0], g_ref[0], m_ref[0], v_ref[0])

    blk = pl.BlockSpec((1, rows, cols), lambda l: (l, 0, 0))
    return _launch(
        body, name=name, grid=(layers,), jobs=jobs,
        in_specs=[blk] * 3 + [pl.BlockSpec(p.shape, lambda l: (0, 0, 0)) for p in parts],
        out_specs=[blk] * 4,
        out_shape=[jax.ShapeDtypeStruct(w.shape, f32)] * 4,
        args=(w, m, v, *parts))


def _sum_leading(x, name):
    n, rows, cols = x.shape
    tr = _row_tile(rows)

    def body(x_ref, o_ref):
        acc = x_ref[0].astype(f32)
        for k in range(1, n):
            acc = acc + x_ref[k].astype(f32)
        o_ref[...] = acc

    return pl.pallas_call(
        body, name=name, grid=(rows // tr,),
        in_specs=[pl.BlockSpec((n, tr, cols), lambda i: (0, i, 0))],
        out_specs=pl.BlockSpec((tr, cols), lambda i: (i, 0)),
        out_shape=jax.ShapeDtypeStruct((rows, cols), f32),
        compiler_params=_params(1),
    )(x)


def _pair_sum(gs, recvs, c_idx, name):
    n = len(gs)

    def body(c_ref, *refs):
        for g_ref, r_ref, o_ref in zip(refs[:n], refs[n:2 * n], refs[2 * n:]):
            o_ref[...] = (g_ref[...].astype(f32) + r_ref[...].astype(f32)).astype(o_ref.dtype)

    own = [pl.BlockSpec((1,) + g.shape[1:], lambda k, c: (2 * k + c[0], 0, 0)) for g in gs]
    by_chip = [pl.BlockSpec((1,) + g.shape[1:], lambda k, c: (k, 0, 0)) for g in gs]
    return list(pl.pallas_call(
        body, name=name,
        grid_spec=pltpu.PrefetchScalarGridSpec(num_scalar_prefetch=1, grid=(N_CHIP,),
                                               in_specs=own + by_chip, out_specs=by_chip),
        out_shape=[jax.ShapeDtypeStruct((N_CHIP,) + g.shape[1:], g.dtype) for g in gs],
        compiler_params=_params(1),
    )(c_idx, *gs, *recvs))


def _pack_rows(w):
    return w.reshape(N_DEV, -1, D_MODEL)


def kernel(x, even_w_in, even_w_out, a_w_s, a_b_s, a_ln_g, a_ln_b, b_w_pool, b_scale, odd_w_in, odd_w_out, c_w_dw, c_b_dw, c_ln_g, c_ln_b, d_w_dw, norm_mix_g, norm_ffn_g, ffn_w_gate, ffn_w_up, ffn_w_down, final_norm_g, loss_target, m_even_w_in, m_even_w_out, m_a_w_s, m_a_b_s, m_a_ln_g, m_a_ln_b, m_b_w_pool, m_b_scale, m_odd_w_in, m_odd_w_out, m_c_w_dw, m_c_b_dw, m_c_ln_g, m_c_ln_b, m_d_w_dw, m_norm_mix_g, m_norm_ffn_g, m_ffn_w_gate, m_ffn_w_up, m_ffn_w_down, m_final_norm_g, v_even_w_in, v_even_w_out, v_a_w_s, v_a_b_s, v_a_ln_g, v_a_ln_b, v_b_w_pool, v_b_scale, v_odd_w_in, v_odd_w_out, v_c_w_dw, v_c_b_dw, v_c_ln_g, v_c_ln_b, v_d_w_dw, v_norm_mix_g, v_norm_ffn_g, v_ffn_w_gate, v_ffn_w_up, v_ffn_w_down, v_final_norm_g):
    weights = dict(even_w_in=even_w_in, even_w_out=even_w_out, a_w_s=a_w_s, a_b_s=a_b_s, a_ln_g=a_ln_g, a_ln_b=a_ln_b,
                   b_w_pool=b_w_pool, b_scale=b_scale, odd_w_in=odd_w_in, odd_w_out=odd_w_out, c_w_dw=c_w_dw,
                   c_b_dw=c_b_dw, c_ln_g=c_ln_g, c_ln_b=c_ln_b, d_w_dw=d_w_dw, norm_mix_g=norm_mix_g,
                   norm_ffn_g=norm_ffn_g, ffn_w_gate=ffn_w_gate, ffn_w_up=ffn_w_up, ffn_w_down=ffn_w_down,
                   final_norm_g=final_norm_g)
    m_in = dict(even_w_in=m_even_w_in, even_w_out=m_even_w_out, a_w_s=m_a_w_s, a_b_s=m_a_b_s, a_ln_g=m_a_ln_g,
                a_ln_b=m_a_ln_b, b_w_pool=m_b_w_pool, b_scale=m_b_scale, odd_w_in=m_odd_w_in, odd_w_out=m_odd_w_out,
                c_w_dw=m_c_w_dw, c_b_dw=m_c_b_dw, c_ln_g=m_c_ln_g, c_ln_b=m_c_ln_b, d_w_dw=m_d_w_dw,
                norm_mix_g=m_norm_mix_g, norm_ffn_g=m_norm_ffn_g, ffn_w_gate=m_ffn_w_gate, ffn_w_up=m_ffn_w_up,
                ffn_w_down=m_ffn_w_down, final_norm_g=m_final_norm_g)
    v_in = dict(even_w_in=v_even_w_in, even_w_out=v_even_w_out, a_w_s=v_a_w_s, a_b_s=v_a_b_s, a_ln_g=v_a_ln_g,
                a_ln_b=v_a_ln_b, b_w_pool=v_b_w_pool, b_scale=v_b_scale, odd_w_in=v_odd_w_in, odd_w_out=v_odd_w_out,
                c_w_dw=v_c_w_dw, c_b_dw=v_c_b_dw, c_ln_g=v_c_ln_g, c_ln_b=v_c_ln_b, d_w_dw=v_d_w_dw,
                norm_mix_g=v_norm_mix_g, norm_ffn_g=v_norm_ffn_g, ffn_w_gate=v_ffn_w_gate, ffn_w_up=v_ffn_w_up,
                ffn_w_down=v_ffn_w_down, final_norm_g=v_final_norm_g)
    names = list(weights)

    group_parts = {
        "even": [even_w_in[0].T, even_w_out[0]],
        "ffn0": [ffn_w_gate[0].T, ffn_w_up[0].T, ffn_w_down[0]],
        "odd": [odd_w_in[0].T, odd_w_out[0]],
        "ffn1": [ffn_w_gate[1].T, ffn_w_up[1].T, ffn_w_down[1]],
    }

    def gather_jobs(*groups):
        return [_all_gather_job(p.astype(bf16)) for k in groups for p in group_parts[k]]

    def whole(gathered):
        return [g.reshape(-1, D_MODEL) for g in gathered]

    conv_names = ["c_w_dw", "c_b_dw", "c_ln_g", "c_ln_b", "d_w_dw"]
    conv_rows = [C_KERNEL, 1, 1, 1, D_KERNEL]
    conv_local = jnp.concatenate([weights[n].reshape(r, -1) for n, r in zip(conv_names, conv_rows)]
                                 + [jnp.zeros((3, c_b_dw.shape[-1]), f32)], axis=0)
    *even_gathered, conv_all = _run_jobs(gather_jobs("even") + [_all_gather_job(conv_local)], "gather_even_conv")
    w_in_e, w_out_e = whole(even_gathered)
    conv_all = conv_all.transpose(1, 0, 2).reshape(conv_local.shape[0], -1)
    conv_offs = [sum(conv_rows[:k]) for k in range(len(conv_rows) + 1)]
    cw, cb, clg, clb, dw = [conv_all[conv_offs[k]:conv_offs[k + 1]] for k in range(len(conv_rows))]

    ws, bst = a_w_s[0], a_b_s[0].T
    lng, lnb, wp, sc = a_ln_g, a_ln_b, b_w_pool[0], b_scale
    gmix = [norm_mix_g[l:l + 1] for l in range(2)]
    gffn = [norm_ffn_g[l:l + 1] for l in range(2)]
    gfin = final_norm_g.reshape(1, D_MODEL)

    h0 = x[0]
    h1, hn_e, za, pooled, mix_e, *ffn0_gathered = _even_fwd(
        h0, w_in_e, w_out_e, ws, bst, lng, lnb, wp, sc, gmix[0], jobs=gather_jobs("ffn0"))
    w_gate0, w_up0, w_down0 = whole(ffn0_gathered)
    h2, hn_f0, gate0, up0, *rest_gathered = _ffn_fwd(h1, w_gate0, w_up0, w_down0, gffn[0],
                                                     jobs=gather_jobs("odd", "ffn1"))
    w_in_o, w_out_o, w_gate1, w_up1, w_down1 = whole(rest_gathered)
    h3, hn_o, z_o, mix_o, cv_o = _odd_fwd(h2, w_in_o, w_out_o, cw, cb, clg, clb, dw, gmix[1])
    dh4, hn_f1, gate1, up1, loss_local, g_final = _ffn_fwd(h3, w_gate1, w_up1, w_down1, gffn[1],
                                                           head=(loss_target[0], gfin))

    c_idx = lax.axis_index("c").astype(jnp.int32).reshape(1)

    def weight_grad(x, y, name, jobs=()):
        g, *job_results = _weight_grads([(x, y)], name, jobs=jobs)
        return [_pack_rows(g)] + job_results

    def siblings(parts):
        return [_sibling_exchange_job(p) for p in parts]

    def chips(pairs):
        return [_chip_exchange_job(p) for p in pairs]

    dh3, dgate1, dup1, act1, g_ffn1 = _ffn_bwd(dh4, h3, gate1, up1, w_gate1, w_up1, w_down1, gffn[1])
    part_ffn1 = (weight_grad(dgate1, hn_f1, "dw_gate1") + weight_grad(dup1, hn_f1, "dw_up1")
                 + weight_grad(act1, dh4, "dw_down1"))
    dh2, dz_o, g_cw, g_cb, g_clg, g_clb, g_dw, g_mix1, *recv_ffn1 = _odd_bwd(
        dh3, h2, z_o, cv_o, w_in_o, w_out_o, cw, clg, clb, dw, gmix[1], jobs=siblings(part_ffn1))
    pair_ffn1 = _pair_sum(part_ffn1, recv_ffn1, c_idx, "pair_sum_ffn1")
    part_odd = weight_grad(dz_o, hn_o, "dw_odd_in") + weight_grad(mix_o, dh3, "dw_odd_out")
    dh1, dgate0, dup0, act0, g_ffn0, *exchanged = _ffn_bwd(
        dh2, h1, gate0, up0, w_gate0, w_up0, w_down0, gffn[0], jobs=chips(pair_ffn1) + siblings(part_odd))
    chips_ffn1, recv_odd = exchanged[:3], exchanged[3:]
    pair_odd = _pair_sum(part_odd, recv_odd, c_idx, "pair_sum_odd")
    dw_gate0, *chips_odd = weight_grad(dgate0, hn_f0, "dw_gate0", jobs=chips(pair_odd))
    part_ffn0 = [dw_gate0] + weight_grad(dup0, hn_f0, "dw_up0") + weight_grad(act0, dh2, "dw_down0")
    part_even_out, *recv_ffn0 = weight_grad(mix_e, dh1, "dw_even_out", jobs=siblings(part_ffn0))
    pair_ffn0 = _pair_sum(part_ffn0, recv_ffn0, c_idx, "pair_sum_ffn0")
    dh0, dz_e, g_ws, g_bs, g_lng, g_lnb, g_wp, g_sc, g_mix0, *exchanged = _even_bwd(
        dh1, h0, za, pooled, w_in_e, w_out_e, ws, bst, lng, lnb, wp, sc, gmix[0],
        jobs=chips(pair_ffn0) + siblings([part_even_out]))
    chips_ffn0, recv_even_out = exchanged[:3], exchanged[3:]
    pair_even_out = _pair_sum([part_even_out], recv_even_out, c_idx, "pair_sum_even_out")

    lanes = HEAD
    small = [("a_w_s", g_ws), ("a_b_s", g_bs), ("a_ln_g", g_lng), ("a_ln_b", g_lnb), ("b_w_pool", g_wp),
             ("b_scale", g_sc), ("norm_mix_g", jnp.concatenate([g_mix0, g_mix1], axis=0)),
             ("norm_ffn_g", jnp.concatenate([g_ffn0, g_ffn1], axis=0)), ("final_norm_g", g_final),
             ("c_w_dw", g_cw), ("c_b_dw", g_cb), ("c_ln_g", g_clg), ("c_ln_b", g_clb), ("d_w_dw", g_dw),
             ("loss", loss_local)]
    small_rows = [-(-g.size // (8 * lanes)) * 8 for _, g in small]
    small_offs = [sum(small_rows[:k]) for k in range(len(small) + 1)]
    pad_rows = -small_offs[-1] % 256
    small_buf = jnp.concatenate(
        [jnp.pad(g.reshape(-1), (0, r * lanes - g.size)).reshape(r, lanes) for (_, g), r in zip(small, small_rows)]
        + [jnp.zeros((pad_rows, lanes), f32)], axis=0)
    part_even_in, small_all, chips_even_out = weight_grad(
        dz_e, hn_e, "dw_even_in", jobs=[_all_gather_job(small_buf)] + chips(pair_even_out))
    small_sum = _sum_leading(small_all, "small_grad_sum")
    grads = {}
    for k, (n, g) in enumerate(small):
        grads[n] = small_sum[small_offs[k]:small_offs[k + 1]].reshape(-1)[:g.size].reshape(g.shape)
    me = 4 * lax.axis_index("x") + 2 * lax.axis_index("y") + lax.axis_index("c")
    shard = c_b_dw.shape[-1]
    for n in conv_names:
        grads[n] = lax.dynamic_slice_in_dim(grads[n], me * shard, shard, axis=1)

    col_sharded = ("even_w_in", "odd_w_in", "ffn_w_gate", "ffn_w_up")

    def rows_view(n, a):
        return jnp.swapaxes(a, -1, -2) if n in col_sharded else a

    loss = grads.pop("loss")[0, 0]
    chip_parts = {"even_w_out": [chips_even_out], "odd_w_in": chips_odd[:1], "odd_w_out": chips_odd[1:]}
    for k, n in enumerate(["ffn_w_gate", "ffn_w_up", "ffn_w_down"]):
        chip_parts[n] = [chips_ffn0[k], chips_ffn1[k]]

    delta, new_m, new_v = {}, {}, {}

    def adamw_reduced(n, jobs=()):
        w_rows, m_rows, v_rows = [rows_view(n, a) for a in (weights[n], m_in[n], v_in[n])]
        outs = _adamw_reduced(w_rows, chip_parts[n], m_rows, v_rows, "adamw_" + n, jobs=jobs)
        grads[n], delta[n], new_m[n], new_v[n] = [rows_view(n, o) for o in outs[:4]]
        return outs[4:]

    recv_even_in = adamw_reduced("ffn_w_gate", jobs=siblings([part_even_in]))
    pair_even_in = _pair_sum([part_even_in], recv_even_in, c_idx, "pair_sum_even_in")
    chip_parts["even_w_in"] = adamw_reduced("ffn_w_up", jobs=chips(pair_even_in))
    for n in names:
        if n in chip_parts and n not in grads:
            adamw_reduced(n)
        elif n not in chip_parts:
            w_rows, m_rows, v_rows = [rows_view(n, a) for a in (weights[n], m_in[n], v_in[n])]
            view = (-1, w_rows.shape[-1])
            outs = _adamw(w_rows.reshape(view), grads[n].reshape(view), m_rows.reshape(view), v_rows.reshape(view),
                          "adamw_" + n)
            grads[n] = grads[n].reshape(w_rows.shape)
            delta[n], new_m[n], new_v[n] = [o.reshape(w_rows.shape) for o in outs]

    return (loss, dh0[None], *[grads[n] for n in names], *[delta[n] for n in names],
            *[new_m[n] for n in names], *[new_v[n] for n in names])
```

```python
import jax
import jax.numpy as jnp
from jax import lax
from jax.experimental import pallas as pl
from jax.experimental.pallas import tpu as pltpu

f32 = jnp.float32
bf16 = jnp.bfloat16

EPS = 1e-6
D_MODEL = 1024
A_WIDTH = 512
HEAD = 128
N_HEADS = 4
CHUNK = 64
POOL_WINDOWS = (2, 4, 8, 16)
POOL_HALO = 16
C_KERNEL = 31
D_KERNEL = 3
CONV_HALO = 32
D_FF = 2816
N_DEV = 8
N_CHIP = 4

ADAM_LR = 0.001
ADAM_B1 = 0.9
ADAM_B2 = 0.999
ADAM_EPS = 1e-08
ADAM_WD = 0.01
ADAM_STEP = 10

MIX_TILE = 512
ODD_FWD_TILE = 1024
FFN_TILE = 256
FFN_CHUNKS = (1536, 1280)
DW_TK = 2048
DW_TM = 1536
MIDDLE_AT, MIDDLE_OF = 7, 8
VMEM_LIMIT = 56 * 1024 * 1024

MESH = pl.DeviceIdType.MESH
ANY = pl.BlockSpec(memory_space=pl.ANY)


def _params(n_axes):
    return pltpu.CompilerParams(dimension_semantics=("arbitrary",) * n_axes, vmem_limit_bytes=VMEM_LIMIT)


def _mm(a, b):
    return jnp.dot(a, b, preferred_element_type=f32)


def _mm_nt(a, b):
    return lax.dot_general(a, b, (((1,), (1,)), ((), ())), preferred_element_type=f32)


def _mm_tn(a, b):
    return lax.dot_general(a, b, (((0,), (0,)), ((), ())), preferred_element_type=f32)


def _sigmoid(x):
    return 1.0 / (1.0 + jnp.exp(-x))


def _rms_r(h):
    return lax.rsqrt(jnp.mean(h * h, axis=-1, keepdims=True) + EPS)


def _rms_bwd(dy, h, g):
    r = _rms_r(h)
    xh = h * r
    dxh = dy * g
    dh = r * (dxh - xh * jnp.mean(dxh * xh, axis=-1, keepdims=True))
    return dh, jnp.sum(dy * xh, axis=0, keepdims=True)


def _ln_fwd(x, g, b):
    mu = jnp.mean(x, axis=-1, keepdims=True)
    xc = x - mu
    r = lax.rsqrt(jnp.mean(xc * xc, axis=-1, keepdims=True) + EPS)
    xh = xc * r
    return xh * g + b, xh, r


def _ln_bwd(dy, xh, r, g):
    dxh = dy * g
    return r * (dxh - jnp.mean(dxh, axis=-1, keepdims=True) - xh * jnp.mean(dxh * xh, axis=-1, keepdims=True))


_GELU_C = 0.7978845608028654
_GELU_A = 0.044715


def _gelu(x):
    th = jnp.tanh(x * (_GELU_C + (_GELU_C * _GELU_A) * (x * x)))
    half = 0.5 * x
    return half + half * th, th


def _gelu_grad(x, th):
    return 0.5 + 0.5 * th + (1.0 - th * th) * (x * (0.5 * _GELU_C + (1.5 * _GELU_C * _GELU_A) * (x * x)))


def _down(x, k):
    return x if k == 0 else pltpu.roll(x, k, 0)


def _up(x, k):
    return x if k == 0 else pltpu.roll(x, x.shape[0] - k, 0)


def _window_sum(x, win, shift):
    s = x
    step = 1
    while step < win:
        s = s + shift(s, step)
        step *= 2
    return s


def _inv_count(t0, rows, win):
    t = t0 + lax.broadcasted_iota(jnp.int32, (rows, 1), 0)
    return 1.0 / jnp.minimum(t + 1, win).astype(f32)


def _chunk_mask():
    i = lax.broadcasted_iota(jnp.int32, (HEAD, HEAD), 0)
    j = lax.broadcasted_iota(jnp.int32, (HEAD, HEAD), 1)
    return jnp.logical_or(i >= CHUNK, j < CHUNK)


def _const(shape, n_axes):
    zeros = (0,) * len(shape)
    if n_axes == 1:
        return pl.BlockSpec(shape, lambda i: zeros)
    return pl.BlockSpec(shape, lambda i, j: zeros)


def _prev_halo(tile, halo, cols):
    return pl.BlockSpec((halo, cols), lambda i: (jnp.maximum(i * (tile // halo) - 1, 0), 0))


def _next_halo(tile, halo, cols, seq):
    return pl.BlockSpec((halo, cols), lambda i: (jnp.minimum((i + 1) * (tile // halo), seq // halo - 1), 0))


class _Job:
    def __init__(self, inputs, out_shape, sems, hooks):
        self.inputs, self.out_shape, self.sems, self.hooks = inputs, out_shape, sems, hooks


def _position():
    return lax.axis_index("x"), lax.axis_index("y"), lax.axis_index("c")


def _all_gather_job(block):
    rows, cols = block.shape

    def hooks(ins, outs, sems):
        (x_ref,), (out_ref,), (send_sems, recv_sems, local_sem) = ins, outs, sems
        x, y, c = _position()
        me, sibling = (x, y, c), (x, y, 1 - c)
        chips = [(1 - x, y), (x, 1 - y), (1 - x, 1 - y)]

        def slot(px, py, pc):
            return out_ref.at[4 * px + 2 * py + pc]

        def copy(k, block_of, to, src=None):
            return pltpu.make_async_remote_copy(
                src_ref=slot(*block_of) if src is None else src, dst_ref=slot(*block_of),
                send_sem=send_sems.at[k], recv_sem=recv_sems.at[k], device_id=to, device_id_type=MESH)

        mine = pltpu.make_async_copy(x_ref, slot(*me), local_sem)
        first = [copy(0, me, sibling, src=x_ref)]
        first += [copy(1 + j, me, (*chip, c), src=x_ref) for j, chip in enumerate(chips)]
        passed = [copy(4 + j, (*chip, c), sibling) for j, chip in enumerate(chips)]

        def start():
            mine.start()
            for cp in first:
                cp.start()

        def middle():
            for j, chip in enumerate(chips):
                copy(1 + j, (*chip, c), me).wait_recv()
                passed[j].start()

        def finish():
            copy(0, sibling, me).wait_recv()
            for j, chip in enumerate(chips):
                copy(4 + j, (*chip, 1 - c), me).wait_recv()
            for cp in first + passed:
                cp.wait_send()
            mine.wait()

        return start, middle, finish

    return _Job([block], [jax.ShapeDtypeStruct((N_DEV, rows, cols), block.dtype)],
                [pltpu.SemaphoreType.DMA((7,)), pltpu.SemaphoreType.DMA((7,)), pltpu.SemaphoreType.DMA], hooks)


def _sibling_exchange_job(g):
    _, rows, cols = g.shape

    def hooks(ins, outs, sems):
        (g_ref,), (recv_ref,), (send_sems, recv_sems) = ins, outs, sems
        x, y, c = _position()
        copies = [pltpu.make_async_remote_copy(
            src_ref=g_ref.at[2 * k + (1 - c)], dst_ref=recv_ref.at[k], send_sem=send_sems.at[k],
            recv_sem=recv_sems.at[k], device_id=(x, y, 1 - c), device_id_type=MESH) for k in range(N_CHIP)]

        def start():
            for cp in copies:
                cp.start()

        def finish():
            for cp in copies:
                cp.wait()

        return start, lambda: None, finish

    return _Job([g], [jax.ShapeDtypeStruct((N_CHIP, rows, cols), g.dtype)],
                [pltpu.SemaphoreType.DMA((N_CHIP,)), pltpu.SemaphoreType.DMA((N_CHIP,))], hooks)


def _chip_exchange_job(p):
    _, rows, cols = p.shape

    def hooks(ins, outs, sems):
        (p_ref,), (recv_ref,), (send_sems, recv_sems, local_sem) = ins, outs, sems
        x, y, c = _position()
        k_me = 2 * x + y
        mine = pltpu.make_async_copy(p_ref.at[k_me], recv_ref.at[k_me], local_sem)
        copies = [pltpu.make_async_remote_copy(
            src_ref=p_ref.at[2 * px + py], dst_ref=recv_ref.at[k_me], send_sem=send_sems.at[j],
            recv_sem=recv_sems.at[j], device_id=(px, py, c), device_id_type=MESH)
            for j, (px, py) in enumerate([(1 - x, y), (x, 1 - y), (1 - x, 1 - y)])]

        def start():
            mine.start()
            for cp in copies:
                cp.start()

        def finish():
            for cp in copies:
                cp.wait()
            mine.wait()

        return start, lambda: None, finish

    return _Job([p], [jax.ShapeDtypeStruct((N_CHIP, rows, cols), p.dtype)],
                [pltpu.SemaphoreType.DMA((3,)), pltpu.SemaphoreType.DMA((3,)), pltpu.SemaphoreType.DMA], hooks)


def _job_hooks(jobs, ins, outs, sems):
    hooks = []
    for job in jobs:
        n_in, n_out, n_sem = len(job.inputs), len(job.out_shape), len(job.sems)
        hooks.append(job.hooks(ins[:n_in], outs[:n_out], sems[:n_sem]))
        ins, outs, sems = ins[n_in:], outs[n_out:], sems[n_sem:]
    return hooks


def _run_jobs(jobs, name):
    n_in = sum(len(job.inputs) for job in jobs)
    n_out = sum(len(job.out_shape) for job in jobs)

    def body(*refs):
        hooks = _job_hooks(jobs, refs[:n_in], refs[n_in:n_in + n_out], refs[n_in + n_out:])
        for phase in range(3):
            for h in hooks:
                h[phase]()

    return list(pl.pallas_call(
        body, name=name, in_specs=[ANY] * n_in, out_specs=[ANY] * n_out,
        out_shape=[s for job in jobs for s in job.out_shape],
        scratch_shapes=[s for job in jobs for s in job.sems],
    )(*[a for job in jobs for a in job.inputs]))


def _launch(body, *, name, grid, in_specs, out_specs, out_shape, args, scratch=(), jobs=()):
    in_specs, out_specs, out_shape, scratch = list(in_specs), list(out_specs), list(out_shape), list(scratch)
    if not jobs:
        return list(pl.pallas_call(body, name=name, grid=grid, in_specs=in_specs, out_specs=out_specs,
                                   out_shape=out_shape, scratch_shapes=scratch,
                                   compiler_params=_params(len(grid)))(*args))
    n_in, n_out, n_sc = len(in_specs), len(out_specs), len(scratch)
    j_in = [a for job in jobs for a in job.inputs]
    j_out = [s for job in jobs for s in job.out_shape]
    j_sems = [s for job in jobs for s in job.sems]
    n_steps = 1
    for g in grid:
        n_steps *= g

    def wrapped(*refs):
        ins, refs = refs[:n_in], refs[n_in:]
        jins, refs = refs[:len(j_in)], refs[len(j_in):]
        outs, refs = refs[:n_out], refs[n_out:]
        jouts, refs = refs[:len(j_out)], refs[len(j_out):]
        sc, jsems = refs[:n_sc], refs[n_sc:]
        step = pl.program_id(0)
        for axis in range(1, len(grid)):
            step = step * grid[axis] + pl.program_id(axis)
        hooks = _job_hooks(jobs, jins, jouts, jsems)

        @pl.when(step == 0)
        def _():
            for h in hooks:
                h[0]()

        body(*ins, *outs, *sc)

        @pl.when(step == (MIDDLE_AT * n_steps) // MIDDLE_OF)
        def _():
            for h in hooks:
                h[1]()

        @pl.when(step == n_steps - 1)
        def _():
            for h in hooks:
                h[2]()

    return list(pl.pallas_call(
        wrapped, name=name, grid=grid, in_specs=in_specs + [ANY] * len(j_in), out_specs=out_specs + [ANY] * len(j_out),
        out_shape=out_shape + j_out, scratch_shapes=scratch + j_sems, compiler_params=_params(len(grid)),
    )(*args, *j_in))


def _gmlp_gate(vnb, wsm, bst, tile):
    rows = []
    for n in range(tile // HEAD):
        cols = []
        for hh in range(N_HEADS):
            blk = vnb[n * HEAD:(n + 1) * HEAD, hh * HEAD:(hh + 1) * HEAD]
            cols.append(_mm(wsm[hh], blk) + bst[:, hh:hh + 1])
        rows.append(jnp.concatenate(cols, axis=1))
    return jnp.concatenate(rows, axis=0)


def _even_fwd(h, w_in, w_out, ws, bst, lng, lnb, wp, sc, gm, jobs=()):
    seq = h.shape[0]
    tile = min(MIX_TILE, seq)
    n_tiles = seq // tile

    def body(h_ref, hp_ref, win_ref, wout_ref, ws_ref, bst_ref, lng_ref, lnb_ref, wp_ref, sc_ref, g_ref,
             ho_ref, hn_ref, za_ref, pool_ref, mix_ref):
        i = pl.program_id(0)
        g = g_ref[...]
        h = h_ref[...]
        hnb = (h * _rms_r(h) * g).astype(bf16)
        hn_ref[...] = hnb
        z = _mm_nt(hnb, win_ref[...])
        zab = z[:, :2 * A_WIDTH].astype(bf16)
        za_ref[...] = zab
        hp = hp_ref[...]
        zbp = _mm_nt((hp * _rms_r(hp) * g).astype(bf16), win_ref[2 * A_WIDTH:, :])
        zbe = jnp.concatenate([jnp.where(i > 0, zbp, 0.0), z[:, 2 * A_WIDTH:]], axis=0)
        pooled = []
        for gi, win in enumerate(POOL_WINDOWS):
            xg = zbe[:, gi * HEAD:(gi + 1) * HEAD]
            s = _window_sum(xg, win, _down)
            pooled.append(s[POOL_HALO:] * _inv_count(i * tile, tile, win) - xg[POOL_HALO:])
        plb = jnp.concatenate(pooled, axis=1).astype(bf16)
        pool_ref[...] = plb

        ga, _ = _gelu(zab.astype(f32))
        vn, _, _ = _ln_fwd(ga[:, A_WIDTH:], lng_ref[...], lnb_ref[...])
        mask = _chunk_mask()
        wsm = [jnp.where(mask, ws_ref[hh], 0.0).astype(bf16) for hh in range(N_HEADS)]
        ya = ga[:, :A_WIDTH] * _gmlp_gate(vn.astype(bf16), wsm, bst_ref[...], tile)
        yb = jnp.concatenate([_mm(plb[:, gi * HEAD:(gi + 1) * HEAD], wp_ref[gi].astype(bf16))
                              for gi in range(len(POOL_WINDOWS))], axis=1) * sc_ref[...]
        mix = jnp.concatenate([ya, yb], axis=1).astype(bf16)
        mix_ref[...] = mix
        ho_ref[...] = h + _mm(mix, wout_ref[...])

    row = lambda cols: pl.BlockSpec((tile, cols), lambda i: (i, 0))
    return _launch(
        body, name="even_fwd", grid=(n_tiles,), jobs=jobs,
        in_specs=[row(D_MODEL), _prev_halo(tile, POOL_HALO, D_MODEL), _const(w_in.shape, 1), _const(w_out.shape, 1),
                  _const(ws.shape, 1), _const(bst.shape, 1), _const(lng.shape, 1), _const(lnb.shape, 1),
                  _const(wp.shape, 1), _const(sc.shape, 1), _const(gm.shape, 1)],
        out_specs=[row(D_MODEL), row(D_MODEL), row(2 * A_WIDTH), row(A_WIDTH), row(D_MODEL)],
        out_shape=[jax.ShapeDtypeStruct((seq, D_MODEL), f32), jax.ShapeDtypeStruct((seq, D_MODEL), bf16),
                   jax.ShapeDtypeStruct((seq, 2 * A_WIDTH), bf16), jax.ShapeDtypeStruct((seq, A_WIDTH), bf16),
                   jax.ShapeDtypeStruct((seq, D_MODEL), bf16)],
        args=(h, h, w_in, w_out, ws, bst, lng, lnb, wp, sc, gm))


def _even_bwd(dh, h, za, pooled, w_in, w_out, ws, bst, lng, lnb, wp, sc, gm, jobs=()):
    seq = h.shape[0]
    tile = min(MIX_TILE, seq)
    n_tiles = seq // tile
    n_groups = len(POOL_WINDOWS)

    def body(dh_ref, dhx_ref, h_ref, za_ref, pool_ref, win_ref, wout_ref, ws_ref, bst_ref, lng_ref, lnb_ref,
             wp_ref, sc_ref, g_ref,
             dhi_ref, dz_ref, dws_ref, dbs_ref, dlng_ref, dlnb_ref, dwp_ref, dsc_ref, dg_ref):
        i = pl.program_id(0)

        @pl.when(i == 0)
        def _():
            for ref in (dws_ref, dbs_ref, dlng_ref, dlnb_ref, dwp_ref, dsc_ref, dg_ref):
                ref[...] = jnp.zeros_like(ref)

        dh = dh_ref[...]
        dmix = _mm_nt(dh.astype(bf16), wout_ref[...])
        dya = dmix[:, :A_WIDTH]
        dyb = dmix[:, A_WIDTH:]
        dybx = _mm_nt(dhx_ref[...].astype(bf16), wout_ref[A_WIDTH:, :])
        dybx = jnp.where(i < n_tiles - 1, dybx, 0.0)

        za = za_ref[...].astype(f32)
        ga, th = _gelu(za)
        u = ga[:, :A_WIDTH]
        lng = lng_ref[...]
        vn, vh, r = _ln_fwd(ga[:, A_WIDTH:], lng, lnb_ref[...])
        vnb = vn.astype(bf16)
        mask = _chunk_mask()
        wsf = [jnp.where(mask, ws_ref[hh], 0.0) for hh in range(N_HEADS)]
        sv = _gmlp_gate(vnb, [w.astype(bf16) for w in wsf], bst_ref[...], tile)
        du = dya * sv
        dsvb = (dya * u).astype(bf16)
        wst = [w.T.astype(bf16) for w in wsf]
        ones = jnp.ones((8, HEAD), bf16)
        dws = [jnp.zeros((HEAD, HEAD), f32) for _ in range(N_HEADS)]
        dbs = [jnp.zeros((8, HEAD), f32) for _ in range(N_HEADS)]
        rows = []
        for n in range(tile // HEAD):
            cols = []
            for hh in range(N_HEADS):
                blk = dsvb[n * HEAD:(n + 1) * HEAD, hh * HEAD:(hh + 1) * HEAD]
                cols.append(_mm(wst[hh], blk))
                dws[hh] = dws[hh] + _mm_nt(blk, vnb[n * HEAD:(n + 1) * HEAD, hh * HEAD:(hh + 1) * HEAD])
                dbs[hh] = dbs[hh] + _mm_nt(ones, blk)
            rows.append(jnp.concatenate(cols, axis=1))
        dvn = jnp.concatenate(rows, axis=0)
        for hh in range(N_HEADS):
            dws_ref[hh] += jnp.where(mask, dws[hh], 0.0)
            dbs_ref[pl.ds(hh, 1), :] += dbs[hh][0:1, :]
        dlng_ref[...] += jnp.sum(dvn * vh, axis=0, keepdims=True)
        dlnb_ref[...] += jnp.sum(dvn, axis=0, keepdims=True)
        dv = _ln_bwd(dvn, vh, r, lng)
        dza = jnp.concatenate([du, dv], axis=1) * _gelu_grad(za, th)

        plb = pool_ref[...]
        sc = sc_ref[...]
        dzb = []
        dsc = []
        for gi, win in enumerate(POOL_WINDOWS):
            cs = slice(gi * HEAD, (gi + 1) * HEAD)
            wpb = wp_ref[gi].astype(bf16)
            dsc.append(jnp.sum(dyb[:, cs] * _mm(plb[:, cs], wpb), axis=0, keepdims=True))
            dpre = (dyb[:, cs] * sc[:, cs]).astype(bf16)
            dprex = (dybx[:, cs] * sc[:, cs]).astype(bf16)
            dwp_ref[gi] += _mm_tn(plb[:, cs], dpre)
            dpl = _mm_nt(dpre, wpb)
            dple = jnp.concatenate([dpl, _mm_nt(dprex, wpb)], axis=0)
            q = dple * _inv_count(i * tile, tile + POOL_HALO, win)
            dzb.append(_window_sum(q, win, _up)[:tile] - dpl)
        dsc_ref[...] += jnp.concatenate(dsc, axis=1)

        dzf = jnp.concatenate([dza] + dzb, axis=1).astype(bf16)
        dz_ref[...] = dzf
        dhn = _mm(dzf, win_ref[...])
        dhr, dg = _rms_bwd(dhn, h_ref[...], g_ref[...])
        dhi_ref[...] = dh + dhr
        dg_ref[...] += dg

    row = lambda cols: pl.BlockSpec((tile, cols), lambda i: (i, 0))
    small = [ws.shape, (N_HEADS, HEAD), lng.shape, lnb.shape, wp.shape, sc.shape, gm.shape]
    return _launch(
        body, name="even_bwd", grid=(n_tiles,), jobs=jobs,
        in_specs=[row(D_MODEL), _next_halo(tile, POOL_HALO, D_MODEL, seq), row(D_MODEL), row(2 * A_WIDTH), row(A_WIDTH),
                  _const(w_in.shape, 1), _const(w_out.shape, 1), _const(ws.shape, 1), _const(bst.shape, 1),
                  _const(lng.shape, 1), _const(lnb.shape, 1), _const(wp.shape, 1), _const(sc.shape, 1), _const(gm.shape, 1)],
        out_specs=[row(D_MODEL), row(3 * A_WIDTH)] + [_const(s, 1) for s in small],
        out_shape=[jax.ShapeDtypeStruct((seq, D_MODEL), f32), jax.ShapeDtypeStruct((seq, 3 * A_WIDTH), bf16)]
                  + [jax.ShapeDtypeStruct(s, f32) for s in small],
        args=(dh, dh, h, za, pooled, w_in, w_out, ws, bst, lng, lnb, wp, sc, gm))


SUBLANES = 8


class _Shifted:
    def __init__(self, x, shift, max_shift):
        self.rolled = [shift(x, b) for b in range(min(SUBLANES, max_shift + 1))]
        self.back = shift is _down

    def rows(self, k, start, count):
        whole = k - k % SUBLANES
        lo = start - whole if self.back else start + whole
        return self.rolled[k % SUBLANES][lo:lo + count]


def _conv_taps(xs, w_ref, n_taps, halo, rows):
    acc = None
    for j in range(n_taps):
        term = w_ref[pl.ds(j, 1), :] * xs.rows(n_taps - 1 - j, halo, rows)
        acc = term if acc is None else acc + term
    return acc


def _odd_fwd(h, w_in, w_out, cw, cb, clg, clb, dw, gm):
    seq = h.shape[0]
    tile = min(ODD_FWD_TILE, seq)
    n_tiles = seq // tile
    w = A_WIDTH

    def body(h_ref, hp_ref, win_ref, wout_ref, cw_ref, cb_ref, clg_ref, clb_ref, dw_ref, g_ref,
             ho_ref, hn_ref, z_ref, mix_ref, cv_ref):
        i = pl.program_id(0)
        g = g_ref[...]
        h = h_ref[...]
        hnb = (h * _rms_r(h) * g).astype(bf16)
        hn_ref[...] = hnb
        zb = _mm_nt(hnb, win_ref[...]).astype(bf16)
        z_ref[...] = zb
        hp = hp_ref[...]
        zp = _mm_nt((hp * _rms_r(hp) * g).astype(bf16), win_ref[...]).astype(bf16).astype(f32)
        z = zb.astype(f32)
        ze = jnp.concatenate([jnp.where(i > 0, zp, 0.0), z], axis=0)
        hc = ze[:, :w] * _sigmoid(ze[:, w:2 * w])
        cv = _conv_taps(_Shifted(hc, _down, C_KERNEL - 1), cw_ref, C_KERNEL, CONV_HALO, tile) + cb_ref[...]
        cv_ref[...] = cv
        ln, _, _ = _ln_fwd(cv, clg_ref[...], clb_ref[...])
        yc = ln * _sigmoid(ln)
        p = ze[:, 3 * w:4 * w] * ze[:, 4 * w:]
        yd = z[:, 2 * w:3 * w] * _conv_taps(_Shifted(p, _down, D_KERNEL - 1), dw_ref, D_KERNEL, CONV_HALO, tile)
        mix = jnp.concatenate([yc, yd], axis=1).astype(bf16)
        mix_ref[...] = mix
        ho_ref[...] = h + _mm(mix, wout_ref[...])

    row = lambda cols: pl.BlockSpec((tile, cols), lambda i: (i, 0))
    return pl.pallas_call(
        body, name="odd_fwd", grid=(n_tiles,),
        in_specs=[row(D_MODEL), _prev_halo(tile, CONV_HALO, D_MODEL), _const(w_in.shape, 1), _const(w_out.shape, 1),
                  _const(cw.shape, 1), _const(cb.shape, 1), _const(clg.shape, 1), _const(clb.shape, 1),
                  _const(dw.shape, 1), _const(gm.shape, 1)],
        out_specs=[row(D_MODEL), row(D_MODEL), row(5 * w), row(D_MODEL), row(w)],
        out_shape=[jax.ShapeDtypeStruct((seq, D_MODEL), f32), jax.ShapeDtypeStruct((seq, D_MODEL), bf16),
                   jax.ShapeDtypeStruct((seq, 5 * w), bf16), jax.ShapeDtypeStruct((seq, D_MODEL), bf16),
                   jax.ShapeDtypeStruct((seq, w), f32)],
        compiler_params=_params(1),
    )(h, h, w_in, w_out, cw, cb, clg, clb, dw, gm)


def _odd_bwd(dh, h, z, cv, w_in, w_out, cw, clg, clb, dw, gm, jobs=()):
    seq = h.shape[0]
    tile = min(MIX_TILE, seq)
    n_tiles = seq // tile
    w = A_WIDTH
    halo = CONV_HALO

    def body(dh_ref, dhx_ref, h_ref, z_ref, zp_ref, zx_ref, cv_ref, cvx_ref, win_ref, wout_ref, cw_ref,
             clg_ref, clb_ref, dw_ref, g_ref,
             dhi_ref, dz_ref, dcw_ref, dcb_ref, dclg_ref, dclb_ref, ddw_ref, dg_ref):
        i = pl.program_id(0)

        @pl.when(i == 0)
        def _():
            for ref in (dcw_ref, dcb_ref, dclg_ref, dclb_ref, ddw_ref, dg_ref):
                ref[...] = jnp.zeros_like(ref)

        dh = dh_ref[...]
        dhe = jnp.concatenate([dh, jnp.where(i < n_tiles - 1, dhx_ref[...], 0.0)], axis=0)
        dmix = _mm_nt(dhe.astype(bf16), wout_ref[...])
        ze = jnp.concatenate([jnp.where(i > 0, zp_ref[...].astype(f32), 0.0), z_ref[...].astype(f32),
                              zx_ref[...].astype(f32)], axis=0)

        sg = _sigmoid(ze[:, w:2 * w])
        ca = ze[:, :w]
        hc = ca * sg
        hcs = _Shifted(hc, _down, C_KERNEL - 1)
        cv = jnp.concatenate([cv_ref[...], cvx_ref[...]], axis=0)
        clg = clg_ref[...]
        ln, xh, r = _ln_fwd(cv, clg, clb_ref[...])
        sl = _sigmoid(ln)
        dln = dmix[:, :w] * (sl * (1.0 + ln * (1.0 - sl)))
        dclg_ref[...] += jnp.sum((dln * xh)[:tile], axis=0, keepdims=True)
        dclb_ref[...] += jnp.sum(dln[:tile], axis=0, keepdims=True)
        dcv = _ln_bwd(dln, xh, r, clg)
        dcb_ref[...] += jnp.sum(dcv[:tile], axis=0, keepdims=True)
        dcvs = _Shifted(dcv, _up, C_KERNEL - 1)
        dhc = None
        for j in range(C_KERNEL):
            k = C_KERNEL - 1 - j
            dcw_ref[pl.ds(j, 1), :] += jnp.sum(dcv[:tile] * hcs.rows(k, halo, tile), axis=0, keepdims=True)
            term = cw_ref[pl.ds(j, 1), :] * dcvs.rows(k, 0, tile)
            dhc = term if dhc is None else dhc + term
        sgt = sg[halo:halo + tile]
        cat = ca[halo:halo + tile]
        dca = dhc * sgt
        dcg = dhc * cat * sgt * (1.0 - sgt)

        dcgv = ze[:, 3 * w:4 * w]
        dxin = ze[:, 4 * w:]
        p = dcgv * dxin
        ps = _Shifted(p, _down, D_KERNEL - 1)
        q = _conv_taps(ps, dw_ref, D_KERNEL, halo, tile)
        dyd = dmix[:, w:]
        dq = dyd * ze[halo:, 2 * w:3 * w]
        ddbg = dyd[:tile] * q
        dqs = _Shifted(dq, _up, D_KERNEL - 1)
        dp = None
        for j in range(D_KERNEL):
            k = D_KERNEL - 1 - j
            ddw_ref[pl.ds(j, 1), :] += jnp.sum(dq[:tile] * ps.rows(k, halo, tile), axis=0, keepdims=True)
            term = dw_ref[pl.ds(j, 1), :] * dqs.rows(k, 0, tile)
            dp = term if dp is None else dp + term
        ddcg = dp * dxin[halo:halo + tile]
        ddxin = dp * dcgv[halo:halo + tile]

        dzf = jnp.concatenate([dca, dcg, ddbg, ddcg, ddxin], axis=1).astype(bf16)
        dz_ref[...] = dzf
        dhn = _mm(dzf, win_ref[...])
        dhr, dg = _rms_bwd(dhn, h_ref[...], g_ref[...])
        dhi_ref[...] = dh + dhr
        dg_ref[...] += dg

    row = lambda cols: pl.BlockSpec((tile, cols), lambda i: (i, 0))
    small = [cw.shape, clg.shape, clg.shape, clb.shape, dw.shape, gm.shape]
    return _launch(
        body, name="odd_bwd", grid=(n_tiles,), jobs=jobs,
        in_specs=[row(D_MODEL), _next_halo(tile, halo, D_MODEL, seq), row(D_MODEL), row(5 * w),
                  _prev_halo(tile, halo, 5 * w), _next_halo(tile, halo, 5 * w, seq),
                  row(w), _next_halo(tile, halo, w, seq),
                  _const(w_in.shape, 1), _const(w_out.shape, 1), _const(cw.shape, 1),
                  _const(clg.shape, 1), _const(clb.shape, 1), _const(dw.shape, 1), _const(gm.shape, 1)],
        out_specs=[row(D_MODEL), row(5 * w)] + [_const(s, 1) for s in small],
        out_shape=[jax.ShapeDtypeStruct((seq, D_MODEL), f32), jax.ShapeDtypeStruct((seq, 5 * w), bf16)]
                  + [jax.ShapeDtypeStruct(s, f32) for s in small],
        args=(dh, dh, h, z, z, z, cv, cv, w_in, w_out, cw, clg, clb, dw, gm))


def _ffn_chunks():
    assert sum(FFN_CHUNKS) == D_FF
    start = 0
    for size in FFN_CHUNKS:
        yield slice(start, start + size)
        start += size


def _ffn_fwd(h, wg, wu, wd, gm, jobs=(), head=None):
    seq = h.shape[0]
    tile = min(FFN_TILE, seq)

    def body(h_ref, g_ref, wg_ref, wu_ref, wd_ref, *refs):
        if head is None:
            ho_ref, hn_ref, gate_ref, up_ref = refs
        else:
            t_ref, gf_ref, ho_ref, hn_ref, gate_ref, up_ref, loss_ref, dgf_ref = refs
        h = h_ref[...]
        hnb = (h * _rms_r(h) * g_ref[...]).astype(bf16)
        hn_ref[...] = hnb
        acc = None
        for rows in _ffn_chunks():
            gb = _mm_nt(hnb, wg_ref[rows, :]).astype(bf16)
            ub = _mm_nt(hnb, wu_ref[rows, :]).astype(bf16)
            gate_ref[:, rows] = gb
            up_ref[:, rows] = ub
            gf = gb.astype(f32)
            act = gf * _sigmoid(gf) * ub.astype(f32)
            part = _mm(act.astype(bf16), wd_ref[rows, :])
            acc = part if acc is None else acc + part
        ho = h + acc
        if head is None:
            ho_ref[...] = ho
            return

        @pl.when(pl.program_id(0) == 0)
        def _():
            loss_ref[...] = jnp.zeros_like(loss_ref)
            dgf_ref[...] = jnp.zeros_like(dgf_ref)

        g_final = gf_ref[...]
        err = ho * _rms_r(ho) * g_final - t_ref[...]
        loss_ref[...] += (0.5 / D_MODEL) * jnp.sum(jnp.sum(err * err, axis=1, keepdims=True), axis=0, keepdims=True)
        dho, dg = _rms_bwd(err * (1.0 / D_MODEL), ho, g_final)
        ho_ref[...] = dho
        dgf_ref[...] += dg

    row = pl.BlockSpec((tile, D_MODEL), lambda i: (i, 0))
    wide = pl.BlockSpec((tile, D_FF), lambda i: (i, 0))
    in_specs = [row, _const(gm.shape, 1), _const(wg.shape, 1), _const(wu.shape, 1), _const(wd.shape, 1)]
    out_specs = [row, row, wide, wide]
    out_shape = [jax.ShapeDtypeStruct((seq, D_MODEL), f32), jax.ShapeDtypeStruct((seq, D_MODEL), bf16),
                 jax.ShapeDtypeStruct((seq, D_FF), bf16), jax.ShapeDtypeStruct((seq, D_FF), bf16)]
    args = (h, gm, wg, wu, wd)
    if head is not None:
        target, g_final = head
        in_specs += [row, _const(g_final.shape, 1)]
        out_specs += [_const((1, 1), 1), _const(g_final.shape, 1)]
        out_shape += [jax.ShapeDtypeStruct((1, 1), f32), jax.ShapeDtypeStruct(g_final.shape, f32)]
        args += (target, g_final)
    return _launch(
        body, name="ffn_fwd" if head is None else "ffn_fwd_loss", grid=(seq // tile,), jobs=jobs,
        in_specs=in_specs, out_specs=out_specs, out_shape=out_shape, args=args)


def _ffn_bwd(dh, h, gate, up, wg, wu, wd, gm, jobs=()):
    seq = h.shape[0]
    tile = min(FFN_TILE, seq)
    n_tiles = seq // tile

    def body(dh_ref, h_ref, g_ref, gate_ref, up_ref, wg_ref, wu_ref, wd_ref,
             dhi_ref, dgate_ref, dup_ref, act_ref, dg_ref):
        @pl.when(pl.program_id(0) == 0)
        def _():
            dg_ref[...] = jnp.zeros_like(dg_ref)

        dh = dh_ref[...]
        dhb = dh.astype(bf16)
        acc = None
        for rows in _ffn_chunks():
            dact = _mm_nt(dhb, wd_ref[rows, :])
            gf = gate_ref[:, rows].astype(f32)
            uf = up_ref[:, rows].astype(f32)
            s = _sigmoid(gf)
            silu = gf * s
            act_ref[:, rows] = (silu * uf).astype(bf16)
            dgb = (dact * uf * (s * (1.0 + gf * (1.0 - s)))).astype(bf16)
            dub = (dact * silu).astype(bf16)
            dgate_ref[:, rows] = dgb
            dup_ref[:, rows] = dub
            part = _mm(dgb, wg_ref[rows, :]) + _mm(dub, wu_ref[rows, :])
            acc = part if acc is None else acc + part
        dhr, dg = _rms_bwd(acc, h_ref[...], g_ref[...])
        dhi_ref[...] = dh + dhr
        dg_ref[...] += dg

    row = pl.BlockSpec((tile, D_MODEL), lambda i: (i, 0))
    wide = pl.BlockSpec((tile, D_FF), lambda i: (i, 0))
    return _launch(
        body, name="ffn_bwd", grid=(n_tiles,), jobs=jobs,
        in_specs=[row, row, _const(gm.shape, 1), wide, wide, _const(wg.shape, 1), _const(wu.shape, 1), _const(wd.shape, 1)],
        out_specs=[row, wide, wide, wide, _const(gm.shape, 1)],
        out_shape=[jax.ShapeDtypeStruct((seq, D_MODEL), f32), jax.ShapeDtypeStruct((seq, D_FF), bf16),
                   jax.ShapeDtypeStruct((seq, D_FF), bf16), jax.ShapeDtypeStruct((seq, D_FF), bf16),
                   jax.ShapeDtypeStruct(gm.shape, f32)],
        args=(dh, h, gm, gate, up, wg, wu, wd))


def _weight_grads(pairs, name, jobs=()):
    seq, m = pairs[0][0].shape
    tk = min(DW_TK, seq)
    tm = m if m <= DW_TM else m // 2
    n_k = seq // tk
    n_pairs = len(pairs)

    def body(*refs):
        x_refs = refs[0:2 * n_pairs:2]
        y_refs = refs[1:2 * n_pairs:2]
        o_refs = refs[2 * n_pairs:3 * n_pairs]
        acc_refs = refs[3 * n_pairs:]
        k = pl.program_id(1)
        @pl.when(k == 0)
        def _():
            for acc_ref in acc_refs:
                acc_ref[...] = jnp.zeros_like(acc_ref)

        for x_ref, y_ref, acc_ref in zip(x_refs, y_refs, acc_refs):
            acc_ref[...] += _mm_tn(x_ref[...].astype(bf16), y_ref[...].astype(bf16))

        @pl.when(k == n_k - 1)
        def _():
            for o_ref, acc_ref in zip(o_refs, acc_refs):
                o_ref[...] = acc_ref[...].astype(bf16)

    in_specs = []
    for _ in pairs:
        in_specs += [pl.BlockSpec((tk, tm), lambda j, k: (k, j)), pl.BlockSpec((tk, D_MODEL), lambda j, k: (k, 0))]
    return _launch(
        body, name=name, grid=(m // tm, n_k), jobs=jobs,
        in_specs=in_specs,
        out_specs=[pl.BlockSpec((tm, D_MODEL), lambda j, k: (j, 0))] * n_pairs,
        out_shape=[jax.ShapeDtypeStruct((m, D_MODEL), bf16)] * n_pairs,
        scratch=[pltpu.VMEM((tm, D_MODEL), f32)] * n_pairs,
        args=[a for pair in pairs for a in pair])


def _row_tile(rows, limit=512):
    best = rows
    for t in range(8, min(rows, limit) + 1, 8):
        if rows % t == 0:
            best = t
    return best if rows > limit else rows


def _adam_step(w, g, m, v):
    m2 = ADAM_B1 * m + (1.0 - ADAM_B1) * g
    v2 = ADAM_B2 * v + (1.0 - ADAM_B2) * (g * g)
    m_hat = m2 / (1.0 - ADAM_B1 ** ADAM_STEP)
    v_hat = v2 / (1.0 - ADAM_B2 ** ADAM_STEP)
    return -ADAM_LR * (m_hat / (jnp.sqrt(v_hat) + ADAM_EPS) + ADAM_WD * w), m2, v2


def _adamw_small(items, name):
    n = len(items)

    def body(*refs):
        ins, outs = refs[:4 * n], refs[4 * n:]
        for k in range(n):
            w_ref, g_ref, m_ref, v_ref = ins[4 * k:4 * k + 4]
            d_ref, mo_ref, vo_ref = outs[3 * k:3 * k + 3]
            d_ref[...], mo_ref[...], vo_ref[...] = _adam_step(w_ref[...], g_ref[...], m_ref[...], v_ref[...])

    def whole(a):
        return pl.BlockSpec(a.shape, lambda i: (0, 0))

    results = pl.pallas_call(
        body, name=name, grid=(1,),
        in_specs=[whole(a) for item in items for a in item],
        out_specs=[whole(item[0]) for item in items for _ in range(3)],
        out_shape=[jax.ShapeDtypeStruct(item[0].shape, f32) for item in items for _ in range(3)],
        compiler_params=_params(1),
    )(*[a for item in items for a in item])
    return [results[3 * k:3 * k + 3] for k in range(n)]


def _adamw_reduced(w, parts, m, v, name):
    layers, rows, cols = w.shape

    def body(*refs):
        w_ref, m_ref, v_ref = refs[:3]
        part_refs = refs[3:3 + layers]
        g_ref, d_ref, mo_ref, vo_ref = refs[3 + layers:]
        layer = pl.program_id(0)
        for l, p_ref in enumerate(part_refs):
            @pl.when(layer == l)
            def _():
                acc = p_ref[0].astype(f32)
                for k in range(1, N_CHIP):
                    acc = acc + p_ref[k].astype(f32)
                g_ref[0] = acc

        d_ref[0], mo_ref[0], vo_ref[0] = _adam_step(w_ref[0], g_ref[0], m_ref[0], v_ref[0])

    blk = pl.BlockSpec((1, rows, cols), lambda l: (l, 0, 0))
    return pl.pallas_call(
        body, name=name, grid=(layers,),
        in_specs=[blk] * 3 + [pl.BlockSpec(p.shape, lambda l: (0, 0, 0)) for p in parts],
        out_specs=[blk] * 4,
        out_shape=[jax.ShapeDtypeStruct(w.shape, f32)] * 4,
        compiler_params=_params(1),
    )(w, m, v, *parts)


def _sum_leading(x, name):
    n, rows, cols = x.shape
    tr = _row_tile(rows)

    def body(x_ref, o_ref):
        acc = x_ref[0].astype(f32)
        for k in range(1, n):
            acc = acc + x_ref[k].astype(f32)
        o_ref[...] = acc

    return pl.pallas_call(
        body, name=name, grid=(rows // tr,),
        in_specs=[pl.BlockSpec((n, tr, cols), lambda i: (0, i, 0))],
        out_specs=pl.BlockSpec((tr, cols), lambda i: (i, 0)),
        out_shape=jax.ShapeDtypeStruct((rows, cols), f32),
        compiler_params=_params(1),
    )(x)


def _pair_sum(gs, recvs, c_idx, name):
    n = len(gs)

    def body(c_ref, *refs):
        for g_ref, r_ref, o_ref in zip(refs[:n], refs[n:2 * n], refs[2 * n:]):
            o_ref[...] = (g_ref[...].astype(f32) + r_ref[...].astype(f32)).astype(o_ref.dtype)

    own = [pl.BlockSpec((1,) + g.shape[1:], lambda k, c: (2 * k + c[0], 0, 0)) for g in gs]
    by_chip = [pl.BlockSpec((1,) + g.shape[1:], lambda k, c: (k, 0, 0)) for g in gs]
    return list(pl.pallas_call(
        body, name=name,
        grid_spec=pltpu.PrefetchScalarGridSpec(num_scalar_prefetch=1, grid=(N_CHIP,),
                                               in_specs=own + by_chip, out_specs=by_chip),
        out_shape=[jax.ShapeDtypeStruct((N_CHIP,) + g.shape[1:], g.dtype) for g in gs],
        compiler_params=_params(1),
    )(c_idx, *gs, *recvs))


def _pack_rows(w):
    return w.reshape(N_DEV, -1, D_MODEL)


def kernel(x, even_w_in, even_w_out, a_w_s, a_b_s, a_ln_g, a_ln_b, b_w_pool, b_scale, odd_w_in, odd_w_out, c_w_dw, c_b_dw, c_ln_g, c_ln_b, d_w_dw, norm_mix_g, norm_ffn_g, ffn_w_gate, ffn_w_up, ffn_w_down, final_norm_g, loss_target, m_even_w_in, m_even_w_out, m_a_w_s, m_a_b_s, m_a_ln_g, m_a_ln_b, m_b_w_pool, m_b_scale, m_odd_w_in, m_odd_w_out, m_c_w_dw, m_c_b_dw, m_c_ln_g, m_c_ln_b, m_d_w_dw, m_norm_mix_g, m_norm_ffn_g, m_ffn_w_gate, m_ffn_w_up, m_ffn_w_down, m_final_norm_g, v_even_w_in, v_even_w_out, v_a_w_s, v_a_b_s, v_a_ln_g, v_a_ln_b, v_b_w_pool, v_b_scale, v_odd_w_in, v_odd_w_out, v_c_w_dw, v_c_b_dw, v_c_ln_g, v_c_ln_b, v_d_w_dw, v_norm_mix_g, v_norm_ffn_g, v_ffn_w_gate, v_ffn_w_up, v_ffn_w_down, v_final_norm_g):
    weights = dict(even_w_in=even_w_in, even_w_out=even_w_out, a_w_s=a_w_s, a_b_s=a_b_s, a_ln_g=a_ln_g, a_ln_b=a_ln_b,
                   b_w_pool=b_w_pool, b_scale=b_scale, odd_w_in=odd_w_in, odd_w_out=odd_w_out, c_w_dw=c_w_dw,
                   c_b_dw=c_b_dw, c_ln_g=c_ln_g, c_ln_b=c_ln_b, d_w_dw=d_w_dw, norm_mix_g=norm_mix_g,
                   norm_ffn_g=norm_ffn_g, ffn_w_gate=ffn_w_gate, ffn_w_up=ffn_w_up, ffn_w_down=ffn_w_down,
                   final_norm_g=final_norm_g)
    m_in = dict(even_w_in=m_even_w_in, even_w_out=m_even_w_out, a_w_s=m_a_w_s, a_b_s=m_a_b_s, a_ln_g=m_a_ln_g,
                a_ln_b=m_a_ln_b, b_w_pool=m_b_w_pool, b_scale=m_b_scale, odd_w_in=m_odd_w_in, odd_w_out=m_odd_w_out,
                c_w_dw=m_c_w_dw, c_b_dw=m_c_b_dw, c_ln_g=m_c_ln_g, c_ln_b=m_c_ln_b, d_w_dw=m_d_w_dw,
                norm_mix_g=m_norm_mix_g, norm_ffn_g=m_norm_ffn_g, ffn_w_gate=m_ffn_w_gate, ffn_w_up=m_ffn_w_up,
                ffn_w_down=m_ffn_w_down, final_norm_g=m_final_norm_g)
    v_in = dict(even_w_in=v_even_w_in, even_w_out=v_even_w_out, a_w_s=v_a_w_s, a_b_s=v_a_b_s, a_ln_g=v_a_ln_g,
                a_ln_b=v_a_ln_b, b_w_pool=v_b_w_pool, b_scale=v_b_scale, odd_w_in=v_odd_w_in, odd_w_out=v_odd_w_out,
                c_w_dw=v_c_w_dw, c_b_dw=v_c_b_dw, c_ln_g=v_c_ln_g, c_ln_b=v_c_ln_b, d_w_dw=v_d_w_dw,
                norm_mix_g=v_norm_mix_g, norm_ffn_g=v_norm_ffn_g, ffn_w_gate=v_ffn_w_gate, ffn_w_up=v_ffn_w_up,
                ffn_w_down=v_ffn_w_down, final_norm_g=v_final_norm_g)
    names = list(weights)

    group_parts = {
        "even": [even_w_in[0].T, even_w_out[0]],
        "ffn0": [ffn_w_gate[0].T, ffn_w_up[0].T, ffn_w_down[0]],
        "odd": [odd_w_in[0].T, odd_w_out[0]],
        "ffn1": [ffn_w_gate[1].T, ffn_w_up[1].T, ffn_w_down[1]],
    }

    def gather_jobs(*groups):
        return [_all_gather_job(p.astype(bf16)) for k in groups for p in group_parts[k]]

    def whole(gathered):
        return [g.reshape(-1, D_MODEL) for g in gathered]

    conv_names = ["c_w_dw", "c_b_dw", "c_ln_g", "c_ln_b", "d_w_dw"]
    conv_rows = [C_KERNEL, 1, 1, 1, D_KERNEL]
    conv_local = jnp.concatenate([weights[n].reshape(r, -1) for n, r in zip(conv_names, conv_rows)]
                                 + [jnp.zeros((3, c_b_dw.shape[-1]), f32)], axis=0)
    *even_gathered, conv_all = _run_jobs(gather_jobs("even") + [_all_gather_job(conv_local)], "gather_even_conv")
    w_in_e, w_out_e = whole(even_gathered)
    conv_all = conv_all.transpose(1, 0, 2).reshape(conv_local.shape[0], -1)
    conv_offs = [sum(conv_rows[:k]) for k in range(len(conv_rows) + 1)]
    cw, cb, clg, clb, dw = [conv_all[conv_offs[k]:conv_offs[k + 1]] for k in range(len(conv_rows))]

    ws, bst = a_w_s[0], a_b_s[0].T
    lng, lnb, wp, sc = a_ln_g, a_ln_b, b_w_pool[0], b_scale
    gmix = [norm_mix_g[l:l + 1] for l in range(2)]
    gffn = [norm_ffn_g[l:l + 1] for l in range(2)]
    gfin = final_norm_g.reshape(1, D_MODEL)

    h0 = x[0]
    h1, hn_e, za, pooled, mix_e, *ffn0_gathered = _even_fwd(
        h0, w_in_e, w_out_e, ws, bst, lng, lnb, wp, sc, gmix[0], jobs=gather_jobs("ffn0"))
    w_gate0, w_up0, w_down0 = whole(ffn0_gathered)
    h2, hn_f0, gate0, up0, *rest_gathered = _ffn_fwd(h1, w_gate0, w_up0, w_down0, gffn[0],
                                                     jobs=gather_jobs("odd", "ffn1"))
    w_in_o, w_out_o, w_gate1, w_up1, w_down1 = whole(rest_gathered)
    h3, hn_o, z_o, mix_o, cv_o = _odd_fwd(h2, w_in_o, w_out_o, cw, cb, clg, clb, dw, gmix[1])
    dh4, hn_f1, gate1, up1, loss_local, g_final = _ffn_fwd(h3, w_gate1, w_up1, w_down1, gffn[1],
                                                           head=(loss_target[0], gfin))

    c_idx = lax.axis_index("c").astype(jnp.int32).reshape(1)

    def weight_grad(x, y, name, jobs=()):
        g, *job_results = _weight_grads([(x, y)], name, jobs=jobs)
        return [_pack_rows(g)] + job_results

    def siblings(parts):
        return [_sibling_exchange_job(p) for p in parts]

    def chips(pairs):
        return [_chip_exchange_job(p) for p in pairs]

    dh3, dgate1, dup1, act1, g_ffn1 = _ffn_bwd(dh4, h3, gate1, up1, w_gate1, w_up1, w_down1, gffn[1])
    part_ffn1 = (weight_grad(dgate1, hn_f1, "dw_gate1") + weight_grad(dup1, hn_f1, "dw_up1")
                 + weight_grad(act1, dh4, "dw_down1"))
    dh2, dz_o, g_cw, g_cb, g_clg, g_clb, g_dw, g_mix1, *recv_ffn1 = _odd_bwd(
        dh3, h2, z_o, cv_o, w_in_o, w_out_o, cw, clg, clb, dw, gmix[1], jobs=siblings(part_ffn1))
    pair_ffn1 = _pair_sum(part_ffn1, recv_ffn1, c_idx, "pair_sum_ffn1")
    part_odd = weight_grad(dz_o, hn_o, "dw_odd_in") + weight_grad(mix_o, dh3, "dw_odd_out")
    dh1, dgate0, dup0, act0, g_ffn0, *exchanged = _ffn_bwd(
        dh2, h1, gate0, up0, w_gate0, w_up0, w_down0, gffn[0], jobs=chips(pair_ffn1) + siblings(part_odd))
    chips_ffn1, recv_odd = exchanged[:3], exchanged[3:]
    pair_odd = _pair_sum(part_odd, recv_odd, c_idx, "pair_sum_odd")
    dw_gate0, *chips_odd = weight_grad(dgate0, hn_f0, "dw_gate0", jobs=chips(pair_odd))
    part_ffn0 = [dw_gate0] + weight_grad(dup0, hn_f0, "dw_up0") + weight_grad(act0, dh2, "dw_down0")
    part_even_out, *recv_ffn0 = weight_grad(mix_e, dh1, "dw_even_out", jobs=siblings(part_ffn0))
    pair_ffn0 = _pair_sum(part_ffn0, recv_ffn0, c_idx, "pair_sum_ffn0")
    dh0, dz_e, g_ws, g_bs, g_lng, g_lnb, g_wp, g_sc, g_mix0, *exchanged = _even_bwd(
        dh1, h0, za, pooled, w_in_e, w_out_e, ws, bst, lng, lnb, wp, sc, gmix[0],
        jobs=chips(pair_ffn0) + siblings([part_even_out]))
    chips_ffn0, recv_even_out = exchanged[:3], exchanged[3:]
    pair_even_out = _pair_sum([part_even_out], recv_even_out, c_idx, "pair_sum_even_out")

    lanes = HEAD
    small = [("a_w_s", g_ws), ("a_b_s", g_bs), ("a_ln_g", g_lng), ("a_ln_b", g_lnb), ("b_w_pool", g_wp),
             ("b_scale", g_sc), ("norm_mix_g", jnp.concatenate([g_mix0, g_mix1], axis=0)),
             ("norm_ffn_g", jnp.concatenate([g_ffn0, g_ffn1], axis=0)), ("final_norm_g", g_final),
             ("c_w_dw", g_cw), ("c_b_dw", g_cb), ("c_ln_g", g_clg), ("c_ln_b", g_clb), ("d_w_dw", g_dw),
             ("loss", loss_local)]
    small_rows = [-(-g.size // (8 * lanes)) * 8 for _, g in small]
    small_offs = [sum(small_rows[:k]) for k in range(len(small) + 1)]
    pad_rows = -small_offs[-1] % 256
    small_buf = jnp.concatenate(
        [jnp.pad(g.reshape(-1), (0, r * lanes - g.size)).reshape(r, lanes) for (_, g), r in zip(small, small_rows)]
        + [jnp.zeros((pad_rows, lanes), f32)], axis=0)
    part_even_in, small_all, chips_even_out = weight_grad(
        dz_e, hn_e, "dw_even_in", jobs=[_all_gather_job(small_buf)] + chips(pair_even_out))
    small_sum = _sum_leading(small_all, "small_grad_sum")
    recv_even_in = _run_jobs(siblings([part_even_in]), "sibling_exchange_even_in")
    chips_even_in = _run_jobs(chips(_pair_sum([part_even_in], recv_even_in, c_idx, "pair_sum_even_in")),
                              "chip_exchange_even_in")
    grads = {}
    for k, (n, g) in enumerate(small):
        grads[n] = small_sum[small_offs[k]:small_offs[k + 1]].reshape(-1)[:g.size].reshape(g.shape)
    me = 4 * lax.axis_index("x") + 2 * lax.axis_index("y") + lax.axis_index("c")
    shard = c_b_dw.shape[-1]
    for n in conv_names:
        grads[n] = lax.dynamic_slice_in_dim(grads[n], me * shard, shard, axis=1)

    col_sharded = ("even_w_in", "odd_w_in", "ffn_w_gate", "ffn_w_up")

    def rows_view(n, a):
        return jnp.swapaxes(a, -1, -2) if n in col_sharded else a

    loss = grads.pop("loss")[0, 0]
    chip_parts = {"even_w_in": chips_even_in, "even_w_out": [chips_even_out],
                  "odd_w_in": chips_odd[:1], "odd_w_out": chips_odd[1:]}
    for k, n in enumerate(["ffn_w_gate", "ffn_w_up", "ffn_w_down"]):
        chip_parts[n] = [chips_ffn0[k], chips_ffn1[k]]

    delta, new_m, new_v = {}, {}, {}
    for n in chip_parts:
        w_rows, m_rows, v_rows = [rows_view(n, a) for a in (weights[n], m_in[n], v_in[n])]
        outs = _adamw_reduced(w_rows, chip_parts[n], m_rows, v_rows, "adamw_" + n)
        grads[n], delta[n], new_m[n], new_v[n] = [rows_view(n, o) for o in outs]
    others = [n for n in names if n not in chip_parts]
    view = {n: (-1, weights[n].shape[-1]) for n in others}
    stepped = _adamw_small([tuple(a.reshape(view[n]) for a in (weights[n], grads[n], m_in[n], v_in[n])) for n in others],
                           "adamw_small")
    for n, outs in zip(others, stepped):
        grads[n] = grads[n].reshape(weights[n].shape)
        delta[n], new_m[n], new_v[n] = [o.reshape(weights[n].shape) for o in outs]

    return (loss, dh0[None], *[grads[n] for n in names], *[delta[n] for n in names],
            *[new_m[n] for n in names], *[new_v[n] for n in names])
```

```python
import jax
import jax.numpy as jnp
from jax import lax
from jax.experimental import pallas as pl
from jax.experimental.pallas import tpu as pltpu

f32 = jnp.float32
bf16 = jnp.bfloat16

EPS = 1e-6
D_MODEL = 1024
A_WIDTH = 512
HEAD = 128
N_HEADS = 4
CHUNK = 64
POOL_WINDOWS = (2, 4, 8, 16)
POOL_HALO = 16
C_KERNEL = 31
D_KERNEL = 3
CONV_HALO = 32
D_FF = 2816
N_DEV = 8
N_CHIP = 4

ADAM_LR = 0.001
ADAM_B1 = 0.9
ADAM_B2 = 0.999
ADAM_EPS = 1e-08
ADAM_WD = 0.01
ADAM_STEP = 10

MIX_TILE = 512
ODD_FWD_TILE = 1024
FFN_TILE = 256
FFN_CHUNKS = (1536, 1280)
DW_TK = 2048
DW_TM = 1536
MIDDLE_AT, MIDDLE_OF = 7, 8
VMEM_LIMIT = 56 * 1024 * 1024

MESH = pl.DeviceIdType.MESH
ANY = pl.BlockSpec(memory_space=pl.ANY)


def _params(n_axes):
    return pltpu.CompilerParams(dimension_semantics=("arbitrary",) * n_axes, vmem_limit_bytes=VMEM_LIMIT)


def _mm(a, b):
    return jnp.dot(a, b, preferred_element_type=f32)


def _mm_nt(a, b):
    return lax.dot_general(a, b, (((1,), (1,)), ((), ())), preferred_element_type=f32)


def _mm_tn(a, b):
    return lax.dot_general(a, b, (((0,), (0,)), ((), ())), preferred_element_type=f32)


def _sigmoid(x):
    return 1.0 / (1.0 + jnp.exp(-x))


def _rms_r(h):
    return lax.rsqrt(jnp.mean(h * h, axis=-1, keepdims=True) + EPS)


def _rms_bwd(dy, h, g):
    r = _rms_r(h)
    xh = h * r
    dxh = dy * g
    dh = r * (dxh - xh * jnp.mean(dxh * xh, axis=-1, keepdims=True))
    return dh, jnp.sum(dy * xh, axis=0, keepdims=True)


def _ln_fwd(x, g, b):
    mu = jnp.mean(x, axis=-1, keepdims=True)
    xc = x - mu
    r = lax.rsqrt(jnp.mean(xc * xc, axis=-1, keepdims=True) + EPS)
    xh = xc * r
    return xh * g + b, xh, r


def _ln_bwd(dy, xh, r, g):
    dxh = dy * g
    return r * (dxh - jnp.mean(dxh, axis=-1, keepdims=True) - xh * jnp.mean(dxh * xh, axis=-1, keepdims=True))


_GELU_C = 0.7978845608028654
_GELU_A = 0.044715


def _gelu(x):
    th = jnp.tanh(x * (_GELU_C + (_GELU_C * _GELU_A) * (x * x)))
    half = 0.5 * x
    return half + half * th, th


def _gelu_grad(x, th):
    return 0.5 + 0.5 * th + (1.0 - th * th) * (x * (0.5 * _GELU_C + (1.5 * _GELU_C * _GELU_A) * (x * x)))


def _down(x, k):
    return x if k == 0 else pltpu.roll(x, k, 0)


def _up(x, k):
    return x if k == 0 else pltpu.roll(x, x.shape[0] - k, 0)


def _window_sum(x, win, shift):
    s = x
    step = 1
    while step < win:
        s = s + shift(s, step)
        step *= 2
    return s


def _inv_count(t0, rows, win):
    t = t0 + lax.broadcasted_iota(jnp.int32, (rows, 1), 0)
    return 1.0 / jnp.minimum(t + 1, win).astype(f32)


def _chunk_mask():
    i = lax.broadcasted_iota(jnp.int32, (HEAD, HEAD), 0)
    j = lax.broadcasted_iota(jnp.int32, (HEAD, HEAD), 1)
    return jnp.logical_or(i >= CHUNK, j < CHUNK)


def _const(shape, n_axes):
    zeros = (0,) * len(shape)
    if n_axes == 1:
        return pl.BlockSpec(shape, lambda i: zeros)
    return pl.BlockSpec(shape, lambda i, j: zeros)


def _prev_halo(tile, halo, cols):
    return pl.BlockSpec((halo, cols), lambda i: (jnp.maximum(i * (tile // halo) - 1, 0), 0))


def _next_halo(tile, halo, cols, seq):
    return pl.BlockSpec((halo, cols), lambda i: (jnp.minimum((i + 1) * (tile // halo), seq // halo - 1), 0))


class _Job:
    def __init__(self, inputs, out_shape, sems, hooks):
        self.inputs, self.out_shape, self.sems, self.hooks = inputs, out_shape, sems, hooks


def _position():
    return lax.axis_index("x"), lax.axis_index("y"), lax.axis_index("c")


def _all_gather_job(block):
    rows, cols = block.shape

    def hooks(ins, outs, sems):
        (x_ref,), (out_ref,), (send_sems, recv_sems, local_sem) = ins, outs, sems
        x, y, c = _position()
        me, sibling = (x, y, c), (x, y, 1 - c)
        chips = [(1 - x, y), (x, 1 - y), (1 - x, 1 - y)]

        def slot(px, py, pc):
            return out_ref.at[4 * px + 2 * py + pc]

        def copy(k, block_of, to, src=None):
            return pltpu.make_async_remote_copy(
                src_ref=slot(*block_of) if src is None else src, dst_ref=slot(*block_of),
                send_sem=send_sems.at[k], recv_sem=recv_sems.at[k], device_id=to, device_id_type=MESH)

        mine = pltpu.make_async_copy(x_ref, slot(*me), local_sem)
        first = [copy(0, me, sibling, src=x_ref)]
        first += [copy(1 + j, me, (*chip, c), src=x_ref) for j, chip in enumerate(chips)]
        passed = [copy(4 + j, (*chip, c), sibling) for j, chip in enumerate(chips)]

        def start():
            mine.start()
            for cp in first:
                cp.start()

        def middle():
            for j, chip in enumerate(chips):
                copy(1 + j, (*chip, c), me).wait_recv()
                passed[j].start()

        def finish():
            copy(0, sibling, me).wait_recv()
            for j, chip in enumerate(chips):
                copy(4 + j, (*chip, 1 - c), me).wait_recv()
            for cp in first + passed:
                cp.wait_send()
            mine.wait()

        return start, middle, finish

    return _Job([block], [jax.ShapeDtypeStruct((N_DEV, rows, cols), block.dtype)],
                [pltpu.SemaphoreType.DMA((7,)), pltpu.SemaphoreType.DMA((7,)), pltpu.SemaphoreType.DMA], hooks)


def _sibling_exchange_job(g):
    _, rows, cols = g.shape

    def hooks(ins, outs, sems):
        (g_ref,), (recv_ref,), (send_sems, recv_sems) = ins, outs, sems
        x, y, c = _position()
        copies = [pltpu.make_async_remote_copy(
            src_ref=g_ref.at[2 * k + (1 - c)], dst_ref=recv_ref.at[k], send_sem=send_sems.at[k],
            recv_sem=recv_sems.at[k], device_id=(x, y, 1 - c), device_id_type=MESH) for k in range(N_CHIP)]

        def start():
            for cp in copies:
                cp.start()

        def finish():
            for cp in copies:
                cp.wait()

        return start, lambda: None, finish

    return _Job([g], [jax.ShapeDtypeStruct((N_CHIP, rows, cols), g.dtype)],
                [pltpu.SemaphoreType.DMA((N_CHIP,)), pltpu.SemaphoreType.DMA((N_CHIP,))], hooks)


def _chip_exchange_job(p):
    _, rows, cols = p.shape

    def hooks(ins, outs, sems):
        (p_ref,), (recv_ref,), (send_sems, recv_sems, local_sem) = ins, outs, sems
        x, y, c = _position()
        k_me = 2 * x + y
        mine = pltpu.make_async_copy(p_ref.at[k_me], recv_ref.at[k_me], local_sem)
        copies = [pltpu.make_async_remote_copy(
            src_ref=p_ref.at[2 * px + py], dst_ref=recv_ref.at[k_me], send_sem=send_sems.at[j],
            recv_sem=recv_sems.at[j], device_id=(px, py, c), device_id_type=MESH)
            for j, (px, py) in enumerate([(1 - x, y), (x, 1 - y), (1 - x, 1 - y)])]

        def start():
            mine.start()
            for cp in copies:
                cp.start()

        def finish():
            for cp in copies:
                cp.wait()
            mine.wait()

        return start, lambda: None, finish

    return _Job([p], [jax.ShapeDtypeStruct((N_CHIP, rows, cols), p.dtype)],
                [pltpu.SemaphoreType.DMA((3,)), pltpu.SemaphoreType.DMA((3,)), pltpu.SemaphoreType.DMA], hooks)


def _job_hooks(jobs, ins, outs, sems):
    hooks = []
    for job in jobs:
        n_in, n_out, n_sem = len(job.inputs), len(job.out_shape), len(job.sems)
        hooks.append(job.hooks(ins[:n_in], outs[:n_out], sems[:n_sem]))
        ins, outs, sems = ins[n_in:], outs[n_out:], sems[n_sem:]
    return hooks


def _run_jobs(jobs, name):
    n_in = sum(len(job.inputs) for job in jobs)
    n_out = sum(len(job.out_shape) for job in jobs)

    def body(*refs):
        hooks = _job_hooks(jobs, refs[:n_in], refs[n_in:n_in + n_out], refs[n_in + n_out:])
        for phase in range(3):
            for h in hooks:
                h[phase]()

    return list(pl.pallas_call(
        body, name=name, in_specs=[ANY] * n_in, out_specs=[ANY] * n_out,
        out_shape=[s for job in jobs for s in job.out_shape],
        scratch_shapes=[s for job in jobs for s in job.sems],
    )(*[a for job in jobs for a in job.inputs]))


def _exchange_in_vmem(g, name):
    _, rows, cols = g.shape
    vmem = pl.BlockSpec(memory_space=pltpu.VMEM)

    def body(g_ref, out_ref, recv_ref, pair_ref, sib_send, sib_recv, chip_send, chip_recv):
        x, y, c = _position()
        to_sibling = [pltpu.make_async_remote_copy(
            src_ref=g_ref.at[2 * k + (1 - c)], dst_ref=recv_ref.at[k], send_sem=sib_send.at[k],
            recv_sem=sib_recv.at[k], device_id=(x, y, 1 - c), device_id_type=MESH) for k in range(N_CHIP)]
        for cp in to_sibling:
            cp.start()
        for cp in to_sibling:
            cp.wait()
        for k in range(N_CHIP):
            pair_ref[k] = (g_ref[2 * k + c].astype(f32) + recv_ref[k].astype(f32)).astype(pair_ref.dtype)
        k_me = 2 * x + y
        to_chips = [pltpu.make_async_remote_copy(
            src_ref=pair_ref.at[2 * px + py], dst_ref=out_ref.at[k_me], send_sem=chip_send.at[j],
            recv_sem=chip_recv.at[j], device_id=(px, py, c), device_id_type=MESH)
            for j, (px, py) in enumerate([(1 - x, y), (x, 1 - y), (1 - x, 1 - y)])]
        for cp in to_chips:
            cp.start()
        out_ref[k_me] = pair_ref[k_me]
        for cp in to_chips:
            cp.wait()

    return pl.pallas_call(
        body, name=name, in_specs=[vmem], out_specs=vmem,
        out_shape=jax.ShapeDtypeStruct((N_CHIP, rows, cols), g.dtype),
        scratch_shapes=[pltpu.VMEM((N_CHIP, rows, cols), g.dtype), pltpu.VMEM((N_CHIP, rows, cols), g.dtype),
                        pltpu.SemaphoreType.DMA((N_CHIP,)), pltpu.SemaphoreType.DMA((N_CHIP,)),
                        pltpu.SemaphoreType.DMA((3,)), pltpu.SemaphoreType.DMA((3,))],
    )(g)


def _launch(body, *, name, grid, in_specs, out_specs, out_shape, args, scratch=(), jobs=()):
    in_specs, out_specs, out_shape, scratch = list(in_specs), list(out_specs), list(out_shape), list(scratch)
    if not jobs:
        return list(pl.pallas_call(body, name=name, grid=grid, in_specs=in_specs, out_specs=out_specs,
                                   out_shape=out_shape, scratch_shapes=scratch,
                                   compiler_params=_params(len(grid)))(*args))
    n_in, n_out, n_sc = len(in_specs), len(out_specs), len(scratch)
    j_in = [a for job in jobs for a in job.inputs]
    j_out = [s for job in jobs for s in job.out_shape]
    j_sems = [s for job in jobs for s in job.sems]
    n_steps = 1
    for g in grid:
        n_steps *= g

    def wrapped(*refs):
        ins, refs = refs[:n_in], refs[n_in:]
        jins, refs = refs[:len(j_in)], refs[len(j_in):]
        outs, refs = refs[:n_out], refs[n_out:]
        jouts, refs = refs[:len(j_out)], refs[len(j_out):]
        sc, jsems = refs[:n_sc], refs[n_sc:]
        step = pl.program_id(0)
        for axis in range(1, len(grid)):
            step = step * grid[axis] + pl.program_id(axis)
        hooks = _job_hooks(jobs, jins, jouts, jsems)

        @pl.when(step == 0)
        def _():
            for h in hooks:
                h[0]()

        body(*ins, *outs, *sc)

        @pl.when(step == (MIDDLE_AT * n_steps) // MIDDLE_OF)
        def _():
            for h in hooks:
                h[1]()

        @pl.when(step == n_steps - 1)
        def _():
            for h in hooks:
                h[2]()

    return list(pl.pallas_call(
        wrapped, name=name, grid=grid, in_specs=in_specs + [ANY] * len(j_in), out_specs=out_specs + [ANY] * len(j_out),
        out_shape=out_shape + j_out, scratch_shapes=scratch + j_sems, compiler_params=_params(len(grid)),
    )(*args, *j_in))


def _gmlp_gate(vnb, wsm, bst, tile):
    rows = []
    for n in range(tile // HEAD):
        cols = []
        for hh in range(N_HEADS):
            blk = vnb[n * HEAD:(n + 1) * HEAD, hh * HEAD:(hh + 1) * HEAD]
            cols.append(_mm(wsm[hh], blk) + bst[:, hh:hh + 1])
        rows.append(jnp.concatenate(cols, axis=1))
    return jnp.concatenate(rows, axis=0)


def _even_fwd(h, w_in, w_out, ws, bst, lng, lnb, wp, sc, gm, jobs=()):
    seq = h.shape[0]
    tile = min(MIX_TILE, seq)
    n_tiles = seq // tile

    def body(h_ref, hp_ref, win_ref, wout_ref, ws_ref, bst_ref, lng_ref, lnb_ref, wp_ref, sc_ref, g_ref,
             ho_ref, hn_ref, za_ref, pool_ref, mix_ref):
        i = pl.program_id(0)
        g = g_ref[...]
        h = h_ref[...]
        hnb = (h * _rms_r(h) * g).astype(bf16)
        hn_ref[...] = hnb
        z = _mm_nt(hnb, win_ref[...])
        zab = z[:, :2 * A_WIDTH].astype(bf16)
        za_ref[...] = zab
        hp = hp_ref[...]
        zbp = _mm_nt((hp * _rms_r(hp) * g).astype(bf16), win_ref[2 * A_WIDTH:, :])
        zbe = jnp.concatenate([jnp.where(i > 0, zbp, 0.0), z[:, 2 * A_WIDTH:]], axis=0)
        pooled = []
        for gi, win in enumerate(POOL_WINDOWS):
            xg = zbe[:, gi * HEAD:(gi + 1) * HEAD]
            s = _window_sum(xg, win, _down)
            pooled.append(s[POOL_HALO:] * _inv_count(i * tile, tile, win) - xg[POOL_HALO:])
        plb = jnp.concatenate(pooled, axis=1).astype(bf16)
        pool_ref[...] = plb

        ga, _ = _gelu(zab.astype(f32))
        vn, _, _ = _ln_fwd(ga[:, A_WIDTH:], lng_ref[...], lnb_ref[...])
        mask = _chunk_mask()
        wsm = [jnp.where(mask, ws_ref[hh], 0.0).astype(bf16) for hh in range(N_HEADS)]
        ya = ga[:, :A_WIDTH] * _gmlp_gate(vn.astype(bf16), wsm, bst_ref[...], tile)
        yb = jnp.concatenate([_mm(plb[:, gi * HEAD:(gi + 1) * HEAD], wp_ref[gi].astype(bf16))
                              for gi in range(len(POOL_WINDOWS))], axis=1) * sc_ref[...]
        mix = jnp.concatenate([ya, yb], axis=1).astype(bf16)
        mix_ref[...] = mix
        ho_ref[...] = h + _mm(mix, wout_ref[...])

    row = lambda cols: pl.BlockSpec((tile, cols), lambda i: (i, 0))
    return _launch(
        body, name="even_fwd", grid=(n_tiles,), jobs=jobs,
        in_specs=[row(D_MODEL), _prev_halo(tile, POOL_HALO, D_MODEL), _const(w_in.shape, 1), _const(w_out.shape, 1),
                  _const(ws.shape, 1), _const(bst.shape, 1), _const(lng.shape, 1), _const(lnb.shape, 1),
                  _const(wp.shape, 1), _const(sc.shape, 1), _const(gm.shape, 1)],
        out_specs=[row(D_MODEL), row(D_MODEL), row(2 * A_WIDTH), row(A_WIDTH), row(D_MODEL)],
        out_shape=[jax.ShapeDtypeStruct((seq, D_MODEL), f32), jax.ShapeDtypeStruct((seq, D_MODEL), bf16),
                   jax.ShapeDtypeStruct((seq, 2 * A_WIDTH), bf16), jax.ShapeDtypeStruct((seq, A_WIDTH), bf16),
                   jax.ShapeDtypeStruct((seq, D_MODEL), bf16)],
        args=(h, h, w_in, w_out, ws, bst, lng, lnb, wp, sc, gm))


def _even_bwd(dh, h, za, pooled, w_in, w_out, ws, bst, lng, lnb, wp, sc, gm, jobs=()):
    seq = h.shape[0]
    tile = min(MIX_TILE, seq)
    n_tiles = seq // tile
    n_groups = len(POOL_WINDOWS)

    def body(dh_ref, dhx_ref, h_ref, za_ref, pool_ref, win_ref, wout_ref, ws_ref, bst_ref, lng_ref, lnb_ref,
             wp_ref, sc_ref, g_ref,
             dhi_ref, dz_ref, dws_ref, dbs_ref, dlng_ref, dlnb_ref, dwp_ref, dsc_ref, dg_ref):
        i = pl.program_id(0)

        @pl.when(i == 0)
        def _():
            for ref in (dws_ref, dbs_ref, dlng_ref, dlnb_ref, dwp_ref, dsc_ref, dg_ref):
                ref[...] = jnp.zeros_like(ref)

        dh = dh_ref[...]
        dmix = _mm_nt(dh.astype(bf16), wout_ref[...])
        dya = dmix[:, :A_WIDTH]
        dyb = dmix[:, A_WIDTH:]
        dybx = _mm_nt(dhx_ref[...].astype(bf16), wout_ref[A_WIDTH:, :])
        dybx = jnp.where(i < n_tiles - 1, dybx, 0.0)

        za = za_ref[...].astype(f32)
        ga, th = _gelu(za)
        u = ga[:, :A_WIDTH]
        lng = lng_ref[...]
        vn, vh, r = _ln_fwd(ga[:, A_WIDTH:], lng, lnb_ref[...])
        vnb = vn.astype(bf16)
        mask = _chunk_mask()
        wsf = [jnp.where(mask, ws_ref[hh], 0.0) for hh in range(N_HEADS)]
        sv = _gmlp_gate(vnb, [w.astype(bf16) for w in wsf], bst_ref[...], tile)
        du = dya * sv
        dsvb = (dya * u).astype(bf16)
        wst = [w.T.astype(bf16) for w in wsf]
        ones = jnp.ones((8, HEAD), bf16)
        dws = [jnp.zeros((HEAD, HEAD), f32) for _ in range(N_HEADS)]
        dbs = [jnp.zeros((8, HEAD), f32) for _ in range(N_HEADS)]
        rows = []
        for n in range(tile // HEAD):
            cols = []
            for hh in range(N_HEADS):
                blk = dsvb[n * HEAD:(n + 1) * HEAD, hh * HEAD:(hh + 1) * HEAD]
                cols.append(_mm(wst[hh], blk))
                dws[hh] = dws[hh] + _mm_nt(blk, vnb[n * HEAD:(n + 1) * HEAD, hh * HEAD:(hh + 1) * HEAD])
                dbs[hh] = dbs[hh] + _mm_nt(ones, blk)
            rows.append(jnp.concatenate(cols, axis=1))
        dvn = jnp.concatenate(rows, axis=0)
        for hh in range(N_HEADS):
            dws_ref[hh] += jnp.where(mask, dws[hh], 0.0)
            dbs_ref[pl.ds(hh, 1), :] += dbs[hh][0:1, :]
        dlng_ref[...] += jnp.sum(dvn * vh, axis=0, keepdims=True)
        dlnb_ref[...] += jnp.sum(dvn, axis=0, keepdims=True)
        dv = _ln_bwd(dvn, vh, r, lng)
        dza = jnp.concatenate([du, dv], axis=1) * _gelu_grad(za, th)

        plb = pool_ref[...]
        sc = sc_ref[...]
        dzb = []
        dsc = []
        for gi, win in enumerate(POOL_WINDOWS):
            cs = slice(gi * HEAD, (gi + 1) * HEAD)
            wpb = wp_ref[gi].astype(bf16)
            dsc.append(jnp.sum(dyb[:, cs] * _mm(plb[:, cs], wpb), axis=0, keepdims=True))
            dpre = (dyb[:, cs] * sc[:, cs]).astype(bf16)
            dprex = (dybx[:, cs] * sc[:, cs]).astype(bf16)
            dwp_ref[gi] += _mm_tn(plb[:, cs], dpre)
            dpl = _mm_nt(dpre, wpb)
            dple = jnp.concatenate([dpl, _mm_nt(dprex, wpb)], axis=0)
            q = dple * _inv_count(i * tile, tile + POOL_HALO, win)
            dzb.append(_window_sum(q, win, _up)[:tile] - dpl)
        dsc_ref[...] += jnp.concatenate(dsc, axis=1)

        dzf = jnp.concatenate([dza] + dzb, axis=1).astype(bf16)
        dz_ref[...] = dzf
        dhn = _mm(dzf, win_ref[...])
        dhr, dg = _rms_bwd(dhn, h_ref[...], g_ref[...])
        dhi_ref[...] = dh + dhr
        dg_ref[...] += dg

    row = lambda cols: pl.BlockSpec((tile, cols), lambda i: (i, 0))
    small = [ws.shape, (N_HEADS, HEAD), lng.shape, lnb.shape, wp.shape, sc.shape, gm.shape]
    return _launch(
        body, name="even_bwd", grid=(n_tiles,), jobs=jobs,
        in_specs=[row(D_MODEL), _next_halo(tile, POOL_HALO, D_MODEL, seq), row(D_MODEL), row(2 * A_WIDTH), row(A_WIDTH),
                  _const(w_in.shape, 1), _const(w_out.shape, 1), _const(ws.shape, 1), _const(bst.shape, 1),
                  _const(lng.shape, 1), _const(lnb.shape, 1), _const(wp.shape, 1), _const(sc.shape, 1), _const(gm.shape, 1)],
        out_specs=[row(D_MODEL), row(3 * A_WIDTH)] + [_const(s, 1) for s in small],
        out_shape=[jax.ShapeDtypeStruct((seq, D_MODEL), f32), jax.ShapeDtypeStruct((seq, 3 * A_WIDTH), bf16)]
                  + [jax.ShapeDtypeStruct(s, f32) for s in small],
        args=(dh, dh, h, za, pooled, w_in, w_out, ws, bst, lng, lnb, wp, sc, gm))


SUBLANES = 8


class _Shifted:
    def __init__(self, x, shift, max_shift):
        self.rolled = [shift(x, b) for b in range(min(SUBLANES, max_shift + 1))]
        self.back = shift is _down

    def rows(self, k, start, count):
        whole = k - k % SUBLANES
        lo = start - whole if self.back else start + whole
        return self.rolled[k % SUBLANES][lo:lo + count]


def _conv_taps(xs, w_ref, n_taps, halo, rows):
    acc = None
    for j in range(n_taps):
        term = w_ref[pl.ds(j, 1), :] * xs.rows(n_taps - 1 - j, halo, rows)
        acc = term if acc is None else acc + term
    return acc


def _odd_fwd(h, w_in, w_out, cw, cb, clg, clb, dw, gm):
    seq = h.shape[0]
    tile = min(ODD_FWD_TILE, seq)
    n_tiles = seq // tile
    w = A_WIDTH

    def body(h_ref, hp_ref, win_ref, wout_ref, cw_ref, cb_ref, clg_ref, clb_ref, dw_ref, g_ref,
             ho_ref, hn_ref, z_ref, mix_ref, cv_ref):
        i = pl.program_id(0)
        g = g_ref[...]
        h = h_ref[...]
        hnb = (h * _rms_r(h) * g).astype(bf16)
        hn_ref[...] = hnb
        zb = _mm_nt(hnb, win_ref[...]).astype(bf16)
        z_ref[...] = zb
        hp = hp_ref[...]
        zp = _mm_nt((hp * _rms_r(hp) * g).astype(bf16), win_ref[...]).astype(bf16).astype(f32)
        z = zb.astype(f32)
        ze = jnp.concatenate([jnp.where(i > 0, zp, 0.0), z], axis=0)
        hc = ze[:, :w] * _sigmoid(ze[:, w:2 * w])
        cv = _conv_taps(_Shifted(hc, _down, C_KERNEL - 1), cw_ref, C_KERNEL, CONV_HALO, tile) + cb_ref[...]
        cv_ref[...] = cv
        ln, _, _ = _ln_fwd(cv, clg_ref[...], clb_ref[...])
        yc = ln * _sigmoid(ln)
        p = ze[:, 3 * w:4 * w] * ze[:, 4 * w:]
        yd = z[:, 2 * w:3 * w] * _conv_taps(_Shifted(p, _down, D_KERNEL - 1), dw_ref, D_KERNEL, CONV_HALO, tile)
        mix = jnp.concatenate([yc, yd], axis=1).astype(bf16)
        mix_ref[...] = mix
        ho_ref[...] = h + _mm(mix, wout_ref[...])

    row = lambda cols: pl.BlockSpec((tile, cols), lambda i: (i, 0))
    return pl.pallas_call(
        body, name="odd_fwd", grid=(n_tiles,),
        in_specs=[row(D_MODEL), _prev_halo(tile, CONV_HALO, D_MODEL), _const(w_in.shape, 1), _const(w_out.shape, 1),
                  _const(cw.shape, 1), _const(cb.shape, 1), _const(clg.shape, 1), _const(clb.shape, 1),
                  _const(dw.shape, 1), _const(gm.shape, 1)],
        out_specs=[row(D_MODEL), row(D_MODEL), row(5 * w), row(D_MODEL), row(w)],
        out_shape=[jax.ShapeDtypeStruct((seq, D_MODEL), f32), jax.ShapeDtypeStruct((seq, D_MODEL), bf16),
                   jax.ShapeDtypeStruct((seq, 5 * w), bf16), jax.ShapeDtypeStruct((seq, D_MODEL), bf16),
                   jax.ShapeDtypeStruct((seq, w), f32)],
        compiler_params=_params(1),
    )(h, h, w_in, w_out, cw, cb, clg, clb, dw, gm)


def _odd_bwd(dh, h, z, cv, w_in, w_out, cw, clg, clb, dw, gm, jobs=()):
    seq = h.shape[0]
    tile = min(MIX_TILE, seq)
    n_tiles = seq // tile
    w = A_WIDTH
    halo = CONV_HALO

    def body(dh_ref, dhx_ref, h_ref, z_ref, zp_ref, zx_ref, cv_ref, cvx_ref, win_ref, wout_ref, cw_ref,
             clg_ref, clb_ref, dw_ref, g_ref,
             dhi_ref, dz_ref, dcw_ref, dcb_ref, dclg_ref, dclb_ref, ddw_ref, dg_ref):
        i = pl.program_id(0)

        @pl.when(i == 0)
        def _():
            for ref in (dcw_ref, dcb_ref, dclg_ref, dclb_ref, ddw_ref, dg_ref):
                ref[...] = jnp.zeros_like(ref)

        dh = dh_ref[...]
        dhe = jnp.concatenate([dh, jnp.where(i < n_tiles - 1, dhx_ref[...], 0.0)], axis=0)
        dmix = _mm_nt(dhe.astype(bf16), wout_ref[...])
        ze = jnp.concatenate([jnp.where(i > 0, zp_ref[...].astype(f32), 0.0), z_ref[...].astype(f32),
                              zx_ref[...].astype(f32)], axis=0)

        sg = _sigmoid(ze[:, w:2 * w])
        ca = ze[:, :w]
        hc = ca * sg
        hcs = _Shifted(hc, _down, C_KERNEL - 1)
        cv = jnp.concatenate([cv_ref[...], cvx_ref[...]], axis=0)
        clg = clg_ref[...]
        ln, xh, r = _ln_fwd(cv, clg, clb_ref[...])
        sl = _sigmoid(ln)
        dln = dmix[:, :w] * (sl * (1.0 + ln * (1.0 - sl)))
        dclg_ref[...] += jnp.sum((dln * xh)[:tile], axis=0, keepdims=True)
        dclb_ref[...] += jnp.sum(dln[:tile], axis=0, keepdims=True)
        dcv = _ln_bwd(dln, xh, r, clg)
        dcb_ref[...] += jnp.sum(dcv[:tile], axis=0, keepdims=True)
        dcvs = _Shifted(dcv, _up, C_KERNEL - 1)
        dhc = None
        for j in range(C_KERNEL):
            k = C_KERNEL - 1 - j
            dcw_ref[pl.ds(j, 1), :] += jnp.sum(dcv[:tile] * hcs.rows(k, halo, tile), axis=0, keepdims=True)
            term = cw_ref[pl.ds(j, 1), :] * dcvs.rows(k, 0, tile)
            dhc = term if dhc is None else dhc + term
        sgt = sg[halo:halo + tile]
        cat = ca[halo:halo + tile]
        dca = dhc * sgt
        dcg = dhc * cat * sgt * (1.0 - sgt)

        dcgv = ze[:, 3 * w:4 * w]
        dxin = ze[:, 4 * w:]
        p = dcgv * dxin
        ps = _Shifted(p, _down, D_KERNEL - 1)
        q = _conv_taps(ps, dw_ref, D_KERNEL, halo, tile)
        dyd = dmix[:, w:]
        dq = dyd * ze[halo:, 2 * w:3 * w]
        ddbg = dyd[:tile] * q
        dqs = _Shifted(dq, _up, D_KERNEL - 1)
        dp = None
        for j in range(D_KERNEL):
            k = D_KERNEL - 1 - j
            ddw_ref[pl.ds(j, 1), :] += jnp.sum(dq[:tile] * ps.rows(k, halo, tile), axis=0, keepdims=True)
            term = dw_ref[pl.ds(j, 1), :] * dqs.rows(k, 0, tile)
            dp = term if dp is None else dp + term
        ddcg = dp * dxin[halo:halo + tile]
        ddxin = dp * dcgv[halo:halo + tile]

        dzf = jnp.concatenate([dca, dcg, ddbg, ddcg, ddxin], axis=1).astype(bf16)
        dz_ref[...] = dzf
        dhn = _mm(dzf, win_ref[...])
        dhr, dg = _rms_bwd(dhn, h_ref[...], g_ref[...])
        dhi_ref[...] = dh + dhr
        dg_ref[...] += dg

    row = lambda cols: pl.BlockSpec((tile, cols), lambda i: (i, 0))
    small = [cw.shape, clg.shape, clg.shape, clb.shape, dw.shape, gm.shape]
    return _launch(
        body, name="odd_bwd", grid=(n_tiles,), jobs=jobs,
        in_specs=[row(D_MODEL), _next_halo(tile, halo, D_MODEL, seq), row(D_MODEL), row(5 * w),
                  _prev_halo(tile, halo, 5 * w), _next_halo(tile, halo, 5 * w, seq),
                  row(w), _next_halo(tile, halo, w, seq),
                  _const(w_in.shape, 1), _const(w_out.shape, 1), _const(cw.shape, 1),
                  _const(clg.shape, 1), _const(clb.shape, 1), _const(dw.shape, 1), _const(gm.shape, 1)],
        out_specs=[row(D_MODEL), row(5 * w)] + [_const(s, 1) for s in small],
        out_shape=[jax.ShapeDtypeStruct((seq, D_MODEL), f32), jax.ShapeDtypeStruct((seq, 5 * w), bf16)]
                  + [jax.ShapeDtypeStruct(s, f32) for s in small],
        args=(dh, dh, h, z, z, z, cv, cv, w_in, w_out, cw, clg, clb, dw, gm))


def _ffn_chunks():
    assert sum(FFN_CHUNKS) == D_FF
    start = 0
    for size in FFN_CHUNKS:
        yield slice(start, start + size)
        start += size


def _ffn_fwd(h, wg, wu, wd, gm, jobs=(), head=None):
    seq = h.shape[0]
    tile = min(FFN_TILE, seq)

    def body(h_ref, g_ref, wg_ref, wu_ref, wd_ref, *refs):
        if head is None:
            ho_ref, hn_ref, gate_ref, up_ref = refs
        else:
            t_ref, gf_ref, ho_ref, hn_ref, gate_ref, up_ref, loss_ref, dgf_ref = refs
        h = h_ref[...]
        hnb = (h * _rms_r(h) * g_ref[...]).astype(bf16)
        hn_ref[...] = hnb
        acc = None
        for rows in _ffn_chunks():
            gb = _mm_nt(hnb, wg_ref[rows, :]).astype(bf16)
            ub = _mm_nt(hnb, wu_ref[rows, :]).astype(bf16)
            gate_ref[:, rows] = gb
            up_ref[:, rows] = ub
            gf = gb.astype(f32)
            act = gf * _sigmoid(gf) * ub.astype(f32)
            part = _mm(act.astype(bf16), wd_ref[rows, :])
            acc = part if acc is None else acc + part
        ho = h + acc
        if head is None:
            ho_ref[...] = ho
            return

        @pl.when(pl.program_id(0) == 0)
        def _():
            loss_ref[...] = jnp.zeros_like(loss_ref)
            dgf_ref[...] = jnp.zeros_like(dgf_ref)

        g_final = gf_ref[...]
        err = ho * _rms_r(ho) * g_final - t_ref[...]
        loss_ref[...] += (0.5 / D_MODEL) * jnp.sum(jnp.sum(err * err, axis=1, keepdims=True), axis=0, keepdims=True)
        dho, dg = _rms_bwd(err * (1.0 / D_MODEL), ho, g_final)
        ho_ref[...] = dho
        dgf_ref[...] += dg

    row = pl.BlockSpec((tile, D_MODEL), lambda i: (i, 0))
    wide = pl.BlockSpec((tile, D_FF), lambda i: (i, 0))
    in_specs = [row, _const(gm.shape, 1), _const(wg.shape, 1), _const(wu.shape, 1), _const(wd.shape, 1)]
    out_specs = [row, row, wide, wide]
    out_shape = [jax.ShapeDtypeStruct((seq, D_MODEL), f32), jax.ShapeDtypeStruct((seq, D_MODEL), bf16),
                 jax.ShapeDtypeStruct((seq, D_FF), bf16), jax.ShapeDtypeStruct((seq, D_FF), bf16)]
    args = (h, gm, wg, wu, wd)
    if head is not None:
        target, g_final = head
        in_specs += [row, _const(g_final.shape, 1)]
        out_specs += [_const((1, 1), 1), _const(g_final.shape, 1)]
        out_shape += [jax.ShapeDtypeStruct((1, 1), f32), jax.ShapeDtypeStruct(g_final.shape, f32)]
        args += (target, g_final)
    return _launch(
        body, name="ffn_fwd" if head is None else "ffn_fwd_loss", grid=(seq // tile,), jobs=jobs,
        in_specs=in_specs, out_specs=out_specs, out_shape=out_shape, args=args)


def _ffn_bwd(dh, h, gate, up, wg, wu, wd, gm, jobs=()):
    seq = h.shape[0]
    tile = min(FFN_TILE, seq)
    n_tiles = seq // tile

    def body(dh_ref, h_ref, g_ref, gate_ref, up_ref, wg_ref, wu_ref, wd_ref,
             dhi_ref, dgate_ref, dup_ref, act_ref, dg_ref):
        @pl.when(pl.program_id(0) == 0)
        def _():
            dg_ref[...] = jnp.zeros_like(dg_ref)

        dh = dh_ref[...]
        dhb = dh.astype(bf16)
        acc = None
        for rows in _ffn_chunks():
            dact = _mm_nt(dhb, wd_ref[rows, :])
            gf = gate_ref[:, rows].astype(f32)
            uf = up_ref[:, rows].astype(f32)
            s = _sigmoid(gf)
            silu = gf * s
            act_ref[:, rows] = (silu * uf).astype(bf16)
            dgb = (dact * uf * (s * (1.0 + gf * (1.0 - s)))).astype(bf16)
            dub = (dact * silu).astype(bf16)
            dgate_ref[:, rows] = dgb
            dup_ref[:, rows] = dub
            part = _mm(dgb, wg_ref[rows, :]) + _mm(dub, wu_ref[rows, :])
            acc = part if acc is None else acc + part
        dhr, dg = _rms_bwd(acc, h_ref[...], g_ref[...])
        dhi_ref[...] = dh + dhr
        dg_ref[...] += dg

    row = pl.BlockSpec((tile, D_MODEL), lambda i: (i, 0))
    wide = pl.BlockSpec((tile, D_FF), lambda i: (i, 0))
    return _launch(
        body, name="ffn_bwd", grid=(n_tiles,), jobs=jobs,
        in_specs=[row, row, _const(gm.shape, 1), wide, wide, _const(wg.shape, 1), _const(wu.shape, 1), _const(wd.shape, 1)],
        out_specs=[row, wide, wide, wide, _const(gm.shape, 1)],
        out_shape=[jax.ShapeDtypeStruct((seq, D_MODEL), f32), jax.ShapeDtypeStruct((seq, D_FF), bf16),
                   jax.ShapeDtypeStruct((seq, D_FF), bf16), jax.ShapeDtypeStruct((seq, D_FF), bf16),
                   jax.ShapeDtypeStruct(gm.shape, f32)],
        args=(dh, h, gm, gate, up, wg, wu, wd))


def _weight_grads(pairs, name, jobs=()):
    seq, m = pairs[0][0].shape
    tk = min(DW_TK, seq)
    tm = m if m <= DW_TM else m // 2
    n_k = seq // tk
    n_pairs = len(pairs)

    def body(*refs):
        x_refs = refs[0:2 * n_pairs:2]
        y_refs = refs[1:2 * n_pairs:2]
        o_refs = refs[2 * n_pairs:3 * n_pairs]
        acc_refs = refs[3 * n_pairs:]
        k = pl.program_id(1)
        @pl.when(k == 0)
        def _():
            for acc_ref in acc_refs:
                acc_ref[...] = jnp.zeros_like(acc_ref)

        for x_ref, y_ref, acc_ref in zip(x_refs, y_refs, acc_refs):
            acc_ref[...] += _mm_tn(x_ref[...].astype(bf16), y_ref[...].astype(bf16))

        @pl.when(k == n_k - 1)
        def _():
            for o_ref, acc_ref in zip(o_refs, acc_refs):
                o_ref[...] = acc_ref[...].astype(bf16)

    in_specs = []
    for _ in pairs:
        in_specs += [pl.BlockSpec((tk, tm), lambda j, k: (k, j)), pl.BlockSpec((tk, D_MODEL), lambda j, k: (k, 0))]
    return _launch(
        body, name=name, grid=(m // tm, n_k), jobs=jobs,
        in_specs=in_specs,
        out_specs=[pl.BlockSpec((tm, D_MODEL), lambda j, k: (j, 0))] * n_pairs,
        out_shape=[jax.ShapeDtypeStruct((m, D_MODEL), bf16)] * n_pairs,
        scratch=[pltpu.VMEM((tm, D_MODEL), f32)] * n_pairs,
        args=[a for pair in pairs for a in pair])


def _row_tile(rows, limit=512):
    best = rows
    for t in range(8, min(rows, limit) + 1, 8):
        if rows % t == 0:
            best = t
    return best if rows > limit else rows


def _adam_step(w, g, m, v):
    m2 = ADAM_B1 * m + (1.0 - ADAM_B1) * g
    v2 = ADAM_B2 * v + (1.0 - ADAM_B2) * (g * g)
    m_hat = m2 / (1.0 - ADAM_B1 ** ADAM_STEP)
    v_hat = v2 / (1.0 - ADAM_B2 ** ADAM_STEP)
    return -ADAM_LR * (m_hat / (jnp.sqrt(v_hat) + ADAM_EPS) + ADAM_WD * w), m2, v2


def _adamw_small(items, name):
    n = len(items)

    def body(*refs):
        ins, outs = refs[:4 * n], refs[4 * n:]
        for k in range(n):
            w_ref, g_ref, m_ref, v_ref = ins[4 * k:4 * k + 4]
            d_ref, mo_ref, vo_ref = outs[3 * k:3 * k + 3]
            d_ref[...], mo_ref[...], vo_ref[...] = _adam_step(w_ref[...], g_ref[...], m_ref[...], v_ref[...])

    def whole(a):
        return pl.BlockSpec(a.shape, lambda i: (0, 0))

    results = pl.pallas_call(
        body, name=name, grid=(1,),
        in_specs=[whole(a) for item in items for a in item],
        out_specs=[whole(item[0]) for item in items for _ in range(3)],
        out_shape=[jax.ShapeDtypeStruct(item[0].shape, f32) for item in items for _ in range(3)],
        compiler_params=_params(1),
    )(*[a for item in items for a in item])
    return [results[3 * k:3 * k + 3] for k in range(n)]


def _adamw_reduced(w, parts, m, v, name):
    layers, rows, cols = w.shape

    def body(*refs):
        w_ref, m_ref, v_ref = refs[:3]
        part_refs = refs[3:3 + layers]
        g_ref, d_ref, mo_ref, vo_ref = refs[3 + layers:]
        layer = pl.program_id(0)
        for l, p_ref in enumerate(part_refs):
            @pl.when(layer == l)
            def _():
                acc = p_ref[0].astype(f32)
                for k in range(1, N_CHIP):
                    acc = acc + p_ref[k].astype(f32)
                g_ref[0] = acc

        d_ref[0], mo_ref[0], vo_ref[0] = _adam_step(w_ref[0], g_ref[0], m_ref[0], v_ref[0])

    blk = pl.BlockSpec((1, rows, cols), lambda l: (l, 0, 0))
    return pl.pallas_call(
        body, name=name, grid=(layers,),
        in_specs=[blk] * 3 + [pl.BlockSpec(p.shape, lambda l: (0, 0, 0)) for p in parts],
        out_specs=[blk] * 4,
        out_shape=[jax.ShapeDtypeStruct(w.shape, f32)] * 4,
        compiler_params=_params(1),
    )(w, m, v, *parts)


def _sum_leading(x, name):
    n, rows, cols = x.shape
    tr = _row_tile(rows)

    def body(x_ref, o_ref):
        acc = x_ref[0].astype(f32)
        for k in range(1, n):
            acc = acc + x_ref[k].astype(f32)
        o_ref[...] = acc

    return pl.pallas_call(
        body, name=name, grid=(rows // tr,),
        in_specs=[pl.BlockSpec((n, tr, cols), lambda i: (0, i, 0))],
        out_specs=pl.BlockSpec((tr, cols), lambda i: (i, 0)),
        out_shape=jax.ShapeDtypeStruct((rows, cols), f32),
        compiler_params=_params(1),
    )(x)


def _pair_sum(gs, recvs, c_idx, name):
    n = len(gs)

    def body(c_ref, *refs):
        for g_ref, r_ref, o_ref in zip(refs[:n], refs[n:2 * n], refs[2 * n:]):
            o_ref[...] = (g_ref[...].astype(f32) + r_ref[...].astype(f32)).astype(o_ref.dtype)

    own = [pl.BlockSpec((1,) + g.shape[1:], lambda k, c: (2 * k + c[0], 0, 0)) for g in gs]
    by_chip = [pl.BlockSpec((1,) + g.shape[1:], lambda k, c: (k, 0, 0)) for g in gs]
    return list(pl.pallas_call(
        body, name=name,
        grid_spec=pltpu.PrefetchScalarGridSpec(num_scalar_prefetch=1, grid=(N_CHIP,),
                                               in_specs=own + by_chip, out_specs=by_chip),
        out_shape=[jax.ShapeDtypeStruct((N_CHIP,) + g.shape[1:], g.dtype) for g in gs],
        compiler_params=_params(1),
    )(c_idx, *gs, *recvs))


def _pack_rows(w):
    return w.reshape(N_DEV, -1, D_MODEL)


def kernel(x, even_w_in, even_w_out, a_w_s, a_b_s, a_ln_g, a_ln_b, b_w_pool, b_scale, odd_w_in, odd_w_out, c_w_dw, c_b_dw, c_ln_g, c_ln_b, d_w_dw, norm_mix_g, norm_ffn_g, ffn_w_gate, ffn_w_up, ffn_w_down, final_norm_g, loss_target, m_even_w_in, m_even_w_out, m_a_w_s, m_a_b_s, m_a_ln_g, m_a_ln_b, m_b_w_pool, m_b_scale, m_odd_w_in, m_odd_w_out, m_c_w_dw, m_c_b_dw, m_c_ln_g, m_c_ln_b, m_d_w_dw, m_norm_mix_g, m_norm_ffn_g, m_ffn_w_gate, m_ffn_w_up, m_ffn_w_down, m_final_norm_g, v_even_w_in, v_even_w_out, v_a_w_s, v_a_b_s, v_a_ln_g, v_a_ln_b, v_b_w_pool, v_b_scale, v_odd_w_in, v_odd_w_out, v_c_w_dw, v_c_b_dw, v_c_ln_g, v_c_ln_b, v_d_w_dw, v_norm_mix_g, v_norm_ffn_g, v_ffn_w_gate, v_ffn_w_up, v_ffn_w_down, v_final_norm_g):
    weights = dict(even_w_in=even_w_in, even_w_out=even_w_out, a_w_s=a_w_s, a_b_s=a_b_s, a_ln_g=a_ln_g, a_ln_b=a_ln_b,
                   b_w_pool=b_w_pool, b_scale=b_scale, odd_w_in=odd_w_in, odd_w_out=odd_w_out, c_w_dw=c_w_dw,
                   c_b_dw=c_b_dw, c_ln_g=c_ln_g, c_ln_b=c_ln_b, d_w_dw=d_w_dw, norm_mix_g=norm_mix_g,
                   norm_ffn_g=norm_ffn_g, ffn_w_gate=ffn_w_gate, ffn_w_up=ffn_w_up, ffn_w_down=ffn_w_down,
                   final_norm_g=final_norm_g)
    m_in = dict(even_w_in=m_even_w_in, even_w_out=m_even_w_out, a_w_s=m_a_w_s, a_b_s=m_a_b_s, a_ln_g=m_a_ln_g,
                a_ln_b=m_a_ln_b, b_w_pool=m_b_w_pool, b_scale=m_b_scale, odd_w_in=m_odd_w_in, odd_w_out=m_odd_w_out,
                c_w_dw=m_c_w_dw, c_b_dw=m_c_b_dw, c_ln_g=m_c_ln_g, c_ln_b=m_c_ln_b, d_w_dw=m_d_w_dw,
                norm_mix_g=m_norm_mix_g, norm_ffn_g=m_norm_ffn_g, ffn_w_gate=m_ffn_w_gate, ffn_w_up=m_ffn_w_up,
                ffn_w_down=m_ffn_w_down, final_norm_g=m_final_norm_g)
    v_in = dict(even_w_in=v_even_w_in, even_w_out=v_even_w_out, a_w_s=v_a_w_s, a_b_s=v_a_b_s, a_ln_g=v_a_ln_g,
                a_ln_b=v_a_ln_b, b_w_pool=v_b_w_pool, b_scale=v_b_scale, odd_w_in=v_odd_w_in, odd_w_out=v_odd_w_out,
                c_w_dw=v_c_w_dw, c_b_dw=v_c_b_dw, c_ln_g=v_c_ln_g, c_ln_b=v_c_ln_b, d_w_dw=v_d_w_dw,
                norm_mix_g=v_norm_mix_g, norm_ffn_g=v_norm_ffn_g, ffn_w_gate=v_ffn_w_gate, ffn_w_up=v_ffn_w_up,
                ffn_w_down=v_ffn_w_down, final_norm_g=v_final_norm_g)
    names = list(weights)

    group_parts = {
        "even": [even_w_in[0].T, even_w_out[0]],
        "ffn0": [ffn_w_gate[0].T, ffn_w_up[0].T, ffn_w_down[0]],
        "odd": [odd_w_in[0].T, odd_w_out[0]],
        "ffn1": [ffn_w_gate[1].T, ffn_w_up[1].T, ffn_w_down[1]],
    }

    def gather_jobs(*groups):
        return [_all_gather_job(p.astype(bf16)) for k in groups for p in group_parts[k]]

    def whole(gathered):
        return [g.reshape(-1, D_MODEL) for g in gathered]

    conv_names = ["c_w_dw", "c_b_dw", "c_ln_g", "c_ln_b", "d_w_dw"]
    conv_rows = [C_KERNEL, 1, 1, 1, D_KERNEL]
    conv_local = jnp.concatenate([weights[n].reshape(r, -1) for n, r in zip(conv_names, conv_rows)]
                                 + [jnp.zeros((3, c_b_dw.shape[-1]), f32)], axis=0)
    *even_gathered, conv_all = _run_jobs(gather_jobs("even") + [_all_gather_job(conv_local)], "gather_even_conv")
    w_in_e, w_out_e = whole(even_gathered)
    conv_all = conv_all.transpose(1, 0, 2).reshape(conv_local.shape[0], -1)
    conv_offs = [sum(conv_rows[:k]) for k in range(len(conv_rows) + 1)]
    cw, cb, clg, clb, dw = [conv_all[conv_offs[k]:conv_offs[k + 1]] for k in range(len(conv_rows))]

    ws, bst = a_w_s[0], a_b_s[0].T
    lng, lnb, wp, sc = a_ln_g, a_ln_b, b_w_pool[0], b_scale
    gmix = [norm_mix_g[l:l + 1] for l in range(2)]
    gffn = [norm_ffn_g[l:l + 1] for l in range(2)]
    gfin = final_norm_g.reshape(1, D_MODEL)

    h0 = x[0]
    h1, hn_e, za, pooled, mix_e, *ffn0_gathered = _even_fwd(
        h0, w_in_e, w_out_e, ws, bst, lng, lnb, wp, sc, gmix[0], jobs=gather_jobs("ffn0"))
    w_gate0, w_up0, w_down0 = whole(ffn0_gathered)
    h2, hn_f0, gate0, up0, *rest_gathered = _ffn_fwd(h1, w_gate0, w_up0, w_down0, gffn[0],
                                                     jobs=gather_jobs("odd", "ffn1"))
    w_in_o, w_out_o, w_gate1, w_up1, w_down1 = whole(rest_gathered)
    h3, hn_o, z_o, mix_o, cv_o = _odd_fwd(h2, w_in_o, w_out_o, cw, cb, clg, clb, dw, gmix[1])
    dh4, hn_f1, gate1, up1, loss_local, g_final = _ffn_fwd(h3, w_gate1, w_up1, w_down1, gffn[1],
                                                           head=(loss_target[0], gfin))

    c_idx = lax.axis_index("c").astype(jnp.int32).reshape(1)

    def weight_grad(x, y, name, jobs=()):
        g, *job_results = _weight_grads([(x, y)], name, jobs=jobs)
        return [_pack_rows(g)] + job_results

    def siblings(parts):
        return [_sibling_exchange_job(p) for p in parts]

    def chips(pairs):
        return [_chip_exchange_job(p) for p in pairs]

    dh3, dgate1, dup1, act1, g_ffn1 = _ffn_bwd(dh4, h3, gate1, up1, w_gate1, w_up1, w_down1, gffn[1])
    part_ffn1 = (weight_grad(dgate1, hn_f1, "dw_gate1") + weight_grad(dup1, hn_f1, "dw_up1")
                 + weight_grad(act1, dh4, "dw_down1"))
    dh2, dz_o, g_cw, g_cb, g_clg, g_clb, g_dw, g_mix1, *recv_ffn1 = _odd_bwd(
        dh3, h2, z_o, cv_o, w_in_o, w_out_o, cw, clg, clb, dw, gmix[1], jobs=siblings(part_ffn1))
    pair_ffn1 = _pair_sum(part_ffn1, recv_ffn1, c_idx, "pair_sum_ffn1")
    part_odd = weight_grad(dz_o, hn_o, "dw_odd_in") + weight_grad(mix_o, dh3, "dw_odd_out")
    dh1, dgate0, dup0, act0, g_ffn0, *exchanged = _ffn_bwd(
        dh2, h1, gate0, up0, w_gate0, w_up0, w_down0, gffn[0], jobs=chips(pair_ffn1) + siblings(part_odd))
    chips_ffn1, recv_odd = exchanged[:3], exchanged[3:]
    pair_odd = _pair_sum(part_odd, recv_odd, c_idx, "pair_sum_odd")
    dw_gate0, *chips_odd = weight_grad(dgate0, hn_f0, "dw_gate0", jobs=chips(pair_odd))
    part_ffn0 = [dw_gate0] + weight_grad(dup0, hn_f0, "dw_up0") + weight_grad(act0, dh2, "dw_down0")
    part_even_out, *recv_ffn0 = weight_grad(mix_e, dh1, "dw_even_out", jobs=siblings(part_ffn0))
    pair_ffn0 = _pair_sum(part_ffn0, recv_ffn0, c_idx, "pair_sum_ffn0")
    dh0, dz_e, g_ws, g_bs, g_lng, g_lnb, g_wp, g_sc, g_mix0, *exchanged = _even_bwd(
        dh1, h0, za, pooled, w_in_e, w_out_e, ws, bst, lng, lnb, wp, sc, gmix[0],
        jobs=chips(pair_ffn0) + siblings([part_even_out]))
    chips_ffn0, recv_even_out = exchanged[:3], exchanged[3:]
    pair_even_out = _pair_sum([part_even_out], recv_even_out, c_idx, "pair_sum_even_out")

    lanes = HEAD
    small = [("a_w_s", g_ws), ("a_b_s", g_bs), ("a_ln_g", g_lng), ("a_ln_b", g_lnb), ("b_w_pool", g_wp),
             ("b_scale", g_sc), ("norm_mix_g", jnp.concatenate([g_mix0, g_mix1], axis=0)),
             ("norm_ffn_g", jnp.concatenate([g_ffn0, g_ffn1], axis=0)), ("final_norm_g", g_final),
             ("c_w_dw", g_cw), ("c_b_dw", g_cb), ("c_ln_g", g_clg), ("c_ln_b", g_clb), ("d_w_dw", g_dw),
             ("loss", loss_local)]
    small_rows = [-(-g.size // (8 * lanes)) * 8 for _, g in small]
    small_offs = [sum(small_rows[:k]) for k in range(len(small) + 1)]
    pad_rows = -small_offs[-1] % 256
    small_buf = jnp.concatenate(
        [jnp.pad(g.reshape(-1), (0, r * lanes - g.size)).reshape(r, lanes) for (_, g), r in zip(small, small_rows)]
        + [jnp.zeros((pad_rows, lanes), f32)], axis=0)
    part_even_in, small_all, chips_even_out = weight_grad(
        dz_e, hn_e, "dw_even_in", jobs=[_all_gather_job(small_buf)] + chips(pair_even_out))
    small_sum = _sum_leading(small_all, "small_grad_sum")
    chips_even_in = [_exchange_in_vmem(part_even_in, "reduce_scatter_even_in")]
    grads = {}
    for k, (n, g) in enumerate(small):
        grads[n] = small_sum[small_offs[k]:small_offs[k + 1]].reshape(-1)[:g.size].reshape(g.shape)
    me = 4 * lax.axis_index("x") + 2 * lax.axis_index("y") + lax.axis_index("c")
    shard = c_b_dw.shape[-1]
    for n in conv_names:
        grads[n] = lax.dynamic_slice_in_dim(grads[n], me * shard, shard, axis=1)

    col_sharded = ("even_w_in", "odd_w_in", "ffn_w_gate", "ffn_w_up")

    def rows_view(n, a):
        return jnp.swapaxes(a, -1, -2) if n in col_sharded else a

    loss = grads.pop("loss")[0, 0]
    chip_parts = {"even_w_in": chips_even_in, "even_w_out": [chips_even_out],
                  "odd_w_in": chips_odd[:1], "odd_w_out": chips_odd[1:]}
    for k, n in enumerate(["ffn_w_gate", "ffn_w_up", "ffn_w_down"]):
        chip_parts[n] = [chips_ffn0[k], chips_ffn1[k]]

    delta, new_m, new_v = {}, {}, {}
    for n in chip_parts:
        w_rows, m_rows, v_rows = [rows_view(n, a) for a in (weights[n], m_in[n], v_in[n])]
        outs = _adamw_reduced(w_rows, chip_parts[n], m_rows, v_rows, "adamw_" + n)
        grads[n], delta[n], new_m[n], new_v[n] = [rows_view(n, o) for o in outs]
    others = [n for n in names if n not in chip_parts]
    view = {n: (-1, weights[n].shape[-1]) for n in others}
    stepped = _adamw_small([tuple(a.reshape(view[n]) for a in (weights[n], grads[n], m_in[n], v_in[n])) for n in others],
                           "adamw_small")
    for n, outs in zip(others, stepped):
        grads[n] = grads[n].reshape(weights[n].shape)
        delta[n], new_m[n], new_v[n] = [o.reshape(weights[n].shape) for o in outs]

    return (loss, dh0[None], *[grads[n] for n in names], *[delta[n] for n in names],
            *[new_m[n] for n in names], *[new_v[n] for n in names])
```

```python
import jax
import jax.numpy as jnp
from jax import lax
from jax.experimental import pallas as pl
from jax.experimental.pallas import tpu as pltpu

f32 = jnp.float32
bf16 = jnp.bfloat16

EPS = 1e-6
D_MODEL = 1024
A_WIDTH = 512
HEAD = 128
N_HEADS = 4
CHUNK = 64
POOL_WINDOWS = (2, 4, 8, 16)
POOL_HALO = 16
C_KERNEL = 31
D_KERNEL = 3
CONV_HALO = 32
D_FF = 2816
N_DEV = 8
N_CHIP = 4

ADAM_LR = 0.001
ADAM_B1 = 0.9
ADAM_B2 = 0.999
ADAM_EPS = 1e-08
ADAM_WD = 0.01
ADAM_STEP = 10

MIX_TILE = 512
ODD_FWD_TILE = 1024
FFN_TILE = 256
FFN_CHUNKS = (1536, 1280)
DW_TK = 2048
DW_TM = 1536
MIDDLE_AT, MIDDLE_OF = 7, 8
VMEM_LIMIT = 56 * 1024 * 1024

MESH = pl.DeviceIdType.MESH
ANY = pl.BlockSpec(memory_space=pl.ANY)


def _params(n_axes):
    return pltpu.CompilerParams(dimension_semantics=("arbitrary",) * n_axes, vmem_limit_bytes=VMEM_LIMIT)


def _mm(a, b):
    return jnp.dot(a, b, preferred_element_type=f32)


def _mm_nt(a, b):
    return lax.dot_general(a, b, (((1,), (1,)), ((), ())), preferred_element_type=f32)


def _mm_tn(a, b):
    return lax.dot_general(a, b, (((0,), (0,)), ((), ())), preferred_element_type=f32)


def _sigmoid(x):
    return 1.0 / (1.0 + jnp.exp(-x))


def _rms_r(h):
    return lax.rsqrt(jnp.mean(h * h, axis=-1, keepdims=True) + EPS)


def _rms_bwd(dy, h, g):
    r = _rms_r(h)
    xh = h * r
    dxh = dy * g
    dh = r * (dxh - xh * jnp.mean(dxh * xh, axis=-1, keepdims=True))
    return dh, jnp.sum(dy * xh, axis=0, keepdims=True)


def _ln_fwd(x, g, b):
    mu = jnp.mean(x, axis=-1, keepdims=True)
    xc = x - mu
    r = lax.rsqrt(jnp.mean(xc * xc, axis=-1, keepdims=True) + EPS)
    xh = xc * r
    return xh * g + b, xh, r


def _ln_bwd(dy, xh, r, g):
    dxh = dy * g
    return r * (dxh - jnp.mean(dxh, axis=-1, keepdims=True) - xh * jnp.mean(dxh * xh, axis=-1, keepdims=True))


_GELU_C = 0.7978845608028654
_GELU_A = 0.044715


def _gelu(x):
    th = jnp.tanh(x * (_GELU_C + (_GELU_C * _GELU_A) * (x * x)))
    half = 0.5 * x
    return half + half * th, th


def _gelu_grad(x, th):
    return 0.5 + 0.5 * th + (1.0 - th * th) * (x * (0.5 * _GELU_C + (1.5 * _GELU_C * _GELU_A) * (x * x)))


def _down(x, k):
    return x if k == 0 else pltpu.roll(x, k, 0)


def _up(x, k):
    return x if k == 0 else pltpu.roll(x, x.shape[0] - k, 0)


def _window_sum(x, win, shift):
    s = x
    step = 1
    while step < win:
        s = s + shift(s, step)
        step *= 2
    return s


def _inv_count(t0, rows, win):
    t = t0 + lax.broadcasted_iota(jnp.int32, (rows, 1), 0)
    return 1.0 / jnp.minimum(t + 1, win).astype(f32)


def _chunk_mask():
    i = lax.broadcasted_iota(jnp.int32, (HEAD, HEAD), 0)
    j = lax.broadcasted_iota(jnp.int32, (HEAD, HEAD), 1)
    return jnp.logical_or(i >= CHUNK, j < CHUNK)


def _const(shape, n_axes):
    zeros = (0,) * len(shape)
    if n_axes == 1:
        return pl.BlockSpec(shape, lambda i: zeros)
    return pl.BlockSpec(shape, lambda i, j: zeros)


def _prev_halo(tile, halo, cols):
    return pl.BlockSpec((halo, cols), lambda i: (jnp.maximum(i * (tile // halo) - 1, 0), 0))


def _next_halo(tile, halo, cols, seq):
    return pl.BlockSpec((halo, cols), lambda i: (jnp.minimum((i + 1) * (tile // halo), seq // halo - 1), 0))


class _Job:
    def __init__(self, inputs, out_shape, sems, hooks):
        self.inputs, self.out_shape, self.sems, self.hooks = inputs, out_shape, sems, hooks


def _position():
    return lax.axis_index("x"), lax.axis_index("y"), lax.axis_index("c")


def _all_gather_job(block):
    rows, cols = block.shape

    def hooks(ins, outs, sems):
        (x_ref,), (out_ref,), (send_sems, recv_sems, local_sem) = ins, outs, sems
        x, y, c = _position()
        me, sibling = (x, y, c), (x, y, 1 - c)
        chips = [(1 - x, y), (x, 1 - y), (1 - x, 1 - y)]

        def slot(px, py, pc):
            return out_ref.at[4 * px + 2 * py + pc]

        def copy(k, block_of, to, src=None):
            return pltpu.make_async_remote_copy(
                src_ref=slot(*block_of) if src is None else src, dst_ref=slot(*block_of),
                send_sem=send_sems.at[k], recv_sem=recv_sems.at[k], device_id=to, device_id_type=MESH)

        mine = pltpu.make_async_copy(x_ref, slot(*me), local_sem)
        first = [copy(0, me, sibling, src=x_ref)]
        first += [copy(1 + j, me, (*chip, c), src=x_ref) for j, chip in enumerate(chips)]
        passed = [copy(4 + j, (*chip, c), sibling) for j, chip in enumerate(chips)]

        def start():
            mine.start()
            for cp in first:
                cp.start()

        def middle():
            for j, chip in enumerate(chips):
                copy(1 + j, (*chip, c), me).wait_recv()
                passed[j].start()

        def finish():
            copy(0, sibling, me).wait_recv()
            for j, chip in enumerate(chips):
                copy(4 + j, (*chip, 1 - c), me).wait_recv()
            for cp in first + passed:
                cp.wait_send()
            mine.wait()

        return start, middle, finish

    return _Job([block], [jax.ShapeDtypeStruct((N_DEV, rows, cols), block.dtype)],
                [pltpu.SemaphoreType.DMA((7,)), pltpu.SemaphoreType.DMA((7,)), pltpu.SemaphoreType.DMA], hooks)


def _sibling_exchange_job(g):
    _, rows, cols = g.shape

    def hooks(ins, outs, sems):
        (g_ref,), (recv_ref,), (send_sems, recv_sems) = ins, outs, sems
        x, y, c = _position()
        copies = [pltpu.make_async_remote_copy(
            src_ref=g_ref.at[2 * k + (1 - c)], dst_ref=recv_ref.at[k], send_sem=send_sems.at[k],
            recv_sem=recv_sems.at[k], device_id=(x, y, 1 - c), device_id_type=MESH) for k in range(N_CHIP)]

        def start():
            for cp in copies:
                cp.start()

        def finish():
            for cp in copies:
                cp.wait()

        return start, lambda: None, finish

    return _Job([g], [jax.ShapeDtypeStruct((N_CHIP, rows, cols), g.dtype)],
                [pltpu.SemaphoreType.DMA((N_CHIP,)), pltpu.SemaphoreType.DMA((N_CHIP,))], hooks)


def _chip_exchange_job(p):
    _, rows, cols = p.shape

    def hooks(ins, outs, sems):
        (p_ref,), (recv_ref,), (send_sems, recv_sems, local_sem) = ins, outs, sems
        x, y, c = _position()
        k_me = 2 * x + y
        mine = pltpu.make_async_copy(p_ref.at[k_me], recv_ref.at[k_me], local_sem)
        copies = [pltpu.make_async_remote_copy(
            src_ref=p_ref.at[2 * px + py], dst_ref=recv_ref.at[k_me], send_sem=send_sems.at[j],
            recv_sem=recv_sems.at[j], device_id=(px, py, c), device_id_type=MESH)
            for j, (px, py) in enumerate([(1 - x, y), (x, 1 - y), (1 - x, 1 - y)])]

        def start():
            mine.start()
            for cp in copies:
                cp.start()

        def finish():
            for cp in copies:
                cp.wait()
            mine.wait()

        return start, lambda: None, finish

    return _Job([p], [jax.ShapeDtypeStruct((N_CHIP, rows, cols), p.dtype)],
                [pltpu.SemaphoreType.DMA((3,)), pltpu.SemaphoreType.DMA((3,)), pltpu.SemaphoreType.DMA], hooks)


def _job_hooks(jobs, ins, outs, sems):
    hooks = []
    for job in jobs:
        n_in, n_out, n_sem = len(job.inputs), len(job.out_shape), len(job.sems)
        hooks.append(job.hooks(ins[:n_in], outs[:n_out], sems[:n_sem]))
        ins, outs, sems = ins[n_in:], outs[n_out:], sems[n_sem:]
    return hooks


def _run_jobs(jobs, name):
    n_in = sum(len(job.inputs) for job in jobs)
    n_out = sum(len(job.out_shape) for job in jobs)

    def body(*refs):
        hooks = _job_hooks(jobs, refs[:n_in], refs[n_in:n_in + n_out], refs[n_in + n_out:])
        for phase in range(3):
            for h in hooks:
                h[phase]()

    return list(pl.pallas_call(
        body, name=name, in_specs=[ANY] * n_in, out_specs=[ANY] * n_out,
        out_shape=[s for job in jobs for s in job.out_shape],
        scratch_shapes=[s for job in jobs for s in job.sems],
    )(*[a for job in jobs for a in job.inputs]))


def _exchange_in_vmem(g, name):
    _, rows, cols = g.shape
    vmem = pl.BlockSpec(memory_space=pltpu.VMEM)

    def body(g_ref, out_ref, recv_ref, pair_ref, sib_send, sib_recv, chip_send, chip_recv):
        x, y, c = _position()
        to_sibling = [pltpu.make_async_remote_copy(
            src_ref=g_ref.at[2 * k + (1 - c)], dst_ref=recv_ref.at[k], send_sem=sib_send.at[k],
            recv_sem=sib_recv.at[k], device_id=(x, y, 1 - c), device_id_type=MESH) for k in range(N_CHIP)]
        for cp in to_sibling:
            cp.start()
        for cp in to_sibling:
            cp.wait()
        for k in range(N_CHIP):
            pair_ref[k] = (g_ref[2 * k + c].astype(f32) + recv_ref[k].astype(f32)).astype(pair_ref.dtype)
        k_me = 2 * x + y
        to_chips = [pltpu.make_async_remote_copy(
            src_ref=pair_ref.at[2 * px + py], dst_ref=out_ref.at[k_me], send_sem=chip_send.at[j],
            recv_sem=chip_recv.at[j], device_id=(px, py, c), device_id_type=MESH)
            for j, (px, py) in enumerate([(1 - x, y), (x, 1 - y), (1 - x, 1 - y)])]
        for cp in to_chips:
            cp.start()
        out_ref[k_me] = pair_ref[k_me]
        for cp in to_chips:
            cp.wait()

    return pl.pallas_call(
        body, name=name, in_specs=[vmem], out_specs=vmem,
        out_shape=jax.ShapeDtypeStruct((N_CHIP, rows, cols), g.dtype),
        scratch_shapes=[pltpu.VMEM((N_CHIP, rows, cols), g.dtype), pltpu.VMEM((N_CHIP, rows, cols), g.dtype),
                        pltpu.SemaphoreType.DMA((N_CHIP,)), pltpu.SemaphoreType.DMA((N_CHIP,)),
                        pltpu.SemaphoreType.DMA((3,)), pltpu.SemaphoreType.DMA((3,))],
    )(g)


def _launch(body, *, name, grid, in_specs, out_specs, out_shape, args, scratch=(), jobs=()):
    in_specs, out_specs, out_shape, scratch = list(in_specs), list(out_specs), list(out_shape), list(scratch)
    if not jobs:
        return list(pl.pallas_call(body, name=name, grid=grid, in_specs=in_specs, out_specs=out_specs,
                                   out_shape=out_shape, scratch_shapes=scratch,
                                   compiler_params=_params(len(grid)))(*args))
    n_in, n_out, n_sc = len(in_specs), len(out_specs), len(scratch)
    j_in = [a for job in jobs for a in job.inputs]
    j_out = [s for job in jobs for s in job.out_shape]
    j_sems = [s for job in jobs for s in job.sems]
    n_steps = 1
    for g in grid:
        n_steps *= g

    def wrapped(*refs):
        ins, refs = refs[:n_in], refs[n_in:]
        jins, refs = refs[:len(j_in)], refs[len(j_in):]
        outs, refs = refs[:n_out], refs[n_out:]
        jouts, refs = refs[:len(j_out)], refs[len(j_out):]
        sc, jsems = refs[:n_sc], refs[n_sc:]
        step = pl.program_id(0)
        for axis in range(1, len(grid)):
            step = step * grid[axis] + pl.program_id(axis)
        hooks = _job_hooks(jobs, jins, jouts, jsems)

        @pl.when(step == 0)
        def _():
            for h in hooks:
                h[0]()

        body(*ins, *outs, *sc)

        @pl.when(step == (MIDDLE_AT * n_steps) // MIDDLE_OF)
        def _():
            for h in hooks:
                h[1]()

        @pl.when(step == n_steps - 1)
        def _():
            for h in hooks:
                h[2]()

    return list(pl.pallas_call(
        wrapped, name=name, grid=grid, in_specs=in_specs + [ANY] * len(j_in), out_specs=out_specs + [ANY] * len(j_out),
        out_shape=out_shape + j_out, scratch_shapes=scratch + j_sems, compiler_params=_params(len(grid)),
    )(*args, *j_in))


def _gmlp_gate(vnb, wsm, bst, tile):
    rows = []
    for n in range(tile // HEAD):
        cols = []
        for hh in range(N_HEADS):
            blk = vnb[n * HEAD:(n + 1) * HEAD, hh * HEAD:(hh + 1) * HEAD]
            cols.append(_mm(wsm[hh], blk) + bst[:, hh:hh + 1])
        rows.append(jnp.concatenate(cols, axis=1))
    return jnp.concatenate(rows, axis=0)


def _even_fwd(h, w_in, w_out, ws, bst, lng, lnb, wp, sc, gm, jobs=()):
    seq = h.shape[0]
    tile = min(MIX_TILE, seq)
    n_tiles = seq // tile

    def body(h_ref, hp_ref, win_ref, wout_ref, ws_ref, bst_ref, lng_ref, lnb_ref, wp_ref, sc_ref, g_ref,
             ho_ref, hn_ref, za_ref, pool_ref, mix_ref):
        i = pl.program_id(0)
        g = g_ref[...]
        h = h_ref[...]
        hnb = (h * _rms_r(h) * g).astype(bf16)
        hn_ref[...] = hnb
        z = _mm_nt(hnb, win_ref[...])
        zab = z[:, :2 * A_WIDTH].astype(bf16)
        za_ref[...] = zab
        hp = hp_ref[...]
        zbp = _mm_nt((hp * _rms_r(hp) * g).astype(bf16), win_ref[2 * A_WIDTH:, :])
        zbe = jnp.concatenate([jnp.where(i > 0, zbp, 0.0), z[:, 2 * A_WIDTH:]], axis=0)
        pooled = []
        for gi, win in enumerate(POOL_WINDOWS):
            xg = zbe[:, gi * HEAD:(gi + 1) * HEAD]
            s = _window_sum(xg, win, _down)
            pooled.append(s[POOL_HALO:] * _inv_count(i * tile, tile, win) - xg[POOL_HALO:])
        plb = jnp.concatenate(pooled, axis=1).astype(bf16)
        pool_ref[...] = plb

        ga, _ = _gelu(zab.astype(f32))
        vn, _, _ = _ln_fwd(ga[:, A_WIDTH:], lng_ref[...], lnb_ref[...])
        mask = _chunk_mask()
        wsm = [jnp.where(mask, ws_ref[hh], 0.0).astype(bf16) for hh in range(N_HEADS)]
        ya = ga[:, :A_WIDTH] * _gmlp_gate(vn.astype(bf16), wsm, bst_ref[...], tile)
        yb = jnp.concatenate([_mm(plb[:, gi * HEAD:(gi + 1) * HEAD], wp_ref[gi].astype(bf16))
                              for gi in range(len(POOL_WINDOWS))], axis=1) * sc_ref[...]
        mix = jnp.concatenate([ya, yb], axis=1).astype(bf16)
        mix_ref[...] = mix
        ho_ref[...] = h + _mm(mix, wout_ref[...])

    row = lambda cols: pl.BlockSpec((tile, cols), lambda i: (i, 0))
    return _launch(
        body, name="even_fwd", grid=(n_tiles,), jobs=jobs,
        in_specs=[row(D_MODEL), _prev_halo(tile, POOL_HALO, D_MODEL), _const(w_in.shape, 1), _const(w_out.shape, 1),
                  _const(ws.shape, 1), _const(bst.shape, 1), _const(lng.shape, 1), _const(lnb.shape, 1),
                  _const(wp.shape, 1), _const(sc.shape, 1), _const(gm.shape, 1)],
        out_specs=[row(D_MODEL), row(D_MODEL), row(2 * A_WIDTH), row(A_WIDTH), row(D_MODEL)],
        out_shape=[jax.ShapeDtypeStruct((seq, D_MODEL), f32), jax.ShapeDtypeStruct((seq, D_MODEL), bf16),
                   jax.ShapeDtypeStruct((seq, 2 * A_WIDTH), bf16), jax.ShapeDtypeStruct((seq, A_WIDTH), bf16),
                   jax.ShapeDtypeStruct((seq, D_MODEL), bf16)],
        args=(h, h, w_in, w_out, ws, bst, lng, lnb, wp, sc, gm))


def _even_bwd(dh, h, za, pooled, w_in, w_out, ws, bst, lng, lnb, wp, sc, gm, jobs=()):
    seq = h.shape[0]
    tile = min(MIX_TILE, seq)
    n_tiles = seq // tile
    n_groups = len(POOL_WINDOWS)

    def body(dh_ref, dhx_ref, h_ref, za_ref, pool_ref, win_ref, wout_ref, ws_ref, bst_ref, lng_ref, lnb_ref,
             wp_ref, sc_ref, g_ref,
             dhi_ref, dz_ref, dws_ref, dbs_ref, dlng_ref, dlnb_ref, dwp_ref, dsc_ref, dg_ref):
        i = pl.program_id(0)

        @pl.when(i == 0)
        def _():
            for ref in (dws_ref, dbs_ref, dlng_ref, dlnb_ref, dwp_ref, dsc_ref, dg_ref):
                ref[...] = jnp.zeros_like(ref)

        dh = dh_ref[...]
        dmix = _mm_nt(dh.astype(bf16), wout_ref[...])
        dya = dmix[:, :A_WIDTH]
        dyb = dmix[:, A_WIDTH:]
        dybx = _mm_nt(dhx_ref[...].astype(bf16), wout_ref[A_WIDTH:, :])
        dybx = jnp.where(i < n_tiles - 1, dybx, 0.0)

        za = za_ref[...].astype(f32)
        ga, th = _gelu(za)
        u = ga[:, :A_WIDTH]
        lng = lng_ref[...]
        vn, vh, r = _ln_fwd(ga[:, A_WIDTH:], lng, lnb_ref[...])
        vnb = vn.astype(bf16)
        mask = _chunk_mask()
        wsf = [jnp.where(mask, ws_ref[hh], 0.0) for hh in range(N_HEADS)]
        sv = _gmlp_gate(vnb, [w.astype(bf16) for w in wsf], bst_ref[...], tile)
        du = dya * sv
        dsvb = (dya * u).astype(bf16)
        wst = [w.T.astype(bf16) for w in wsf]
        ones = jnp.ones((8, HEAD), bf16)
        dws = [jnp.zeros((HEAD, HEAD), f32) for _ in range(N_HEADS)]
        dbs = [jnp.zeros((8, HEAD), f32) for _ in range(N_HEADS)]
        rows = []
        for n in range(tile // HEAD):
            cols = []
            for hh in range(N_HEADS):
                blk = dsvb[n * HEAD:(n + 1) * HEAD, hh * HEAD:(hh + 1) * HEAD]
                cols.append(_mm(wst[hh], blk))
                dws[hh] = dws[hh] + _mm_nt(blk, vnb[n * HEAD:(n + 1) * HEAD, hh * HEAD:(hh + 1) * HEAD])
                dbs[hh] = dbs[hh] + _mm_nt(ones, blk)
            rows.append(jnp.concatenate(cols, axis=1))
        dvn = jnp.concatenate(rows, axis=0)
        for hh in range(N_HEADS):
            dws_ref[hh] += jnp.where(mask, dws[hh], 0.0)
            dbs_ref[pl.ds(hh, 1), :] += dbs[hh][0:1, :]
        dlng_ref[...] += jnp.sum(dvn * vh, axis=0, keepdims=True)
        dlnb_ref[...] += jnp.sum(dvn, axis=0, keepdims=True)
        dv = _ln_bwd(dvn, vh, r, lng)
        dza = jnp.concatenate([du, dv], axis=1) * _gelu_grad(za, th)

        plb = pool_ref[...]
        sc = sc_ref[...]
        dzb = []
        dsc = []
        for gi, win in enumerate(POOL_WINDOWS):
            cs = slice(gi * HEAD, (gi + 1) * HEAD)
            wpb = wp_ref[gi].astype(bf16)
            dsc.append(jnp.sum(dyb[:, cs] * _mm(plb[:, cs], wpb), axis=0, keepdims=True))
            dpre = (dyb[:, cs] * sc[:, cs]).astype(bf16)
            dprex = (dybx[:, cs] * sc[:, cs]).astype(bf16)
            dwp_ref[gi] += _mm_tn(plb[:, cs], dpre)
            dpl = _mm_nt(dpre, wpb)
            dple = jnp.concatenate([dpl, _mm_nt(dprex, wpb)], axis=0)
            q = dple * _inv_count(i * tile, tile + POOL_HALO, win)
            dzb.append(_window_sum(q, win, _up)[:tile] - dpl)
        dsc_ref[...] += jnp.concatenate(dsc, axis=1)

        dzf = jnp.concatenate([dza] + dzb, axis=1).astype(bf16)
        dz_ref[...] = dzf
        dhn = _mm(dzf, win_ref[...])
        dhr, dg = _rms_bwd(dhn, h_ref[...], g_ref[...])
        dhi_ref[...] = dh + dhr
        dg_ref[...] += dg

    row = lambda cols: pl.BlockSpec((tile, cols), lambda i: (i, 0))
    small = [ws.shape, (N_HEADS, HEAD), lng.shape, lnb.shape, wp.shape, sc.shape, gm.shape]
    return _launch(
        body, name="even_bwd", grid=(n_tiles,), jobs=jobs,
        in_specs=[row(D_MODEL), _next_halo(tile, POOL_HALO, D_MODEL, seq), row(D_MODEL), row(2 * A_WIDTH), row(A_WIDTH),
                  _const(w_in.shape, 1), _const(w_out.shape, 1), _const(ws.shape, 1), _const(bst.shape, 1),
                  _const(lng.shape, 1), _const(lnb.shape, 1), _const(wp.shape, 1), _const(sc.shape, 1), _const(gm.shape, 1)],
        out_specs=[row(D_MODEL), row(3 * A_WIDTH)] + [_const(s, 1) for s in small],
        out_shape=[jax.ShapeDtypeStruct((seq, D_MODEL), f32), jax.ShapeDtypeStruct((seq, 3 * A_WIDTH), bf16)]
                  + [jax.ShapeDtypeStruct(s, f32) for s in small],
        args=(dh, dh, h, za, pooled, w_in, w_out, ws, bst, lng, lnb, wp, sc, gm))


SUBLANES = 8


class _Shifted:
    def __init__(self, x, shift, max_shift):
        self.rolled = [shift(x, b) for b in range(min(SUBLANES, max_shift + 1))]
        self.back = shift is _down

    def rows(self, k, start, count):
        whole = k - k % SUBLANES
        lo = start - whole if self.back else start + whole
        return self.rolled[k % SUBLANES][lo:lo + count]


def _conv_taps(xs, w_ref, n_taps, halo, rows):
    acc = None
    for j in range(n_taps):
        term = w_ref[pl.ds(j, 1), :] * xs.rows(n_taps - 1 - j, halo, rows)
        acc = term if acc is None else acc + term
    return acc


def _odd_fwd(h, w_in, w_out, cw, cb, clg, clb, dw, gm):
    seq = h.shape[0]
    tile = min(ODD_FWD_TILE, seq)
    n_tiles = seq // tile
    w = A_WIDTH

    def body(h_ref, hp_ref, win_ref, wout_ref, cw_ref, cb_ref, clg_ref, clb_ref, dw_ref, g_ref,
             ho_ref, hn_ref, z_ref, mix_ref, cv_ref):
        i = pl.program_id(0)
        g = g_ref[...]
        h = h_ref[...]
        hnb = (h * _rms_r(h) * g).astype(bf16)
        hn_ref[...] = hnb
        zb = _mm_nt(hnb, win_ref[...]).astype(bf16)
        z_ref[...] = zb
        hp = hp_ref[...]
        zp = _mm_nt((hp * _rms_r(hp) * g).astype(bf16), win_ref[...]).astype(bf16).astype(f32)
        z = zb.astype(f32)
        ze = jnp.concatenate([jnp.where(i > 0, zp, 0.0), z], axis=0)
        hc = ze[:, :w] * _sigmoid(ze[:, w:2 * w])
        cv = _conv_taps(_Shifted(hc, _down, C_KERNEL - 1), cw_ref, C_KERNEL, CONV_HALO, tile) + cb_ref[...]
        cv_ref[...] = cv
        ln, _, _ = _ln_fwd(cv, clg_ref[...], clb_ref[...])
        yc = ln * _sigmoid(ln)
        p = ze[:, 3 * w:4 * w] * ze[:, 4 * w:]
        yd = z[:, 2 * w:3 * w] * _conv_taps(_Shifted(p, _down, D_KERNEL - 1), dw_ref, D_KERNEL, CONV_HALO, tile)
        mix = jnp.concatenate([yc, yd], axis=1).astype(bf16)
        mix_ref[...] = mix
        ho_ref[...] = h + _mm(mix, wout_ref[...])

    row = lambda cols: pl.BlockSpec((tile, cols), lambda i: (i, 0))
    return pl.pallas_call(
        body, name="odd_fwd", grid=(n_tiles,),
        in_specs=[row(D_MODEL), _prev_halo(tile, CONV_HALO, D_MODEL), _const(w_in.shape, 1), _const(w_out.shape, 1),
                  _const(cw.shape, 1), _const(cb.shape, 1), _const(clg.shape, 1), _const(clb.shape, 1),
                  _const(dw.shape, 1), _const(gm.shape, 1)],
        out_specs=[row(D_MODEL), row(D_MODEL), row(5 * w), row(D_MODEL), row(w)],
        out_shape=[jax.ShapeDtypeStruct((seq, D_MODEL), f32), jax.ShapeDtypeStruct((seq, D_MODEL), bf16),
                   jax.ShapeDtypeStruct((seq, 5 * w), bf16), jax.ShapeDtypeStruct((seq, D_MODEL), bf16),
                   jax.ShapeDtypeStruct((seq, w), f32)],
        compiler_params=_params(1),
    )(h, h, w_in, w_out, cw, cb, clg, clb, dw, gm)


def _odd_bwd(dh, h, z, cv, w_in, w_out, cw, clg, clb, dw, gm, jobs=()):
    seq = h.shape[0]
    tile = min(MIX_TILE, seq)
    n_tiles = seq // tile
    w = A_WIDTH
    halo = CONV_HALO

    def body(dh_ref, dhx_ref, h_ref, z_ref, zp_ref, zx_ref, cv_ref, cvx_ref, win_ref, wout_ref, cw_ref,
             clg_ref, clb_ref, dw_ref, g_ref,
             dhi_ref, dz_ref, dcw_ref, dcb_ref, dclg_ref, dclb_ref, ddw_ref, dg_ref):
        i = pl.program_id(0)

        @pl.when(i == 0)
        def _():
            for ref in (dcw_ref, dcb_ref, dclg_ref, dclb_ref, ddw_ref, dg_ref):
                ref[...] = jnp.zeros_like(ref)

        dh = dh_ref[...]
        dhe = jnp.concatenate([dh, jnp.where(i < n_tiles - 1, dhx_ref[...], 0.0)], axis=0)
        dmix = _mm_nt(dhe.astype(bf16), wout_ref[...])
        ze = jnp.concatenate([jnp.where(i > 0, zp_ref[...].astype(f32), 0.0), z_ref[...].astype(f32),
                              zx_ref[...].astype(f32)], axis=0)

        sg = _sigmoid(ze[:, w:2 * w])
        ca = ze[:, :w]
        hc = ca * sg
        hcs = _Shifted(hc, _down, C_KERNEL - 1)
        cv = jnp.concatenate([cv_ref[...], cvx_ref[...]], axis=0)
        clg = clg_ref[...]
        ln, xh, r = _ln_fwd(cv, clg, clb_ref[...])
        sl = _sigmoid(ln)
        dln = dmix[:, :w] * (sl * (1.0 + ln * (1.0 - sl)))
        dclg_ref[...] += jnp.sum((dln * xh)[:tile], axis=0, keepdims=True)
        dclb_ref[...] += jnp.sum(dln[:tile], axis=0, keepdims=True)
        dcv = _ln_bwd(dln, xh, r, clg)
        dcb_ref[...] += jnp.sum(dcv[:tile], axis=0, keepdims=True)
        dcvs = _Shifted(dcv, _up, C_KERNEL - 1)
        dhc = None
        for j in range(C_KERNEL):
            k = C_KERNEL - 1 - j
            dcw_ref[pl.ds(j, 1), :] += jnp.sum(dcv[:tile] * hcs.rows(k, halo, tile), axis=0, keepdims=True)
            term = cw_ref[pl.ds(j, 1), :] * dcvs.rows(k, 0, tile)
            dhc = term if dhc is None else dhc + term
        sgt = sg[halo:halo + tile]
        cat = ca[halo:halo + tile]
        dca = dhc * sgt
        dcg = dhc * cat * sgt * (1.0 - sgt)

        dcgv = ze[:, 3 * w:4 * w]
        dxin = ze[:, 4 * w:]
        p = dcgv * dxin
        ps = _Shifted(p, _down, D_KERNEL - 1)
        q = _conv_taps(ps, dw_ref, D_KERNEL, halo, tile)
        dyd = dmix[:, w:]
        dq = dyd * ze[halo:, 2 * w:3 * w]
        ddbg = dyd[:tile] * q
        dqs = _Shifted(dq, _up, D_KERNEL - 1)
        dp = None
        for j in range(D_KERNEL):
            k = D_KERNEL - 1 - j
            ddw_ref[pl.ds(j, 1), :] += jnp.sum(dq[:tile] * ps.rows(k, halo, tile), axis=0, keepdims=True)
            term = dw_ref[pl.ds(j, 1), :] * dqs.rows(k, 0, tile)
            dp = term if dp is None else dp + term
        ddcg = dp * dxin[halo:halo + tile]
        ddxin = dp * dcgv[halo:halo + tile]

        dzf = jnp.concatenate([dca, dcg, ddbg, ddcg, ddxin], axis=1).astype(bf16)
        dz_ref[...] = dzf
        dhn = _mm(dzf, win_ref[...])
        dhr, dg = _rms_bwd(dhn, h_ref[...], g_ref[...])
        dhi_ref[...] = dh + dhr
        dg_ref[...] += dg

    row = lambda cols: pl.BlockSpec((tile, cols), lambda i: (i, 0))
    small = [cw.shape, clg.shape, clg.shape, clb.shape, dw.shape, gm.shape]
    return _launch(
        body, name="odd_bwd", grid=(n_tiles,), jobs=jobs,
        in_specs=[row(D_MODEL), _next_halo(tile, halo, D_MODEL, seq), row(D_MODEL), row(5 * w),
                  _prev_halo(tile, halo, 5 * w), _next_halo(tile, halo, 5 * w, seq),
                  row(w), _next_halo(tile, halo, w, seq),
                  _const(w_in.shape, 1), _const(w_out.shape, 1), _const(cw.shape, 1),
                  _const(clg.shape, 1), _const(clb.shape, 1), _const(dw.shape, 1), _const(gm.shape, 1)],
        out_specs=[row(D_MODEL), row(5 * w)] + [_const(s, 1) for s in small],
        out_shape=[jax.ShapeDtypeStruct((seq, D_MODEL), f32), jax.ShapeDtypeStruct((seq, 5 * w), bf16)]
                  + [jax.ShapeDtypeStruct(s, f32) for s in small],
        args=(dh, dh, h, z, z, z, cv, cv, w_in, w_out, cw, clg, clb, dw, gm))


def _ffn_chunks():
    assert sum(FFN_CHUNKS) == D_FF
    start = 0
    for size in FFN_CHUNKS:
        yield slice(start, start + size)
        start += size


def _ffn_fwd(h, wg, wu, wd, gm, jobs=(), head=None):
    seq = h.shape[0]
    tile = min(FFN_TILE, seq)

    def body(h_ref, g_ref, wg_ref, wu_ref, wd_ref, *refs):
        if head is None:
            ho_ref, hn_ref, gate_ref, up_ref = refs
        else:
            t_ref, gf_ref, ho_ref, hn_ref, gate_ref, up_ref, loss_ref, dgf_ref = refs
        h = h_ref[...]
        hnb = (h * _rms_r(h) * g_ref[...]).astype(bf16)
        hn_ref[...] = hnb
        acc = None
        for rows in _ffn_chunks():
            gb = _mm_nt(hnb, wg_ref[rows, :]).astype(bf16)
            ub = _mm_nt(hnb, wu_ref[rows, :]).astype(bf16)
            gate_ref[:, rows] = gb
            up_ref[:, rows] = ub
            gf = gb.astype(f32)
            act = gf * _sigmoid(gf) * ub.astype(f32)
            part = _mm(act.astype(bf16), wd_ref[rows, :])
            acc = part if acc is None else acc + part
        ho = h + acc
        if head is None:
            ho_ref[...] = ho
            return

        @pl.when(pl.program_id(0) == 0)
        def _():
            loss_ref[...] = jnp.zeros_like(loss_ref)
            dgf_ref[...] = jnp.zeros_like(dgf_ref)

        g_final = gf_ref[...]
        err = ho * _rms_r(ho) * g_final - t_ref[...]
        loss_ref[...] += (0.5 / D_MODEL) * jnp.sum(jnp.sum(err * err, axis=1, keepdims=True), axis=0, keepdims=True)
        dho, dg = _rms_bwd(err * (1.0 / D_MODEL), ho, g_final)
        ho_ref[...] = dho
        dgf_ref[...] += dg

    row = pl.BlockSpec((tile, D_MODEL), lambda i: (i, 0))
    wide = pl.BlockSpec((tile, D_FF), lambda i: (i, 0))
    in_specs = [row, _const(gm.shape, 1), _const(wg.shape, 1), _const(wu.shape, 1), _const(wd.shape, 1)]
    out_specs = [row, row, wide, wide]
    out_shape = [jax.ShapeDtypeStruct((seq, D_MODEL), f32), jax.ShapeDtypeStruct((seq, D_MODEL), bf16),
                 jax.ShapeDtypeStruct((seq, D_FF), bf16), jax.ShapeDtypeStruct((seq, D_FF), bf16)]
    args = (h, gm, wg, wu, wd)
    if head is not None:
        target, g_final = head
        in_specs += [row, _const(g_final.shape, 1)]
        out_specs += [_const((1, 1), 1), _const(g_final.shape, 1)]
        out_shape += [jax.ShapeDtypeStruct((1, 1), f32), jax.ShapeDtypeStruct(g_final.shape, f32)]
        args += (target, g_final)
    return _launch(
        body, name="ffn_fwd" if head is None else "ffn_fwd_loss", grid=(seq // tile,), jobs=jobs,
        in_specs=in_specs, out_specs=out_specs, out_shape=out_shape, args=args)


def _ffn_bwd(dh, h, gate, up, wg, wu, wd, gm, jobs=()):
    seq = h.shape[0]
    tile = min(FFN_TILE, seq)
    n_tiles = seq // tile

    def body(dh_ref, h_ref, g_ref, gate_ref, up_ref, wg_ref, wu_ref, wd_ref,
             dhi_ref, dgate_ref, dup_ref, act_ref, dg_ref):
        @pl.when(pl.program_id(0) == 0)
        def _():
            dg_ref[...] = jnp.zeros_like(dg_ref)

        dh = dh_ref[...]
        dhb = dh.astype(bf16)
        acc = None
        for rows in _ffn_chunks():
            dact = _mm_nt(dhb, wd_ref[rows, :])
            gf = gate_ref[:, rows].astype(f32)
            uf = up_ref[:, rows].astype(f32)
            s = _sigmoid(gf)
            silu = gf * s
            act_ref[:, rows] = (silu * uf).astype(bf16)
            dgb = (dact * uf * (s * (1.0 + gf * (1.0 - s)))).astype(bf16)
            dub = (dact * silu).astype(bf16)
            dgate_ref[:, rows] = dgb
            dup_ref[:, rows] = dub
            part = _mm(dgb, wg_ref[rows, :]) + _mm(dub, wu_ref[rows, :])
            acc = part if acc is None else acc + part
        dhr, dg = _rms_bwd(acc, h_ref[...], g_ref[...])
        dhi_ref[...] = dh + dhr
        dg_ref[...] += dg

    row = pl.BlockSpec((tile, D_MODEL), lambda i: (i, 0))
    wide = pl.BlockSpec((tile, D_FF), lambda i: (i, 0))
    return _launch(
        body, name="ffn_bwd", grid=(n_tiles,), jobs=jobs,
        in_specs=[row, row, _const(gm.shape, 1), wide, wide, _const(wg.shape, 1), _const(wu.shape, 1), _const(wd.shape, 1)],
        out_specs=[row, wide, wide, wide, _const(gm.shape, 1)],
        out_shape=[jax.ShapeDtypeStruct((seq, D_MODEL), f32), jax.ShapeDtypeStruct((seq, D_FF), bf16),
                   jax.ShapeDtypeStruct((seq, D_FF), bf16), jax.ShapeDtypeStruct((seq, D_FF), bf16),
                   jax.ShapeDtypeStruct(gm.shape, f32)],
        args=(dh, h, gm, gate, up, wg, wu, wd))


def _weight_grads(pairs, name, jobs=()):
    seq, m = pairs[0][0].shape
    tk = min(DW_TK, seq)
    tm = m if m <= DW_TM else m // 2
    n_k = seq // tk
    n_pairs = len(pairs)

    def body(*refs):
        x_refs = refs[0:2 * n_pairs:2]
        y_refs = refs[1:2 * n_pairs:2]
        o_refs = refs[2 * n_pairs:3 * n_pairs]
        acc_refs = refs[3 * n_pairs:]
        k = pl.program_id(1)
        @pl.when(k == 0)
        def _():
            for acc_ref in acc_refs:
                acc_ref[...] = jnp.zeros_like(acc_ref)

        for x_ref, y_ref, acc_ref in zip(x_refs, y_refs, acc_refs):
            acc_ref[...] += _mm_tn(x_ref[...].astype(bf16), y_ref[...].astype(bf16))

        @pl.when(k == n_k - 1)
        def _():
            for o_ref, acc_ref in zip(o_refs, acc_refs):
                o_ref[...] = acc_ref[...].astype(bf16)

    in_specs = []
    for _ in pairs:
        in_specs += [pl.BlockSpec((tk, tm), lambda j, k: (k, j)), pl.BlockSpec((tk, D_MODEL), lambda j, k: (k, 0))]
    return _launch(
        body, name=name, grid=(m // tm, n_k), jobs=jobs,
        in_specs=in_specs,
        out_specs=[pl.BlockSpec((tm, D_MODEL), lambda j, k: (j, 0))] * n_pairs,
        out_shape=[jax.ShapeDtypeStruct((m, D_MODEL), bf16)] * n_pairs,
        scratch=[pltpu.VMEM((tm, D_MODEL), f32)] * n_pairs,
        args=[a for pair in pairs for a in pair])


def _row_tile(rows, limit=512):
    best = rows
    for t in range(8, min(rows, limit) + 1, 8):
        if rows % t == 0:
            best = t
    return best if rows > limit else rows


def _adam_step(w, g, m, v):
    m2 = ADAM_B1 * m + (1.0 - ADAM_B1) * g
    v2 = ADAM_B2 * v + (1.0 - ADAM_B2) * (g * g)
    m_hat = m2 / (1.0 - ADAM_B1 ** ADAM_STEP)
    v_hat = v2 / (1.0 - ADAM_B2 ** ADAM_STEP)
    return -ADAM_LR * (m_hat / (jnp.sqrt(v_hat) + ADAM_EPS) + ADAM_WD * w), m2, v2


def _adamw_small(items, name):
    n = len(items)

    def body(*refs):
        ins, outs = refs[:4 * n], refs[4 * n:]
        for k in range(n):
            w_ref, g_ref, m_ref, v_ref = ins[4 * k:4 * k + 4]
            d_ref, mo_ref, vo_ref = outs[3 * k:3 * k + 3]
            d_ref[...], mo_ref[...], vo_ref[...] = _adam_step(w_ref[...], g_ref[...], m_ref[...], v_ref[...])

    def whole(a):
        return pl.BlockSpec(a.shape, lambda i: (0, 0))

    results = pl.pallas_call(
        body, name=name, grid=(1,),
        in_specs=[whole(a) for item in items for a in item],
        out_specs=[whole(item[0]) for item in items for _ in range(3)],
        out_shape=[jax.ShapeDtypeStruct(item[0].shape, f32) for item in items for _ in range(3)],
        compiler_params=_params(1),
    )(*[a for item in items for a in item])
    return [results[3 * k:3 * k + 3] for k in range(n)]


def _adamw_reduced(w, parts, m, v, name):
    layers, rows, cols = w.shape

    def body(*refs):
        w_ref, m_ref, v_ref = refs[:3]
        part_refs = refs[3:3 + layers]
        g_ref, d_ref, mo_ref, vo_ref = refs[3 + layers:]
        layer = pl.program_id(0)
        for l, p_ref in enumerate(part_refs):
            @pl.when(layer == l)
            def _():
                acc = p_ref[0].astype(f32)
                for k in range(1, N_CHIP):
                    acc = acc + p_ref[k].astype(f32)
                g_ref[0] = acc

        d_ref[0], mo_ref[0], vo_ref[0] = _adam_step(w_ref[0], g_ref[0], m_ref[0], v_ref[0])

    blk = pl.BlockSpec((1, rows, cols), lambda l: (l, 0, 0))
    return pl.pallas_call(
        body, name=name, grid=(layers,),
        in_specs=[blk] * 3 + [pl.BlockSpec(p.shape, lambda l: (0, 0, 0)) for p in parts],
        out_specs=[blk] * 4,
        out_shape=[jax.ShapeDtypeStruct(w.shape, f32)] * 4,
        compiler_params=_params(1),
    )(w, m, v, *parts)


def _sum_leading(x, name):
    n, rows, cols = x.shape
    tr = _row_tile(rows)

    def body(x_ref, o_ref):
        acc = x_ref[0].astype(f32)
        for k in range(1, n):
            acc = acc + x_ref[k].astype(f32)
        o_ref[...] = acc

    return pl.pallas_call(
        body, name=name, grid=(rows // tr,),
        in_specs=[pl.BlockSpec((n, tr, cols), lambda i: (0, i, 0))],
        out_specs=pl.BlockSpec((tr, cols), lambda i: (i, 0)),
        out_shape=jax.ShapeDtypeStruct((rows, cols), f32),
        compiler_params=_params(1),
    )(x)


def _pair_sum(gs, recvs, c_idx, name):
    n = len(gs)

    def body(c_ref, *refs):
        for g_ref, r_ref, o_ref in zip(refs[:n], refs[n:2 * n], refs[2 * n:]):
            o_ref[...] = (g_ref[...].astype(f32) + r_ref[...].astype(f32)).astype(o_ref.dtype)

    own = [pl.BlockSpec((1,) + g.shape[1:], lambda k, c: (2 * k + c[0], 0, 0)) for g in gs]
    by_chip = [pl.BlockSpec((1,) + g.shape[1:], lambda k, c: (k, 0, 0)) for g in gs]
    return list(pl.pallas_call(
        body, name=name,
        grid_spec=pltpu.PrefetchScalarGridSpec(num_scalar_prefetch=1, grid=(N_CHIP,),
                                               in_specs=own + by_chip, out_specs=by_chip),
        out_shape=[jax.ShapeDtypeStruct((N_CHIP,) + g.shape[1:], g.dtype) for g in gs],
        compiler_params=_params(1),
    )(c_idx, *gs, *recvs))


def _pack_rows(w):
    return w.reshape(N_DEV, -1, D_MODEL)


def kernel(x, even_w_in, even_w_out, a_w_s, a_b_s, a_ln_g, a_ln_b, b_w_pool, b_scale, odd_w_in, odd_w_out, c_w_dw, c_b_dw, c_ln_g, c_ln_b, d_w_dw, norm_mix_g, norm_ffn_g, ffn_w_gate, ffn_w_up, ffn_w_down, final_norm_g, loss_target, m_even_w_in, m_even_w_out, m_a_w_s, m_a_b_s, m_a_ln_g, m_a_ln_b, m_b_w_pool, m_b_scale, m_odd_w_in, m_odd_w_out, m_c_w_dw, m_c_b_dw, m_c_ln_g, m_c_ln_b, m_d_w_dw, m_norm_mix_g, m_norm_ffn_g, m_ffn_w_gate, m_ffn_w_up, m_ffn_w_down, m_final_norm_g, v_even_w_in, v_even_w_out, v_a_w_s, v_a_b_s, v_a_ln_g, v_a_ln_b, v_b_w_pool, v_b_scale, v_odd_w_in, v_odd_w_out, v_c_w_dw, v_c_b_dw, v_c_ln_g, v_c_ln_b, v_d_w_dw, v_norm_mix_g, v_norm_ffn_g, v_ffn_w_gate, v_ffn_w_up, v_ffn_w_down, v_final_norm_g):
    weights = dict(even_w_in=even_w_in, even_w_out=even_w_out, a_w_s=a_w_s, a_b_s=a_b_s, a_ln_g=a_ln_g, a_ln_b=a_ln_b,
                   b_w_pool=b_w_pool, b_scale=b_scale, odd_w_in=odd_w_in, odd_w_out=odd_w_out, c_w_dw=c_w_dw,
                   c_b_dw=c_b_dw, c_ln_g=c_ln_g, c_ln_b=c_ln_b, d_w_dw=d_w_dw, norm_mix_g=norm_mix_g,
                   norm_ffn_g=norm_ffn_g, ffn_w_gate=ffn_w_gate, ffn_w_up=ffn_w_up, ffn_w_down=ffn_w_down,
                   final_norm_g=final_norm_g)
    m_in = dict(even_w_in=m_even_w_in, even_w_out=m_even_w_out, a_w_s=m_a_w_s, a_b_s=m_a_b_s, a_ln_g=m_a_ln_g,
                a_ln_b=m_a_ln_b, b_w_pool=m_b_w_pool, b_scale=m_b_scale, odd_w_in=m_odd_w_in, odd_w_out=m_odd_w_out,
                c_w_dw=m_c_w_dw, c_b_dw=m_c_b_dw, c_ln_g=m_c_ln_g, c_ln_b=m_c_ln_b, d_w_dw=m_d_w_dw,
                norm_mix_g=m_norm_mix_g, norm_ffn_g=m_norm_ffn_g, ffn_w_gate=m_ffn_w_gate, ffn_w_up=m_ffn_w_up,
                ffn_w_down=m_ffn_w_down, final_norm_g=m_final_norm_g)
    v_in = dict(even_w_in=v_even_w_in, even_w_out=v_even_w_out, a_w_s=v_a_w_s, a_b_s=v_a_b_s, a_ln_g=v_a_ln_g,
                a_ln_b=v_a_ln_b, b_w_pool=v_b_w_pool, b_scale=v_b_scale, odd_w_in=v_odd_w_in, odd_w_out=v_odd_w_out,
                c_w_dw=v_c_w_dw, c_b_dw=v_c_b_dw, c_ln_g=v_c_ln_g, c_ln_b=v_c_ln_b, d_w_dw=v_d_w_dw,
                norm_mix_g=v_norm_mix_g, norm_ffn_g=v_norm_ffn_g, ffn_w_gate=v_ffn_w_gate, ffn_w_up=v_ffn_w_up,
                ffn_w_down=v_ffn_w_down, final_norm_g=v_final_norm_g)
    names = list(weights)

    group_parts = {
        "even": [even_w_in[0].T, even_w_out[0]],
        "ffn0": [ffn_w_gate[0].T, ffn_w_up[0].T, ffn_w_down[0]],
        "odd": [odd_w_in[0].T, odd_w_out[0]],
        "ffn1": [ffn_w_gate[1].T, ffn_w_up[1].T, ffn_w_down[1]],
    }

    def gather_jobs(*groups):
        return [_all_gather_job(p.astype(bf16)) for k in groups for p in group_parts[k]]

    def whole(gathered):
        return [g.reshape(-1, D_MODEL) for g in gathered]

    conv_names = ["c_w_dw", "c_b_dw", "c_ln_g", "c_ln_b", "d_w_dw"]
    conv_rows = [C_KERNEL, 1, 1, 1, D_KERNEL]
    conv_local = jnp.concatenate([weights[n].reshape(r, -1) for n, r in zip(conv_names, conv_rows)]
                                 + [jnp.zeros((3, c_b_dw.shape[-1]), f32)], axis=0)
    w_in_e, w_out_e = whole(_run_jobs(gather_jobs("even"), "gather_even"))

    ws, bst = a_w_s[0], a_b_s[0].T
    lng, lnb, wp, sc = a_ln_g, a_ln_b, b_w_pool[0], b_scale
    gmix = [norm_mix_g[l:l + 1] for l in range(2)]
    gffn = [norm_ffn_g[l:l + 1] for l in range(2)]
    gfin = final_norm_g.reshape(1, D_MODEL)

    h0 = x[0]
    h1, hn_e, za, pooled, mix_e, *ffn0_gathered, conv_all = _even_fwd(
        h0, w_in_e, w_out_e, ws, bst, lng, lnb, wp, sc, gmix[0],
        jobs=gather_jobs("ffn0") + [_all_gather_job(conv_local)])
    w_gate0, w_up0, w_down0 = whole(ffn0_gathered)
    conv_all = conv_all.transpose(1, 0, 2).reshape(conv_local.shape[0], -1)
    conv_offs = [sum(conv_rows[:k]) for k in range(len(conv_rows) + 1)]
    cw, cb, clg, clb, dw = [conv_all[conv_offs[k]:conv_offs[k + 1]] for k in range(len(conv_rows))]
    h2, hn_f0, gate0, up0, *rest_gathered = _ffn_fwd(h1, w_gate0, w_up0, w_down0, gffn[0],
                                                     jobs=gather_jobs("odd", "ffn1"))
    w_in_o, w_out_o, w_gate1, w_up1, w_down1 = whole(rest_gathered)
    h3, hn_o, z_o, mix_o, cv_o = _odd_fwd(h2, w_in_o, w_out_o, cw, cb, clg, clb, dw, gmix[1])
    dh4, hn_f1, gate1, up1, loss_local, g_final = _ffn_fwd(h3, w_gate1, w_up1, w_down1, gffn[1],
                                                           head=(loss_target[0], gfin))

    c_idx = lax.axis_index("c").astype(jnp.int32).reshape(1)

    def weight_grad(x, y, name, jobs=()):
        g, *job_results = _weight_grads([(x, y)], name, jobs=jobs)
        return [_pack_rows(g)] + job_results

    def siblings(parts):
        return [_sibling_exchange_job(p) for p in parts]

    def chips(pairs):
        return [_chip_exchange_job(p) for p in pairs]

    dh3, dgate1, dup1, act1, g_ffn1 = _ffn_bwd(dh4, h3, gate1, up1, w_gate1, w_up1, w_down1, gffn[1])
    part_ffn1 = (weight_grad(dgate1, hn_f1, "dw_gate1") + weight_grad(dup1, hn_f1, "dw_up1")
                 + weight_grad(act1, dh4, "dw_down1"))
    dh2, dz_o, g_cw, g_cb, g_clg, g_clb, g_dw, g_mix1, *recv_ffn1 = _odd_bwd(
        dh3, h2, z_o, cv_o, w_in_o, w_out_o, cw, clg, clb, dw, gmix[1], jobs=siblings(part_ffn1))
    pair_ffn1 = _pair_sum(part_ffn1, recv_ffn1, c_idx, "pair_sum_ffn1")
    part_odd = weight_grad(dz_o, hn_o, "dw_odd_in") + weight_grad(mix_o, dh3, "dw_odd_out")
    dh1, dgate0, dup0, act0, g_ffn0, *exchanged = _ffn_bwd(
        dh2, h1, gate0, up0, w_gate0, w_up0, w_down0, gffn[0], jobs=chips(pair_ffn1) + siblings(part_odd))
    chips_ffn1, recv_odd = exchanged[:3], exchanged[3:]
    pair_odd = _pair_sum(part_odd, recv_odd, c_idx, "pair_sum_odd")
    dw_gate0, *chips_odd = weight_grad(dgate0, hn_f0, "dw_gate0", jobs=chips(pair_odd))
    part_ffn0 = [dw_gate0] + weight_grad(dup0, hn_f0, "dw_up0") + weight_grad(act0, dh2, "dw_down0")
    part_even_out, *recv_ffn0 = weight_grad(mix_e, dh1, "dw_even_out", jobs=siblings(part_ffn0))
    pair_ffn0 = _pair_sum(part_ffn0, recv_ffn0, c_idx, "pair_sum_ffn0")
    dh0, dz_e, g_ws, g_bs, g_lng, g_lnb, g_wp, g_sc, g_mix0, *exchanged = _even_bwd(
        dh1, h0, za, pooled, w_in_e, w_out_e, ws, bst, lng, lnb, wp, sc, gmix[0],
        jobs=chips(pair_ffn0) + siblings([part_even_out]))
    chips_ffn0, recv_even_out = exchanged[:3], exchanged[3:]
    pair_even_out = _pair_sum([part_even_out], recv_even_out, c_idx, "pair_sum_even_out")

    lanes = HEAD
    small = [("a_w_s", g_ws), ("a_b_s", g_bs), ("a_ln_g", g_lng), ("a_ln_b", g_lnb), ("b_w_pool", g_wp),
             ("b_scale", g_sc), ("norm_mix_g", jnp.concatenate([g_mix0, g_mix1], axis=0)),
             ("norm_ffn_g", jnp.concatenate([g_ffn0, g_ffn1], axis=0)), ("final_norm_g", g_final),
             ("c_w_dw", g_cw), ("c_b_dw", g_cb), ("c_ln_g", g_clg), ("c_ln_b", g_clb), ("d_w_dw", g_dw),
             ("loss", loss_local)]
    small_rows = [-(-g.size // (8 * lanes)) * 8 for _, g in small]
    small_offs = [sum(small_rows[:k]) for k in range(len(small) + 1)]
    pad_rows = -small_offs[-1] % 256
    small_buf = jnp.concatenate(
        [jnp.pad(g.reshape(-1), (0, r * lanes - g.size)).reshape(r, lanes) for (_, g), r in zip(small, small_rows)]
        + [jnp.zeros((pad_rows, lanes), f32)], axis=0)
    part_even_in, small_all, chips_even_out = weight_grad(
        dz_e, hn_e, "dw_even_in", jobs=[_all_gather_job(small_buf)] + chips(pair_even_out))
    small_sum = _sum_leading(small_all, "small_grad_sum")
    chips_even_in = [_exchange_in_vmem(part_even_in, "reduce_scatter_even_in")]
    grads = {}
    for k, (n, g) in enumerate(small):
        grads[n] = small_sum[small_offs[k]:small_offs[k + 1]].reshape(-1)[:g.size].reshape(g.shape)
    me = 4 * lax.axis_index("x") + 2 * lax.axis_index("y") + lax.axis_index("c")
    shard = c_b_dw.shape[-1]
    for n in conv_names:
        grads[n] = lax.dynamic_slice_in_dim(grads[n], me * shard, shard, axis=1)

    col_sharded = ("even_w_in", "odd_w_in", "ffn_w_gate", "ffn_w_up")

    def rows_view(n, a):
        return jnp.swapaxes(a, -1, -2) if n in col_sharded else a

    loss = grads.pop("loss")[0, 0]
    chip_parts = {"even_w_in": chips_even_in, "even_w_out": [chips_even_out],
                  "odd_w_in": chips_odd[:1], "odd_w_out": chips_odd[1:]}
    for k, n in enumerate(["ffn_w_gate", "ffn_w_up", "ffn_w_down"]):
        chip_parts[n] = [chips_ffn0[k], chips_ffn1[k]]

    delta, new_m, new_v = {}, {}, {}
    for n in chip_parts:
        w_rows, m_rows, v_rows = [rows_view(n, a) for a in (weights[n], m_in[n], v_in[n])]
        outs = _adamw_reduced(w_rows, chip_parts[n], m_rows, v_rows, "adamw_" + n)
        grads[n], delta[n], new_m[n], new_v[n] = [rows_view(n, o) for o in outs]
    others = [n for n in names if n not in chip_parts]
    view = {n: (-1, weights[n].shape[-1]) for n in others}
    stepped = _adamw_small([tuple(a.reshape(view[n]) for a in (weights[n], grads[n], m_in[n], v_in[n])) for n in others],
                           "adamw_small")
    for n, outs in zip(others, stepped):
        grads[n] = grads[n].reshape(weights[n].shape)
        delta[n], new_m[n], new_v[n] = [o.reshape(weights[n].shape) for o in outs]

    return (loss, dh0[None], *[grads[n] for n in names], *[delta[n] for n in names],
            *[new_m[n] for n in names], *[new_v[n] for n in names])
```

```python
import jax
import jax.numpy as jnp
from jax import lax
from jax.experimental import pallas as pl
from jax.experimental.pallas import tpu as pltpu

f32 = jnp.float32
bf16 = jnp.bfloat16

EPS = 1e-6
D_MODEL = 1024
A_WIDTH = 512
HEAD = 128
N_HEADS = 4
CHUNK = 64
POOL_WINDOWS = (2, 4, 8, 16)
POOL_HALO = 16
C_KERNEL = 31
D_KERNEL = 3
CONV_HALO = 32
D_FF = 2816
N_DEV = 8
N_CHIP = 4

ADAM_LR = 0.001
ADAM_B1 = 0.9
ADAM_B2 = 0.999
ADAM_EPS = 1e-08
ADAM_WD = 0.01
ADAM_STEP = 10

MIX_TILE = 512
ODD_FWD_TILE = 1024
FFN_TILE = 256
FFN_CHUNKS = (1536, 1280)
DW_TK = 2048
DW_TM = 1536
MIDDLE_AT, MIDDLE_OF = 7, 8
VMEM_LIMIT = 56 * 1024 * 1024

MESH = pl.DeviceIdType.MESH
ANY = pl.BlockSpec(memory_space=pl.ANY)


def _params(n_axes):
    return pltpu.CompilerParams(dimension_semantics=("arbitrary",) * n_axes, vmem_limit_bytes=VMEM_LIMIT)


def _mm(a, b):
    return jnp.dot(a, b, preferred_element_type=f32)


def _mm_nt(a, b):
    return lax.dot_general(a, b, (((1,), (1,)), ((), ())), preferred_element_type=f32)


def _mm_tn(a, b):
    return lax.dot_general(a, b, (((0,), (0,)), ((), ())), preferred_element_type=f32)


def _sigmoid(x):
    return 1.0 / (1.0 + jnp.exp(-x))


def _rms_r(h):
    return lax.rsqrt(jnp.mean(h * h, axis=-1, keepdims=True) + EPS)


def _rms_bwd(dy, h, g):
    r = _rms_r(h)
    xh = h * r
    dxh = dy * g
    dh = r * (dxh - xh * jnp.mean(dxh * xh, axis=-1, keepdims=True))
    return dh, jnp.sum(dy * xh, axis=0, keepdims=True)


def _ln_fwd(x, g, b):
    mu = jnp.mean(x, axis=-1, keepdims=True)
    xc = x - mu
    r = lax.rsqrt(jnp.mean(xc * xc, axis=-1, keepdims=True) + EPS)
    xh = xc * r
    return xh * g + b, xh, r


def _ln_bwd(dy, xh, r, g):
    dxh = dy * g
    return r * (dxh - jnp.mean(dxh, axis=-1, keepdims=True) - xh * jnp.mean(dxh * xh, axis=-1, keepdims=True))


_GELU_C = 0.7978845608028654
_GELU_A = 0.044715


def _gelu(x):
    th = jnp.tanh(x * (_GELU_C + (_GELU_C * _GELU_A) * (x * x)))
    half = 0.5 * x
    return half + half * th, th


def _gelu_grad(x, th):
    return 0.5 + 0.5 * th + (1.0 - th * th) * (x * (0.5 * _GELU_C + (1.5 * _GELU_C * _GELU_A) * (x * x)))


def _down(x, k):
    return x if k == 0 else pltpu.roll(x, k, 0)


def _up(x, k):
    return x if k == 0 else pltpu.roll(x, x.shape[0] - k, 0)


def _window_sum(x, win, shift):
    s = x
    step = 1
    while step < win:
        s = s + shift(s, step)
        step *= 2
    return s


def _inv_count(t0, rows, win):
    t = t0 + lax.broadcasted_iota(jnp.int32, (rows, 1), 0)
    return 1.0 / jnp.minimum(t + 1, win).astype(f32)


def _chunk_mask():
    i = lax.broadcasted_iota(jnp.int32, (HEAD, HEAD), 0)
    j = lax.broadcasted_iota(jnp.int32, (HEAD, HEAD), 1)
    return jnp.logical_or(i >= CHUNK, j < CHUNK)


def _const(shape, n_axes):
    zeros = (0,) * len(shape)
    if n_axes == 1:
        return pl.BlockSpec(shape, lambda i: zeros)
    return pl.BlockSpec(shape, lambda i, j: zeros)


def _prev_halo(tile, halo, cols):
    return pl.BlockSpec((halo, cols), lambda i: (jnp.maximum(i * (tile // halo) - 1, 0), 0))


def _next_halo(tile, halo, cols, seq):
    return pl.BlockSpec((halo, cols), lambda i: (jnp.minimum((i + 1) * (tile // halo), seq // halo - 1), 0))


class _Job:
    def __init__(self, inputs, out_shape, sems, hooks):
        self.inputs, self.out_shape, self.sems, self.hooks = inputs, out_shape, sems, hooks


def _position():
    return lax.axis_index("x"), lax.axis_index("y"), lax.axis_index("c")


def _all_gather_job(block):
    rows, cols = block.shape

    def hooks(ins, outs, sems):
        (x_ref,), (out_ref,), (send_sems, recv_sems, local_sem) = ins, outs, sems
        x, y, c = _position()
        me, sibling = (x, y, c), (x, y, 1 - c)
        chips = [(1 - x, y), (x, 1 - y), (1 - x, 1 - y)]

        def slot(px, py, pc):
            return out_ref.at[4 * px + 2 * py + pc]

        def copy(k, block_of, to, src=None):
            return pltpu.make_async_remote_copy(
                src_ref=slot(*block_of) if src is None else src, dst_ref=slot(*block_of),
                send_sem=send_sems.at[k], recv_sem=recv_sems.at[k], device_id=to, device_id_type=MESH)

        mine = pltpu.make_async_copy(x_ref, slot(*me), local_sem)
        first = [copy(0, me, sibling, src=x_ref)]
        first += [copy(1 + j, me, (*chip, c), src=x_ref) for j, chip in enumerate(chips)]
        passed = [copy(4 + j, (*chip, c), sibling) for j, chip in enumerate(chips)]

        def start():
            mine.start()
            for cp in first:
                cp.start()

        def middle():
            for j, chip in enumerate(chips):
                copy(1 + j, (*chip, c), me).wait_recv()
                passed[j].start()

        def finish():
            copy(0, sibling, me).wait_recv()
            for j, chip in enumerate(chips):
                copy(4 + j, (*chip, 1 - c), me).wait_recv()
            for cp in first + passed:
                cp.wait_send()
            mine.wait()

        return start, middle, finish

    return _Job([block], [jax.ShapeDtypeStruct((N_DEV, rows, cols), block.dtype)],
                [pltpu.SemaphoreType.DMA((7,)), pltpu.SemaphoreType.DMA((7,)), pltpu.SemaphoreType.DMA], hooks)


def _sibling_exchange_job(g):
    _, rows, cols = g.shape

    def hooks(ins, outs, sems):
        (g_ref,), (recv_ref,), (send_sems, recv_sems) = ins, outs, sems
        x, y, c = _position()
        copies = [pltpu.make_async_remote_copy(
            src_ref=g_ref.at[2 * k + (1 - c)], dst_ref=recv_ref.at[k], send_sem=send_sems.at[k],
            recv_sem=recv_sems.at[k], device_id=(x, y, 1 - c), device_id_type=MESH) for k in range(N_CHIP)]

        def start():
            for cp in copies:
                cp.start()

        def finish():
            for cp in copies:
                cp.wait()

        return start, lambda: None, finish

    return _Job([g], [jax.ShapeDtypeStruct((N_CHIP, rows, cols), g.dtype)],
                [pltpu.SemaphoreType.DMA((N_CHIP,)), pltpu.SemaphoreType.DMA((N_CHIP,))], hooks)


def _chip_exchange_job(p):
    _, rows, cols = p.shape

    def hooks(ins, outs, sems):
        (p_ref,), (recv_ref,), (send_sems, recv_sems, local_sem) = ins, outs, sems
        x, y, c = _position()
        k_me = 2 * x + y
        mine = pltpu.make_async_copy(p_ref.at[k_me], recv_ref.at[k_me], local_sem)
        copies = [pltpu.make_async_remote_copy(
            src_ref=p_ref.at[2 * px + py], dst_ref=recv_ref.at[k_me], send_sem=send_sems.at[j],
            recv_sem=recv_sems.at[j], device_id=(px, py, c), device_id_type=MESH)
            for j, (px, py) in enumerate([(1 - x, y), (x, 1 - y), (1 - x, 1 - y)])]

        def start():
            mine.start()
            for cp in copies:
                cp.start()

        def finish():
            for cp in copies:
                cp.wait()
            mine.wait()

        return start, lambda: None, finish

    return _Job([p], [jax.ShapeDtypeStruct((N_CHIP, rows, cols), p.dtype)],
                [pltpu.SemaphoreType.DMA((3,)), pltpu.SemaphoreType.DMA((3,)), pltpu.SemaphoreType.DMA], hooks)


def _pair_chip_exchange_job(g, recv):
    _, rows, cols = g.shape
    buf = pltpu.VMEM((N_CHIP, rows, cols), g.dtype)

    def hooks(ins, outs, scratch):
        (g_ref, r_ref), (out_ref,), (own, got, pair, send_sems, recv_sems, local_sem) = ins, outs, scratch
        x, y, c = _position()
        k_me = 2 * x + y
        mine = pltpu.make_async_copy(pair.at[k_me], out_ref.at[k_me], local_sem)
        copies = [pltpu.make_async_remote_copy(
            src_ref=pair.at[2 * px + py], dst_ref=out_ref.at[k_me], send_sem=send_sems.at[j],
            recv_sem=recv_sems.at[j], device_id=(px, py, c), device_id_type=MESH)
            for j, (px, py) in enumerate([(1 - x, y), (x, 1 - y), (1 - x, 1 - y)])]

        def start():
            for k in range(N_CHIP):
                pltpu.sync_copy(g_ref.at[2 * k + c], own.at[k])
                pltpu.sync_copy(r_ref.at[k], got.at[k])
            pair[...] = (own[...].astype(f32) + got[...].astype(f32)).astype(pair.dtype)
            mine.start()
            for cp in copies:
                cp.start()

        def finish():
            for cp in copies:
                cp.wait()
            mine.wait()

        return start, lambda: None, finish

    return _Job([g, recv], [jax.ShapeDtypeStruct((N_CHIP, rows, cols), g.dtype)],
                [buf, buf, buf, pltpu.SemaphoreType.DMA((3,)), pltpu.SemaphoreType.DMA((3,)),
                 pltpu.SemaphoreType.DMA], hooks)


def _job_hooks(jobs, ins, outs, sems):
    hooks = []
    for job in jobs:
        n_in, n_out, n_sem = len(job.inputs), len(job.out_shape), len(job.sems)
        hooks.append(job.hooks(ins[:n_in], outs[:n_out], sems[:n_sem]))
        ins, outs, sems = ins[n_in:], outs[n_out:], sems[n_sem:]
    return hooks


def _run_jobs(jobs, name):
    n_in = sum(len(job.inputs) for job in jobs)
    n_out = sum(len(job.out_shape) for job in jobs)

    def body(*refs):
        hooks = _job_hooks(jobs, refs[:n_in], refs[n_in:n_in + n_out], refs[n_in + n_out:])
        for phase in range(3):
            for h in hooks:
                h[phase]()

    return list(pl.pallas_call(
        body, name=name, in_specs=[ANY] * n_in, out_specs=[ANY] * n_out,
        out_shape=[s for job in jobs for s in job.out_shape],
        scratch_shapes=[s for job in jobs for s in job.sems],
    )(*[a for job in jobs for a in job.inputs]))


def _exchange_in_vmem(g, name):
    _, rows, cols = g.shape
    vmem = pl.BlockSpec(memory_space=pltpu.VMEM)

    def body(g_ref, out_ref, recv_ref, pair_ref, sib_send, sib_recv, chip_send, chip_recv):
        x, y, c = _position()
        to_sibling = [pltpu.make_async_remote_copy(
            src_ref=g_ref.at[2 * k + (1 - c)], dst_ref=recv_ref.at[k], send_sem=sib_send.at[k],
            recv_sem=sib_recv.at[k], device_id=(x, y, 1 - c), device_id_type=MESH) for k in range(N_CHIP)]
        for cp in to_sibling:
            cp.start()
        for cp in to_sibling:
            cp.wait()
        for k in range(N_CHIP):
            pair_ref[k] = (g_ref[2 * k + c].astype(f32) + recv_ref[k].astype(f32)).astype(pair_ref.dtype)
        k_me = 2 * x + y
        to_chips = [pltpu.make_async_remote_copy(
            src_ref=pair_ref.at[2 * px + py], dst_ref=out_ref.at[k_me], send_sem=chip_send.at[j],
            recv_sem=chip_recv.at[j], device_id=(px, py, c), device_id_type=MESH)
            for j, (px, py) in enumerate([(1 - x, y), (x, 1 - y), (1 - x, 1 - y)])]
        for cp in to_chips:
            cp.start()
        out_ref[k_me] = pair_ref[k_me]
        for cp in to_chips:
            cp.wait()

    return pl.pallas_call(
        body, name=name, in_specs=[vmem], out_specs=vmem,
        out_shape=jax.ShapeDtypeStruct((N_CHIP, rows, cols), g.dtype),
        scratch_shapes=[pltpu.VMEM((N_CHIP, rows, cols), g.dtype), pltpu.VMEM((N_CHIP, rows, cols), g.dtype),
                        pltpu.SemaphoreType.DMA((N_CHIP,)), pltpu.SemaphoreType.DMA((N_CHIP,)),
                        pltpu.SemaphoreType.DMA((3,)), pltpu.SemaphoreType.DMA((3,))],
    )(g)


def _launch(body, *, name, grid, in_specs, out_specs, out_shape, args, scratch=(), jobs=()):
    in_specs, out_specs, out_shape, scratch = list(in_specs), list(out_specs), list(out_shape), list(scratch)
    if not jobs:
        return list(pl.pallas_call(body, name=name, grid=grid, in_specs=in_specs, out_specs=out_specs,
                                   out_shape=out_shape, scratch_shapes=scratch,
                                   compiler_params=_params(len(grid)))(*args))
    n_in, n_out, n_sc = len(in_specs), len(out_specs), len(scratch)
    j_in = [a for job in jobs for a in job.inputs]
    j_out = [s for job in jobs for s in job.out_shape]
    j_sems = [s for job in jobs for s in job.sems]
    n_steps = 1
    for g in grid:
        n_steps *= g

    def wrapped(*refs):
        ins, refs = refs[:n_in], refs[n_in:]
        jins, refs = refs[:len(j_in)], refs[len(j_in):]
        outs, refs = refs[:n_out], refs[n_out:]
        jouts, refs = refs[:len(j_out)], refs[len(j_out):]
        sc, jsems = refs[:n_sc], refs[n_sc:]
        step = pl.program_id(0)
        for axis in range(1, len(grid)):
            step = step * grid[axis] + pl.program_id(axis)
        hooks = _job_hooks(jobs, jins, jouts, jsems)

        @pl.when(step == 0)
        def _():
            for h in hooks:
                h[0]()

        body(*ins, *outs, *sc)

        @pl.when(step == (MIDDLE_AT * n_steps) // MIDDLE_OF)
        def _():
            for h in hooks:
                h[1]()

        @pl.when(step == n_steps - 1)
        def _():
            for h in hooks:
                h[2]()

    return list(pl.pallas_call(
        wrapped, name=name, grid=grid, in_specs=in_specs + [ANY] * len(j_in), out_specs=out_specs + [ANY] * len(j_out),
        out_shape=out_shape + j_out, scratch_shapes=scratch + j_sems, compiler_params=_params(len(grid)),
    )(*args, *j_in))


def _gmlp_gate(vnb, wsm, bst, tile):
    rows = []
    for n in range(tile // HEAD):
        cols = []
        for hh in range(N_HEADS):
            blk = vnb[n * HEAD:(n + 1) * HEAD, hh * HEAD:(hh + 1) * HEAD]
            cols.append(_mm(wsm[hh], blk) + bst[:, hh:hh + 1])
        rows.append(jnp.concatenate(cols, axis=1))
    return jnp.concatenate(rows, axis=0)


def _even_fwd(h, w_in, w_out, ws, bst, lng, lnb, wp, sc, gm, jobs=()):
    seq = h.shape[0]
    tile = min(MIX_TILE, seq)
    n_tiles = seq // tile

    def body(h_ref, hp_ref, win_ref, wout_ref, ws_ref, bst_ref, lng_ref, lnb_ref, wp_ref, sc_ref, g_ref,
             ho_ref, hn_ref, za_ref, pool_ref, mix_ref):
        i = pl.program_id(0)
        g = g_ref[...]
        h = h_ref[...]
        hnb = (h * _rms_r(h) * g).astype(bf16)
        hn_ref[...] = hnb
        z = _mm_nt(hnb, win_ref[...])
        zab = z[:, :2 * A_WIDTH].astype(bf16)
        za_ref[...] = zab
        hp = hp_ref[...]
        zbp = _mm_nt((hp * _rms_r(hp) * g).astype(bf16), win_ref[2 * A_WIDTH:, :])
        zbe = jnp.concatenate([jnp.where(i > 0, zbp, 0.0), z[:, 2 * A_WIDTH:]], axis=0)
        pooled = []
        for gi, win in enumerate(POOL_WINDOWS):
            xg = zbe[:, gi * HEAD:(gi + 1) * HEAD]
            s = _window_sum(xg, win, _down)
            pooled.append(s[POOL_HALO:] * _inv_count(i * tile, tile, win) - xg[POOL_HALO:])
        plb = jnp.concatenate(pooled, axis=1).astype(bf16)
        pool_ref[...] = plb

        ga, _ = _gelu(zab.astype(f32))
        vn, _, _ = _ln_fwd(ga[:, A_WIDTH:], lng_ref[...], lnb_ref[...])
        mask = _chunk_mask()
        wsm = [jnp.where(mask, ws_ref[hh], 0.0).astype(bf16) for hh in range(N_HEADS)]
        ya = ga[:, :A_WIDTH] * _gmlp_gate(vn.astype(bf16), wsm, bst_ref[...], tile)
        yb = jnp.concatenate([_mm(plb[:, gi * HEAD:(gi + 1) * HEAD], wp_ref[gi].astype(bf16))
                              for gi in range(len(POOL_WINDOWS))], axis=1) * sc_ref[...]
        mix = jnp.concatenate([ya, yb], axis=1).astype(bf16)
        mix_ref[...] = mix
        ho_ref[...] = h + _mm(mix, wout_ref[...])

    row = lambda cols: pl.BlockSpec((tile, cols), lambda i: (i, 0))
    return _launch(
        body, name="even_fwd", grid=(n_tiles,), jobs=jobs,
        in_specs=[row(D_MODEL), _prev_halo(tile, POOL_HALO, D_MODEL), _const(w_in.shape, 1), _const(w_out.shape, 1),
                  _const(ws.shape, 1), _const(bst.shape, 1), _const(lng.shape, 1), _const(lnb.shape, 1),
                  _const(wp.shape, 1), _const(sc.shape, 1), _const(gm.shape, 1)],
        out_specs=[row(D_MODEL), row(D_MODEL), row(2 * A_WIDTH), row(A_WIDTH), row(D_MODEL)],
        out_shape=[jax.ShapeDtypeStruct((seq, D_MODEL), f32), jax.ShapeDtypeStruct((seq, D_MODEL), bf16),
                   jax.ShapeDtypeStruct((seq, 2 * A_WIDTH), bf16), jax.ShapeDtypeStruct((seq, A_WIDTH), bf16),
                   jax.ShapeDtypeStruct((seq, D_MODEL), bf16)],
        args=(h, h, w_in, w_out, ws, bst, lng, lnb, wp, sc, gm))


def _even_bwd(dh, h, za, pooled, w_in, w_out, ws, bst, lng, lnb, wp, sc, gm, jobs=()):
    seq = h.shape[0]
    tile = min(MIX_TILE, seq)
    n_tiles = seq // tile
    n_groups = len(POOL_WINDOWS)

    def body(dh_ref, dhx_ref, h_ref, za_ref, pool_ref, win_ref, wout_ref, ws_ref, bst_ref, lng_ref, lnb_ref,
             wp_ref, sc_ref, g_ref,
             dhi_ref, dz_ref, dws_ref, dbs_ref, dlng_ref, dlnb_ref, dwp_ref, dsc_ref, dg_ref):
        i = pl.program_id(0)

        @pl.when(i == 0)
        def _():
            for ref in (dws_ref, dbs_ref, dlng_ref, dlnb_ref, dwp_ref, dsc_ref, dg_ref):
                ref[...] = jnp.zeros_like(ref)

        dh = dh_ref[...]
        dmix = _mm_nt(dh.astype(bf16), wout_ref[...])
        dya = dmix[:, :A_WIDTH]
        dyb = dmix[:, A_WIDTH:]
        dybx = _mm_nt(dhx_ref[...].astype(bf16), wout_ref[A_WIDTH:, :])
        dybx = jnp.where(i < n_tiles - 1, dybx, 0.0)

        za = za_ref[...].astype(f32)
        ga, th = _gelu(za)
        u = ga[:, :A_WIDTH]
        lng = lng_ref[...]
        vn, vh, r = _ln_fwd(ga[:, A_WIDTH:], lng, lnb_ref[...])
        vnb = vn.astype(bf16)
        mask = _chunk_mask()
        wsf = [jnp.where(mask, ws_ref[hh], 0.0) for hh in range(N_HEADS)]
        sv = _gmlp_gate(vnb, [w.astype(bf16) for w in wsf], bst_ref[...], tile)
        du = dya * sv
        dsvb = (dya * u).astype(bf16)
        wst = [w.T.astype(bf16) for w in wsf]
        ones = jnp.ones((8, HEAD), bf16)
        dws = [jnp.zeros((HEAD, HEAD), f32) for _ in range(N_HEADS)]
        dbs = [jnp.zeros((8, HEAD), f32) for _ in range(N_HEADS)]
        rows = []
        for n in range(tile // HEAD):
            cols = []
            for hh in range(N_HEADS):
                blk = dsvb[n * HEAD:(n + 1) * HEAD, hh * HEAD:(hh + 1) * HEAD]
                cols.append(_mm(wst[hh], blk))
                dws[hh] = dws[hh] + _mm_nt(blk, vnb[n * HEAD:(n + 1) * HEAD, hh * HEAD:(hh + 1) * HEAD])
                dbs[hh] = dbs[hh] + _mm_nt(ones, blk)
            rows.append(jnp.concatenate(cols, axis=1))
        dvn = jnp.concatenate(rows, axis=0)
        for hh in range(N_HEADS):
            dws_ref[hh] += jnp.where(mask, dws[hh], 0.0)
            dbs_ref[pl.ds(hh, 1), :] += dbs[hh][0:1, :]
        dlng_ref[...] += jnp.sum(dvn * vh, axis=0, keepdims=True)
        dlnb_ref[...] += jnp.sum(dvn, axis=0, keepdims=True)
        dv = _ln_bwd(dvn, vh, r, lng)
        dza = jnp.concatenate([du, dv], axis=1) * _gelu_grad(za, th)

        plb = pool_ref[...]
        sc = sc_ref[...]
        dzb = []
        dsc = []
        for gi, win in enumerate(POOL_WINDOWS):
            cs = slice(gi * HEAD, (gi + 1) * HEAD)
            wpb = wp_ref[gi].astype(bf16)
            dsc.append(jnp.sum(dyb[:, cs] * _mm(plb[:, cs], wpb), axis=0, keepdims=True))
            dpre = (dyb[:, cs] * sc[:, cs]).astype(bf16)
            dprex = (dybx[:, cs] * sc[:, cs]).astype(bf16)
            dwp_ref[gi] += _mm_tn(plb[:, cs], dpre)
            dpl = _mm_nt(dpre, wpb)
            dple = jnp.concatenate([dpl, _mm_nt(dprex, wpb)], axis=0)
            q = dple * _inv_count(i * tile, tile + POOL_HALO, win)
            dzb.append(_window_sum(q, win, _up)[:tile] - dpl)
        dsc_ref[...] += jnp.concatenate(dsc, axis=1)

        dzf = jnp.concatenate([dza] + dzb, axis=1).astype(bf16)
        dz_ref[...] = dzf
        dhn = _mm(dzf, win_ref[...])
        dhr, dg = _rms_bwd(dhn, h_ref[...], g_ref[...])
        dhi_ref[...] = dh + dhr
        dg_ref[...] += dg

    row = lambda cols: pl.BlockSpec((tile, cols), lambda i: (i, 0))
    small = [ws.shape, (N_HEADS, HEAD), lng.shape, lnb.shape, wp.shape, sc.shape, gm.shape]
    return _launch(
        body, name="even_bwd", grid=(n_tiles,), jobs=jobs,
        in_specs=[row(D_MODEL), _next_halo(tile, POOL_HALO, D_MODEL, seq), row(D_MODEL), row(2 * A_WIDTH), row(A_WIDTH),
                  _const(w_in.shape, 1), _const(w_out.shape, 1), _const(ws.shape, 1), _const(bst.shape, 1),
                  _const(lng.shape, 1), _const(lnb.shape, 1), _const(wp.shape, 1), _const(sc.shape, 1), _const(gm.shape, 1)],
        out_specs=[row(D_MODEL), row(3 * A_WIDTH)] + [_const(s, 1) for s in small],
        out_shape=[jax.ShapeDtypeStruct((seq, D_MODEL), f32), jax.ShapeDtypeStruct((seq, 3 * A_WIDTH), bf16)]
                  + [jax.ShapeDtypeStruct(s, f32) for s in small],
        args=(dh, dh, h, za, pooled, w_in, w_out, ws, bst, lng, lnb, wp, sc, gm))


SUBLANES = 8


class _Shifted:
    def __init__(self, x, shift, max_shift):
        self.rolled = [shift(x, b) for b in range(min(SUBLANES, max_shift + 1))]
        self.back = shift is _down

    def rows(self, k, start, count):
        whole = k - k % SUBLANES
        lo = start - whole if self.back else start + whole
        return self.rolled[k % SUBLANES][lo:lo + count]


def _conv_taps(xs, w_ref, n_taps, halo, rows):
    acc = None
    for j in range(n_taps):
        term = w_ref[pl.ds(j, 1), :] * xs.rows(n_taps - 1 - j, halo, rows)
        acc = term if acc is None else acc + term
    return acc


def _odd_fwd(h, w_in, w_out, cw, cb, clg, clb, dw, gm):
    seq = h.shape[0]
    tile = min(ODD_FWD_TILE, seq)
    n_tiles = seq // tile
    w = A_WIDTH

    def body(h_ref, hp_ref, win_ref, wout_ref, cw_ref, cb_ref, clg_ref, clb_ref, dw_ref, g_ref,
             ho_ref, hn_ref, z_ref, mix_ref, cv_ref):
        i = pl.program_id(0)
        g = g_ref[...]
        h = h_ref[...]
        hnb = (h * _rms_r(h) * g).astype(bf16)
        hn_ref[...] = hnb
        zb = _mm_nt(hnb, win_ref[...]).astype(bf16)
        z_ref[...] = zb
        hp = hp_ref[...]
        zp = _mm_nt((hp * _rms_r(hp) * g).astype(bf16), win_ref[...]).astype(bf16).astype(f32)
        z = zb.astype(f32)
        ze = jnp.concatenate([jnp.where(i > 0, zp, 0.0), z], axis=0)
        hc = ze[:, :w] * _sigmoid(ze[:, w:2 * w])
        cv = _conv_taps(_Shifted(hc, _down, C_KERNEL - 1), cw_ref, C_KERNEL, CONV_HALO, tile) + cb_ref[...]
        cv_ref[...] = cv
        ln, _, _ = _ln_fwd(cv, clg_ref[...], clb_ref[...])
        yc = ln * _sigmoid(ln)
        p = ze[:, 3 * w:4 * w] * ze[:, 4 * w:]
        yd = z[:, 2 * w:3 * w] * _conv_taps(_Shifted(p, _down, D_KERNEL - 1), dw_ref, D_KERNEL, CONV_HALO, tile)
        mix = jnp.concatenate([yc, yd], axis=1).astype(bf16)
        mix_ref[...] = mix
        ho_ref[...] = h + _mm(mix, wout_ref[...])

    row = lambda cols: pl.BlockSpec((tile, cols), lambda i: (i, 0))
    return pl.pallas_call(
        body, name="odd_fwd", grid=(n_tiles,),
        in_specs=[row(D_MODEL), _prev_halo(tile, CONV_HALO, D_MODEL), _const(w_in.shape, 1), _const(w_out.shape, 1),
                  _const(cw.shape, 1), _const(cb.shape, 1), _const(clg.shape, 1), _const(clb.shape, 1),
                  _const(dw.shape, 1), _const(gm.shape, 1)],
        out_specs=[row(D_MODEL), row(D_MODEL), row(5 * w), row(D_MODEL), row(w)],
        out_shape=[jax.ShapeDtypeStruct((seq, D_MODEL), f32), jax.ShapeDtypeStruct((seq, D_MODEL), bf16),
                   jax.ShapeDtypeStruct((seq, 5 * w), bf16), jax.ShapeDtypeStruct((seq, D_MODEL), bf16),
                   jax.ShapeDtypeStruct((seq, w), f32)],
        compiler_params=_params(1),
    )(h, h, w_in, w_out, cw, cb, clg, clb, dw, gm)


def _odd_bwd(dh, h, z, cv, w_in, w_out, cw, clg, clb, dw, gm, jobs=()):
    seq = h.shape[0]
    tile = min(MIX_TILE, seq)
    n_tiles = seq // tile
    w = A_WIDTH
    halo = CONV_HALO

    def body(dh_ref, dhx_ref, h_ref, z_ref, zp_ref, zx_ref, cv_ref, cvx_ref, win_ref, wout_ref, cw_ref,
             clg_ref, clb_ref, dw_ref, g_ref,
             dhi_ref, dz_ref, dcw_ref, dcb_ref, dclg_ref, dclb_ref, ddw_ref, dg_ref):
        i = pl.program_id(0)

        @pl.when(i == 0)
        def _():
            for ref in (dcw_ref, dcb_ref, dclg_ref, dclb_ref, ddw_ref, dg_ref):
                ref[...] = jnp.zeros_like(ref)

        dh = dh_ref[...]
        dhe = jnp.concatenate([dh, jnp.where(i < n_tiles - 1, dhx_ref[...], 0.0)], axis=0)
        dmix = _mm_nt(dhe.astype(bf16), wout_ref[...])
        ze = jnp.concatenate([jnp.where(i > 0, zp_ref[...].astype(f32), 0.0), z_ref[...].astype(f32),
                              zx_ref[...].astype(f32)], axis=0)

        sg = _sigmoid(ze[:, w:2 * w])
        ca = ze[:, :w]
        hc = ca * sg
        hcs = _Shifted(hc, _down, C_KERNEL - 1)
        cv = jnp.concatenate([cv_ref[...], cvx_ref[...]], axis=0)
        clg = clg_ref[...]
        ln, xh, r = _ln_fwd(cv, clg, clb_ref[...])
        sl = _sigmoid(ln)
        dln = dmix[:, :w] * (sl * (1.0 + ln * (1.0 - sl)))
        dclg_ref[...] += jnp.sum((dln * xh)[:tile], axis=0, keepdims=True)
        dclb_ref[...] += jnp.sum(dln[:tile], axis=0, keepdims=True)
        dcv = _ln_bwd(dln, xh, r, clg)
        dcb_ref[...] += jnp.sum(dcv[:tile], axis=0, keepdims=True)
        dcvs = _Shifted(dcv, _up, C_KERNEL - 1)
        dhc = None
        for j in range(C_KERNEL):
            k = C_KERNEL - 1 - j
            dcw_ref[pl.ds(j, 1), :] += jnp.sum(dcv[:tile] * hcs.rows(k, halo, tile), axis=0, keepdims=True)
            term = cw_ref[pl.ds(j, 1), :] * dcvs.rows(k, 0, tile)
            dhc = term if dhc is None else dhc + term
        sgt = sg[halo:halo + tile]
        cat = ca[halo:halo + tile]
        dca = dhc * sgt
        dcg = dhc * cat * sgt * (1.0 - sgt)

        dcgv = ze[:, 3 * w:4 * w]
        dxin = ze[:, 4 * w:]
        p = dcgv * dxin
        ps = _Shifted(p, _down, D_KERNEL - 1)
        q = _conv_taps(ps, dw_ref, D_KERNEL, halo, tile)
        dyd = dmix[:, w:]
        dq = dyd * ze[halo:, 2 * w:3 * w]
        ddbg = dyd[:tile] * q
        dqs = _Shifted(dq, _up, D_KERNEL - 1)
        dp = None
        for j in range(D_KERNEL):
            k = D_KERNEL - 1 - j
            ddw_ref[pl.ds(j, 1), :] += jnp.sum(dq[:tile] * ps.rows(k, halo, tile), axis=0, keepdims=True)
            term = dw_ref[pl.ds(j, 1), :] * dqs.rows(k, 0, tile)
            dp = term if dp is None else dp + term
        ddcg = dp * dxin[halo:halo + tile]
        ddxin = dp * dcgv[halo:halo + tile]

        dzf = jnp.concatenate([dca, dcg, ddbg, ddcg, ddxin], axis=1).astype(bf16)
        dz_ref[...] = dzf
        dhn = _mm(dzf, win_ref[...])
        dhr, dg = _rms_bwd(dhn, h_ref[...], g_ref[...])
        dhi_ref[...] = dh + dhr
        dg_ref[...] += dg

    row = lambda cols: pl.BlockSpec((tile, cols), lambda i: (i, 0))
    small = [cw.shape, clg.shape, clg.shape, clb.shape, dw.shape, gm.shape]
    return _launch(
        body, name="odd_bwd", grid=(n_tiles,), jobs=jobs,
        in_specs=[row(D_MODEL), _next_halo(tile, halo, D_MODEL, seq), row(D_MODEL), row(5 * w),
                  _prev_halo(tile, halo, 5 * w), _next_halo(tile, halo, 5 * w, seq),
                  row(w), _next_halo(tile, halo, w, seq),
                  _const(w_in.shape, 1), _const(w_out.shape, 1), _const(cw.shape, 1),
                  _const(clg.shape, 1), _const(clb.shape, 1), _const(dw.shape, 1), _const(gm.shape, 1)],
        out_specs=[row(D_MODEL), row(5 * w)] + [_const(s, 1) for s in small],
        out_shape=[jax.ShapeDtypeStruct((seq, D_MODEL), f32), jax.ShapeDtypeStruct((seq, 5 * w), bf16)]
                  + [jax.ShapeDtypeStruct(s, f32) for s in small],
        args=(dh, dh, h, z, z, z, cv, cv, w_in, w_out, cw, clg, clb, dw, gm))


def _ffn_chunks():
    assert sum(FFN_CHUNKS) == D_FF
    start = 0
    for size in FFN_CHUNKS:
        yield slice(start, start + size)
        start += size


def _ffn_fwd(h, wg, wu, wd, gm, jobs=(), head=None):
    seq = h.shape[0]
    tile = min(FFN_TILE, seq)

    def body(h_ref, g_ref, wg_ref, wu_ref, wd_ref, *refs):
        if head is None:
            ho_ref, hn_ref, gate_ref, up_ref = refs
        else:
            t_ref, gf_ref, ho_ref, hn_ref, gate_ref, up_ref, loss_ref, dgf_ref = refs
        h = h_ref[...]
        hnb = (h * _rms_r(h) * g_ref[...]).astype(bf16)
        hn_ref[...] = hnb
        acc = None
        for rows in _ffn_chunks():
            gb = _mm_nt(hnb, wg_ref[rows, :]).astype(bf16)
            ub = _mm_nt(hnb, wu_ref[rows, :]).astype(bf16)
            gate_ref[:, rows] = gb
            up_ref[:, rows] = ub
            gf = gb.astype(f32)
            act = gf * _sigmoid(gf) * ub.astype(f32)
            part = _mm(act.astype(bf16), wd_ref[rows, :])
            acc = part if acc is None else acc + part
        ho = h + acc
        if head is None:
            ho_ref[...] = ho
            return

        @pl.when(pl.program_id(0) == 0)
        def _():
            loss_ref[...] = jnp.zeros_like(loss_ref)
            dgf_ref[...] = jnp.zeros_like(dgf_ref)

        g_final = gf_ref[...]
        err = ho * _rms_r(ho) * g_final - t_ref[...]
        loss_ref[...] += (0.5 / D_MODEL) * jnp.sum(jnp.sum(err * err, axis=1, keepdims=True), axis=0, keepdims=True)
        dho, dg = _rms_bwd(err * (1.0 / D_MODEL), ho, g_final)
        ho_ref[...] = dho
        dgf_ref[...] += dg

    row = pl.BlockSpec((tile, D_MODEL), lambda i: (i, 0))
    wide = pl.BlockSpec((tile, D_FF), lambda i: (i, 0))
    in_specs = [row, _const(gm.shape, 1), _const(wg.shape, 1), _const(wu.shape, 1), _const(wd.shape, 1)]
    out_specs = [row, row, wide, wide]
    out_shape = [jax.ShapeDtypeStruct((seq, D_MODEL), f32), jax.ShapeDtypeStruct((seq, D_MODEL), bf16),
                 jax.ShapeDtypeStruct((seq, D_FF), bf16), jax.ShapeDtypeStruct((seq, D_FF), bf16)]
    args = (h, gm, wg, wu, wd)
    if head is not None:
        target, g_final = head
        in_specs += [row, _const(g_final.shape, 1)]
        out_specs += [_const((1, 1), 1), _const(g_final.shape, 1)]
        out_shape += [jax.ShapeDtypeStruct((1, 1), f32), jax.ShapeDtypeStruct(g_final.shape, f32)]
        args += (target, g_final)
    return _launch(
        body, name="ffn_fwd" if head is None else "ffn_fwd_loss", grid=(seq // tile,), jobs=jobs,
        in_specs=in_specs, out_specs=out_specs, out_shape=out_shape, args=args)


def _ffn_bwd(dh, h, gate, up, wg, wu, wd, gm, jobs=()):
    seq = h.shape[0]
    tile = min(FFN_TILE, seq)
    n_tiles = seq // tile

    def body(dh_ref, h_ref, g_ref, gate_ref, up_ref, wg_ref, wu_ref, wd_ref,
             dhi_ref, dgate_ref, dup_ref, act_ref, dg_ref):
        @pl.when(pl.program_id(0) == 0)
        def _():
            dg_ref[...] = jnp.zeros_like(dg_ref)

        dh = dh_ref[...]
        dhb = dh.astype(bf16)
        acc = None
        for rows in _ffn_chunks():
            dact = _mm_nt(dhb, wd_ref[rows, :])
            gf = gate_ref[:, rows].astype(f32)
            uf = up_ref[:, rows].astype(f32)
            s = _sigmoid(gf)
            silu = gf * s
            act_ref[:, rows] = (silu * uf).astype(bf16)
            dgb = (dact * uf * (s * (1.0 + gf * (1.0 - s)))).astype(bf16)
            dub = (dact * silu).astype(bf16)
            dgate_ref[:, rows] = dgb
            dup_ref[:, rows] = dub
            part = _mm(dgb, wg_ref[rows, :]) + _mm(dub, wu_ref[rows, :])
            acc = part if acc is None else acc + part
        dhr, dg = _rms_bwd(acc, h_ref[...], g_ref[...])
        dhi_ref[...] = dh + dhr
        dg_ref[...] += dg

    row = pl.BlockSpec((tile, D_MODEL), lambda i: (i, 0))
    wide = pl.BlockSpec((tile, D_FF), lambda i: (i, 0))
    return _launch(
        body, name="ffn_bwd", grid=(n_tiles,), jobs=jobs,
        in_specs=[row, row, _const(gm.shape, 1), wide, wide, _const(wg.shape, 1), _const(wu.shape, 1), _const(wd.shape, 1)],
        out_specs=[row, wide, wide, wide, _const(gm.shape, 1)],
        out_shape=[jax.ShapeDtypeStruct((seq, D_MODEL), f32), jax.ShapeDtypeStruct((seq, D_FF), bf16),
                   jax.ShapeDtypeStruct((seq, D_FF), bf16), jax.ShapeDtypeStruct((seq, D_FF), bf16),
                   jax.ShapeDtypeStruct(gm.shape, f32)],
        args=(dh, h, gm, gate, up, wg, wu, wd))


def _weight_grads(pairs, name, jobs=()):
    seq, m = pairs[0][0].shape
    tk = min(DW_TK, seq)
    tm = m if m <= DW_TM else m // 2
    n_k = seq // tk
    n_pairs = len(pairs)

    def body(*refs):
        x_refs = refs[0:2 * n_pairs:2]
        y_refs = refs[1:2 * n_pairs:2]
        o_refs = refs[2 * n_pairs:3 * n_pairs]
        acc_refs = refs[3 * n_pairs:]
        k = pl.program_id(1)
        @pl.when(k == 0)
        def _():
            for acc_ref in acc_refs:
                acc_ref[...] = jnp.zeros_like(acc_ref)

        for x_ref, y_ref, acc_ref in zip(x_refs, y_refs, acc_refs):
            acc_ref[...] += _mm_tn(x_ref[...].astype(bf16), y_ref[...].astype(bf16))

        @pl.when(k == n_k - 1)
        def _():
            for o_ref, acc_ref in zip(o_refs, acc_refs):
                o_ref[...] = acc_ref[...].astype(bf16)

    in_specs = []
    for _ in pairs:
        in_specs += [pl.BlockSpec((tk, tm), lambda j, k: (k, j)), pl.BlockSpec((tk, D_MODEL), lambda j, k: (k, 0))]
    return _launch(
        body, name=name, grid=(m // tm, n_k), jobs=jobs,
        in_specs=in_specs,
        out_specs=[pl.BlockSpec((tm, D_MODEL), lambda j, k: (j, 0))] * n_pairs,
        out_shape=[jax.ShapeDtypeStruct((m, D_MODEL), bf16)] * n_pairs,
        scratch=[pltpu.VMEM((tm, D_MODEL), f32)] * n_pairs,
        args=[a for pair in pairs for a in pair])


def _row_tile(rows, limit=512):
    best = rows
    for t in range(8, min(rows, limit) + 1, 8):
        if rows % t == 0:
            best = t
    return best if rows > limit else rows


def _adam_step(w, g, m, v):
    m2 = ADAM_B1 * m + (1.0 - ADAM_B1) * g
    v2 = ADAM_B2 * v + (1.0 - ADAM_B2) * (g * g)
    m_hat = m2 / (1.0 - ADAM_B1 ** ADAM_STEP)
    v_hat = v2 / (1.0 - ADAM_B2 ** ADAM_STEP)
    return -ADAM_LR * (m_hat / (jnp.sqrt(v_hat) + ADAM_EPS) + ADAM_WD * w), m2, v2


def _adamw_small(items, name):
    n = len(items)

    def body(*refs):
        ins, outs = refs[:4 * n], refs[4 * n:]
        for k in range(n):
            w_ref, g_ref, m_ref, v_ref = ins[4 * k:4 * k + 4]
            d_ref, mo_ref, vo_ref = outs[3 * k:3 * k + 3]
            d_ref[...], mo_ref[...], vo_ref[...] = _adam_step(w_ref[...], g_ref[...], m_ref[...], v_ref[...])

    def whole(a):
        return pl.BlockSpec(a.shape, lambda i: (0, 0))

    results = pl.pallas_call(
        body, name=name, grid=(1,),
        in_specs=[whole(a) for item in items for a in item],
        out_specs=[whole(item[0]) for item in items for _ in range(3)],
        out_shape=[jax.ShapeDtypeStruct(item[0].shape, f32) for item in items for _ in range(3)],
        compiler_params=_params(1),
    )(*[a for item in items for a in item])
    return [results[3 * k:3 * k + 3] for k in range(n)]


def _adamw_reduced(w, parts, m, v, name):
    layers, rows, cols = w.shape

    def body(*refs):
        w_ref, m_ref, v_ref = refs[:3]
        part_refs = refs[3:3 + layers]
        g_ref, d_ref, mo_ref, vo_ref = refs[3 + layers:]
        layer = pl.program_id(0)
        for l, p_ref in enumerate(part_refs):
            @pl.when(layer == l)
            def _():
                acc = p_ref[0].astype(f32)
                for k in range(1, N_CHIP):
                    acc = acc + p_ref[k].astype(f32)
                g_ref[0] = acc

        d_ref[0], mo_ref[0], vo_ref[0] = _adam_step(w_ref[0], g_ref[0], m_ref[0], v_ref[0])

    blk = pl.BlockSpec((1, rows, cols), lambda l: (l, 0, 0))
    return pl.pallas_call(
        body, name=name, grid=(layers,),
        in_specs=[blk] * 3 + [pl.BlockSpec(p.shape, lambda l: (0, 0, 0)) for p in parts],
        out_specs=[blk] * 4,
        out_shape=[jax.ShapeDtypeStruct(w.shape, f32)] * 4,
        compiler_params=_params(1),
    )(w, m, v, *parts)


def _sum_leading(x, name):
    n, rows, cols = x.shape
    tr = _row_tile(rows)

    def body(x_ref, o_ref):
        acc = x_ref[0].astype(f32)
        for k in range(1, n):
            acc = acc + x_ref[k].astype(f32)
        o_ref[...] = acc

    return pl.pallas_call(
        body, name=name, grid=(rows // tr,),
        in_specs=[pl.BlockSpec((n, tr, cols), lambda i: (0, i, 0))],
        out_specs=pl.BlockSpec((tr, cols), lambda i: (i, 0)),
        out_shape=jax.ShapeDtypeStruct((rows, cols), f32),
        compiler_params=_params(1),
    )(x)


def _pair_sum(gs, recvs, c_idx, name):
    n = len(gs)

    def body(c_ref, *refs):
        for g_ref, r_ref, o_ref in zip(refs[:n], refs[n:2 * n], refs[2 * n:]):
            o_ref[...] = (g_ref[...].astype(f32) + r_ref[...].astype(f32)).astype(o_ref.dtype)

    own = [pl.BlockSpec((1,) + g.shape[1:], lambda k, c: (2 * k + c[0], 0, 0)) for g in gs]
    by_chip = [pl.BlockSpec((1,) + g.shape[1:], lambda k, c: (k, 0, 0)) for g in gs]
    return list(pl.pallas_call(
        body, name=name,
        grid_spec=pltpu.PrefetchScalarGridSpec(num_scalar_prefetch=1, grid=(N_CHIP,),
                                               in_specs=own + by_chip, out_specs=by_chip),
        out_shape=[jax.ShapeDtypeStruct((N_CHIP,) + g.shape[1:], g.dtype) for g in gs],
        compiler_params=_params(1),
    )(c_idx, *gs, *recvs))


def _pack_rows(w):
    return w.reshape(N_DEV, -1, D_MODEL)


def kernel(x, even_w_in, even_w_out, a_w_s, a_b_s, a_ln_g, a_ln_b, b_w_pool, b_scale, odd_w_in, odd_w_out, c_w_dw, c_b_dw, c_ln_g, c_ln_b, d_w_dw, norm_mix_g, norm_ffn_g, ffn_w_gate, ffn_w_up, ffn_w_down, final_norm_g, loss_target, m_even_w_in, m_even_w_out, m_a_w_s, m_a_b_s, m_a_ln_g, m_a_ln_b, m_b_w_pool, m_b_scale, m_odd_w_in, m_odd_w_out, m_c_w_dw, m_c_b_dw, m_c_ln_g, m_c_ln_b, m_d_w_dw, m_norm_mix_g, m_norm_ffn_g, m_ffn_w_gate, m_ffn_w_up, m_ffn_w_down, m_final_norm_g, v_even_w_in, v_even_w_out, v_a_w_s, v_a_b_s, v_a_ln_g, v_a_ln_b, v_b_w_pool, v_b_scale, v_odd_w_in, v_odd_w_out, v_c_w_dw, v_c_b_dw, v_c_ln_g, v_c_ln_b, v_d_w_dw, v_norm_mix_g, v_norm_ffn_g, v_ffn_w_gate, v_ffn_w_up, v_ffn_w_down, v_final_norm_g):
    weights = dict(even_w_in=even_w_in, even_w_out=even_w_out, a_w_s=a_w_s, a_b_s=a_b_s, a_ln_g=a_ln_g, a_ln_b=a_ln_b,
                   b_w_pool=b_w_pool, b_scale=b_scale, odd_w_in=odd_w_in, odd_w_out=odd_w_out, c_w_dw=c_w_dw,
                   c_b_dw=c_b_dw, c_ln_g=c_ln_g, c_ln_b=c_ln_b, d_w_dw=d_w_dw, norm_mix_g=norm_mix_g,
                   norm_ffn_g=norm_ffn_g, ffn_w_gate=ffn_w_gate, ffn_w_up=ffn_w_up, ffn_w_down=ffn_w_down,
                   final_norm_g=final_norm_g)
    m_in = dict(even_w_in=m_even_w_in, even_w_out=m_even_w_out, a_w_s=m_a_w_s, a_b_s=m_a_b_s, a_ln_g=m_a_ln_g,
                a_ln_b=m_a_ln_b, b_w_pool=m_b_w_pool, b_scale=m_b_scale, odd_w_in=m_odd_w_in, odd_w_out=m_odd_w_out,
                c_w_dw=m_c_w_dw, c_b_dw=m_c_b_dw, c_ln_g=m_c_ln_g, c_ln_b=m_c_ln_b, d_w_dw=m_d_w_dw,
                norm_mix_g=m_norm_mix_g, norm_ffn_g=m_norm_ffn_g, ffn_w_gate=m_ffn_w_gate, ffn_w_up=m_ffn_w_up,
                ffn_w_down=m_ffn_w_down, final_norm_g=m_final_norm_g)
    v_in = dict(even_w_in=v_even_w_in, even_w_out=v_even_w_out, a_w_s=v_a_w_s, a_b_s=v_a_b_s, a_ln_g=v_a_ln_g,
                a_ln_b=v_a_ln_b, b_w_pool=v_b_w_pool, b_scale=v_b_scale, odd_w_in=v_odd_w_in, odd_w_out=v_odd_w_out,
                c_w_dw=v_c_w_dw, c_b_dw=v_c_b_dw, c_ln_g=v_c_ln_g, c_ln_b=v_c_ln_b, d_w_dw=v_d_w_dw,
                norm_mix_g=v_norm_mix_g, norm_ffn_g=v_norm_ffn_g, ffn_w_gate=v_ffn_w_gate, ffn_w_up=v_ffn_w_up,
                ffn_w_down=v_ffn_w_down, final_norm_g=v_final_norm_g)
    names = list(weights)

    group_parts = {
        "even": [even_w_in[0].T, even_w_out[0]],
        "ffn0": [ffn_w_gate[0].T, ffn_w_up[0].T, ffn_w_down[0]],
        "odd": [odd_w_in[0].T, odd_w_out[0]],
        "ffn1": [ffn_w_gate[1].T, ffn_w_up[1].T, ffn_w_down[1]],
    }

    def gather_jobs(*groups):
        return [_all_gather_job(p.astype(bf16)) for k in groups for p in group_parts[k]]

    def whole(gathered):
        return [g.reshape(-1, D_MODEL) for g in gathered]

    conv_names = ["c_w_dw", "c_b_dw", "c_ln_g", "c_ln_b", "d_w_dw"]
    conv_rows = [C_KERNEL, 1, 1, 1, D_KERNEL]
    conv_local = jnp.concatenate([weights[n].reshape(r, -1) for n, r in zip(conv_names, conv_rows)]
                                 + [jnp.zeros((3, c_b_dw.shape[-1]), f32)], axis=0)
    w_in_e, w_out_e = whole(_run_jobs(gather_jobs("even"), "gather_even"))

    ws, bst = a_w_s[0], a_b_s[0].T
    lng, lnb, wp, sc = a_ln_g, a_ln_b, b_w_pool[0], b_scale
    gmix = [norm_mix_g[l:l + 1] for l in range(2)]
    gffn = [norm_ffn_g[l:l + 1] for l in range(2)]
    gfin = final_norm_g.reshape(1, D_MODEL)

    h0 = x[0]
    h1, hn_e, za, pooled, mix_e, *ffn0_gathered, conv_all = _even_fwd(
        h0, w_in_e, w_out_e, ws, bst, lng, lnb, wp, sc, gmix[0],
        jobs=gather_jobs("ffn0") + [_all_gather_job(conv_local)])
    w_gate0, w_up0, w_down0 = whole(ffn0_gathered)
    conv_all = conv_all.transpose(1, 0, 2).reshape(conv_local.shape[0], -1)
    conv_offs = [sum(conv_rows[:k]) for k in range(len(conv_rows) + 1)]
    cw, cb, clg, clb, dw = [conv_all[conv_offs[k]:conv_offs[k + 1]] for k in range(len(conv_rows))]
    h2, hn_f0, gate0, up0, *rest_gathered = _ffn_fwd(h1, w_gate0, w_up0, w_down0, gffn[0],
                                                     jobs=gather_jobs("odd", "ffn1"))
    w_in_o, w_out_o, w_gate1, w_up1, w_down1 = whole(rest_gathered)
    h3, hn_o, z_o, mix_o, cv_o = _odd_fwd(h2, w_in_o, w_out_o, cw, cb, clg, clb, dw, gmix[1])
    dh4, hn_f1, gate1, up1, loss_local, g_final = _ffn_fwd(h3, w_gate1, w_up1, w_down1, gffn[1],
                                                           head=(loss_target[0], gfin))

    c_idx = lax.axis_index("c").astype(jnp.int32).reshape(1)

    def weight_grad(x, y, name, jobs=()):
        g, *job_results = _weight_grads([(x, y)], name, jobs=jobs)
        return [_pack_rows(g)] + job_results

    def siblings(parts):
        return [_sibling_exchange_job(p) for p in parts]

    def chips(pairs):
        return [_chip_exchange_job(p) for p in pairs]

    dh3, dgate1, dup1, act1, g_ffn1 = _ffn_bwd(dh4, h3, gate1, up1, w_gate1, w_up1, w_down1, gffn[1])
    part_ffn1 = (weight_grad(dgate1, hn_f1, "dw_gate1") + weight_grad(dup1, hn_f1, "dw_up1")
                 + weight_grad(act1, dh4, "dw_down1"))
    dh2, dz_o, g_cw, g_cb, g_clg, g_clb, g_dw, g_mix1, *recv_ffn1 = _odd_bwd(
        dh3, h2, z_o, cv_o, w_in_o, w_out_o, cw, clg, clb, dw, gmix[1], jobs=siblings(part_ffn1))
    pair_ffn1 = _pair_sum(part_ffn1, recv_ffn1, c_idx, "pair_sum_ffn1")
    part_odd = weight_grad(dz_o, hn_o, "dw_odd_in") + weight_grad(mix_o, dh3, "dw_odd_out")
    dh1, dgate0, dup0, act0, g_ffn0, *exchanged = _ffn_bwd(
        dh2, h1, gate0, up0, w_gate0, w_up0, w_down0, gffn[0], jobs=chips(pair_ffn1) + siblings(part_odd))
    chips_ffn1, recv_odd = exchanged[:3], exchanged[3:]
    pair_odd = _pair_sum(part_odd, recv_odd, c_idx, "pair_sum_odd")
    dw_gate0, *chips_odd = weight_grad(dgate0, hn_f0, "dw_gate0", jobs=chips(pair_odd))
    part_ffn0 = [dw_gate0] + weight_grad(dup0, hn_f0, "dw_up0") + weight_grad(act0, dh2, "dw_down0")
    part_even_out, *recv_ffn0 = weight_grad(mix_e, dh1, "dw_even_out", jobs=siblings(part_ffn0))
    pair_ffn0 = _pair_sum(part_ffn0, recv_ffn0, c_idx, "pair_sum_ffn0")
    dh0, dz_e, g_ws, g_bs, g_lng, g_lnb, g_wp, g_sc, g_mix0, *exchanged = _even_bwd(
        dh1, h0, za, pooled, w_in_e, w_out_e, ws, bst, lng, lnb, wp, sc, gmix[0],
        jobs=chips(pair_ffn0) + siblings([part_even_out]))
    chips_ffn0, recv_even_out = exchanged[:3], exchanged[3]

    lanes = HEAD
    small = [("a_w_s", g_ws), ("a_b_s", g_bs), ("a_ln_g", g_lng), ("a_ln_b", g_lnb), ("b_w_pool", g_wp),
             ("b_scale", g_sc), ("norm_mix_g", jnp.concatenate([g_mix0, g_mix1], axis=0)),
             ("norm_ffn_g", jnp.concatenate([g_ffn0, g_ffn1], axis=0)), ("final_norm_g", g_final),
             ("c_w_dw", g_cw), ("c_b_dw", g_cb), ("c_ln_g", g_clg), ("c_ln_b", g_clb), ("d_w_dw", g_dw),
             ("loss", loss_local)]
    small_rows = [-(-g.size // (8 * lanes)) * 8 for _, g in small]
    small_offs = [sum(small_rows[:k]) for k in range(len(small) + 1)]
    pad_rows = -small_offs[-1] % 256
    small_buf = jnp.concatenate(
        [jnp.pad(g.reshape(-1), (0, r * lanes - g.size)).reshape(r, lanes) for (_, g), r in zip(small, small_rows)]
        + [jnp.zeros((pad_rows, lanes), f32)], axis=0)
    part_even_in, small_all, chips_even_out = weight_grad(
        dz_e, hn_e, "dw_even_in",
        jobs=[_all_gather_job(small_buf), _pair_chip_exchange_job(part_even_out, recv_even_out)])
    small_sum = _sum_leading(small_all, "small_grad_sum")
    chips_even_in = [_exchange_in_vmem(part_even_in, "reduce_scatter_even_in")]
    grads = {}
    for k, (n, g) in enumerate(small):
        grads[n] = small_sum[small_offs[k]:small_offs[k + 1]].reshape(-1)[:g.size].reshape(g.shape)
    me = 4 * lax.axis_index("x") + 2 * lax.axis_index("y") + lax.axis_index("c")
    shard = c_b_dw.shape[-1]
    for n in conv_names:
        grads[n] = lax.dynamic_slice_in_dim(grads[n], me * shard, shard, axis=1)

    col_sharded = ("even_w_in", "odd_w_in", "ffn_w_gate", "ffn_w_up")

    def rows_view(n, a):
        return jnp.swapaxes(a, -1, -2) if n in col_sharded else a

    loss = grads.pop("loss")[0, 0]
    chip_parts = {"even_w_in": chips_even_in, "even_w_out": [chips_even_out],
                  "odd_w_in": chips_odd[:1], "odd_w_out": chips_odd[1:]}
    for k, n in enumerate(["ffn_w_gate", "ffn_w_up", "ffn_w_down"]):
        chip_parts[n] = [chips_ffn0[k], chips_ffn1[k]]

    delta, new_m, new_v = {}, {}, {}
    for n in chip_parts:
        w_rows, m_rows, v_rows = [rows_view(n, a) for a in (weights[n], m_in[n], v_in[n])]
        outs = _adamw_reduced(w_rows, chip_parts[n], m_rows, v_rows, "adamw_" + n)
        grads[n], delta[n], new_m[n], new_v[n] = [rows_view(n, o) for o in outs]
    others = [n for n in names if n not in chip_parts]
    view = {n: (-1, weights[n].shape[-1]) for n in others}
    stepped = _adamw_small([tuple(a.reshape(view[n]) for a in (weights[n], grads[n], m_in[n], v_in[n])) for n in others],
                           "adamw_small")
    for n, outs in zip(others, stepped):
        grads[n] = grads[n].reshape(weights[n].shape)
        delta[n], new_m[n], new_v[n] = [o.reshape(weights[n].shape) for o in outs]

    return (loss, dh0[None], *[grads[n] for n in names], *[delta[n] for n in names],
            *[new_m[n] for n in names], *[new_v[n] for n in names])
```
